```python
import jax
import jax.numpy as jnp
from jax import lax
import numpy as np

D_MODEL = 1024
BATCH = 2
SEQ = 8192
DEPTH = 2

GRID_W = 64
CTX_LEN = 256
N_EVEN = (DEPTH + 1) // 2
N_ODD = DEPTH // 2
D_MIX = D_MODEL
D_FF = 4 * D_MODEL
N_MOD = 6
EPS = 1e-6

ML_WIDTH = D_MIX // 2
ML_HEADS = 4
ML_DIM = ML_WIDTH // ML_HEADS
ML_CHUNK = 64
ML_CONV = 3
RW_WIDTH = D_MIX - ML_WIDTH
RW_HEAD_DIM = 64
RW_HEADS = RW_WIDTH // RW_HEAD_DIM
RW_DECAY_RANK = 64
RW_ICL_RANK = 64
RW_GATE_RANK = 128
RW_GN_EPS = 64e-5
RW_FEAT = 3 * RW_WIDTH + 2 * RW_DECAY_RANK + 2 * RW_ICL_RANK + RW_GATE_RANK
EV_SIZES = (ML_WIDTH, ML_WIDTH, ML_WIDTH, ML_WIDTH, 4 * ML_HEADS, RW_FEAT)
EV_PROJ = sum(EV_SIZES)
HG_WIDTH = D_MIX // 2
HG_HEADS = 4
HG_DIM = HG_WIDTH // HG_HEADS
HG_CHUNK = 32
MLA_WIDTH = D_MIX - HG_WIDTH
MLA_HEADS = 4
MLA_NOPE = 128
MLA_ROPE = 64
MLA_V = MLA_WIDTH // MLA_HEADS
MLA_Q_RANK = 256
MLA_KV_RANK = 128
OD_SIZES = (HG_WIDTH, HG_WIDTH, HG_WIDTH, HG_WIDTH, HG_WIDTH, MLA_Q_RANK, MLA_KV_RANK + MLA_ROPE)
OD_PROJ = sum(OD_SIZES)
ATTN_BLOCK = 128
ROPE_BASE = 10000.0

kernel_name = 'hybrid_mlstm_rwkv7_hgrn2_mla_dit'


def split_cols(t, sizes):
    cuts = [int(s) for s in np.cumsum(sizes)[:-1]]
    return jnp.split(t, cuts, axis=-1)


def rmsnorm(x, g):
    xf = x.astype(jnp.float32)
    y = xf * lax.rsqrt(jnp.mean(xf * xf, axis=-1, keepdims=True) + EPS)
    return (y * g.astype(jnp.float32)).astype(x.dtype)


def head_rmsnorm(y, g):
    B, T, H, d = y.shape
    yf = y.astype(jnp.float32)
    yf = yf * lax.rsqrt(jnp.mean(yf * yf, axis=-1, keepdims=True) + EPS)
    return yf.reshape(B, T, H * d) * g.astype(jnp.float32)


def head_groupnorm(y, w, b):
    H, d = y.shape[-2:]
    mu = jnp.mean(y, axis=-1, keepdims=True)
    var = jnp.mean(jnp.square(y - mu), axis=-1, keepdims=True)
    return (y - mu) * lax.rsqrt(var + RW_GN_EPS) * w.astype(jnp.float32).reshape(H, d) + b.astype(jnp.float32).reshape(H, d)


def modulate(h, shift, scale):
    return h * (1.0 + scale) + shift


def sqrelu_mlp(h, w1, w2):
    return jnp.square(jax.nn.relu(h @ w1)) @ w2


def two_seq(fn, t, n_ctx):
    return jnp.concatenate([fn(t[:, :n_ctx]), fn(t[:, n_ctx:])], axis=1)


def neighbour_mean(t):
    tp = jnp.pad(t, ((0, 0), (1, 1), (0, 0)))
    return 0.5 * (tp[:, :-2] + tp[:, 2:])


def centred_dwconv(t, w):
    k = w.shape[0]
    half = k // 2
    n = t.shape[1]
    tp = jnp.pad(t, ((0, 0), (half, half), (0, 0)))
    out = tp[:, 0:n] * w[0]
    for j in range(1, k):
        out = out + tp[:, j:j + n] * w[j]
    return out


def dir_stack(t_fw, t_bw, n_ctx):
    bw = jnp.concatenate([jnp.flip(t_bw[:, :n_ctx], 1), jnp.flip(t_bw[:, n_ctx:], 1)], axis=1)
    return jnp.concatenate([t_fw, bw], axis=0)


def dir_merge(y, n_batch, n_ctx):
    y_fw, y_bw = y[:n_batch], y[n_batch:]
    bw = jnp.concatenate([jnp.flip(y_bw[:, :n_ctx], 1), jnp.flip(y_bw[:, n_ctx:], 1)], axis=1)
    return y_fw + bw


def to_chunks(a, L):
    Z, T = a.shape[:2]
    a = a.reshape((Z, T // L, L) + a.shape[2:])
    perm = (1, 0, 3, 2) + tuple(range(4, a.ndim))
    return a.transpose(perm)


def from_chunks(a):
    nc, Z, H, L, d = a.shape
    return a.transpose((1, 0, 3, 2, 4)).reshape(Z, nc * L, H, d)


def axial_rope(x, row, col):
    half = x.shape[-1] // 2
    quarter = half // 2
    inv = ROPE_BASE ** (-jnp.arange(quarter, dtype=jnp.float32) / quarter)

    def rot(xa, pos):
        ang = pos.astype(jnp.float32)[:, None] * inv[None, :]
        ang = ang.reshape((ang.shape[0],) + (1,) * (xa.ndim - 3) + (quarter,))
        cos = jnp.cos(ang).astype(xa.dtype)
        sin = jnp.sin(ang).astype(xa.dtype)
        x1, x2 = xa[..., :quarter], xa[..., quarter:]
        return jnp.concatenate([x1 * cos - x2 * sin, x1 * sin + x2 * cos], axis=-1)

    return jnp.concatenate([rot(x[..., :half], row), rot(x[..., half:], col)], axis=-1)


def block_attention(q_nope, q_rope, k_nope, k_rope, v, scale):
    B, Nq, H, _ = q_nope.shape
    nb = Nq // ATTN_BLOCK

    def blocks(t):
        return jnp.moveaxis(t.reshape((B, nb, ATTN_BLOCK) + t.shape[2:]), 1, 0)

    def attend(qs):
        qn, qr = qs
        s = jnp.einsum('bqhd,bkhd->bhqk', qn, k_nope) + jnp.einsum('bqhd,bkd->bhqk', qr, k_rope)
        p = jax.nn.softmax(s.astype(jnp.float32) * scale, axis=-1)
        return jnp.einsum('bhqk,bkhd->bqhd', p.astype(v.dtype), v)

    o = lax.map(attend, (blocks(q_nope), blocks(q_rope)))
    return jnp.moveaxis(o, 0, 1).reshape(B, Nq, H * v.shape[-1])


def mlstm_chunkwise(q, k, v, ig, lf):
    f32 = jnp.float32
    q, k, v, ig, lf = (t.astype(f32) for t in (q, k, v, ig, lf))
    Z, T, H, dk = q.shape
    dv = v.shape[-1]
    L = ML_CHUNK
    k = k * dk ** -0.5
    xs = tuple(to_chunks(t, L) for t in (q, k, v, ig, lf))
    mask = jnp.tril(jnp.ones((L, L), dtype=bool))

    def step(carry, inp):
        C, n, m = carry
        qc, kc, vc, ic, fc = inp
        b = jnp.cumsum(fc, axis=-1)
        logd = jnp.where(mask, b[..., :, None] - b[..., None, :] + ic[..., None, :], -jnp.inf)
        m_inter = b + m[..., None]
        m_t = jnp.maximum(m_inter, jnp.max(logd, axis=-1))
        s = jnp.einsum('zhtd,zhsd->zhts', qc, kc) * jnp.exp(logd - m_t[..., None])
        w_inter = jnp.exp(m_inter - m_t)
        num = jnp.einsum('zhts,zhsv->zhtv', s, vc) + w_inter[..., None] * jnp.einsum('zhtd,zhvd->zhtv', qc, C)
        den = jnp.sum(s, axis=-1) + w_inter * jnp.einsum('zhtd,zhd->zht', qc, n)
        h = num / jnp.maximum(jnp.abs(den), jnp.exp(-m_t))[..., None]
        b_last = b[..., -1]
        logw = b_last[..., None] - b + ic
        m_new = jnp.maximum(b_last + m, jnp.max(logw, axis=-1))
        w = jnp.exp(logw - m_new[..., None])
        decay = jnp.exp(b_last + m - m_new)
        C = decay[..., None, None] * C + jnp.einsum('zhs,zhsv,zhsd->zhvd', w, vc, kc)
        n = decay[..., None] * n + jnp.einsum('zhs,zhsd->zhd', w, kc)
        return (C, n, m_new), h

    init = (jnp.zeros((Z, H, dv, dk), f32), jnp.zeros((Z, H, dk), f32), jnp.zeros((Z, H), f32))
    _, h = lax.scan(step, init, xs)
    return from_chunks(h)


def rwkv7_scan(r, w, k, v, a_vec, b_vec):
    Z, T, H, dk = k.shape
    dv = v.shape[-1]
    xs = tuple(jnp.moveaxis(t, 1, 0) for t in (r, w, k, v, a_vec, b_vec))

    def step(S, inp):
        rt, wt, kt, vt, at, bt = inp
        sa = jnp.einsum('zhvk,zhk->zhv', S, at)
        S = S * wt[:, :, None, :] + sa[..., None] * bt[:, :, None, :] + vt[..., None] * kt[:, :, None, :]
        return S, jnp.einsum('zhvk,zhk->zhv', S, rt)

    _, y = lax.scan(step, jnp.zeros((Z, H, dv, dk), jnp.float32), xs)
    return jnp.moveaxis(y, 0, 1)


def hgrn2_chunkwise(q, k, v, log_f):
    Z, T, H, dk = q.shape
    dv = v.shape[-1]
    L = HG_CHUNK
    q = q * dk ** -0.5
    xs = tuple(to_chunks(t, L) for t in (q, k, v, log_f))
    mask = jnp.tril(jnp.ones((L, L), dtype=bool))[:, :, None]

    def step(S, inp):
        qc, kc, vc, gc = inp
        bc = jnp.cumsum(gc, axis=2)
        rel = jnp.where(mask, bc[:, :, :, None, :] - bc[:, :, None, :, :], -jnp.inf)
        att = jnp.einsum('zhtd,zhsd,zhtsd->zhts', qc, kc, jnp.exp(rel))
        o = att @ vc + jnp.einsum('zhtd,zhdv->zhtv', qc * jnp.exp(bc), S)
        b_last = bc[:, :, -1, :]
        S = jnp.exp(b_last)[..., None] * S + jnp.einsum('zhsd,zhsv->zhdv', kc * jnp.exp(b_last[:, :, None, :] - bc), vc)
        return S, o

    _, o = lax.scan(step, jnp.zeros((Z, H, dk, dv), jnp.float32), xs)
    return from_chunks(o)


def even_mixer(hj, n_ctx, need_ctx, w_in, ml_conv, ml_gate_b, ml_norm, rw_mu, rw_w0, rw_w2, rw_a0, rw_a2,
               rw_g2, rw_kk, rw_ka, rw_rk, rw_ln_w, rw_ln_b):
    dt = hj.dtype
    f32 = jnp.float32
    B, T, _ = hj.shape

    def hd(t, d):
        return t.reshape(B, T, -1, d)

    def z(a, b=None):
        return dir_stack(a, a if b is None else b, n_ctx)

    mq, mk, mv, mo, mg, rfeat = split_cols(hj @ w_in, EV_SIZES)

    qk = two_seq(lambda t: centred_dwconv(t, ml_conv), jnp.concatenate([mq, mk], axis=-1), n_ctx)
    q, k = jnp.split(qk, 2, axis=-1)
    gates = mg.reshape(B, T, 4, ML_HEADS).astype(f32) + ml_gate_b.astype(f32)
    ig_fw, lf_fw = gates[:, :, 0], jax.nn.log_sigmoid(gates[:, :, 1])
    ig_bw, lf_bw = gates[:, :, 2], jax.nn.log_sigmoid(gates[:, :, 3])
    hm = mlstm_chunkwise(z(hd(q, ML_DIM)), z(hd(k, ML_DIM)), z(hd(mv, ML_DIM)), z(ig_fw, ig_bw), z(lf_fw, lf_bw))
    y_ml = head_rmsnorm(dir_merge(hm, B, n_ctx), ml_norm) * jax.nn.sigmoid(mo.astype(f32))

    rfeat = two_seq(lambda t: t + (neighbour_mean(t) - t) * rw_mu, rfeat, n_ctx)
    r, k, v, wl, al, gl = split_cols(rfeat, (RW_WIDTH, RW_WIDTH, RW_WIDTH, 2 * RW_DECAY_RANK, 2 * RW_ICL_RANK, RW_GATE_RANK))
    r = hd(r.astype(f32), RW_HEAD_DIM)
    k = k.astype(f32)
    v = hd(v.astype(f32), RW_HEAD_DIM)
    kk = hd(k * rw_kk, RW_HEAD_DIM)
    kk = kk * lax.rsqrt(jnp.sum(kk * kk, axis=-1, keepdims=True) + 1e-12)
    rk = rw_rk.reshape(RW_HEADS, RW_HEAD_DIM)
    ka = rw_ka.reshape(RW_HEADS, RW_HEAD_DIM)
    wl_dirs = jnp.split(wl.astype(f32), 2, axis=-1)
    al_dirs = jnp.split(al.astype(f32), 2, axis=-1)
    decay, key_d, icl = [], [], []
    bonus = 0.0
    for d in range(2):
        w_log = -jax.nn.softplus(-(rw_w0[d] + jnp.tanh(wl_dirs[d]) @ rw_w2[d])) - 0.5
        a = hd(jax.nn.sigmoid(rw_a0[d] + al_dirs[d] @ rw_a2[d]), RW_HEAD_DIM)
        k_d = hd(k, RW_HEAD_DIM) * (1.0 + (a - 1.0) * ka)
        decay.append(hd(jnp.exp(-jnp.exp(w_log)), RW_HEAD_DIM))
        key_d.append(k_d)
        icl.append(kk * a)
        bonus = bonus + jnp.sum(r * k_d * rk, axis=-1, keepdims=True) * v
    y = rwkv7_scan(z(r), z(decay[0], decay[1]), z(key_d[0], key_d[1]), z(v), z(-kk), z(icl[0], icl[1]))
    y = head_groupnorm(dir_merge(y, B, n_ctx), rw_ln_w, rw_ln_b) + bonus
    y_rw = y.reshape(B, T, RW_WIDTH) * (jax.nn.sigmoid(gl.astype(f32)) @ rw_g2)

    out = jnp.concatenate([y_ml, y_rw], axis=-1).astype(dt)
    y_ctx = out[:, :n_ctx] if need_ctx else None
    return y_ctx, out[:, n_ctx:]


def odd_mixer(hj, n_ctx, need_ctx, lb, row, col, w_in, hg_norm, mla_q_norm, mla_w_qb, mla_kv_norm, mla_w_kvb):
    dt = hj.dtype
    f32 = jnp.float32
    B, T, _ = hj.shape

    def hd(t, d):
        return t.reshape(B, T, -1, d)

    def z(a, b=None):
        return dir_stack(a, a if b is None else b, n_ctx)

    hq, hf_fw, hf_bw, hi, hg, cq, ckv = split_cols(hj @ w_in, OD_SIZES)

    def forget(pre):
        pre = pre.astype(f32)
        log_f = jnp.logaddexp(jnp.log(lb), jnp.log1p(-lb) + jax.nn.log_sigmoid(pre))
        one_minus_f = (1.0 - lb) * jax.nn.sigmoid(-pre)
        return hd(log_f, HG_DIM), hd(one_minus_f, HG_DIM)

    lf_fw, kf_fw = forget(hf_fw)
    lf_bw, kf_bw = forget(hf_bw)
    q = hd(jax.nn.silu(hq.astype(f32)), HG_DIM)
    i = hd(hi.astype(f32), HG_DIM)
    o = hgrn2_chunkwise(z(q), z(kf_fw, kf_bw), z(i), z(lf_fw, lf_bw))
    y_hg = (head_rmsnorm(dir_merge(o, B, n_ctx), hg_norm) * jax.nn.silu(hg.astype(f32))).astype(dt)

    q_all = hd(rmsnorm(cq, mla_q_norm) @ mla_w_qb, MLA_NOPE + MLA_ROPE)
    q_nope, q_rope = q_all[..., :MLA_NOPE], q_all[..., MLA_NOPE:]
    c_kv, k_rope = ckv[..., :MLA_KV_RANK], ckv[..., MLA_KV_RANK:]
    kv = hd(rmsnorm(c_kv, mla_kv_norm) @ mla_w_kvb, MLA_NOPE + MLA_V)
    k_nope, v = kv[..., :MLA_NOPE], kv[..., MLA_NOPE:]
    k_rope = jnp.concatenate([k_rope[:, :n_ctx], axial_rope(k_rope[:, n_ctx:], row, col)], axis=1)
    scale = (MLA_NOPE + MLA_ROPE) ** -0.5
    y_att = block_attention(q_nope[:, n_ctx:], axial_rope(q_rope[:, n_ctx:], row, col), k_nope, k_rope, v, scale)
    y_lat = jnp.concatenate([y_hg[:, n_ctx:], y_att.astype(dt)], axis=-1)
    y_ctx = None
    if need_ctx:
        y_c = block_attention(q_nope[:, :n_ctx], q_rope[:, :n_ctx], k_nope[:, :n_ctx], k_rope[:, :n_ctx], v[:, :n_ctx], scale)
        y_ctx = jnp.concatenate([y_hg[:, :n_ctx], y_c.astype(dt)], axis=-1)
    return y_ctx, y_lat


def setup_inputs(seed: int = 0) -> dict:
    key = jax.random.key(seed)
    ks = iter(jax.random.split(key, 48))

    def nrm(shape, s):
        return jax.random.normal(next(ks), shape, jnp.float32) * s

    def uni(shape, lo, hi):
        return jax.random.uniform(next(ks), shape, jnp.float32, lo, hi)

    D = D_MODEL
    ig_b = nrm((N_EVEN, 2, ML_HEADS), 0.1)
    fg_b = uni((N_EVEN, 2, ML_HEADS), 3.0, 6.0)
    return {
        'x': nrm((BATCH, SEQ, D), 1.0),
        'c': nrm((BATCH, D), 1.0),
        'ctx': nrm((BATCH, CTX_LEN, D), 1.0),
        'c_ctx': nrm((D,), 1.0),
        'ada_w': nrm((DEPTH, D, N_MOD * D), 0.5 * D ** -0.5),
        'ada_b': nrm((DEPTH, N_MOD * D), 0.02),
        'norm_g': 1.0 + nrm((DEPTH, 4, D), 0.02),
        'out_w': nrm((DEPTH, D_MIX, D), D_MIX ** -0.5),
        'mlp_w1': nrm((DEPTH, D, D_FF), D ** -0.5),
        'mlp_w2': nrm((DEPTH, D_FF, D), D_FF ** -0.5),
        'hg_lb': nrm((DEPTH, HG_WIDTH), 0.5),
        'ev_w_in': nrm((N_EVEN, D, EV_PROJ), D ** -0.5),
        'ml_conv': nrm((N_EVEN, ML_CONV, 2 * ML_WIDTH), ML_CONV ** -0.5),
        'ml_gate_b': jnp.stack([ig_b[:, 0], fg_b[:, 0], ig_b[:, 1], fg_b[:, 1]], axis=1),
        'ml_norm': 1.0 + nrm((N_EVEN, ML_WIDTH), 0.02),
        'rw_mu': uni((N_EVEN, RW_FEAT), 0.0, 1.0),
        'rw_w0': uni((N_EVEN, 2, RW_WIDTH), -6.0, -1.0),
        'rw_w2': nrm((N_EVEN, 2, RW_DECAY_RANK, RW_WIDTH), 0.5 * RW_DECAY_RANK ** -0.5),
        'rw_a0': nrm((N_EVEN, 2, RW_WIDTH), 0.1),
        'rw_a2': nrm((N_EVEN, 2, RW_ICL_RANK, RW_WIDTH), 0.5 * RW_ICL_RANK ** -0.5),
        'rw_g2': nrm((N_EVEN, RW_GATE_RANK, RW_WIDTH), RW_GATE_RANK ** -0.5),
        'rw_kk': 0.85 + nrm((N_EVEN, RW_WIDTH), 0.05),
        'rw_ka': 1.0 + nrm((N_EVEN, RW_WIDTH), 0.05),
        'rw_rk': nrm((N_EVEN, RW_WIDTH), 0.1),
        'rw_ln_w': 1.0 + nrm((N_EVEN, RW_WIDTH), 0.02),
        'rw_ln_b': nrm((N_EVEN, RW_WIDTH), 0.02),
        'od_w_in': nrm((N_ODD, D, OD_PROJ), D ** -0.5),
        'hg_norm': 1.0 + nrm((N_ODD, HG_WIDTH), 0.02),
        'mla_q_norm': 1.0 + nrm((N_ODD, MLA_Q_RANK), 0.02),
        'mla_w_qb': nrm((N_ODD, MLA_Q_RANK, MLA_HEADS * (MLA_NOPE + MLA_ROPE)), MLA_Q_RANK ** -0.5),
        'mla_kv_norm': 1.0 + nrm((N_ODD, MLA_KV_RANK), 0.02),
        'mla_w_kvb': nrm((N_ODD, MLA_KV_RANK, MLA_HEADS * (MLA_NOPE + MLA_V)), MLA_KV_RANK ** -0.5),
    }


def reference(x, c, ctx, c_ctx, ada_w, ada_b, norm_g, out_w, mlp_w1, mlp_w2, hg_lb, ev_w_in, ml_conv, ml_gate_b,
              ml_norm, rw_mu, rw_w0, rw_w2, rw_a0, rw_a2, rw_g2, rw_kk, rw_ka, rw_rk, rw_ln_w, rw_ln_b, od_w_in,
              hg_norm, mla_q_norm, mla_w_qb, mla_kv_norm, mla_w_kvb):
    B, N, D = x.shape
    n_ctx = ctx.shape[1]
    rows = N // GRID_W
    row = jnp.broadcast_to(jnp.arange(rows)[:, None], (rows, GRID_W)).reshape(-1)
    col = jnp.broadcast_to(jnp.arange(GRID_W)[None, :], (rows, GRID_W)).reshape(-1)
    lb_all = jax.nn.softmax(hg_lb.astype(jnp.float32), axis=0)
    lb_all = jnp.cumsum(lb_all, axis=0) - lb_all[0]
    silu_c = jax.nn.silu(c)
    silu_cc = jax.nn.silu(c_ctx)
    xc = ctx
    for l in range(DEPTH):
        last = l == DEPTH - 1
        mod = (silu_c @ ada_w[l] + ada_b[l])[:, None, :]
        modc = (silu_cc @ ada_w[l] + ada_b[l])[None, None, :]
        sh_a, sc_a, g_a, sh_m, sc_m, g_m = jnp.split(mod, N_MOD, axis=-1)
        shc_a, scc_a, gc_a, shc_m, scc_m, gc_m = jnp.split(modc, N_MOD, axis=-1)
        hj = jnp.concatenate([modulate(rmsnorm(xc, norm_g[l, 0]), shc_a, scc_a),
                              modulate(rmsnorm(x, norm_g[l, 0]), sh_a, sc_a)], axis=1)
        if l % 2 == 0:
            e = l // 2
            y_ctx, y_lat = even_mixer(hj, n_ctx, not last, ev_w_in[e], ml_conv[e], ml_gate_b[e], ml_norm[e], rw_mu[e],
                                      rw_w0[e], rw_w2[e], rw_a0[e], rw_a2[e], rw_g2[e], rw_kk[e], rw_ka[e], rw_rk[e],
                                      rw_ln_w[e], rw_ln_b[e])
        else:
            o = l // 2
            y_ctx, y_lat = odd_mixer(hj, n_ctx, not last, lb_all[l], row, col, od_w_in[o], hg_norm[o], mla_q_norm[o],
                                     mla_w_qb[o], mla_kv_norm[o], mla_w_kvb[o])
        x = x + g_a * rmsnorm(y_lat @ out_w[l], norm_g[l, 1])
        x = x + g_m * rmsnorm(sqrelu_mlp(modulate(rmsnorm(x, norm_g[l, 2]), sh_m, sc_m), mlp_w1[l], mlp_w2[l]), norm_g[l, 3])
        if not last:
            xc = xc + gc_a * rmsnorm(y_ctx @ out_w[l], norm_g[l, 1])
            xc = xc + gc_m * rmsnorm(sqrelu_mlp(modulate(rmsnorm(xc, norm_g[l, 2]), shc_m, scc_m), mlp_w1[l], mlp_w2[l]), norm_g[l, 3])
    return x
```

```python
import functools

import numpy as np
import jax
import jax.numpy as jnp
from jax import lax
from jax.experimental import pallas as pl
from jax.experimental.pallas import tpu as pltpu

F32 = jnp.float32
BF16 = jnp.bfloat16
HI = lax.Precision.HIGHEST

D_MODEL = 1024
GRID_W = 64
D_FF = 4 * D_MODEL
N_MOD = 6
EPS = 1e-6
ML_WIDTH = 512
ML_HEADS = 4
ML_DIM = 128
RW_WIDTH = 512
RW_HEAD_DIM = 64
RW_GN_EPS = 64e-5
RW_FEAT = 1920
HG_WIDTH = 512
HG_HEADS = 4
HG_DIM = 128
MLA_HEADS = 4
MLA_NOPE = 128
MLA_ROPE = 64
MLA_V = 128
MLA_Q_RANK = 256
MLA_KV_RANK = 128
ROPE_BASE = 10000.0

LANES = 128
SUBLANES = 8
VMEM_LIMIT_BYTES = 56 * 2**20

ROW_TILE = 256
ML_CHUNK = 128
RW_CHUNK = 64
HG_BLOCK = 128
HG_CHUNK = 32
ATTN_TQ = 256

EV_PROJ_PAD = 4096
OD_PROJ_PAD = 3328


def _dot(a, b, prec=None):
    return jnp.dot(a, b, preferred_element_type=F32, precision=prec)


def _dot_nt(a, b, prec=None):
    return lax.dot_general(a, b, (((1,), (1,)), ((), ())), preferred_element_type=F32, precision=prec)


def _dot_tn(a, b, prec=None):
    return lax.dot_general(a, b, (((0,), (0,)), ((), ())), preferred_element_type=F32, precision=prec)


def _bf(x):
    return x.astype(BF16)


def _mm(a, b):
    return _dot(_bf(a), _bf(b))


def _mm_nt(a, b):
    return _dot_nt(_bf(a), _bf(b))


def _mm_tn(a, b):
    return _dot_tn(_bf(a), _bf(b))


def _sigmoid(x):
    return 1.0 / (1.0 + jnp.exp(-x))


def _log_sigmoid(x):
    return jnp.minimum(x, 0.0) - jnp.log(1.0 + jnp.exp(-jnp.abs(x)))


def _softplus(x):
    return jnp.maximum(x, 0.0) + jnp.log(1.0 + jnp.exp(-jnp.abs(x)))


def _rms(x, g):
    return x * lax.rsqrt(jnp.mean(x * x, axis=-1, keepdims=True) + EPS) * g


def _order_mask(n, sgn, strict):
    row = lax.broadcasted_iota(jnp.int32, (n, n), 0)
    col = lax.broadcasted_iota(jnp.int32, (n, n), 1)
    diff = (row - col) * sgn
    return diff > 0 if strict else diff >= 0


def _chunk_index(d, j, nctx, nc):
    bwd = jnp.where(j < nctx, nctx - 1 - j, nc - 1 - (j - nctx))
    return jnp.where(d == 0, j, bwd)


def _params(sem):
    return pltpu.CompilerParams(dimension_semantics=sem, vmem_limit_bytes=VMEM_LIMIT_BYTES)


def _const_spec(shape):
    nd = len(shape)
    return pl.BlockSpec(shape, lambda *_: (0,) * nd)


def _mod_spec(k, seg_of):
    return pl.BlockSpec((None, None, None, 1, D_MODEL), lambda b, i: (b, seg_of(i), k, 0, 0))


def _ada_body(c_ref, w_ref, b_ref, o_ref):
    c = c_ref[...]
    o_ref[...] = _dot(c * _sigmoid(c), w_ref[...], HI) + b_ref[...]


def _ada_mod(c8, ada_w, ada_b):
    depth, d, n = ada_w.shape
    tn = 1024
    return pl.pallas_call(
        _ada_body,
        grid=(depth, n // tn),
        in_specs=[
            _const_spec((SUBLANES, d)),
            pl.BlockSpec((None, d, tn), lambda l, j: (l, 0, j)),
            pl.BlockSpec((None, 1, tn), lambda l, j: (l, 0, j)),
        ],
        out_specs=pl.BlockSpec((None, SUBLANES, tn), lambda l, j: (l, 0, j)),
        out_shape=jax.ShapeDtypeStruct((depth, SUBLANES, n), F32),
        compiler_params=_params(("parallel", "parallel")),
        name="ada_mod",
    )(c8, ada_w, ada_b.reshape(depth, 1, n))


def _inproj_body(x_ref, g_ref, sh_ref, sc_ref, w_ref, o_ref):
    h = _rms(x_ref[...], g_ref[...]) * (1.0 + sc_ref[...]) + sh_ref[...]
    o_ref[...] = _dot(_bf(h), w_ref[...])


def _inproj(xall, g, mods, w, nctx_t):
    b, t, d = xall.shape
    p = w.shape[1]
    tm = ROW_TILE
    seg = lambda i: (i >= nctx_t).astype(jnp.int32)
    return pl.pallas_call(
        _inproj_body,
        grid=(b, t // tm),
        in_specs=[
            pl.BlockSpec((None, tm, d), lambda bb, i: (bb, i, 0)),
            _const_spec((1, d)),
            _mod_spec(0, seg),
            _mod_spec(1, seg),
            _const_spec((d, p)),
        ],
        out_specs=pl.BlockSpec((None, tm, p), lambda bb, i: (bb, i, 0)),
        out_shape=jax.ShapeDtypeStruct((b, t, p), F32),
        compiler_params=_params(("parallel", "parallel")),
        name="inproj",
    )(xall, g, mods, mods, w)


def _even_prep_body(nctx_t, nt, rf_ref, rfp_ref, rfn_ref, qk_ref, qkp_ref, qkn_ref,
                    conv_ref, gb_ref, mu_ref, w0_ref, w2_ref, a0_ref, a2_ref, g2_ref,
                    kk_ref, ka_ref, rk_ref, seg_ref,
                    oqk_ref, og_ref, orws_ref, orwd_ref, obg_ref):
    i = pl.program_id(1)
    has_prev = jnp.logical_and(i != 0, i != nctx_t).astype(F32)
    has_next = jnp.logical_and(i != nctx_t - 1, i != nt - 1).astype(F32)
    tm = rf_ref.shape[0]
    row = lax.broadcasted_iota(jnp.int32, (tm, 1), 0)

    def neighbours(cur, prev8, next8):
        p = prev8[SUBLANES - 1:SUBLANES, :] * has_prev
        n = next8[0:1, :] * has_next
        dn = jnp.where(row == 0, p, pltpu.roll(cur, 1, 0))
        up = jnp.where(row == tm - 1, n, pltpu.roll(cur, tm - 1, 0))
        return dn, up

    qk = qk_ref[...]
    dn, up = neighbours(qk, qkp_ref[...], qkn_ref[...])
    oqk_ref[...] = dn * conv_ref[0:1, :] + qk * conv_ref[1:2, :] + up * conv_ref[2:3, :]

    gv = rf_ref[:, RW_FEAT:RW_FEAT + LANES] + gb_ref[...]
    lane = lax.broadcasted_iota(jnp.int32, (1, LANES), 1)
    is_f = jnp.logical_and(jnp.bitwise_and(jnp.right_shift(lane, 2), 1) == 1, lane < 16)
    og_ref[...] = jnp.where(is_f, _log_sigmoid(gv), gv)

    cur = rf_ref[:, 0:RW_FEAT]
    dn, up = neighbours(cur, rfp_ref[:, 0:RW_FEAT], rfn_ref[:, 0:RW_FEAT])
    rf = cur + (0.5 * (dn + up) - cur) * mu_ref[...]
    w = RW_WIDTH
    r, k, v = rf[:, 0:w], rf[:, w:2 * w], rf[:, 2 * w:3 * w]
    wl = rf[:, 3 * w:3 * w + LANES]
    al = rf[:, 3 * w + LANES:3 * w + 2 * LANES]
    gl = rf[:, 3 * w + 2 * LANES:3 * w + 3 * LANES]
    seg = seg_ref[...]
    kk = k * kk_ref[...]
    kk = kk * lax.rsqrt(_dot(kk * kk, seg, HI) + 1e-12)
    twl = jnp.tanh(wl)
    bonus = jnp.zeros_like(v)
    for d in range(2):
        w_log = -_softplus(-(w0_ref[d:d + 1, :] + _dot(twl, w2_ref[d], HI))) - 0.5
        a = _sigmoid(a0_ref[d:d + 1, :] + _dot(al, a2_ref[d], HI))
        kd = k * (1.0 + (a - 1.0) * ka_ref[...])
        bonus = bonus + _dot(r * kd * rk_ref[...], seg, HI) * v
        orwd_ref[d, :, 0:w] = -jnp.exp(w_log)
        orwd_ref[d, :, w:2 * w] = kd
        orwd_ref[d, :, 2 * w:3 * w] = kk * a
    orws_ref[:, 0:w] = r
    orws_ref[:, w:2 * w] = v
    orws_ref[:, 2 * w:3 * w] = -kk
    obg_ref[:, 0:w] = bonus
    obg_ref[:, w:2 * w] = _dot(_sigmoid(gl), g2_ref[...], HI)


def _even_prep(proj, nctx_t, conv, gate_b, mu, w0, w2cat, a0, a2cat, g2, kk, ka, rk, seg64):
    b, t, _ = proj.shape
    tm = ROW_TILE
    nt = t // tm
    r8 = tm // SUBLANES
    last8 = t // SUBLANES - 1
    cur = lambda c: (lambda bb, i: (bb, i, c))
    prv = lambda c: (lambda bb, i: (bb, jnp.maximum(i * r8 - 1, 0), c))
    nxt = lambda c: (lambda bb, i: (bb, jnp.minimum((i + 1) * r8, last8), c))
    w = RW_WIDTH
    return pl.pallas_call(
        functools.partial(_even_prep_body, nctx_t, nt),
        grid=(b, nt),
        in_specs=[
            pl.BlockSpec((None, tm, 2048), cur(0)),
            pl.BlockSpec((None, SUBLANES, 2048), prv(0)),
            pl.BlockSpec((None, SUBLANES, 2048), nxt(0)),
            pl.BlockSpec((None, tm, 1024), cur(2)),
            pl.BlockSpec((None, SUBLANES, 1024), prv(2)),
            pl.BlockSpec((None, SUBLANES, 1024), nxt(2)),
            _const_spec(conv.shape), _const_spec(gate_b.shape), _const_spec(mu.shape),
            _const_spec(w0.shape), _const_spec(w2cat.shape), _const_spec(a0.shape),
            _const_spec(a2cat.shape), _const_spec(g2.shape), _const_spec(kk.shape),
            _const_spec(ka.shape), _const_spec(rk.shape), _const_spec(seg64.shape),
        ],
        out_specs=[
            pl.BlockSpec((None, tm, 1024), cur(0)),
            pl.BlockSpec((None, tm, LANES), cur(0)),
            pl.BlockSpec((None, tm, 3 * w), cur(0)),
            pl.BlockSpec((2, None, tm, 3 * w), lambda bb, i: (0, bb, i, 0)),
            pl.BlockSpec((None, tm, 2 * w), cur(0)),
        ],
        out_shape=[
            jax.ShapeDtypeStruct((b, t, 1024), F32),
            jax.ShapeDtypeStruct((b, t, LANES), F32),
            jax.ShapeDtypeStruct((b, t, 3 * w), F32),
            jax.ShapeDtypeStruct((2, b, t, 3 * w), F32),
            jax.ShapeDtypeStruct((b, t, 2 * w), F32),
        ],
        compiler_params=_params(("parallel", "parallel")),
        name="even_prep",
    )(proj, proj, proj, proj, proj, proj, conv, gate_b, mu, w0, w2cat, a0, a2cat, g2, kk, ka, rk, seg64)


def _mlstm_body(q_ref, k_ref, v_ref, gc_ref, gr_ref, o_ref, c_scr, n_scr, m_scr):
    d = pl.program_id(0)
    j = pl.program_id(2)
    n_rows = q_ref.shape[0]

    @pl.when(j == 0)
    def _():
        c_scr[...] = jnp.zeros_like(c_scr)
        n_scr[...] = jnp.zeros_like(n_scr)
        m_scr[...] = jnp.zeros_like(m_scr)

    mask = _order_mask(n_rows, 1 - 2 * d, strict=False)
    mf = mask.astype(F32)
    gc = gc_ref[...]
    gr = gr_ref[...]
    b_col = _dot(mf, gc, HI)
    b_row = _dot_nt(gr, mf, HI)
    tot_col = jnp.sum(gc, axis=0, keepdims=True)
    for h in range(ML_HEADS):
        sl = slice(h * ML_DIM, (h + 1) * ML_DIM)
        q = q_ref[:, sl]
        k = k_ref[:, sl] * (ML_DIM ** -0.5)
        v = v_ref[:, sl]
        ic_c, ic_r = gc[:, h:h + 1], gr[h:h + 1, :]
        bc, br = b_col[:, ML_HEADS + h:ML_HEADS + h + 1], b_row[ML_HEADS + h:ML_HEADS + h + 1, :]
        b_tot = tot_col[:, ML_HEADS + h:ML_HEADS + h + 1]
        m_prev = m_scr[h][0:1, 0:1]
        logd = jnp.where(mask, bc - br + ic_r, -1e30)
        m_inter = bc + m_prev
        m_t = jnp.maximum(m_inter, jnp.max(logd, axis=-1, keepdims=True))
        s = _mm_nt(q, k) * jnp.exp(logd - m_t)
        w_inter = jnp.exp(m_inter - m_t)
        c_prev = c_scr[h]
        n_prev = n_scr[h][0:1, :]
        num = _mm(s, v) + w_inter * _mm_nt(q, c_prev)
        den = jnp.sum(s, axis=-1, keepdims=True) + w_inter * jnp.sum(q * n_prev, axis=-1, keepdims=True)
        o_ref[:, sl] = num / jnp.maximum(jnp.abs(den), jnp.exp(-m_t))
        logw = b_tot - bc + ic_c
        m_new = jnp.maximum(b_tot + m_prev, jnp.max(logw, axis=0, keepdims=True))
        wgt = jnp.exp(logw - m_new)
        decay = jnp.exp(b_tot + m_prev - m_new)
        c_scr[h] = decay * c_prev + _mm_tn(v * wgt, k)
        n_scr[h] = jnp.broadcast_to(decay * n_prev + jnp.sum(wgt * k, axis=0, keepdims=True), (SUBLANES, ML_DIM))
        m_scr[h] = jnp.broadcast_to(m_new, (SUBLANES, LANES))


def _mlstm(qk, proj, gcol, grow, n_ctx):
    b, t, _ = qk.shape
    n_rows = ML_CHUNK
    nc, nctx = t // n_rows, n_ctx // n_rows
    ci = lambda d, j: _chunk_index(d, j, nctx, nc)
    return pl.pallas_call(
        _mlstm_body,
        grid=(2, b, nc),
        in_specs=[
            pl.BlockSpec((None, n_rows, ML_WIDTH), lambda d, bb, j: (bb, ci(d, j), 0)),
            pl.BlockSpec((None, n_rows, ML_WIDTH), lambda d, bb, j: (bb, ci(d, j), 1)),
            pl.BlockSpec((None, n_rows, ML_WIDTH), lambda d, bb, j: (bb, ci(d, j), 6)),
            pl.BlockSpec((None, None, n_rows, 2 * ML_HEADS), lambda d, bb, j: (d, bb, ci(d, j), 0)),
            pl.BlockSpec((None, None, 2 * ML_HEADS, n_rows), lambda d, bb, j: (d, bb, 0, ci(d, j))),
        ],
        out_specs=pl.BlockSpec((None, None, n_rows, ML_WIDTH), lambda d, bb, j: (d, bb, ci(d, j), 0)),
        out_shape=jax.ShapeDtypeStruct((2, b, t, ML_WIDTH), F32),
        scratch_shapes=[
            pltpu.VMEM((ML_HEADS, ML_DIM, ML_DIM), F32),
            pltpu.VMEM((ML_HEADS, SUBLANES, ML_DIM), F32),
            pltpu.VMEM((ML_HEADS, SUBLANES, LANES), F32),
        ],
        compiler_params=_params(("parallel", "parallel", "arbitrary")),
        name="mlstm",
    )(qk, qk, proj, gcol, grow)


def _rwkv_body(r_ref, v_ref, a_ref, lw_ref, k_ref, b_ref, o_ref, h_scr):
    d = pl.program_id(0)
    j = pl.program_id(2)
    n_rows = r_ref.shape[0]
    two = 2 * n_rows
    shift = n_rows.bit_length() - 1

    @pl.when(j == 0)
    def _():
        h_scr[...] = jnp.zeros_like(h_scr)

    sgn = 1 - 2 * d
    cum = _order_mask(n_rows, sgn, strict=False).astype(F32)
    row = lax.broadcasted_iota(jnp.int32, (two, two), 0)
    col = lax.broadcasted_iota(jnp.int32, (two, two), 1)
    same = jnp.right_shift(row, shift) == jnp.right_shift(col, shift)
    diff = (jnp.bitwise_and(row, n_rows - 1) - jnp.bitwise_and(col, n_rows - 1)) * sgn
    m_strict = jnp.logical_and(same, diff > 0)
    m_incl = jnp.logical_and(same, diff >= 0)
    eye = (row == col).astype(F32)
    head0 = lax.broadcasted_iota(jnp.int32, (1, LANES), 1) < RW_HEAD_DIM

    def stack(x):
        return jnp.concatenate([jnp.where(head0, x, 0.0), jnp.where(head0, 0.0, x)], axis=0)

    def fold(x):
        return x[:n_rows] + x[n_rows:]

    for p in range(RW_WIDTH // LANES):
        sl = slice(p * LANES, (p + 1) * LANES)
        r, v, a = r_ref[:, sl], v_ref[:, sl], a_ref[:, sl]
        lw, k, b = lw_ref[:, sl], k_ref[:, sl], b_ref[:, sl]
        g = _dot(cum, lw, HI)
        g_tot = jnp.sum(lw, axis=0, keepdims=True)
        inv = jnp.exp(-g)
        to_end = jnp.exp(g_tot - g)
        x_s = stack(a * jnp.exp(g - lw))
        r_s = stack(r * jnp.exp(g))
        v_s = stack(v)
        bh_s = stack(b * to_end)
        kh_s = stack(k * to_end)
        big = _mm_nt(jnp.concatenate([x_s, r_s], axis=0),
                     jnp.concatenate([stack(b * inv), stack(k * inv)], axis=0))
        a_ab = jnp.where(m_strict, big[:two, :two], 0.0)
        a_ak = jnp.where(m_strict, big[:two, two:], 0.0)
        a_rb = jnp.where(m_incl, big[two:, :two], 0.0)
        a_rk = jnp.where(m_incl, big[two:, two:], 0.0)
        t_inv = eye + a_ab
        pw = a_ab
        for _ in range(shift - 1):
            pw = _mm(pw, pw)
            t_inv = t_inv + _mm(t_inv, pw)
        w_u = _mm(t_inv, x_s)
        u_v = _mm(t_inv, _mm(a_ak, v_s))
        q_eff = fold(r_s + _mm(a_rb, w_u))
        y_in = fold(_mm(a_rb, u_v) + _mm(a_rk, v_s))
        h_t = h_scr[p]
        o_ref[:, sl] = _mm_nt(q_eff, h_t) + y_in
        h_scr[p] = (jnp.exp(g_tot) * h_t + _mm(h_t, _mm_tn(w_u, bh_s))
                    + _mm_tn(u_v, bh_s) + _mm_tn(v_s, kh_s))


def _rwkv(rws, rwd, n_ctx):
    b, t, _ = rws.shape
    n_rows = RW_CHUNK
    nc, nctx = t // n_rows, n_ctx // n_rows
    ci = lambda d, j: _chunk_index(d, j, nctx, nc)
    w = RW_WIDTH
    shared = lambda c: pl.BlockSpec((None, n_rows, w), lambda d, bb, j: (bb, ci(d, j), c))
    per_dir = lambda c: pl.BlockSpec((None, None, n_rows, w), lambda d, bb, j: (d, bb, ci(d, j), c))
    return pl.pallas_call(
        _rwkv_body,
        grid=(2, b, nc),
        in_specs=[shared(0), shared(1), shared(2), per_dir(0), per_dir(1), per_dir(2)],
        out_specs=pl.BlockSpec((None, None, n_rows, w), lambda d, bb, j: (d, bb, ci(d, j), 0)),
        out_shape=jax.ShapeDtypeStruct((2, b, t, w), F32),
        scratch_shapes=[pltpu.VMEM((w // LANES, LANES, LANES), F32)],
        compiler_params=_params(("parallel", "parallel", "arbitrary")),
        name="rwkv7",
    )(rws, rws, rws, rwd, rwd, rwd)


def _tail(x, mix, outw_ref, w1_ref, w2_ref, ng_ref, ga_ref, shm_ref, scm_ref, gm_ref):
    x1 = x + ga_ref[...] * _rms(_dot(_bf(mix), outw_ref[...]), ng_ref[1:2, :])
    h = _rms(x1, ng_ref[2:3, :]) * (1.0 + scm_ref[...]) + shm_ref[...]
    u = jnp.maximum(_dot(_bf(h), w1_ref[...]), 0.0)
    dd = _dot(_bf(u * u), w2_ref[...])
    return x1 + gm_ref[...] * _rms(dd, ng_ref[3:4, :])


def _even_post_body(x_ref, hm_ref, mo_ref, yrw_ref, bg_ref, mln_ref, lnw_ref, lnb_ref, s128_ref, s64_ref,
                    outw_ref, w1_ref, w2_ref, ng_ref, ga_ref, shm_ref, scm_ref, gm_ref, o_ref):
    w = RW_WIDTH
    ym = hm_ref[0] + hm_ref[1]
    ym = ym * lax.rsqrt(_dot(ym * ym, s128_ref[...], HI) + EPS) * mln_ref[...] * _sigmoid(mo_ref[...])
    y = yrw_ref[0] + yrw_ref[1]
    yc = y - _dot(y, s64_ref[...], HI)
    y = yc * lax.rsqrt(_dot(yc * yc, s64_ref[...], HI) + RW_GN_EPS) * lnw_ref[...] + lnb_ref[...] + bg_ref[:, 0:w]
    mix = jnp.concatenate([ym, y * bg_ref[:, w:2 * w]], axis=-1)
    o_ref[...] = _tail(x_ref[...], mix, outw_ref, w1_ref, w2_ref, ng_ref, ga_ref, shm_ref, scm_ref, gm_ref)


def _tail_specs(seg):
    d = D_MODEL
    return [
        _const_spec((d, d)), _const_spec((d, D_FF)), _const_spec((D_FF, d)), _const_spec((4, d)),
        _mod_spec(2, seg), _mod_spec(3, seg), _mod_spec(4, seg), _mod_spec(5, seg),
    ]


def _even_post(xall, hm, proj, yrw, bg, mln, lnw, lnb, s128, s64, outw, w1, w2, ng, mods, nctx_t):
    b, t, d = xall.shape
    tm = ROW_TILE
    seg = lambda i: (i >= nctx_t).astype(jnp.int32)
    w = RW_WIDTH
    return pl.pallas_call(
        _even_post_body,
        grid=(b, t // tm),
        in_specs=[
            pl.BlockSpec((None, tm, d), lambda bb, i: (bb, i, 0)),
            pl.BlockSpec((2, None, tm, w), lambda bb, i: (0, bb, i, 0)),
            pl.BlockSpec((None, tm, w), lambda bb, i: (bb, i, 7)),
            pl.BlockSpec((2, None, tm, w), lambda bb, i: (0, bb, i, 0)),
            pl.BlockSpec((None, tm, 2 * w), lambda bb, i: (bb, i, 0)),
            _const_spec((1, w)), _const_spec((1, w)), _const_spec((1, w)),
            _const_spec((w, w)), _const_spec((w, w)),
        ] + _tail_specs(seg),
        out_specs=pl.BlockSpec((None, tm, d), lambda bb, i: (bb, i, 0)),
        out_shape=jax.ShapeDtypeStruct((b, t, d), F32),
        compiler_params=_params(("parallel", "parallel")),
        name="even_post",
    )(xall, hm, proj, yrw, bg, mln, lnw, lnb, s128, s64, outw, w1, w2, ng, mods, mods, mods, mods)


def _hgrn_body(q_ref, f_ref, i_ref, lb_ref, o_ref, s_scr):
    d = pl.program_id(0)
    j = pl.program_id(2)
    n_rows = q_ref.shape[0]
    n_sub = n_rows // HG_CHUNK

    @pl.when(j == 0)
    def _():
        s_scr[...] = jnp.zeros_like(s_scr)

    mask = _order_mask(HG_CHUNK, 1 - 2 * d, strict=False)
    mf = mask.astype(F32)
    lb = lb_ref[...]
    for sc in range(n_sub):
        off = pl.multiple_of(jnp.where(d == 0, sc, n_sub - 1 - sc) * HG_CHUNK, HG_CHUNK)
        rows = pl.ds(off, HG_CHUNK)
        qa, pre, vv = q_ref[rows, :], f_ref[rows, :], i_ref[rows, :]
        q = qa * _sigmoid(qa) * (HG_DIM ** -0.5)
        log_f = jnp.log(lb + (1.0 - lb) * _sigmoid(pre))
        kf = (1.0 - lb) * _sigmoid(-pre)
        bc = _dot(mf, log_f, HI)
        tot = jnp.sum(log_f, axis=0, keepdims=True)
        half = 0.5 * tot
        q_mid = q * jnp.exp(bc - half)
        k_mid = kf * jnp.exp(half - bc)
        q_in = q * jnp.exp(bc)
        k_out = kf * jnp.exp(tot - bc)
        e_tot = jnp.exp(tot)
        for h in range(HG_HEADS):
            sl = slice(h * HG_DIM, (h + 1) * HG_DIM)
            att = jnp.where(mask, _mm_nt(q_mid[:, sl], k_mid[:, sl]), 0.0)
            s_t = s_scr[h]
            o_ref[rows, sl] = _mm(att, vv[:, sl]) + _mm_nt(q_in[:, sl], s_t)
            s_scr[h] = e_tot[:, sl] * s_t + _mm_tn(vv[:, sl], k_out[:, sl])


def _hgrn(proj, lb, n_ctx):
    b, t, _ = proj.shape
    n_rows = HG_BLOCK
    nc, nctx = t // n_rows, n_ctx // n_rows
    ci = lambda d, j: _chunk_index(d, j, nctx, nc)
    w = HG_WIDTH
    return pl.pallas_call(
        _hgrn_body,
        grid=(2, b, nc),
        in_specs=[
            pl.BlockSpec((None, n_rows, w), lambda d, bb, j: (bb, ci(d, j), 0)),
            pl.BlockSpec((None, n_rows, w), lambda d, bb, j: (bb, ci(d, j), 1 + d)),
            pl.BlockSpec((None, n_rows, w), lambda d, bb, j: (bb, ci(d, j), 3)),
            _const_spec((1, w)),
        ],
        out_specs=pl.BlockSpec((None, None, n_rows, w), lambda d, bb, j: (d, bb, ci(d, j), 0)),
        out_shape=jax.ShapeDtypeStruct((2, b, t, w), F32),
        scratch_shapes=[pltpu.VMEM((HG_HEADS, HG_DIM, HG_DIM), F32)],
        compiler_params=_params(("parallel", "parallel", "arbitrary")),
        name="hgrn2",
    )(proj, proj, proj, lb)


def _mla_kv_body(c_ref, s_ref, cos_ref, sin_ref, nrm_ref, w_ref, k_ref, v_ref):
    blk = c_ref[...]
    kv = _dot(_bf(_rms(blk[:, 0:MLA_KV_RANK], nrm_ref[...])), w_ref[...])
    rope = (blk * cos_ref[...] + s_ref[...] * sin_ref[...])[:, LANES:2 * LANES]
    for h in range(MLA_HEADS):
        base = h * (MLA_NOPE + MLA_V)
        k_ref[h] = jnp.concatenate([kv[:, base:base + MLA_NOPE], rope], axis=-1).astype(BF16)
        v_ref[h] = kv[:, base + MLA_NOPE:base + MLA_NOPE + MLA_V].astype(BF16)


def _mla_kv(proj, cos, sin, nrm, wkv):
    b, t, _ = proj.shape
    tm = ROW_TILE
    return pl.pallas_call(
        _mla_kv_body,
        grid=(b, t // tm),
        in_specs=[
            pl.BlockSpec((None, tm, 256), lambda bb, i: (bb, i, 11)),
            pl.BlockSpec((None, tm, 256), lambda bb, i: (bb, i, 12)),
            pl.BlockSpec((tm, 256), lambda bb, i: (i, 0)),
            pl.BlockSpec((tm, 256), lambda bb, i: (i, 0)),
            _const_spec((1, MLA_KV_RANK)),
            _const_spec(wkv.shape),
        ],
        out_specs=[
            pl.BlockSpec((None, MLA_HEADS, tm, 256), lambda bb, i: (bb, 0, i, 0)),
            pl.BlockSpec((None, MLA_HEADS, tm, MLA_V), lambda bb, i: (bb, 0, i, 0)),
        ],
        out_shape=[
            jax.ShapeDtypeStruct((b, MLA_HEADS, t, 256), BF16),
            jax.ShapeDtypeStruct((b, MLA_HEADS, t, MLA_V), BF16),
        ],
        compiler_params=_params(("parallel", "parallel")),
        name="mla_kv",
    )(proj, proj, cos, sin, nrm, wkv)


def _mla_q_body(c_ref, cos_ref, sin_ref, nrm_ref, w1_ref, w2_ref, q_ref):
    cn = _bf(_rms(c_ref[...], nrm_ref[...]))
    qa = _dot(cn, w1_ref[...])
    qb = _dot(cn, w2_ref[...])
    scale = (MLA_NOPE + MLA_ROPE) ** -0.5
    for h in range(MLA_HEADS):
        a = qa[:, h * 256:(h + 1) * 256]
        rope = a * cos_ref[...] + qb[:, h * 256:(h + 1) * 256] * sin_ref[...]
        q = jnp.concatenate([a[:, 0:MLA_NOPE], rope[:, MLA_NOPE:]], axis=-1) * scale
        q_ref[h] = q.astype(BF16)


def _mla_q(proj, cos, sin, nrm, w1, w2, n_ctx):
    b, t, _ = proj.shape
    tm = ROW_TILE
    n = t - n_ctx
    off = n_ctx // tm
    return pl.pallas_call(
        _mla_q_body,
        grid=(b, n // tm),
        in_specs=[
            pl.BlockSpec((None, tm, 256), lambda bb, i: (bb, i + off, 10)),
            pl.BlockSpec((tm, 256), lambda bb, i: (i + off, 0)),
            pl.BlockSpec((tm, 256), lambda bb, i: (i + off, 0)),
            _const_spec((1, MLA_Q_RANK)),
            _const_spec(w1.shape), _const_spec(w2.shape),
        ],
        out_specs=pl.BlockSpec((None, MLA_HEADS, tm, 256), lambda bb, i: (bb, 0, i, 0)),
        out_shape=jax.ShapeDtypeStruct((b, MLA_HEADS, n, 256), BF16),
        compiler_params=_params(("parallel", "parallel")),
        name="mla_q",
    )(proj, cos, sin, nrm, w1, w2)


def _attn_body(q_ref, k_ref, v_ref, o_ref):
    s = _dot_nt(q_ref[...], k_ref[...])
    p = jnp.exp(s - jnp.max(s, axis=-1, keepdims=True))
    o_ref[...] = _dot(_bf(p), v_ref[...]) / jnp.sum(p, axis=-1, keepdims=True)


def _attention(q, k, v):
    b, h, n, dq = q.shape
    t = k.shape[2]
    tq = ATTN_TQ
    return pl.pallas_call(
        _attn_body,
        grid=(b, h, n // tq),
        in_specs=[
            pl.BlockSpec((None, None, tq, dq), lambda bb, hh, i: (bb, hh, i, 0)),
            pl.BlockSpec((None, None, t, dq), lambda bb, hh, i: (bb, hh, 0, 0)),
            pl.BlockSpec((None, None, t, MLA_V), lambda bb, hh, i: (bb, hh, 0, 0)),
        ],
        out_specs=pl.BlockSpec((None, tq, MLA_V), lambda bb, hh, i: (bb, i, hh)),
        out_shape=jax.ShapeDtypeStruct((b, n, h * MLA_V), F32),
        compiler_params=_params(("parallel", "parallel", "parallel")),
        name="mla_attention",
    )(q, k, v)


def _odd_post_body(x_ref, o_ref_in, hg_ref, att_ref, hgn_ref, s128_ref,
                   outw_ref, w1_ref, w2_ref, ng_ref, ga_ref, shm_ref, scm_ref, gm_ref, o_ref):
    y = o_ref_in[0] + o_ref_in[1]
    g = hg_ref[...]
    y = y * lax.rsqrt(_dot(y * y, s128_ref[...], HI) + EPS) * hgn_ref[...] * (g * _sigmoid(g))
    mix = jnp.concatenate([y, att_ref[...]], axis=-1)
    o_ref[...] = _tail(x_ref[...], mix, outw_ref, w1_ref, w2_ref, ng_ref, ga_ref, shm_ref, scm_ref, gm_ref)


def _odd_post(xall, ohg, proj, att, hgn, s128, outw, w1, w2, ng, mods, n_ctx):
    b, t, d = xall.shape
    tm = ROW_TILE
    n = t - n_ctx
    off = n_ctx // tm
    seg = lambda i: 1
    w = HG_WIDTH
    return pl.pallas_call(
        _odd_post_body,
        grid=(b, n // tm),
        in_specs=[
            pl.BlockSpec((None, tm, d), lambda bb, i: (bb, i + off, 0)),
            pl.BlockSpec((2, None, tm, w), lambda bb, i: (0, bb, i + off, 0)),
            pl.BlockSpec((None, tm, w), lambda bb, i: (bb, i + off, 4)),
            pl.BlockSpec((None, tm, w), lambda bb, i: (bb, i, 0)),
            _const_spec((1, w)), _const_spec((w, w)),
        ] + _tail_specs(seg),
        out_specs=pl.BlockSpec((None, tm, d), lambda bb, i: (bb, i, 0)),
        out_shape=jax.ShapeDtypeStruct((b, n, d), F32),
        compiler_params=_params(("parallel", "parallel")),
        name="odd_post",
    )(xall, ohg, proj, att, hgn, s128, outw, w1, w2, ng, mods, mods, mods, mods)


def _block_mean_matrix(width, group):
    idx = np.arange(width) // group
    return jnp.asarray((idx[:, None] == idx[None, :]).astype(np.float32) / group)


def _rope_tables(n, n_ctx):
    quarter = MLA_ROPE // 4
    inv = ROPE_BASE ** (-jnp.arange(quarter, dtype=F32) / quarter)
    tok = jnp.arange(n)
    ang_r = (tok // GRID_W).astype(F32)[:, None] * inv[None, :]
    ang_c = (tok % GRID_W).astype(F32)[:, None] * inv[None, :]
    cos = jnp.concatenate([jnp.cos(ang_r)] * 2 + [jnp.cos(ang_c)] * 2, axis=-1)
    sin = jnp.concatenate([-jnp.sin(ang_r), jnp.sin(ang_r), -jnp.sin(ang_c), jnp.sin(ang_c)], axis=-1)
    cos = jnp.concatenate([jnp.ones((n_ctx, MLA_ROPE), F32), cos], axis=0)
    sin = jnp.concatenate([jnp.zeros((n_ctx, MLA_ROPE), F32), sin], axis=0)
    pad = lambda a: jnp.pad(a, ((0, 0), (LANES, 256 - LANES - MLA_ROPE)))
    return pad(cos), pad(sin)


_ROPE_SWAP = np.concatenate([np.arange(16, 32), np.arange(0, 16), np.arange(48, 64), np.arange(32, 48)])


def kernel(x, c, ctx, c_ctx, ada_w, ada_b, norm_g, out_w, mlp_w1, mlp_w2, hg_lb, ev_w_in, ml_conv, ml_gate_b, ml_norm, rw_mu, rw_w0, rw_w2, rw_a0, rw_a2, rw_g2, rw_kk, rw_ka, rw_rk, rw_ln_w, rw_ln_b, od_w_in, hg_norm, mla_q_norm, mla_w_qb, mla_kv_norm, mla_w_kvb):
    b, n, d = x.shape
    n_ctx = ctx.shape[1]
    t = n_ctx + n
    assert d == D_MODEL and b + 1 <= SUBLANES
    assert n_ctx % ROW_TILE == 0 and n % ROW_TILE == 0 and n % ATTN_TQ == 0
    assert n_ctx % ML_CHUNK == 0 and n % ML_CHUNK == 0 and n_ctx % HG_BLOCK == 0 and n % HG_BLOCK == 0
    nctx_t = n_ctx // ROW_TILE
    row1 = lambda a: a.reshape(1, -1)

    c8 = jnp.concatenate([c, c_ctx[None], jnp.zeros((SUBLANES - b - 1, d), F32)], axis=0)
    mod = _ada_mod(c8, ada_w, ada_b).reshape(-1, SUBLANES, N_MOD, d)

    def layer_mods(l):
        lat = mod[l, :b]
        cm = jnp.broadcast_to(mod[l, b][None], lat.shape)
        return jnp.stack([cm, lat], axis=1)[:, :, :, None, :]

    xall = jnp.concatenate([ctx, x], axis=1)
    s128 = _block_mean_matrix(512, 128)
    s64 = _block_mean_matrix(512, 64)

    l, e = 0, 0
    mods = layer_mods(l)
    wi = ev_w_in[e]
    w_pad = jnp.concatenate([wi[:, 2064:3984], wi[:, 2048:2064], jnp.zeros((d, LANES - 16), F32), wi[:, 0:2048]], axis=1)
    proj = _inproj(xall, row1(norm_g[l, 0]), mods, _bf(w_pad), nctx_t)

    gate_b = jnp.pad(ml_gate_b[e].reshape(1, 16), ((0, 0), (0, LANES - 16)))
    zeros64 = jnp.zeros((64, RW_WIDTH), F32)
    w2cat = jnp.stack([jnp.concatenate([rw_w2[e, 0], zeros64], 0), jnp.concatenate([zeros64, rw_w2[e, 1]], 0)])
    a2cat = jnp.stack([jnp.concatenate([rw_a2[e, 0], zeros64], 0), jnp.concatenate([zeros64, rw_a2[e, 1]], 0)])
    qk, gates, rws, rwd, bg = _even_prep(
        proj, nctx_t, ml_conv[e], gate_b, row1(rw_mu[e]), rw_w0[e], w2cat, rw_a0[e], a2cat, rw_g2[e],
        row1(rw_kk[e]), row1(rw_ka[e]), row1(rw_rk[e]), s64 * RW_HEAD_DIM)

    g4 = gates[:, :, :16].reshape(b, t, 4, ML_HEADS)
    gcol = jnp.stack([g4[:, :, 0:2].reshape(b, t, 8), g4[:, :, 2:4].reshape(b, t, 8)], axis=0)
    grow = jnp.swapaxes(gcol, 2, 3)
    hm = _mlstm(qk, proj, gcol, grow, n_ctx)
    yrw = _rwkv(rws, rwd, n_ctx)
    xall = _even_post(xall, hm, proj, yrw, bg, row1(ml_norm[e]), row1(rw_ln_w[e]), row1(rw_ln_b[e]), s128, s64,
                      _bf(out_w[l]), _bf(mlp_w1[l]), _bf(mlp_w2[l]), norm_g[l], mods, nctx_t)

    l, o = 1, 0
    mods = layer_mods(l)
    lb_all = jax.nn.softmax(hg_lb.astype(F32), axis=0)
    lb_all = jnp.cumsum(lb_all, axis=0) - lb_all[0]
    wi = od_w_in[o]
    kr = wi[:, 2944:3008]
    z = lambda k: jnp.zeros((d, k), F32)
    w_pad = jnp.concatenate([wi[:, 0:2944], kr, z(64), z(LANES), kr[:, _ROPE_SWAP], z(64)], axis=1)
    proj = _inproj(xall, row1(norm_g[l, 0]), mods, _bf(w_pad), nctx_t)

    ohg = _hgrn(proj, row1(lb_all[l]), n_ctx)

    cos, sin = _rope_tables(n, n_ctx)
    k_all, v_all = _mla_kv(proj, cos, sin, row1(mla_kv_norm[o]), _bf(mla_w_kvb[o]))
    wq = mla_w_qb[o].reshape(MLA_Q_RANK, MLA_HEADS, MLA_NOPE + MLA_ROPE)
    zq = lambda k: jnp.zeros((MLA_Q_RANK, MLA_HEADS, k), F32)
    wq1 = jnp.concatenate([wq, zq(64)], axis=-1).reshape(MLA_Q_RANK, -1)
    wq2 = jnp.concatenate([zq(MLA_NOPE), wq[:, :, MLA_NOPE:][:, :, _ROPE_SWAP], zq(64)], axis=-1).reshape(MLA_Q_RANK, -1)
    q_all = _mla_q(proj, cos, sin, row1(mla_q_norm[o]), _bf(wq1), _bf(wq2), n_ctx)
    att = _attention(q_all, k_all, v_all)

    return _odd_post(xall, ohg, proj, att, row1(hg_norm[o]), s128,
                     _bf(out_w[l]), _bf(mlp_w1[l]), _bf(mlp_w2[l]), norm_g[l], mods, n_ctx)
```

```python
import functools

import numpy as np
import jax
import jax.numpy as jnp
from jax import lax
from jax.experimental import pallas as pl
from jax.experimental.pallas import tpu as pltpu

F32 = jnp.float32
BF16 = jnp.bfloat16
HI = lax.Precision.HIGHEST

D_MODEL = 1024
GRID_W = 64
D_FF = 4 * D_MODEL
N_MOD = 6
EPS = 1e-6
ML_WIDTH = 512
ML_HEADS = 4
ML_DIM = 128
RW_WIDTH = 512
RW_HEAD_DIM = 64
RW_GN_EPS = 64e-5
RW_FEAT = 1920
HG_WIDTH = 512
HG_HEADS = 4
HG_DIM = 128
MLA_HEADS = 4
MLA_NOPE = 128
MLA_ROPE = 64
MLA_V = 128
MLA_Q_RANK = 256
MLA_KV_RANK = 128
ROPE_BASE = 10000.0

LANES = 128
SUBLANES = 8
VMEM_LIMIT_BYTES = 56 * 2**20

ROW_TILE = 256
ML_CHUNK = 128
RW_CHUNK = 64
HG_BLOCK = 128
HG_CHUNK = 32
ATTN_TQ = 256

EV_PROJ_PAD = 4096
OD_PROJ_PAD = 3328


def _dot(a, b, prec=None):
    return jnp.dot(a, b, preferred_element_type=F32, precision=prec)


def _dot_nt(a, b, prec=None):
    return lax.dot_general(a, b, (((1,), (1,)), ((), ())), preferred_element_type=F32, precision=prec)


def _dot_tn(a, b, prec=None):
    return lax.dot_general(a, b, (((0,), (0,)), ((), ())), preferred_element_type=F32, precision=prec)


def _bf(x):
    return x.astype(BF16)


def _mm(a, b):
    return _dot(_bf(a), _bf(b))


def _mm_nt(a, b):
    return _dot_nt(_bf(a), _bf(b))


def _mm_tn(a, b):
    return _dot_tn(_bf(a), _bf(b))


def _sigmoid(x):
    return 1.0 / (1.0 + jnp.exp(-x))


def _log_sigmoid(x):
    return jnp.minimum(x, 0.0) - jnp.log(1.0 + jnp.exp(-jnp.abs(x)))


def _softplus(x):
    return jnp.maximum(x, 0.0) + jnp.log(1.0 + jnp.exp(-jnp.abs(x)))


def _rms(x, g):
    return x * lax.rsqrt(jnp.mean(x * x, axis=-1, keepdims=True) + EPS) * g


def _order_mask(n, sgn, strict):
    row = lax.broadcasted_iota(jnp.int32, (n, n), 0)
    col = lax.broadcasted_iota(jnp.int32, (n, n), 1)
    diff = (row - col) * sgn
    return diff > 0 if strict else diff >= 0


def _chunk_index(d, j, nctx, nc):
    bwd = jnp.where(j < nctx, nctx - 1 - j, nc - 1 - (j - nctx))
    return jnp.where(d == 0, j, bwd)


def _params(sem):
    return pltpu.CompilerParams(dimension_semantics=sem, vmem_limit_bytes=VMEM_LIMIT_BYTES)


def _const_spec(shape):
    nd = len(shape)
    return pl.BlockSpec(shape, lambda *_: (0,) * nd)


def _mod_spec(k, seg_of):
    return pl.BlockSpec((None, None, None, 1, D_MODEL), lambda b, i: (b, seg_of(i), k, 0, 0))


def _ada_body(c_ref, w_ref, b_ref, o_ref):
    c = c_ref[...]
    o_ref[...] = _dot(c * _sigmoid(c), w_ref[...], HI) + b_ref[...]


def _ada_mod(c8, ada_w, ada_b):
    depth, d, n = ada_w.shape
    tn = 1024
    return pl.pallas_call(
        _ada_body,
        grid=(depth, n // tn),
        in_specs=[
            _const_spec((SUBLANES, d)),
            pl.BlockSpec((None, d, tn), lambda l, j: (l, 0, j)),
            pl.BlockSpec((None, 1, tn), lambda l, j: (l, 0, j)),
        ],
        out_specs=pl.BlockSpec((None, SUBLANES, tn), lambda l, j: (l, 0, j)),
        out_shape=jax.ShapeDtypeStruct((depth, SUBLANES, n), F32),
        compiler_params=_params(("parallel", "parallel")),
        name="ada_mod",
    )(c8, ada_w, ada_b.reshape(depth, 1, n))


def _inproj_body(x_ref, g_ref, sh_ref, sc_ref, w_ref, o_ref):
    h = _rms(x_ref[...], g_ref[...]) * (1.0 + sc_ref[...]) + sh_ref[...]
    o_ref[...] = _dot(_bf(h), w_ref[...])


def _inproj(xall, g, mods, w, nctx_t):
    b, t, d = xall.shape
    p = w.shape[1]
    tm = ROW_TILE
    seg = lambda i: (i >= nctx_t).astype(jnp.int32)
    return pl.pallas_call(
        _inproj_body,
        grid=(b, t // tm),
        in_specs=[
            pl.BlockSpec((None, tm, d), lambda bb, i: (bb, i, 0)),
            _const_spec((1, d)),
            _mod_spec(0, seg),
            _mod_spec(1, seg),
            _const_spec((d, p)),
        ],
        out_specs=pl.BlockSpec((None, tm, p), lambda bb, i: (bb, i, 0)),
        out_shape=jax.ShapeDtypeStruct((b, t, p), F32),
        compiler_params=_params(("parallel", "parallel")),
        name="inproj",
    )(xall, g, mods, mods, w)


def _even_prep_body(nctx_t, nt, rf_ref, rfp_ref, rfn_ref, qk_ref, qkp_ref, qkn_ref,
                    conv_ref, gb_ref, mu_ref, w0_ref, w2_ref, a0_ref, a2_ref, g2_ref,
                    kk_ref, ka_ref, rk_ref, seg_ref,
                    oqk_ref, og_ref, orws_ref, orwd_ref, obg_ref):
    i = pl.program_id(1)
    has_prev = jnp.logical_and(i != 0, i != nctx_t).astype(F32)
    has_next = jnp.logical_and(i != nctx_t - 1, i != nt - 1).astype(F32)
    tm = rf_ref.shape[0]
    row = lax.broadcasted_iota(jnp.int32, (tm, 1), 0)

    def neighbours(cur, prev8, next8):
        p = prev8[SUBLANES - 1:SUBLANES, :] * has_prev
        n = next8[0:1, :] * has_next
        dn = jnp.where(row == 0, p, pltpu.roll(cur, 1, 0))
        up = jnp.where(row == tm - 1, n, pltpu.roll(cur, tm - 1, 0))
        return dn, up

    qk = qk_ref[...]
    dn, up = neighbours(qk, qkp_ref[...], qkn_ref[...])
    oqk_ref[...] = dn * conv_ref[0:1, :] + qk * conv_ref[1:2, :] + up * conv_ref[2:3, :]

    gv = rf_ref[:, RW_FEAT:RW_FEAT + LANES] + gb_ref[...]
    lane = lax.broadcasted_iota(jnp.int32, (1, LANES), 1)
    is_f = jnp.logical_and(jnp.bitwise_and(jnp.right_shift(lane, 2), 1) == 1, lane < 16)
    og_ref[...] = jnp.where(is_f, _log_sigmoid(gv), gv)

    cur = rf_ref[:, 0:RW_FEAT]
    dn, up = neighbours(cur, rfp_ref[:, 0:RW_FEAT], rfn_ref[:, 0:RW_FEAT])
    rf = cur + (0.5 * (dn + up) - cur) * mu_ref[...]
    w = RW_WIDTH
    r, k, v = rf[:, 0:w], rf[:, w:2 * w], rf[:, 2 * w:3 * w]
    wl = rf[:, 3 * w:3 * w + LANES]
    al = rf[:, 3 * w + LANES:3 * w + 2 * LANES]
    gl = rf[:, 3 * w + 2 * LANES:3 * w + 3 * LANES]
    seg = seg_ref[...]
    kk = k * kk_ref[...]
    kk = kk * lax.rsqrt(_dot(kk * kk, seg, HI) + 1e-12)
    twl = jnp.tanh(wl)
    bonus = jnp.zeros_like(v)
    for d in range(2):
        w_log = -_softplus(-(w0_ref[d:d + 1, :] + _dot(twl, w2_ref[d], HI))) - 0.5
        a = _sigmoid(a0_ref[d:d + 1, :] + _dot(al, a2_ref[d], HI))
        kd = k * (1.0 + (a - 1.0) * ka_ref[...])
        bonus = bonus + _dot(r * kd * rk_ref[...], seg, HI) * v
        orwd_ref[d, :, 0:w] = -jnp.exp(w_log)
        orwd_ref[d, :, w:2 * w] = kd
        orwd_ref[d, :, 2 * w:3 * w] = kk * a
    orws_ref[:, 0:w] = r
    orws_ref[:, w:2 * w] = v
    orws_ref[:, 2 * w:3 * w] = -kk
    obg_ref[:, 0:w] = bonus
    obg_ref[:, w:2 * w] = _dot(_sigmoid(gl), g2_ref[...], HI)


def _even_prep(proj, nctx_t, conv, gate_b, mu, w0, w2cat, a0, a2cat, g2, kk, ka, rk, seg64):
    b, t, _ = proj.shape
    tm = ROW_TILE
    nt = t // tm
    r8 = tm // SUBLANES
    last8 = t // SUBLANES - 1
    cur = lambda c: (lambda bb, i: (bb, i, c))
    prv = lambda c: (lambda bb, i: (bb, jnp.maximum(i * r8 - 1, 0), c))
    nxt = lambda c: (lambda bb, i: (bb, jnp.minimum((i + 1) * r8, last8), c))
    w = RW_WIDTH
    return pl.pallas_call(
        functools.partial(_even_prep_body, nctx_t, nt),
        grid=(b, nt),
        in_specs=[
            pl.BlockSpec((None, tm, 2048), cur(0)),
            pl.BlockSpec((None, SUBLANES, 2048), prv(0)),
            pl.BlockSpec((None, SUBLANES, 2048), nxt(0)),
            pl.BlockSpec((None, tm, 1024), cur(2)),
            pl.BlockSpec((None, SUBLANES, 1024), prv(2)),
            pl.BlockSpec((None, SUBLANES, 1024), nxt(2)),
            _const_spec(conv.shape), _const_spec(gate_b.shape), _const_spec(mu.shape),
            _const_spec(w0.shape), _const_spec(w2cat.shape), _const_spec(a0.shape),
            _const_spec(a2cat.shape), _const_spec(g2.shape), _const_spec(kk.shape),
            _const_spec(ka.shape), _const_spec(rk.shape), _const_spec(seg64.shape),
        ],
        out_specs=[
            pl.BlockSpec((None, tm, 1024), cur(0)),
            pl.BlockSpec((None, tm, LANES), cur(0)),
            pl.BlockSpec((None, tm, 3 * w), cur(0)),
            pl.BlockSpec((2, None, tm, 3 * w), lambda bb, i: (0, bb, i, 0)),
            pl.BlockSpec((None, tm, 2 * w), cur(0)),
        ],
        out_shape=[
            jax.ShapeDtypeStruct((b, t, 1024), F32),
            jax.ShapeDtypeStruct((b, t, LANES), F32),
            jax.ShapeDtypeStruct((b, t, 3 * w), F32),
            jax.ShapeDtypeStruct((2, b, t, 3 * w), F32),
            jax.ShapeDtypeStruct((b, t, 2 * w), F32),
        ],
        compiler_params=_params(("parallel", "parallel")),
        name="even_prep",
    )(proj, proj, proj, proj, proj, proj, conv, gate_b, mu, w0, w2cat, a0, a2cat, g2, kk, ka, rk, seg64)


def _mlstm_body(q_ref, k_ref, v_ref, gc_ref, gr_ref, o_ref, c_scr, n_scr, m_scr):
    d = pl.program_id(0)
    j = pl.program_id(1)
    n_batch, n_rows = q_ref.shape[0], q_ref.shape[1]

    @pl.when(j == 0)
    def _():
        c_scr[...] = jnp.zeros_like(c_scr)
        n_scr[...] = jnp.zeros_like(n_scr)
        m_scr[...] = jnp.zeros_like(m_scr)

    mask = _order_mask(n_rows, 1 - 2 * d, strict=False)
    mf = mask.astype(F32)
    gc = [gc_ref[bi] for bi in range(n_batch)]
    gr = [gr_ref[bi] for bi in range(n_batch)]
    b_col = [_dot(mf, x, HI) for x in gc]
    b_row = [_dot_nt(x, mf, HI) for x in gr]
    tot_col = [jnp.sum(x, axis=0, keepdims=True) for x in gc]
    units = [(bi, h) for bi in range(n_batch) for h in range(ML_HEADS)]
    idx = range(len(units))
    sls = [slice(h * ML_DIM, (h + 1) * ML_DIM) for _, h in units]
    fcol = [ML_HEADS + h for _, h in units]
    q = [_bf(q_ref[bi, :, sls[u]]) for u, (bi, _) in enumerate(units)]
    kf = [k_ref[bi, :, sls[u]] * (ML_DIM ** -0.5) for u, (bi, _) in enumerate(units)]
    k = [_bf(x) for x in kf]
    vf = [v_ref[bi, :, sls[u]] for u, (bi, _) in enumerate(units)]
    c_prev = [c_scr[u] for u in idx]
    n_prev = [n_scr[u][0:1, :] for u in idx]
    m_prev = [m_scr[u][0:1, 0:1] for u in idx]
    qk = [_dot_nt(q[u], k[u]) for u in idx]
    qc = [_dot_nt(q[u], _bf(c_prev[u])) for u in idx]
    bc = [b_col[bi][:, fcol[u]:fcol[u] + 1] for u, (bi, _) in enumerate(units)]
    b_tot = [tot_col[bi][:, fcol[u]:fcol[u] + 1] for u, (bi, _) in enumerate(units)]
    logw = [b_tot[u] - bc[u] + gc[bi][:, h:h + 1] for u, (bi, h) in enumerate(units)]
    m_new = [jnp.maximum(b_tot[u] + m_prev[u], jnp.max(logw[u], axis=0, keepdims=True)) for u in idx]
    wgt = [jnp.exp(logw[u] - m_new[u]) for u in idx]
    decay = [jnp.exp(b_tot[u] + m_prev[u] - m_new[u]) for u in idx]
    kv = [_dot_tn(_bf(vf[u] * wgt[u]), k[u]) for u in idx]
    s, m_t, w_inter = [], [], []
    for u, (bi, h) in enumerate(units):
        logd = jnp.where(mask, bc[u] - b_row[bi][fcol[u]:fcol[u] + 1, :] + gr[bi][h:h + 1, :], -1e30)
        m_inter = bc[u] + m_prev[u]
        m_t.append(jnp.maximum(m_inter, jnp.max(logd, axis=-1, keepdims=True)))
        s.append(qk[u] * jnp.exp(logd - m_t[u]))
        w_inter.append(jnp.exp(m_inter - m_t[u]))
    sv = [_dot(_bf(s[u]), _bf(vf[u])) for u in idx]
    for u, (bi, _) in enumerate(units):
        num = sv[u] + w_inter[u] * qc[u]
        den = (jnp.sum(s[u], axis=-1, keepdims=True)
               + w_inter[u] * jnp.sum(q_ref[bi, :, sls[u]] * n_prev[u], axis=-1, keepdims=True))
        o_ref[bi, :, sls[u]] = num / jnp.maximum(jnp.abs(den), jnp.exp(-m_t[u]))
    for u in idx:
        c_scr[u] = decay[u] * c_prev[u] + kv[u]
        n_scr[u] = jnp.broadcast_to(decay[u] * n_prev[u] + jnp.sum(wgt[u] * kf[u], axis=0, keepdims=True),
                                    (SUBLANES, ML_DIM))
        m_scr[u] = jnp.broadcast_to(m_new[u], (SUBLANES, LANES))


def _mlstm(qk, proj, gcol, grow, n_ctx):
    b, t, _ = qk.shape
    n_rows = ML_CHUNK
    nc, nctx = t // n_rows, n_ctx // n_rows
    ci = lambda d, j: _chunk_index(d, j, nctx, nc)
    return pl.pallas_call(
        _mlstm_body,
        grid=(2, nc),
        in_specs=[
            pl.BlockSpec((b, n_rows, ML_WIDTH), lambda d, j: (0, ci(d, j), 0)),
            pl.BlockSpec((b, n_rows, ML_WIDTH), lambda d, j: (0, ci(d, j), 1)),
            pl.BlockSpec((b, n_rows, ML_WIDTH), lambda d, j: (0, ci(d, j), 6)),
            pl.BlockSpec((None, b, n_rows, 2 * ML_HEADS), lambda d, j: (d, 0, ci(d, j), 0)),
            pl.BlockSpec((None, b, 2 * ML_HEADS, n_rows), lambda d, j: (d, 0, 0, ci(d, j))),
        ],
        out_specs=pl.BlockSpec((None, b, n_rows, ML_WIDTH), lambda d, j: (d, 0, ci(d, j), 0)),
        out_shape=jax.ShapeDtypeStruct((2, b, t, ML_WIDTH), F32),
        scratch_shapes=[
            pltpu.VMEM((b * ML_HEADS, ML_DIM, ML_DIM), F32),
            pltpu.VMEM((b * ML_HEADS, SUBLANES, ML_DIM), F32),
            pltpu.VMEM((b * ML_HEADS, SUBLANES, LANES), F32),
        ],
        compiler_params=_params(("parallel", "arbitrary")),
        name="mlstm",
    )(qk, qk, proj, gcol, grow)


def _rwkv_body(r_ref, v_ref, a_ref, lw_ref, k_ref, b_ref, o_ref, h_scr):
    d = pl.program_id(0)
    j = pl.program_id(1)
    n_rows = r_ref.shape[1]
    two = 2 * n_rows
    shift = n_rows.bit_length() - 1

    @pl.when(j == 0)
    def _():
        h_scr[...] = jnp.zeros_like(h_scr)

    sgn = 1 - 2 * d
    cum = _order_mask(n_rows, sgn, strict=False).astype(F32)
    row = lax.broadcasted_iota(jnp.int32, (two, two), 0)
    col = lax.broadcasted_iota(jnp.int32, (two, two), 1)
    same = jnp.right_shift(row, shift) == jnp.right_shift(col, shift)
    diff = (jnp.bitwise_and(row, n_rows - 1) - jnp.bitwise_and(col, n_rows - 1)) * sgn
    m_strict = jnp.logical_and(same, diff > 0)
    m_incl = jnp.logical_and(same, diff >= 0)
    eye = (row == col).astype(F32)
    head0 = lax.broadcasted_iota(jnp.int32, (1, LANES), 1) < RW_HEAD_DIM

    def stack(x):
        return jnp.concatenate([jnp.where(head0, x, 0.0), jnp.where(head0, 0.0, x)], axis=0)

    def fold(x):
        return x[:n_rows] + x[n_rows:]

    n_batch = r_ref.shape[0]
    groups = [(bi, slice(p * LANES, (p + 1) * LANES)) for bi in range(n_batch) for p in range(RW_WIDTH // LANES)]
    x_s, r_s, v_s, bh_s, kh_s, bi_s, ki_s, e_tot = [], [], [], [], [], [], [], []
    for bi in range(n_batch):
        lw = lw_ref[bi]
        g = _dot(cum, lw, HI)
        g_tot = jnp.sum(lw, axis=0, keepdims=True)
        inv = jnp.exp(-g)
        to_end = jnp.exp(g_tot - g)
        at = a_ref[bi] * jnp.exp(g - lw)
        rt = r_ref[bi] * jnp.exp(g)
        b_in, k_in, vv = b_ref[bi], k_ref[bi], v_ref[bi]
        for gb, sl in groups:
            if gb != bi:
                continue
            x_s.append(stack(at[:, sl]))
            r_s.append(stack(rt[:, sl]))
            v_s.append(stack(vv[:, sl]))
            bh_s.append(stack((b_in * to_end)[:, sl]))
            kh_s.append(stack((k_in * to_end)[:, sl]))
            bi_s.append(stack((b_in * inv)[:, sl]))
            ki_s.append(stack((k_in * inv)[:, sl]))
            e_tot.append(jnp.exp(g_tot)[:, sl])
    big = [_mm_nt(jnp.concatenate([x_s[p], r_s[p]], axis=0), jnp.concatenate([bi_s[p], ki_s[p]], axis=0))
           for p in range(len(groups))]
    a_ab = [jnp.where(m_strict, m[:two, :two], 0.0) for m in big]
    a_rb = [jnp.where(m_incl, m[two:, :two], 0.0) for m in big]
    a_k = [jnp.concatenate([jnp.where(m_strict, m[:two, two:], 0.0), jnp.where(m_incl, m[two:, two:], 0.0)], axis=0)
           for m in big]
    av = [_mm(a_k[p], v_s[p]) for p in range(len(groups))]
    t_inv = [eye + m for m in a_ab]
    pw = a_ab
    for _ in range(shift - 1):
        pw = [_mm(m, m) for m in pw]
        t_inv = [t + _mm(t, m) for t, m in zip(t_inv, pw)]
    wu = [_mm(t_inv[p], jnp.concatenate([x_s[p], av[p][:two]], axis=1)) for p in range(len(groups))]
    rb = [_mm(a_rb[p], wu[p]) for p in range(len(groups))]
    upd = [_mm_tn(wu[p], bh_s[p]) for p in range(len(groups))]
    gt2 = [_mm_tn(v_s[p], kh_s[p]) for p in range(len(groups))]
    h_old = [h_scr[p] for p in range(len(groups))]
    for p, (bi, sl) in enumerate(groups):
        q_eff = fold(r_s[p] + rb[p][:, :LANES])
        y_in = fold(rb[p][:, LANES:] + av[p][two:])
        o_ref[bi, :, sl] = _mm_nt(q_eff, h_old[p]) + y_in
    for p in range(len(groups)):
        h_scr[p] = e_tot[p] * h_old[p] + _mm(h_old[p], upd[p][:LANES]) + upd[p][LANES:] + gt2[p]


def _rwkv(rws, rwd, n_ctx):
    b, t, _ = rws.shape
    n_rows = RW_CHUNK
    nc, nctx = t // n_rows, n_ctx // n_rows
    ci = lambda d, j: _chunk_index(d, j, nctx, nc)
    w = RW_WIDTH
    shared = lambda c: pl.BlockSpec((b, n_rows, w), lambda d, j: (0, ci(d, j), c))
    per_dir = lambda c: pl.BlockSpec((None, b, n_rows, w), lambda d, j: (d, 0, ci(d, j), c))
    return pl.pallas_call(
        _rwkv_body,
        grid=(2, nc),
        in_specs=[shared(0), shared(1), shared(2), per_dir(0), per_dir(1), per_dir(2)],
        out_specs=pl.BlockSpec((None, b, n_rows, w), lambda d, j: (d, 0, ci(d, j), 0)),
        out_shape=jax.ShapeDtypeStruct((2, b, t, w), F32),
        scratch_shapes=[pltpu.VMEM((b * w // LANES, LANES, LANES), F32)],
        compiler_params=_params(("parallel", "arbitrary")),
        name="rwkv7",
    )(rws, rws, rws, rwd, rwd, rwd)


def _tail(x, mix, outw_ref, w1_ref, w2_ref, ng_ref, ga_ref, shm_ref, scm_ref, gm_ref):
    x1 = x + ga_ref[...] * _rms(_dot(_bf(mix), outw_ref[...]), ng_ref[1:2, :])
    h = _rms(x1, ng_ref[2:3, :]) * (1.0 + scm_ref[...]) + shm_ref[...]
    u = jnp.maximum(_dot(_bf(h), w1_ref[...]), 0.0)
    dd = _dot(_bf(u * u), w2_ref[...])
    return x1 + gm_ref[...] * _rms(dd, ng_ref[3:4, :])


def _even_post_body(x_ref, hm_ref, mo_ref, yrw_ref, bg_ref, mln_ref, lnw_ref, lnb_ref, s128_ref, s64_ref,
                    outw_ref, w1_ref, w2_ref, ng_ref, ga_ref, shm_ref, scm_ref, gm_ref, o_ref):
    w = RW_WIDTH
    ym = hm_ref[0] + hm_ref[1]
    ym = ym * lax.rsqrt(_dot(ym * ym, s128_ref[...], HI) + EPS) * mln_ref[...] * _sigmoid(mo_ref[...])
    y = yrw_ref[0] + yrw_ref[1]
    yc = y - _dot(y, s64_ref[...], HI)
    y = yc * lax.rsqrt(_dot(yc * yc, s64_ref[...], HI) + RW_GN_EPS) * lnw_ref[...] + lnb_ref[...] + bg_ref[:, 0:w]
    mix = jnp.concatenate([ym, y * bg_ref[:, w:2 * w]], axis=-1)
    o_ref[...] = _tail(x_ref[...], mix, outw_ref, w1_ref, w2_ref, ng_ref, ga_ref, shm_ref, scm_ref, gm_ref)


def _tail_specs(seg):
    d = D_MODEL
    return [
        _const_spec((d, d)), _const_spec((d, D_FF)), _const_spec((D_FF, d)), _const_spec((4, d)),
        _mod_spec(2, seg), _mod_spec(3, seg), _mod_spec(4, seg), _mod_spec(5, seg),
    ]


def _even_post(xall, hm, proj, yrw, bg, mln, lnw, lnb, s128, s64, outw, w1, w2, ng, mods, nctx_t):
    b, t, d = xall.shape
    tm = ROW_TILE
    seg = lambda i: (i >= nctx_t).astype(jnp.int32)
    w = RW_WIDTH
    return pl.pallas_call(
        _even_post_body,
        grid=(b, t // tm),
        in_specs=[
            pl.BlockSpec((None, tm, d), lambda bb, i: (bb, i, 0)),
            pl.BlockSpec((2, None, tm, w), lambda bb, i: (0, bb, i, 0)),
            pl.BlockSpec((None, tm, w), lambda bb, i: (bb, i, 7)),
            pl.BlockSpec((2, None, tm, w), lambda bb, i: (0, bb, i, 0)),
            pl.BlockSpec((None, tm, 2 * w), lambda bb, i: (bb, i, 0)),
            _const_spec((1, w)), _const_spec((1, w)), _const_spec((1, w)),
            _const_spec((w, w)), _const_spec((w, w)),
        ] + _tail_specs(seg),
        out_specs=pl.BlockSpec((None, tm, d), lambda bb, i: (bb, i, 0)),
        out_shape=jax.ShapeDtypeStruct((b, t, d), F32),
        compiler_params=_params(("parallel", "parallel")),
        name="even_post",
    )(xall, hm, proj, yrw, bg, mln, lnw, lnb, s128, s64, outw, w1, w2, ng, mods, mods, mods, mods)


def _hgrn_body(q_ref, f_ref, i_ref, lb_ref, o_ref, s_scr):
    d = pl.program_id(0)
    j = pl.program_id(1)
    n_batch, n_rows = q_ref.shape[0], q_ref.shape[1]
    n_sub = n_rows // HG_CHUNK

    @pl.when(j == 0)
    def _():
        s_scr[...] = jnp.zeros_like(s_scr)

    mask = _order_mask(HG_CHUNK, 1 - 2 * d, strict=False)
    mf = mask.astype(F32)
    lb = lb_ref[...]
    heads = range(HG_HEADS)
    sls = [slice(h * HG_DIM, (h + 1) * HG_DIM) for h in heads]
    units = [(bi, h) for bi in range(n_batch) for h in heads]
    rows, q_in, e_tot, intra, kv = [], [], [], [], []
    for sc in range(n_sub):
        off = pl.multiple_of(jnp.where(d == 0, sc, n_sub - 1 - sc) * HG_CHUNK, HG_CHUNK)
        rows.append(pl.ds(off, HG_CHUNK))
        q_in.append([])
        e_tot.append([])
        intra.append([])
        kv.append([])
        for bi in range(n_batch):
            qa, pre, vv = q_ref[bi, rows[sc], :], f_ref[bi, rows[sc], :], _bf(i_ref[bi, rows[sc], :])
            q = qa * _sigmoid(qa) * (HG_DIM ** -0.5)
            log_f = jnp.log(lb + (1.0 - lb) * _sigmoid(pre))
            kf = (1.0 - lb) * _sigmoid(-pre)
            bc = _dot(mf, log_f, HI)
            tot = jnp.sum(log_f, axis=0, keepdims=True)
            half = 0.5 * tot
            q_mid = _bf(q * jnp.exp(bc - half))
            k_mid = _bf(kf * jnp.exp(half - bc))
            k_out = _bf(kf * jnp.exp(tot - bc))
            q_all = _bf(q * jnp.exp(bc))
            e_all = jnp.exp(tot)
            att = [jnp.where(mask, _dot_nt(q_mid[:, sl], k_mid[:, sl]), 0.0) for sl in sls]
            intra[sc] += [_dot(_bf(att[h]), vv[:, sls[h]]) for h in heads]
            kv[sc] += [_dot_tn(vv[:, sl], k_out[:, sl]) for sl in sls]
            q_in[sc] += [q_all[:, sl] for sl in sls]
            e_tot[sc] += [e_all[:, sl] for sl in sls]
    s_t = [s_scr[u] for u in range(len(units))]
    for sc in range(n_sub):
        for u, (bi, h) in enumerate(units):
            o_ref[bi, rows[sc], sls[h]] = intra[sc][u] + _dot_nt(q_in[sc][u], _bf(s_t[u]))
        s_t = [e_tot[sc][u] * s_t[u] + kv[sc][u] for u in range(len(units))]
    for u in range(len(units)):
        s_scr[u] = s_t[u]


def _hgrn(proj, lb, n_ctx):
    b, t, _ = proj.shape
    n_rows = HG_BLOCK
    nc, nctx = t // n_rows, n_ctx // n_rows
    ci = lambda d, j: _chunk_index(d, j, nctx, nc)
    w = HG_WIDTH
    return pl.pallas_call(
        _hgrn_body,
        grid=(2, nc),
        in_specs=[
            pl.BlockSpec((b, n_rows, w), lambda d, j: (0, ci(d, j), 0)),
            pl.BlockSpec((b, n_rows, w), lambda d, j: (0, ci(d, j), 1 + d)),
            pl.BlockSpec((b, n_rows, w), lambda d, j: (0, ci(d, j), 3)),
            _const_spec((1, w)),
        ],
        out_specs=pl.BlockSpec((None, b, n_rows, w), lambda d, j: (d, 0, ci(d, j), 0)),
        out_shape=jax.ShapeDtypeStruct((2, b, t, w), F32),
        scratch_shapes=[pltpu.VMEM((b * HG_HEADS, HG_DIM, HG_DIM), F32)],
        compiler_params=_params(("parallel", "arbitrary")),
        name="hgrn2",
    )(proj, proj, proj, lb)


def _mla_kv_body(c_ref, s_ref, cos_ref, sin_ref, nrm_ref, w_ref, k_ref, v_ref):
    blk = c_ref[...]
    kv = _dot(_bf(_rms(blk[:, 0:MLA_KV_RANK], nrm_ref[...])), w_ref[...])
    rope = (blk * cos_ref[...] + s_ref[...] * sin_ref[...])[:, LANES:2 * LANES]
    for h in range(MLA_HEADS):
        base = h * (MLA_NOPE + MLA_V)
        k_ref[h] = jnp.concatenate([kv[:, base:base + MLA_NOPE], rope], axis=-1).astype(BF16)
        v_ref[h] = kv[:, base + MLA_NOPE:base + MLA_NOPE + MLA_V].astype(BF16)


def _mla_kv(proj, cos, sin, nrm, wkv):
    b, t, _ = proj.shape
    tm = ROW_TILE
    return pl.pallas_call(
        _mla_kv_body,
        grid=(b, t // tm),
        in_specs=[
            pl.BlockSpec((None, tm, 256), lambda bb, i: (bb, i, 11)),
            pl.BlockSpec((None, tm, 256), lambda bb, i: (bb, i, 12)),
            pl.BlockSpec((tm, 256), lambda bb, i: (i, 0)),
            pl.BlockSpec((tm, 256), lambda bb, i: (i, 0)),
            _const_spec((1, MLA_KV_RANK)),
            _const_spec(wkv.shape),
        ],
        out_specs=[
            pl.BlockSpec((None, MLA_HEADS, tm, 256), lambda bb, i: (bb, 0, i, 0)),
            pl.BlockSpec((None, MLA_HEADS, tm, MLA_V), lambda bb, i: (bb, 0, i, 0)),
        ],
        out_shape=[
            jax.ShapeDtypeStruct((b, MLA_HEADS, t, 256), BF16),
            jax.ShapeDtypeStruct((b, MLA_HEADS, t, MLA_V), BF16),
        ],
        compiler_params=_params(("parallel", "parallel")),
        name="mla_kv",
    )(proj, proj, cos, sin, nrm, wkv)


def _mla_q_body(c_ref, cos_ref, sin_ref, nrm_ref, w1_ref, w2_ref, q_ref):
    cn = _bf(_rms(c_ref[...], nrm_ref[...]))
    qa = _dot(cn, w1_ref[...])
    qb = _dot(cn, w2_ref[...])
    scale = (MLA_NOPE + MLA_ROPE) ** -0.5
    for h in range(MLA_HEADS):
        a = qa[:, h * 256:(h + 1) * 256]
        rope = a * cos_ref[...] + qb[:, h * 256:(h + 1) * 256] * sin_ref[...]
        q = jnp.concatenate([a[:, 0:MLA_NOPE], rope[:, MLA_NOPE:]], axis=-1) * scale
        q_ref[h] = q.astype(BF16)


def _mla_q(proj, cos, sin, nrm, w1, w2, n_ctx):
    b, t, _ = proj.shape
    tm = ROW_TILE
    n = t - n_ctx
    off = n_ctx // tm
    return pl.pallas_call(
        _mla_q_body,
        grid=(b, n // tm),
        in_specs=[
            pl.BlockSpec((None, tm, 256), lambda bb, i: (bb, i + off, 10)),
            pl.BlockSpec((tm, 256), lambda bb, i: (i + off, 0)),
            pl.BlockSpec((tm, 256), lambda bb, i: (i + off, 0)),
            _const_spec((1, MLA_Q_RANK)),
            _const_spec(w1.shape), _const_spec(w2.shape),
        ],
        out_specs=pl.BlockSpec((None, MLA_HEADS, tm, 256), lambda bb, i: (bb, 0, i, 0)),
        out_shape=jax.ShapeDtypeStruct((b, MLA_HEADS, n, 256), BF16),
        compiler_params=_params(("parallel", "parallel")),
        name="mla_q",
    )(proj, cos, sin, nrm, w1, w2)


def _attn_body(q_ref, k_ref, v_ref, o_ref):
    s = _dot_nt(q_ref[...], k_ref[...])
    p = jnp.exp(s - jnp.max(s, axis=-1, keepdims=True))
    o_ref[...] = _dot(_bf(p), v_ref[...]) / jnp.sum(p, axis=-1, keepdims=True)


def _attention(q, k, v):
    b, h, n, dq = q.shape
    t = k.shape[2]
    tq = ATTN_TQ
    return pl.pallas_call(
        _attn_body,
        grid=(b, h, n // tq),
        in_specs=[
            pl.BlockSpec((None, None, tq, dq), lambda bb, hh, i: (bb, hh, i, 0)),
            pl.BlockSpec((None, None, t, dq), lambda bb, hh, i: (bb, hh, 0, 0)),
            pl.BlockSpec((None, None, t, MLA_V), lambda bb, hh, i: (bb, hh, 0, 0)),
        ],
        out_specs=pl.BlockSpec((None, tq, MLA_V), lambda bb, hh, i: (bb, i, hh)),
        out_shape=jax.ShapeDtypeStruct((b, n, h * MLA_V), F32),
        compiler_params=_params(("parallel", "parallel", "parallel")),
        name="mla_attention",
    )(q, k, v)


def _odd_post_body(x_ref, o_ref_in, hg_ref, att_ref, hgn_ref, s128_ref,
                   outw_ref, w1_ref, w2_ref, ng_ref, ga_ref, shm_ref, scm_ref, gm_ref, o_ref):
    y = o_ref_in[0] + o_ref_in[1]
    g = hg_ref[...]
    y = y * lax.rsqrt(_dot(y * y, s128_ref[...], HI) + EPS) * hgn_ref[...] * (g * _sigmoid(g))
    mix = jnp.concatenate([y, att_ref[...]], axis=-1)
    o_ref[...] = _tail(x_ref[...], mix, outw_ref, w1_ref, w2_ref, ng_ref, ga_ref, shm_ref, scm_ref, gm_ref)


def _odd_post(xall, ohg, proj, att, hgn, s128, outw, w1, w2, ng, mods, n_ctx):
    b, t, d = xall.shape
    tm = ROW_TILE
    n = t - n_ctx
    off = n_ctx // tm
    seg = lambda i: 1
    w = HG_WIDTH
    return pl.pallas_call(
        _odd_post_body,
        grid=(b, n // tm),
        in_specs=[
            pl.BlockSpec((None, tm, d), lambda bb, i: (bb, i + off, 0)),
            pl.BlockSpec((2, None, tm, w), lambda bb, i: (0, bb, i + off, 0)),
            pl.BlockSpec((None, tm, w), lambda bb, i: (bb, i + off, 4)),
            pl.BlockSpec((None, tm, w), lambda bb, i: (bb, i, 0)),
            _const_spec((1, w)), _const_spec((w, w)),
        ] + _tail_specs(seg),
        out_specs=pl.BlockSpec((None, tm, d), lambda bb, i: (bb, i, 0)),
        out_shape=jax.ShapeDtypeStruct((b, n, d), F32),
        compiler_params=_params(("parallel", "parallel")),
        name="odd_post",
    )(xall, ohg, proj, att, hgn, s128, outw, w1, w2, ng, mods, mods, mods, mods)


def _block_mean_matrix(width, group):
    idx = np.arange(width) // group
    return jnp.asarray((idx[:, None] == idx[None, :]).astype(np.float32) / group)


def _rope_tables(n, n_ctx):
    quarter = MLA_ROPE // 4
    inv = ROPE_BASE ** (-jnp.arange(quarter, dtype=F32) / quarter)
    tok = jnp.arange(n)
    ang_r = (tok // GRID_W).astype(F32)[:, None] * inv[None, :]
    ang_c = (tok % GRID_W).astype(F32)[:, None] * inv[None, :]
    cos = jnp.concatenate([jnp.cos(ang_r)] * 2 + [jnp.cos(ang_c)] * 2, axis=-1)
    sin = jnp.concatenate([-jnp.sin(ang_r), jnp.sin(ang_r), -jnp.sin(ang_c), jnp.sin(ang_c)], axis=-1)
    cos = jnp.concatenate([jnp.ones((n_ctx, MLA_ROPE), F32), cos], axis=0)
    sin = jnp.concatenate([jnp.zeros((n_ctx, MLA_ROPE), F32), sin], axis=0)
    pad = lambda a: jnp.pad(a, ((0, 0), (LANES, 256 - LANES - MLA_ROPE)))
    return pad(cos), pad(sin)


_ROPE_SWAP = np.concatenate([np.arange(16, 32), np.arange(0, 16), np.arange(48, 64), np.arange(32, 48)])


def kernel(x, c, ctx, c_ctx, ada_w, ada_b, norm_g, out_w, mlp_w1, mlp_w2, hg_lb, ev_w_in, ml_conv, ml_gate_b, ml_norm, rw_mu, rw_w0, rw_w2, rw_a0, rw_a2, rw_g2, rw_kk, rw_ka, rw_rk, rw_ln_w, rw_ln_b, od_w_in, hg_norm, mla_q_norm, mla_w_qb, mla_kv_norm, mla_w_kvb):
    b, n, d = x.shape
    n_ctx = ctx.shape[1]
    t = n_ctx + n
    assert d == D_MODEL and b + 1 <= SUBLANES
    assert n_ctx % ROW_TILE == 0 and n % ROW_TILE == 0 and n % ATTN_TQ == 0
    assert n_ctx % ML_CHUNK == 0 and n % ML_CHUNK == 0 and n_ctx % HG_BLOCK == 0 and n % HG_BLOCK == 0
    nctx_t = n_ctx // ROW_TILE
    row1 = lambda a: a.reshape(1, -1)

    c8 = jnp.concatenate([c, c_ctx[None], jnp.zeros((SUBLANES - b - 1, d), F32)], axis=0)
    mod = _ada_mod(c8, ada_w, ada_b).reshape(-1, SUBLANES, N_MOD, d)

    def layer_mods(l):
        lat = mod[l, :b]
        cm = jnp.broadcast_to(mod[l, b][None], lat.shape)
        return jnp.stack([cm, lat], axis=1)[:, :, :, None, :]

    xall = jnp.concatenate([ctx, x], axis=1)
    s128 = _block_mean_matrix(512, 128)
    s64 = _block_mean_matrix(512, 64)

    l, e = 0, 0
    mods = layer_mods(l)
    wi = ev_w_in[e]
    w_pad = jnp.concatenate([wi[:, 2064:3984], wi[:, 2048:2064], jnp.zeros((d, LANES - 16), F32), wi[:, 0:2048]], axis=1)
    proj = _inproj(xall, row1(norm_g[l, 0]), mods, _bf(w_pad), nctx_t)

    gate_b = jnp.pad(ml_gate_b[e].reshape(1, 16), ((0, 0), (0, LANES - 16)))
    zeros64 = jnp.zeros((64, RW_WIDTH), F32)
    w2cat = jnp.stack([jnp.concatenate([rw_w2[e, 0], zeros64], 0), jnp.concatenate([zeros64, rw_w2[e, 1]], 0)])
    a2cat = jnp.stack([jnp.concatenate([rw_a2[e, 0], zeros64], 0), jnp.concatenate([zeros64, rw_a2[e, 1]], 0)])
    qk, gates, rws, rwd, bg = _even_prep(
        proj, nctx_t, ml_conv[e], gate_b, row1(rw_mu[e]), rw_w0[e], w2cat, rw_a0[e], a2cat, rw_g2[e],
        row1(rw_kk[e]), row1(rw_ka[e]), row1(rw_rk[e]), s64 * RW_HEAD_DIM)

    g4 = gates[:, :, :16].reshape(b, t, 4, ML_HEADS)
    gcol = jnp.stack([g4[:, :, 0:2].reshape(b, t, 8), g4[:, :, 2:4].reshape(b, t, 8)], axis=0)
    grow = jnp.swapaxes(gcol, 2, 3)
    hm = _mlstm(qk, proj, gcol, grow, n_ctx)
    yrw = _rwkv(rws, rwd, n_ctx)
    xall = _even_post(xall, hm, proj, yrw, bg, row1(ml_norm[e]), row1(rw_ln_w[e]), row1(rw_ln_b[e]), s128, s64,
                      _bf(out_w[l]), _bf(mlp_w1[l]), _bf(mlp_w2[l]), norm_g[l], mods, nctx_t)

    l, o = 1, 0
    mods = layer_mods(l)
    lb_all = jax.nn.softmax(hg_lb.astype(F32), axis=0)
    lb_all = jnp.cumsum(lb_all, axis=0) - lb_all[0]
    wi = od_w_in[o]
    kr = wi[:, 2944:3008]
    z = lambda k: jnp.zeros((d, k), F32)
    w_pad = jnp.concatenate([wi[:, 0:2944], kr, z(64), z(LANES), kr[:, _ROPE_SWAP], z(64)], axis=1)
    proj = _inproj(xall, row1(norm_g[l, 0]), mods, _bf(w_pad), nctx_t)

    ohg = _hgrn(proj, row1(lb_all[l]), n_ctx)

    cos, sin = _rope_tables(n, n_ctx)
    k_all, v_all = _mla_kv(proj, cos, sin, row1(mla_kv_norm[o]), _bf(mla_w_kvb[o]))
    wq = mla_w_qb[o].reshape(MLA_Q_RANK, MLA_HEADS, MLA_NOPE + MLA_ROPE)
    zq = lambda k: jnp.zeros((MLA_Q_RANK, MLA_HEADS, k), F32)
    wq1 = jnp.concatenate([wq, zq(64)], axis=-1).reshape(MLA_Q_RANK, -1)
    wq2 = jnp.concatenate([zq(MLA_NOPE), wq[:, :, MLA_NOPE:][:, :, _ROPE_SWAP], zq(64)], axis=-1).reshape(MLA_Q_RANK, -1)
    q_all = _mla_q(proj, cos, sin, row1(mla_q_norm[o]), _bf(wq1), _bf(wq2), n_ctx)
    att = _attention(q_all, k_all, v_all)

    return _odd_post(xall, ohg, proj, att, row1(hg_norm[o]), s128,
                     _bf(out_w[l]), _bf(mlp_w1[l]), _bf(mlp_w2[l]), norm_g[l], mods, n_ctx)
```

```python
import functools

import numpy as np
import jax
import jax.numpy as jnp
from jax import lax
from jax.experimental import pallas as pl
from jax.experimental.pallas import tpu as pltpu

F32 = jnp.float32
BF16 = jnp.bfloat16
HI = lax.Precision.HIGHEST

D_MODEL = 1024
GRID_W = 64
D_FF = 4 * D_MODEL
N_MOD = 6
EPS = 1e-6
ML_WIDTH = 512
ML_HEADS = 4
ML_DIM = 128
RW_WIDTH = 512
RW_HEAD_DIM = 64
RW_GN_EPS = 64e-5
RW_FEAT = 1920
HG_WIDTH = 512
HG_HEADS = 4
HG_DIM = 128
MLA_HEADS = 4
MLA_NOPE = 128
MLA_ROPE = 64
MLA_V = 128
MLA_Q_RANK = 256
MLA_KV_RANK = 128
ROPE_BASE = 10000.0

LANES = 128
SUBLANES = 8
VMEM_LIMIT_BYTES = 56 * 2**20

ROW_TILE = 256
ML_CHUNK = 128
RW_CHUNK = 64
HG_BLOCK = 128
HG_CHUNK = 32
ATTN_TQ = 512
ATTN_SUB = 256

EV_PROJ_PAD = 4096
OD_PROJ_PAD = 3328


def _dot(a, b, prec=None):
    return jnp.dot(a, b, preferred_element_type=F32, precision=prec)


def _dot_nt(a, b, prec=None):
    return lax.dot_general(a, b, (((1,), (1,)), ((), ())), preferred_element_type=F32, precision=prec)


def _dot_tn(a, b, prec=None):
    return lax.dot_general(a, b, (((0,), (0,)), ((), ())), preferred_element_type=F32, precision=prec)


def _bf(x):
    return x.astype(BF16)


def _mm(a, b):
    return _dot(_bf(a), _bf(b))


def _mm_nt(a, b):
    return _dot_nt(_bf(a), _bf(b))


def _mm_tn(a, b):
    return _dot_tn(_bf(a), _bf(b))


def _split(x):
    hi = _bf(x)
    return hi, _bf(x - hi.astype(F32))


def _sel_dot(sel, x):
    hi, lo = _split(x)
    return _dot(sel, hi) + _dot(sel, lo)


def _dot_sel(x, sel):
    hi, lo = _split(x)
    return _dot(hi, sel) + _dot(lo, sel)


def _dot_sel_nt(x, sel):
    hi, lo = _split(x)
    return _dot_nt(hi, sel) + _dot_nt(lo, sel)


def _dot3(a, b):
    ah, al = _split(a)
    bh, bl = _split(b)
    return _dot(ah, bh) + _dot(ah, bl) + _dot(al, bh)


def _sigmoid(x):
    return 1.0 / (1.0 + jnp.exp(-x))


def _log_sigmoid(x):
    return jnp.minimum(x, 0.0) - jnp.log(1.0 + jnp.exp(-jnp.abs(x)))


def _softplus(x):
    return jnp.maximum(x, 0.0) + jnp.log(1.0 + jnp.exp(-jnp.abs(x)))


def _rms(x, g):
    return x * lax.rsqrt(jnp.mean(x * x, axis=-1, keepdims=True) + EPS) * g


def _order_mask(n, sgn, strict):
    row = lax.broadcasted_iota(jnp.int32, (n, n), 0)
    col = lax.broadcasted_iota(jnp.int32, (n, n), 1)
    diff = (row - col) * sgn
    return diff > 0 if strict else diff >= 0


def _chunk_index(d, j, nctx, nc):
    bwd = jnp.where(j < nctx, nctx - 1 - j, nc - 1 - (j - nctx))
    return jnp.where(d == 0, j, bwd)


def _params(sem):
    return pltpu.CompilerParams(dimension_semantics=sem, vmem_limit_bytes=VMEM_LIMIT_BYTES)


def _const_spec(shape):
    nd = len(shape)
    return pl.BlockSpec(shape, lambda *_: (0,) * nd)


def _mod_spec(k, seg_of):
    return pl.BlockSpec((None, None, None, 1, D_MODEL), lambda b, i: (b, seg_of(i), k, 0, 0))


def _ada_body(c_ref, w_ref, b_ref, o_ref):
    c = c_ref[...]
    o_ref[...] = _dot(c * _sigmoid(c), w_ref[...], HI) + b_ref[...]


def _ada_mod(c8, ada_w, ada_b):
    depth, d, n = ada_w.shape
    tn = 1024
    return pl.pallas_call(
        _ada_body,
        grid=(depth, n // tn),
        in_specs=[
            _const_spec((SUBLANES, d)),
            pl.BlockSpec((None, d, tn), lambda l, j: (l, 0, j)),
            pl.BlockSpec((None, 1, tn), lambda l, j: (l, 0, j)),
        ],
        out_specs=pl.BlockSpec((None, SUBLANES, tn), lambda l, j: (l, 0, j)),
        out_shape=jax.ShapeDtypeStruct((depth, SUBLANES, n), F32),
        compiler_params=_params(("parallel", "parallel")),
        name="ada_mod",
    )(c8, ada_w, ada_b.reshape(depth, 1, n))


def _inproj_body(x_ref, g_ref, sh_ref, sc_ref, w_ref, o_ref):
    h = _rms(x_ref[...], g_ref[...]) * (1.0 + sc_ref[...]) + sh_ref[...]
    o_ref[...] = _dot(_bf(h), w_ref[...])


def _inproj(xall, g, mods, w, nctx_t):
    b, t, d = xall.shape
    p = w.shape[1]
    tm = ROW_TILE
    seg = lambda i: (i >= nctx_t).astype(jnp.int32)
    return pl.pallas_call(
        _inproj_body,
        grid=(b, t // tm),
        in_specs=[
            pl.BlockSpec((None, tm, d), lambda bb, i: (bb, i, 0)),
            _const_spec((1, d)),
            _mod_spec(0, seg),
            _mod_spec(1, seg),
            _const_spec((d, p)),
        ],
        out_specs=pl.BlockSpec((None, tm, p), lambda bb, i: (bb, i, 0)),
        out_shape=jax.ShapeDtypeStruct((b, t, p), F32),
        compiler_params=_params(("parallel", "parallel")),
        name="inproj",
    )(xall, g, mods, mods, w)


def _even_prep_body(nctx_t, nt, rf_ref, rfp_ref, rfn_ref, qk_ref, qkp_ref, qkn_ref,
                    conv_ref, gb_ref, mu_ref, w0_ref, w2_ref, a0_ref, a2_ref, g2_ref,
                    kk_ref, ka_ref, rk_ref, seg_ref,
                    oqk_ref, og_ref, orws_ref, orwd_ref, obg_ref):
    i = pl.program_id(1)
    has_prev = jnp.logical_and(i != 0, i != nctx_t).astype(F32)
    has_next = jnp.logical_and(i != nctx_t - 1, i != nt - 1).astype(F32)
    tm = rf_ref.shape[0]
    row = lax.broadcasted_iota(jnp.int32, (tm, 1), 0)

    def neighbours(cur, prev8, next8):
        p = prev8[SUBLANES - 1:SUBLANES, :] * has_prev
        n = next8[0:1, :] * has_next
        dn = jnp.where(row == 0, p, pltpu.roll(cur, 1, 0))
        up = jnp.where(row == tm - 1, n, pltpu.roll(cur, tm - 1, 0))
        return dn, up

    qk = qk_ref[...]
    dn, up = neighbours(qk, qkp_ref[...], qkn_ref[...])
    oqk_ref[...] = dn * conv_ref[0:1, :] + qk * conv_ref[1:2, :] + up * conv_ref[2:3, :]

    gv = rf_ref[:, RW_FEAT:RW_FEAT + LANES] + gb_ref[...]
    lane = lax.broadcasted_iota(jnp.int32, (1, LANES), 1)
    is_f = jnp.logical_and(jnp.bitwise_and(jnp.right_shift(lane, 2), 1) == 1, lane < 16)
    og_ref[...] = jnp.where(is_f, _log_sigmoid(gv), gv)

    cur = rf_ref[:, 0:RW_FEAT]
    dn, up = neighbours(cur, rfp_ref[:, 0:RW_FEAT], rfn_ref[:, 0:RW_FEAT])
    rf = cur + (0.5 * (dn + up) - cur) * mu_ref[...]
    w = RW_WIDTH
    r, k, v = rf[:, 0:w], rf[:, w:2 * w], rf[:, 2 * w:3 * w]
    wl = rf[:, 3 * w:3 * w + LANES]
    al = rf[:, 3 * w + LANES:3 * w + 2 * LANES]
    gl = rf[:, 3 * w + 2 * LANES:3 * w + 3 * LANES]
    seg = seg_ref[...]
    kk = k * kk_ref[...]
    kk = kk * lax.rsqrt(_dot_sel(kk * kk, seg) + 1e-12)
    twl = jnp.tanh(wl)
    bonus = jnp.zeros_like(v)
    for d in range(2):
        w_log = -_softplus(-(w0_ref[d:d + 1, :] + _dot3(twl, w2_ref[d]))) - 0.5
        a = _sigmoid(a0_ref[d:d + 1, :] + _dot3(al, a2_ref[d]))
        kd = k * (1.0 + (a - 1.0) * ka_ref[...])
        bonus = bonus + _dot_sel(r * kd * rk_ref[...], seg) * v
        orwd_ref[d, :, 0:w] = -jnp.exp(w_log)
        orwd_ref[d, :, w:2 * w] = kd
        orwd_ref[d, :, 2 * w:3 * w] = kk * a
    orws_ref[:, 0:w] = r
    orws_ref[:, w:2 * w] = v
    orws_ref[:, 2 * w:3 * w] = -kk
    obg_ref[:, 0:w] = bonus
    obg_ref[:, w:2 * w] = _dot3(_sigmoid(gl), g2_ref[...])


def _even_prep(proj, nctx_t, conv, gate_b, mu, w0, w2cat, a0, a2cat, g2, kk, ka, rk, seg64):
    b, t, _ = proj.shape
    tm = ROW_TILE
    nt = t // tm
    r8 = tm // SUBLANES
    last8 = t // SUBLANES - 1
    cur = lambda c: (lambda bb, i: (bb, i, c))
    prv = lambda c: (lambda bb, i: (bb, jnp.maximum(i * r8 - 1, 0), c))
    nxt = lambda c: (lambda bb, i: (bb, jnp.minimum((i + 1) * r8, last8), c))
    w = RW_WIDTH
    return pl.pallas_call(
        functools.partial(_even_prep_body, nctx_t, nt),
        grid=(b, nt),
        in_specs=[
            pl.BlockSpec((None, tm, 2048), cur(0)),
            pl.BlockSpec((None, SUBLANES, 2048), prv(0)),
            pl.BlockSpec((None, SUBLANES, 2048), nxt(0)),
            pl.BlockSpec((None, tm, 1024), cur(2)),
            pl.BlockSpec((None, SUBLANES, 1024), prv(2)),
            pl.BlockSpec((None, SUBLANES, 1024), nxt(2)),
            _const_spec(conv.shape), _const_spec(gate_b.shape), _const_spec(mu.shape),
            _const_spec(w0.shape), _const_spec(w2cat.shape), _const_spec(a0.shape),
            _const_spec(a2cat.shape), _const_spec(g2.shape), _const_spec(kk.shape),
            _const_spec(ka.shape), _const_spec(rk.shape), _const_spec(seg64.shape),
        ],
        out_specs=[
            pl.BlockSpec((None, tm, 1024), cur(0)),
            pl.BlockSpec((None, tm, LANES), cur(0)),
            pl.BlockSpec((None, tm, 3 * w), cur(0)),
            pl.BlockSpec((2, None, tm, 3 * w), lambda bb, i: (0, bb, i, 0)),
            pl.BlockSpec((None, tm, 2 * w), cur(0)),
        ],
        out_shape=[
            jax.ShapeDtypeStruct((b, t, 1024), F32),
            jax.ShapeDtypeStruct((b, t, LANES), F32),
            jax.ShapeDtypeStruct((b, t, 3 * w), F32),
            jax.ShapeDtypeStruct((2, b, t, 3 * w), F32),
            jax.ShapeDtypeStruct((b, t, 2 * w), F32),
        ],
        compiler_params=_params(("parallel", "parallel")),
        name="even_prep",
    )(proj, proj, proj, proj, proj, proj, conv, gate_b, mu, w0, w2cat, a0, a2cat, g2, kk, ka, rk, seg64)


def _mlstm_body(q_ref, k_ref, v_ref, gc_ref, gr_ref, o_ref, c_scr, n_scr, m_scr):
    d = pl.program_id(0)
    j = pl.program_id(1)
    n_batch, n_rows = q_ref.shape[0], q_ref.shape[1]

    @pl.when(j == 0)
    def _():
        c_scr[...] = jnp.zeros_like(c_scr)
        n_scr[...] = jnp.zeros_like(n_scr)
        m_scr[...] = jnp.zeros_like(m_scr)

    mask = _order_mask(n_rows, 1 - 2 * d, strict=False)
    mf = mask.astype(BF16)
    gc = [gc_ref[bi] for bi in range(n_batch)]
    gr = [gr_ref[bi] for bi in range(n_batch)]
    b_col = [_sel_dot(mf, x) for x in gc]
    b_row = [_dot_sel_nt(x, mf) for x in gr]
    tot_col = [jnp.sum(x, axis=0, keepdims=True) for x in gc]
    units = [(bi, h) for bi in range(n_batch) for h in range(ML_HEADS)]
    idx = range(len(units))
    sls = [slice(h * ML_DIM, (h + 1) * ML_DIM) for _, h in units]
    fcol = [ML_HEADS + h for _, h in units]
    pick_row = lax.broadcasted_iota(jnp.int32, (2 * ML_HEADS, LANES), 0)
    pick = [(pick_row == c).astype(BF16) for c in range(2 * ML_HEADS)]
    ones_cols = jnp.ones((n_rows, LANES), BF16)
    q = [_bf(q_ref[bi, :, sls[u]]) for u, (bi, _) in enumerate(units)]
    kf = [k_ref[bi, :, sls[u]] * (ML_DIM ** -0.5) for u, (bi, _) in enumerate(units)]
    k = [_bf(x) for x in kf]
    vf = [v_ref[bi, :, sls[u]] for u, (bi, _) in enumerate(units)]
    c_prev = [c_scr[u] for u in idx]
    n_prev = [n_scr[u][0:1, :] for u in idx]
    m_prev = [m_scr[u][0:1, :] for u in idx]
    qk = [_dot_nt(q[u], k[u]) for u in idx]
    qc = [_dot_nt(q[u], jnp.concatenate([_bf(c_prev[u]), _bf(jnp.broadcast_to(n_prev[u], (LANES, ML_DIM)))], axis=0))
          for u in idx]
    ic = [_dot_sel(gc[bi], pick[h]) for bi, h in units]
    bc = [_dot_sel(b_col[bi], pick[fcol[u]]) for u, (bi, _) in enumerate(units)]
    b_tot = [tot_col[bi][:, fcol[u]:fcol[u] + 1] for u, (bi, _) in enumerate(units)]
    logw = [b_tot[u] - bc[u] + ic[u] for u in idx]
    m_new = [jnp.maximum(b_tot[u] + m_prev[u], jnp.max(logw[u], axis=0, keepdims=True)) for u in idx]
    wgt = [jnp.exp(logw[u] - m_new[u]) for u in idx]
    decay = [jnp.exp(b_tot[u] + m_prev[u] - m_new[u]) for u in idx]
    kv = [_dot_tn(_bf(vf[u] * wgt[u]), k[u]) for u in idx]
    s, m_t, w_inter = [], [], []
    for u, (bi, h) in enumerate(units):
        logd = jnp.where(mask, bc[u] - b_row[bi][fcol[u]:fcol[u] + 1, :] + gr[bi][h:h + 1, :], -1e30)
        m_inter = bc[u] + m_prev[u]
        m_t.append(jnp.maximum(m_inter, jnp.max(logd, axis=-1, keepdims=True)))
        s.append(qk[u] * jnp.exp(logd - m_t[u]))
        w_inter.append(jnp.exp(m_inter - m_t[u]))
    sv = [_dot(_bf(s[u]), jnp.concatenate([_bf(vf[u]), ones_cols], axis=1)) for u in idx]
    for u, (bi, _) in enumerate(units):
        num = sv[u][:, :ML_DIM] + w_inter[u] * qc[u][:, :ML_DIM]
        den = sv[u][:, ML_DIM:] + w_inter[u] * qc[u][:, ML_DIM:]
        o_ref[bi, :, sls[u]] = num / jnp.maximum(jnp.abs(den), jnp.exp(-m_t[u]))
    for u in idx:
        c_scr[u] = decay[u] * c_prev[u] + kv[u]
        n_scr[u] = jnp.broadcast_to(decay[u] * n_prev[u] + jnp.sum(wgt[u] * kf[u], axis=0, keepdims=True),
                                    (SUBLANES, ML_DIM))
        m_scr[u] = jnp.broadcast_to(m_new[u], (SUBLANES, LANES))


def _mlstm(qk, proj, gcol, grow, n_ctx):
    b, t, _ = qk.shape
    n_rows = ML_CHUNK
    nc, nctx = t // n_rows, n_ctx // n_rows
    ci = lambda d, j: _chunk_index(d, j, nctx, nc)
    return pl.pallas_call(
        _mlstm_body,
        grid=(2, nc),
        in_specs=[
            pl.BlockSpec((b, n_rows, ML_WIDTH), lambda d, j: (0, ci(d, j), 0)),
            pl.BlockSpec((b, n_rows, ML_WIDTH), lambda d, j: (0, ci(d, j), 1)),
            pl.BlockSpec((b, n_rows, ML_WIDTH), lambda d, j: (0, ci(d, j), 6)),
            pl.BlockSpec((None, b, n_rows, 2 * ML_HEADS), lambda d, j: (d, 0, ci(d, j), 0)),
            pl.BlockSpec((None, b, 2 * ML_HEADS, n_rows), lambda d, j: (d, 0, 0, ci(d, j))),
        ],
        out_specs=pl.BlockSpec((None, b, n_rows, ML_WIDTH), lambda d, j: (d, 0, ci(d, j), 0)),
        out_shape=jax.ShapeDtypeStruct((2, b, t, ML_WIDTH), F32),
        scratch_shapes=[
            pltpu.VMEM((b * ML_HEADS, ML_DIM, ML_DIM), F32),
            pltpu.VMEM((b * ML_HEADS, SUBLANES, ML_DIM), F32),
            pltpu.VMEM((b * ML_HEADS, SUBLANES, LANES), F32),
        ],
        compiler_params=_params(("parallel", "arbitrary")),
        name="mlstm",
    )(qk, qk, proj, gcol, grow)


def _rwkv_body(r_ref, v_ref, a_ref, lw_ref, k_ref, b_ref, o_ref, h_scr):
    d = pl.program_id(0)
    j = pl.program_id(1)
    n_rows = r_ref.shape[1]
    two = 2 * n_rows
    shift = n_rows.bit_length() - 1

    @pl.when(j == 0)
    def _():
        h_scr[...] = jnp.zeros_like(h_scr)

    sgn = 1 - 2 * d
    cum = _order_mask(n_rows, sgn, strict=False).astype(BF16)
    row = lax.broadcasted_iota(jnp.int32, (two, two), 0)
    col = lax.broadcasted_iota(jnp.int32, (two, two), 1)
    same = jnp.right_shift(row, shift) == jnp.right_shift(col, shift)
    diff = (jnp.bitwise_and(row, n_rows - 1) - jnp.bitwise_and(col, n_rows - 1)) * sgn
    m_strict = jnp.logical_and(same, diff > 0)
    m_incl = jnp.logical_and(same, diff >= 0)
    eye = (row == col).astype(F32)
    head0 = lax.broadcasted_iota(jnp.int32, (1, LANES), 1) < RW_HEAD_DIM

    def stack(x):
        return jnp.concatenate([jnp.where(head0, x, 0.0), jnp.where(head0, 0.0, x)], axis=0)

    def fold(x):
        return x[:n_rows] + x[n_rows:]

    n_batch = r_ref.shape[0]
    groups = [(bi, slice(p * LANES, (p + 1) * LANES)) for bi in range(n_batch) for p in range(RW_WIDTH // LANES)]
    x_s, r_s, v_s, bh_s, kh_s, bi_s, ki_s, e_tot = [], [], [], [], [], [], [], []
    for bi in range(n_batch):
        lw = lw_ref[bi]
        g = _sel_dot(cum, lw)
        g_tot = jnp.sum(lw, axis=0, keepdims=True)
        inv = jnp.exp(-g)
        to_end = jnp.exp(g_tot - g)
        at = a_ref[bi] * jnp.exp(g - lw)
        rt = r_ref[bi] * jnp.exp(g)
        b_in, k_in, vv = b_ref[bi], k_ref[bi], v_ref[bi]
        for gb, sl in groups:
            if gb != bi:
                continue
            x_s.append(stack(at[:, sl]))
            r_s.append(stack(rt[:, sl]))
            v_s.append(stack(vv[:, sl]))
            bh_s.append(stack((b_in * to_end)[:, sl]))
            kh_s.append(stack((k_in * to_end)[:, sl]))
            bi_s.append(stack((b_in * inv)[:, sl]))
            ki_s.append(stack((k_in * inv)[:, sl]))
            e_tot.append(jnp.exp(g_tot)[:, sl])
    big = [_mm_nt(jnp.concatenate([x_s[p], r_s[p]], axis=0), jnp.concatenate([bi_s[p], ki_s[p]], axis=0))
           for p in range(len(groups))]
    a_ab = [jnp.where(m_strict, m[:two, :two], 0.0) for m in big]
    a_rb = [jnp.where(m_incl, m[two:, :two], 0.0) for m in big]
    a_k = [jnp.concatenate([jnp.where(m_strict, m[:two, two:], 0.0), jnp.where(m_incl, m[two:, two:], 0.0)], axis=0)
           for m in big]
    av = [_mm(a_k[p], v_s[p]) for p in range(len(groups))]
    t_inv = [eye + m for m in a_ab]
    pw = a_ab
    for _ in range(shift - 1):
        pw = [_mm(m, m) for m in pw]
        t_inv = [t + _mm(t, m) for t, m in zip(t_inv, pw)]
    wu = [_mm(t_inv[p], jnp.concatenate([x_s[p], av[p][:two]], axis=1)) for p in range(len(groups))]
    rb = [_mm(a_rb[p], wu[p]) for p in range(len(groups))]
    upd = [_mm_tn(wu[p], bh_s[p]) for p in range(len(groups))]
    gt2 = [_mm_tn(v_s[p], kh_s[p]) for p in range(len(groups))]
    h_old = [h_scr[p] for p in range(len(groups))]
    for p, (bi, sl) in enumerate(groups):
        q_eff = fold(r_s[p] + rb[p][:, :LANES])
        y_in = fold(rb[p][:, LANES:] + av[p][two:])
        o_ref[bi, :, sl] = _mm_nt(q_eff, h_old[p]) + y_in
    for p in range(len(groups)):
        h_scr[p] = e_tot[p] * h_old[p] + _mm(h_old[p], upd[p][:LANES]) + upd[p][LANES:] + gt2[p]


def _rwkv(rws, rwd, n_ctx):
    b, t, _ = rws.shape
    n_rows = RW_CHUNK
    nc, nctx = t // n_rows, n_ctx // n_rows
    ci = lambda d, j: _chunk_index(d, j, nctx, nc)
    w = RW_WIDTH
    shared = lambda c: pl.BlockSpec((b, n_rows, w), lambda d, j: (0, ci(d, j), c))
    per_dir = lambda c: pl.BlockSpec((None, b, n_rows, w), lambda d, j: (d, 0, ci(d, j), c))
    return pl.pallas_call(
        _rwkv_body,
        grid=(2, nc),
        in_specs=[shared(0), shared(1), shared(2), per_dir(0), per_dir(1), per_dir(2)],
        out_specs=pl.BlockSpec((None, b, n_rows, w), lambda d, j: (d, 0, ci(d, j), 0)),
        out_shape=jax.ShapeDtypeStruct((2, b, t, w), F32),
        scratch_shapes=[pltpu.VMEM((b * w // LANES, LANES, LANES), F32)],
        compiler_params=_params(("parallel", "arbitrary")),
        name="rwkv7",
    )(rws, rws, rws, rwd, rwd, rwd)


def _tail(x, mix, outw_ref, w1_ref, w2_ref, ng_ref, ga_ref, shm_ref, scm_ref, gm_ref):
    x1 = x + ga_ref[...] * _rms(_dot(_bf(mix), outw_ref[...]), ng_ref[1:2, :])
    h = _rms(x1, ng_ref[2:3, :]) * (1.0 + scm_ref[...]) + shm_ref[...]
    u = jnp.maximum(_dot(_bf(h), w1_ref[...]), 0.0)
    dd = _dot(_bf(u * u), w2_ref[...])
    return x1 + gm_ref[...] * _rms(dd, ng_ref[3:4, :])


def _even_post_body(x_ref, hm_ref, mo_ref, yrw_ref, bg_ref, mln_ref, lnw_ref, lnb_ref, s128_ref, s64_ref,
                    outw_ref, w1_ref, w2_ref, ng_ref, ga_ref, shm_ref, scm_ref, gm_ref, o_ref):
    w = RW_WIDTH
    ym = hm_ref[0] + hm_ref[1]
    ym = ym * lax.rsqrt(_dot_sel(ym * ym, s128_ref[...]) + EPS) * mln_ref[...] * _sigmoid(mo_ref[...])
    y = yrw_ref[0] + yrw_ref[1]
    yc = y - _dot_sel(y, s64_ref[...])
    y = yc * lax.rsqrt(_dot_sel(yc * yc, s64_ref[...]) + RW_GN_EPS) * lnw_ref[...] + lnb_ref[...] + bg_ref[:, 0:w]
    mix = jnp.concatenate([ym, y * bg_ref[:, w:2 * w]], axis=-1)
    o_ref[...] = _tail(x_ref[...], mix, outw_ref, w1_ref, w2_ref, ng_ref, ga_ref, shm_ref, scm_ref, gm_ref)


def _tail_specs(seg):
    d = D_MODEL
    return [
        _const_spec((d, d)), _const_spec((d, D_FF)), _const_spec((D_FF, d)), _const_spec((4, d)),
        _mod_spec(2, seg), _mod_spec(3, seg), _mod_spec(4, seg), _mod_spec(5, seg),
    ]


def _even_post(xall, hm, proj, yrw, bg, mln, lnw, lnb, s128, s64, outw, w1, w2, ng, mods, nctx_t):
    b, t, d = xall.shape
    tm = ROW_TILE
    seg = lambda i: (i >= nctx_t).astype(jnp.int32)
    w = RW_WIDTH
    return pl.pallas_call(
        _even_post_body,
        grid=(b, t // tm),
        in_specs=[
            pl.BlockSpec((None, tm, d), lambda bb, i: (bb, i, 0)),
            pl.BlockSpec((2, None, tm, w), lambda bb, i: (0, bb, i, 0)),
            pl.BlockSpec((None, tm, w), lambda bb, i: (bb, i, 7)),
            pl.BlockSpec((2, None, tm, w), lambda bb, i: (0, bb, i, 0)),
            pl.BlockSpec((None, tm, 2 * w), lambda bb, i: (bb, i, 0)),
            _const_spec((1, w)), _const_spec((1, w)), _const_spec((1, w)),
            _const_spec((w, w)), _const_spec((w, w)),
        ] + _tail_specs(seg),
        out_specs=pl.BlockSpec((None, tm, d), lambda bb, i: (bb, i, 0)),
        out_shape=jax.ShapeDtypeStruct((b, t, d), F32),
        compiler_params=_params(("parallel", "parallel")),
        name="even_post",
    )(xall, hm, proj, yrw, bg, mln, lnw, lnb, s128, s64, outw, w1, w2, ng, mods, mods, mods, mods)


def _hgrn_body(q_ref, f_ref, i_ref, lb_ref, o_ref, s_scr):
    d = pl.program_id(0)
    j = pl.program_id(1)
    n_batch, n_rows = q_ref.shape[0], q_ref.shape[1]
    n_sub = n_rows // HG_CHUNK

    @pl.when(j == 0)
    def _():
        s_scr[...] = jnp.zeros_like(s_scr)

    mask = _order_mask(HG_CHUNK, 1 - 2 * d, strict=False)
    mf = mask.astype(BF16)
    lb = lb_ref[...]
    heads = range(HG_HEADS)
    sls = [slice(h * HG_DIM, (h + 1) * HG_DIM) for h in heads]
    units = [(bi, h) for bi in range(n_batch) for h in heads]
    rows, q_in, e_tot, q_mids, k_mids, k_outs, vvs = [], [], [], [], [], [], []
    for sc in range(n_sub):
        off = pl.multiple_of(jnp.where(d == 0, sc, n_sub - 1 - sc) * HG_CHUNK, HG_CHUNK)
        rows.append(pl.ds(off, HG_CHUNK))
        for lst in (q_in, e_tot, q_mids, k_mids, k_outs, vvs):
            lst.append([])
        for bi in range(n_batch):
            qa, pre, vv = q_ref[bi, rows[sc], :], f_ref[bi, rows[sc], :], _bf(i_ref[bi, rows[sc], :])
            q = qa * _sigmoid(qa) * (HG_DIM ** -0.5)
            e_abs = jnp.exp(-jnp.abs(pre))
            big = 1.0 / (1.0 + e_abs)
            small = e_abs * big
            log_f = jnp.log(lb + (1.0 - lb) * jnp.where(pre >= 0.0, big, small))
            kf = (1.0 - lb) * jnp.where(pre >= 0.0, small, big)
            bc = _sel_dot(mf, log_f)
            tot = jnp.sum(log_f, axis=0, keepdims=True)
            half = 0.5 * tot
            e_half = jnp.exp(half)
            q_mid_f = q * jnp.exp(bc - half)
            k_mid_f = kf * jnp.exp(half - bc)
            q_mid = _bf(q_mid_f)
            k_mid = _bf(k_mid_f)
            k_out = _bf(k_mid_f * e_half)
            q_all = _bf(q_mid_f * e_half)
            e_all = e_half * e_half
            q_mids[sc] += [q_mid[:, sl] for sl in sls]
            k_mids[sc] += [k_mid[:, sl] for sl in sls]
            k_outs[sc] += [k_out[:, sl] for sl in sls]
            vvs[sc] += [vv[:, sl] for sl in sls]
            q_in[sc] += [q_all[:, sl] for sl in sls]
            e_tot[sc] += [e_all[:, sl] for sl in sls]
    n_units = len(units)
    att = [[_dot_nt(q_mids[sc][u], k_mids[sc][u]) for u in range(n_units)] for sc in range(n_sub)]
    kv = [[_dot_tn(vvs[sc][u], k_outs[sc][u]) for u in range(n_units)] for sc in range(n_sub)]
    intra = [[_dot(_bf(jnp.where(mask, att[sc][u], 0.0)), vvs[sc][u]) for u in range(n_units)] for sc in range(n_sub)]
    s_t = [s_scr[u] for u in range(len(units))]
    for sc in range(n_sub):
        for u, (bi, h) in enumerate(units):
            o_ref[bi, rows[sc], sls[h]] = intra[sc][u] + _dot_nt(q_in[sc][u], _bf(s_t[u]))
        s_t = [e_tot[sc][u] * s_t[u] + kv[sc][u] for u in range(len(units))]
    for u in range(len(units)):
        s_scr[u] = s_t[u]


def _hgrn(proj, lb, n_ctx):
    b, t, _ = proj.shape
    n_rows = HG_BLOCK
    nc, nctx = t // n_rows, n_ctx // n_rows
    ci = lambda d, j: _chunk_index(d, j, nctx, nc)
    w = HG_WIDTH
    return pl.pallas_call(
        _hgrn_body,
        grid=(2, nc),
        in_specs=[
            pl.BlockSpec((b, n_rows, w), lambda d, j: (0, ci(d, j), 0)),
            pl.BlockSpec((b, n_rows, w), lambda d, j: (0, ci(d, j), 1 + d)),
            pl.BlockSpec((b, n_rows, w), lambda d, j: (0, ci(d, j), 3)),
            _const_spec((1, w)),
        ],
        out_specs=pl.BlockSpec((None, b, n_rows, w), lambda d, j: (d, 0, ci(d, j), 0)),
        out_shape=jax.ShapeDtypeStruct((2, b, t, w), F32),
        scratch_shapes=[pltpu.VMEM((b * HG_HEADS, HG_DIM, HG_DIM), F32)],
        compiler_params=_params(("parallel", "arbitrary")),
        name="hgrn2",
    )(proj, proj, proj, lb)


V_ROWS = MLA_V + SUBLANES


def _mla_kv_body(c_ref, s_ref, cos_ref, sin_ref, nrm_ref, wk_ref, wvt_ref, k_ref, vt_ref):
    blk = c_ref[...]
    kvn = _bf(_rms(blk[:, 0:MLA_KV_RANK], nrm_ref[...]))
    kn = _dot(kvn, wk_ref[...])
    rope = (blk * cos_ref[...] + s_ref[...] * sin_ref[...])[:, LANES:2 * LANES]
    ones = jnp.ones((SUBLANES, blk.shape[0]), BF16)
    for h in range(MLA_HEADS):
        k_ref[h] = jnp.concatenate([kn[:, h * MLA_NOPE:(h + 1) * MLA_NOPE], rope], axis=-1).astype(BF16)
        vt = _dot_nt(wvt_ref[h], kvn)
        vt_ref[h] = jnp.concatenate([vt.astype(BF16), ones], axis=0)


def _mla_kv(proj, cos, sin, nrm, wk, wvt):
    b, t, _ = proj.shape
    tm = ROW_TILE
    return pl.pallas_call(
        _mla_kv_body,
        grid=(b, t // tm),
        in_specs=[
            pl.BlockSpec((None, tm, 256), lambda bb, i: (bb, i, 11)),
            pl.BlockSpec((None, tm, 256), lambda bb, i: (bb, i, 12)),
            pl.BlockSpec((tm, 256), lambda bb, i: (i, 0)),
            pl.BlockSpec((tm, 256), lambda bb, i: (i, 0)),
            _const_spec((1, MLA_KV_RANK)),
            _const_spec(wk.shape),
            _const_spec(wvt.shape),
        ],
        out_specs=[
            pl.BlockSpec((None, MLA_HEADS, tm, 256), lambda bb, i: (bb, 0, i, 0)),
            pl.BlockSpec((None, MLA_HEADS, V_ROWS, tm), lambda bb, i: (bb, 0, 0, i)),
        ],
        out_shape=[
            jax.ShapeDtypeStruct((b, MLA_HEADS, t, 256), BF16),
            jax.ShapeDtypeStruct((b, MLA_HEADS, V_ROWS, t), BF16),
        ],
        compiler_params=_params(("parallel", "parallel")),
        name="mla_kv",
    )(proj, proj, cos, sin, nrm, wk, wvt)


def _mla_q_body(c_ref, cos_ref, sin_ref, nrm_ref, w1_ref, w2_ref, q_ref):
    cn = _bf(_rms(c_ref[...], nrm_ref[...]))
    qa = _dot(cn, w1_ref[...])
    qb = _dot(cn, w2_ref[...])
    scale = (MLA_NOPE + MLA_ROPE) ** -0.5 * float(np.log2(np.e))
    for h in range(MLA_HEADS):
        a = qa[:, h * 256:(h + 1) * 256]
        rope = a * cos_ref[...] + qb[:, h * 256:(h + 1) * 256] * sin_ref[...]
        q = jnp.concatenate([a[:, 0:MLA_NOPE], rope[:, MLA_NOPE:]], axis=-1) * scale
        q_ref[h] = q.astype(BF16)


def _mla_q(proj, cos, sin, nrm, w1, w2, n_ctx):
    b, t, _ = proj.shape
    tm = ROW_TILE
    n = t - n_ctx
    off = n_ctx // tm
    return pl.pallas_call(
        _mla_q_body,
        grid=(b, n // tm),
        in_specs=[
            pl.BlockSpec((None, tm, 256), lambda bb, i: (bb, i + off, 10)),
            pl.BlockSpec((tm, 256), lambda bb, i: (i + off, 0)),
            pl.BlockSpec((tm, 256), lambda bb, i: (i + off, 0)),
            _const_spec((1, MLA_Q_RANK)),
            _const_spec(w1.shape), _const_spec(w2.shape),
        ],
        out_specs=pl.BlockSpec((None, MLA_HEADS, tm, 256), lambda bb, i: (bb, 0, i, 0)),
        out_shape=jax.ShapeDtypeStruct((b, MLA_HEADS, n, 256), BF16),
        compiler_params=_params(("parallel", "parallel")),
        name="mla_q",
    )(proj, cos, sin, nrm, w1, w2)


def _attn_body(q_ref, k_ref, vt_ref, o_ref):
    k = k_ref[...]
    vt = vt_ref[...]
    subs = [slice(i * ATTN_SUB, (i + 1) * ATTN_SUB) for i in range(q_ref.shape[0] // ATTN_SUB)]
    st = [_dot_nt(k, q_ref[sl, :]) for sl in subs]
    pt = [_bf(jnp.exp2(s - jnp.max(s, axis=0, keepdims=True))) for s in st]
    ot = [_dot(vt, p) for p in pt]
    for sl, o in zip(subs, ot):
        o_ref[sl, :] = (o[0:MLA_V] / o[MLA_V:MLA_V + 1]).T


def _attention(q, k, vt):
    b, h, n, dq = q.shape
    t = k.shape[2]
    tq = ATTN_TQ
    return pl.pallas_call(
        _attn_body,
        grid=(b, h, n // tq),
        in_specs=[
            pl.BlockSpec((None, None, tq, dq), lambda bb, hh, i: (bb, hh, i, 0)),
            pl.BlockSpec((None, None, t, dq), lambda bb, hh, i: (bb, hh, 0, 0)),
            pl.BlockSpec((None, None, V_ROWS, t), lambda bb, hh, i: (bb, hh, 0, 0)),
        ],
        out_specs=pl.BlockSpec((None, tq, MLA_V), lambda bb, hh, i: (bb, i, hh)),
        out_shape=jax.ShapeDtypeStruct((b, n, h * MLA_V), F32),
        compiler_params=_params(("parallel", "parallel", "parallel")),
        name="mla_attention",
    )(q, k, vt)


def _odd_post_body(x_ref, o_ref_in, hg_ref, att_ref, hgn_ref, s128_ref,
                   outw_ref, w1_ref, w2_ref, ng_ref, ga_ref, shm_ref, scm_ref, gm_ref, o_ref):
    y = o_ref_in[0] + o_ref_in[1]
    g = hg_ref[...]
    y = y * lax.rsqrt(_dot_sel(y * y, s128_ref[...]) + EPS) * hgn_ref[...] * (g * _sigmoid(g))
    mix = jnp.concatenate([y, att_ref[...]], axis=-1)
    o_ref[...] = _tail(x_ref[...], mix, outw_ref, w1_ref, w2_ref, ng_ref, ga_ref, shm_ref, scm_ref, gm_ref)


def _odd_post(xall, ohg, proj, att, hgn, s128, outw, w1, w2, ng, mods, n_ctx):
    b, t, d = xall.shape
    tm = ROW_TILE
    n = t - n_ctx
    off = n_ctx // tm
    seg = lambda i: 1
    w = HG_WIDTH
    return pl.pallas_call(
        _odd_post_body,
        grid=(b, n // tm),
        in_specs=[
            pl.BlockSpec((None, tm, d), lambda bb, i: (bb, i + off, 0)),
            pl.BlockSpec((2, None, tm, w), lambda bb, i: (0, bb, i + off, 0)),
            pl.BlockSpec((None, tm, w), lambda bb, i: (bb, i + off, 4)),
            pl.BlockSpec((None, tm, w), lambda bb, i: (bb, i, 0)),
            _const_spec((1, w)), _const_spec((w, w)),
        ] + _tail_specs(seg),
        out_specs=pl.BlockSpec((None, tm, d), lambda bb, i: (bb, i, 0)),
        out_shape=jax.ShapeDtypeStruct((b, n, d), F32),
        compiler_params=_params(("parallel", "parallel")),
        name="odd_post",
    )(xall, ohg, proj, att, hgn, s128, outw, w1, w2, ng, mods, mods, mods, mods)


def _block_mean_matrix(width, group):
    idx = np.arange(width) // group
    return jnp.asarray((idx[:, None] == idx[None, :]).astype(np.float32) / group)


def _rope_tables(n, n_ctx):
    quarter = MLA_ROPE // 4
    inv = ROPE_BASE ** (-jnp.arange(quarter, dtype=F32) / quarter)
    tok = jnp.arange(n)
    ang_r = (tok // GRID_W).astype(F32)[:, None] * inv[None, :]
    ang_c = (tok % GRID_W).astype(F32)[:, None] * inv[None, :]
    cos = jnp.concatenate([jnp.cos(ang_r)] * 2 + [jnp.cos(ang_c)] * 2, axis=-1)
    sin = jnp.concatenate([-jnp.sin(ang_r), jnp.sin(ang_r), -jnp.sin(ang_c), jnp.sin(ang_c)], axis=-1)
    cos = jnp.concatenate([jnp.ones((n_ctx, MLA_ROPE), F32), cos], axis=0)
    sin = jnp.concatenate([jnp.zeros((n_ctx, MLA_ROPE), F32), sin], axis=0)
    pad = lambda a: jnp.pad(a, ((0, 0), (LANES, 256 - LANES - MLA_ROPE)))
    return pad(cos), pad(sin)


_ROPE_SWAP = np.concatenate([np.arange(16, 32), np.arange(0, 16), np.arange(48, 64), np.arange(32, 48)])


def kernel(x, c, ctx, c_ctx, ada_w, ada_b, norm_g, out_w, mlp_w1, mlp_w2, hg_lb, ev_w_in, ml_conv, ml_gate_b, ml_norm, rw_mu, rw_w0, rw_w2, rw_a0, rw_a2, rw_g2, rw_kk, rw_ka, rw_rk, rw_ln_w, rw_ln_b, od_w_in, hg_norm, mla_q_norm, mla_w_qb, mla_kv_norm, mla_w_kvb):
    b, n, d = x.shape
    n_ctx = ctx.shape[1]
    t = n_ctx + n
    assert d == D_MODEL and b + 1 <= SUBLANES
    assert n_ctx % ROW_TILE == 0 and n % ROW_TILE == 0 and n % ATTN_TQ == 0
    assert n_ctx % ML_CHUNK == 0 and n % ML_CHUNK == 0 and n_ctx % HG_BLOCK == 0 and n % HG_BLOCK == 0
    nctx_t = n_ctx // ROW_TILE
    row1 = lambda a: a.reshape(1, -1)

    c8 = jnp.concatenate([c, c_ctx[None], jnp.zeros((SUBLANES - b - 1, d), F32)], axis=0)
    mod = _ada_mod(c8, ada_w, ada_b).reshape(-1, SUBLANES, N_MOD, d)

    def layer_mods(l):
        lat = mod[l, :b]
        cm = jnp.broadcast_to(mod[l, b][None], lat.shape)
        return jnp.stack([cm, lat], axis=1)[:, :, :, None, :]

    xall = jnp.concatenate([ctx, x], axis=1)
    s128 = _bf(_block_mean_matrix(512, 128))
    s64 = _bf(_block_mean_matrix(512, 64))

    l, e = 0, 0
    mods = layer_mods(l)
    wi = ev_w_in[e]
    w_pad = jnp.concatenate([wi[:, 2064:3984], wi[:, 2048:2064], jnp.zeros((d, LANES - 16), F32), wi[:, 0:2048]], axis=1)
    proj = _inproj(xall, row1(norm_g[l, 0]), mods, _bf(w_pad), nctx_t)

    gate_b = jnp.pad(ml_gate_b[e].reshape(1, 16), ((0, 0), (0, LANES - 16)))
    zeros64 = jnp.zeros((64, RW_WIDTH), F32)
    w2cat = jnp.stack([jnp.concatenate([rw_w2[e, 0], zeros64], 0), jnp.concatenate([zeros64, rw_w2[e, 1]], 0)])
    a2cat = jnp.stack([jnp.concatenate([rw_a2[e, 0], zeros64], 0), jnp.concatenate([zeros64, rw_a2[e, 1]], 0)])
    qk, gates, rws, rwd, bg = _even_prep(
        proj, nctx_t, ml_conv[e], gate_b, row1(rw_mu[e]), rw_w0[e], w2cat, rw_a0[e], a2cat, rw_g2[e],
        row1(rw_kk[e]), row1(rw_ka[e]), row1(rw_rk[e]), s64 * RW_HEAD_DIM)

    g4 = gates[:, :, :16].reshape(b, t, 4, ML_HEADS)
    gcol = jnp.stack([g4[:, :, 0:2].reshape(b, t, 8), g4[:, :, 2:4].reshape(b, t, 8)], axis=0)
    grow = jnp.swapaxes(gcol, 2, 3)
    hm = _mlstm(qk, proj, gcol, grow, n_ctx)
    yrw = _rwkv(rws, rwd, n_ctx)
    xall = _even_post(xall, hm, proj, yrw, bg, row1(ml_norm[e]), row1(rw_ln_w[e]), row1(rw_ln_b[e]), s128, s64,
                      _bf(out_w[l]), _bf(mlp_w1[l]), _bf(mlp_w2[l]), norm_g[l], mods, nctx_t)

    l, o = 1, 0
    mods = layer_mods(l)
    lb_all = jax.nn.softmax(hg_lb.astype(F32), axis=0)
    lb_all = jnp.cumsum(lb_all, axis=0) - lb_all[0]
    wi = od_w_in[o]
    kr = wi[:, 2944:3008]
    z = lambda k: jnp.zeros((d, k), F32)
    w_pad = jnp.concatenate([wi[:, 0:2944], kr, z(64), z(LANES), kr[:, _ROPE_SWAP], z(64)], axis=1)
    proj = _inproj(xall, row1(norm_g[l, 0]), mods, _bf(w_pad), nctx_t)

    ohg = _hgrn(proj, row1(lb_all[l]), n_ctx)

    cos, sin = _rope_tables(n, n_ctx)
    wkv = mla_w_kvb[o].reshape(MLA_KV_RANK, MLA_HEADS, MLA_NOPE + MLA_V)
    wk = wkv[:, :, :MLA_NOPE].reshape(MLA_KV_RANK, MLA_HEADS * MLA_NOPE)
    wvt = jnp.transpose(wkv[:, :, MLA_NOPE:], (1, 2, 0))
    k_all, v_all = _mla_kv(proj, cos, sin, row1(mla_kv_norm[o]), _bf(wk), _bf(wvt))
    wq = mla_w_qb[o].reshape(MLA_Q_RANK, MLA_HEADS, MLA_NOPE + MLA_ROPE)
    zq = lambda k: jnp.zeros((MLA_Q_RANK, MLA_HEADS, k), F32)
    wq1 = jnp.concatenate([wq, zq(64)], axis=-1).reshape(MLA_Q_RANK, -1)
    wq2 = jnp.concatenate([zq(MLA_NOPE), wq[:, :, MLA_NOPE:][:, :, _ROPE_SWAP], zq(64)], axis=-1).reshape(MLA_Q_RANK, -1)
    q_all = _mla_q(proj, cos, sin, row1(mla_q_norm[o]), _bf(wq1), _bf(wq2), n_ctx)
    att = _attention(q_all, k_all, v_all)

    return _odd_post(xall, ohg, proj, att, row1(hg_norm[o]), s128,
                     _bf(out_w[l]), _bf(mlp_w1[l]), _bf(mlp_w2[l]), norm_g[l], mods, n_ctx)
```

```python
import functools

import numpy as np
import jax
import jax.numpy as jnp
from jax import lax
from jax.experimental import pallas as pl
from jax.experimental.pallas import tpu as pltpu

F32 = jnp.float32
BF16 = jnp.bfloat16
HI = lax.Precision.HIGHEST

D_MODEL = 1024
GRID_W = 64
D_FF = 4 * D_MODEL
N_MOD = 6
EPS = 1e-6
ML_WIDTH = 512
ML_HEADS = 4
ML_DIM = 128
RW_WIDTH = 512
RW_HEAD_DIM = 64
RW_GN_EPS = 64e-5
RW_FEAT = 1920
HG_WIDTH = 512
HG_HEADS = 4
HG_DIM = 128
MLA_HEADS = 4
MLA_NOPE = 128
MLA_ROPE = 64
MLA_V = 128
MLA_Q_RANK = 256
MLA_KV_RANK = 128
ROPE_BASE = 10000.0

LANES = 128
SUBLANES = 8
HALO = 16
VMEM_LIMIT_BYTES = 56 * 2**20

ROW_TILE = 256
LAT_TILE = 512
FF_CHUNK = 1024
ML_CHUNK = 128
RW_CHUNK = 64
HG_BLOCK = 128
HG_CHUNK = 32
ATTN_TQ = 512
ATTN_SUB = 256

EV_PROJ_PAD = 4096
OD_PROJ_PAD = 3328


def _dot(a, b, prec=None):
    return jnp.dot(a, b, preferred_element_type=F32, precision=prec)


def _dot_nt(a, b, prec=None):
    return lax.dot_general(a, b, (((1,), (1,)), ((), ())), preferred_element_type=F32, precision=prec)


def _dot_tn(a, b, prec=None):
    return lax.dot_general(a, b, (((0,), (0,)), ((), ())), preferred_element_type=F32, precision=prec)


def _bf(x):
    return x.astype(BF16)


def _mm(a, b):
    return _dot(_bf(a), _bf(b))


def _mm_nt(a, b):
    return _dot_nt(_bf(a), _bf(b))


def _mm_tn(a, b):
    return _dot_tn(_bf(a), _bf(b))


def _split(x):
    hi = _bf(x)
    return hi, _bf(x - hi.astype(F32))


def _sel_dot(sel, x):
    hi, lo = _split(x)
    return _dot(sel, hi) + _dot(sel, lo)


def _dot_sel(x, sel):
    hi, lo = _split(x)
    return _dot(hi, sel) + _dot(lo, sel)


def _dot_sel_nt(x, sel):
    hi, lo = _split(x)
    return _dot_nt(hi, sel) + _dot_nt(lo, sel)


def _dot3(a, b):
    ah, al = _split(a)
    bh, bl = _split(b)
    return _dot(ah, bh) + _dot(ah, bl) + _dot(al, bh)


def _sigmoid(x):
    return 1.0 / (1.0 + jnp.exp(-x))


def _log_sigmoid(x):
    return jnp.minimum(x, 0.0) - jnp.log(1.0 + jnp.exp(-jnp.abs(x)))


def _softplus(x):
    return jnp.maximum(x, 0.0) + jnp.log(1.0 + jnp.exp(-jnp.abs(x)))


def _rms(x, g):
    return x * lax.rsqrt(jnp.mean(x * x, axis=-1, keepdims=True) + EPS) * g


def _order_mask(n, sgn, strict):
    row = lax.broadcasted_iota(jnp.int32, (n, n), 0)
    col = lax.broadcasted_iota(jnp.int32, (n, n), 1)
    diff = (row - col) * sgn
    return diff > 0 if strict else diff >= 0


def _chunk_index(d, j, nctx, nc):
    bwd = jnp.where(j < nctx, nctx - 1 - j, nc - 1 - (j - nctx))
    return jnp.where(d == 0, j, bwd)


def _params(sem):
    return pltpu.CompilerParams(dimension_semantics=sem, vmem_limit_bytes=VMEM_LIMIT_BYTES)


def _const_spec(shape):
    nd = len(shape)
    return pl.BlockSpec(shape, lambda *_: (0,) * nd, pipeline_mode=pl.Buffered(1))


def _mod_spec(k, seg_of):
    return pl.BlockSpec((None, None, None, 1, D_MODEL), lambda b, i: (b, seg_of(i), k, 0, 0))


def _ada_body(c_ref, w_ref, b_ref, o_ref):
    c = c_ref[...]
    o_ref[...] = _dot(c * _sigmoid(c), w_ref[...], HI) + b_ref[...]


def _ada_mod(c8, ada_w, ada_b):
    depth, d, n = ada_w.shape
    tn = 1024
    return pl.pallas_call(
        _ada_body,
        grid=(depth, n // tn),
        in_specs=[
            _const_spec((SUBLANES, d)),
            pl.BlockSpec((None, d, tn), lambda l, j: (l, 0, j)),
            pl.BlockSpec((None, 1, tn), lambda l, j: (l, 0, j)),
        ],
        out_specs=pl.BlockSpec((None, SUBLANES, tn), lambda l, j: (l, 0, j)),
        out_shape=jax.ShapeDtypeStruct((depth, SUBLANES, n), F32),
        compiler_params=_params(("parallel", "parallel")),
        name="ada_mod",
    )(c8, ada_w, ada_b.reshape(depth, 1, n))


def _inproj_body(x_ref, g_ref, sh_ref, sc_ref, w_ref, o_ref):
    h = _rms(x_ref[...], g_ref[...]) * (1.0 + sc_ref[...]) + sh_ref[...]
    o_ref[...] = _dot(_bf(h), w_ref[...]).astype(o_ref.dtype)


def _inproj(xall, g, mods, w, nctx_t):
    b, t, d = xall.shape
    p = w.shape[1]
    tm = ROW_TILE
    seg = lambda i: (i >= nctx_t).astype(jnp.int32)
    return pl.pallas_call(
        _inproj_body,
        grid=(b, t // tm),
        in_specs=[
            pl.BlockSpec((None, tm, d), lambda bb, i: (bb, i, 0)),
            _const_spec((1, d)),
            _mod_spec(0, seg),
            _mod_spec(1, seg),
            _const_spec((d, p)),
        ],
        out_specs=pl.BlockSpec((None, tm, p), lambda bb, i: (bb, i, 0)),
        out_shape=jax.ShapeDtypeStruct((b, t, p), BF16),
        compiler_params=_params(("parallel", "parallel")),
        name="inproj",
    )(xall, g, mods, mods, w)


def _even_prep_body(nctx_t, nt, rf_ref, rfp_ref, rfn_ref, qk_ref, qkp_ref, qkn_ref,
                    conv_ref, gb_ref, mu_ref, w0_ref, w2_ref, a0_ref, a2_ref, g2_ref,
                    kk_ref, ka_ref, rk_ref, seg_ref,
                    oqk_ref, og_ref, orws_ref, olw_ref, okb_ref, obg_ref):
    i = pl.program_id(1)
    has_prev = jnp.logical_and(i != 0, i != nctx_t).astype(F32)
    has_next = jnp.logical_and(i != nctx_t - 1, i != nt - 1).astype(F32)
    tm = rf_ref.shape[0]
    row = lax.broadcasted_iota(jnp.int32, (tm, 1), 0)

    def neighbours(cur, prev_rows, next_rows):
        p = prev_rows[HALO - 1:HALO, :].astype(F32) * has_prev
        n = next_rows[0:1, :].astype(F32) * has_next
        dn = jnp.where(row == 0, p, pltpu.roll(cur, 1, 0))
        up = jnp.where(row == tm - 1, n, pltpu.roll(cur, tm - 1, 0))
        return dn, up

    qk = qk_ref[...].astype(F32)
    dn, up = neighbours(qk, qkp_ref[...], qkn_ref[...])
    oqk_ref[...] = (dn * conv_ref[0:1, :] + qk * conv_ref[1:2, :] + up * conv_ref[2:3, :]).astype(oqk_ref.dtype)

    gv = rf_ref[:, RW_FEAT:RW_FEAT + LANES].astype(F32) + gb_ref[...]
    lane = lax.broadcasted_iota(jnp.int32, (1, LANES), 1)
    is_f = jnp.logical_and(jnp.bitwise_and(jnp.right_shift(lane, 2), 1) == 1, lane < 16)
    og_ref[...] = jnp.where(is_f, _log_sigmoid(gv), gv)

    cur = rf_ref[:, 0:RW_FEAT].astype(F32)
    dn, up = neighbours(cur, rfp_ref[:, 0:RW_FEAT], rfn_ref[:, 0:RW_FEAT])
    rf = cur + (0.5 * (dn + up) - cur) * mu_ref[...]
    w = RW_WIDTH
    r, k, v = rf[:, 0:w], rf[:, w:2 * w], rf[:, 2 * w:3 * w]
    wl = rf[:, 3 * w:3 * w + LANES]
    al = rf[:, 3 * w + LANES:3 * w + 2 * LANES]
    gl = rf[:, 3 * w + 2 * LANES:3 * w + 3 * LANES]
    seg = seg_ref[...]
    kk = k * kk_ref[...]
    kk = kk * lax.rsqrt(_dot_sel(kk * kk, seg) + 1e-12)
    twl = jnp.tanh(wl)
    kd_sum = jnp.zeros_like(k)
    for d in range(2):
        w_log = -_softplus(-(w0_ref[d:d + 1, :] + _dot3(twl, w2_ref[d]))) - 0.5
        a = _sigmoid(a0_ref[d:d + 1, :] + _dot3(al, a2_ref[d]))
        kd = k * (1.0 + (a - 1.0) * ka_ref[...])
        kd_sum = kd_sum + kd
        olw_ref[d] = -jnp.exp(w_log)
        okb_ref[d, :, 0:w] = kd.astype(okb_ref.dtype)
        okb_ref[d, :, w:2 * w] = (kk * a).astype(okb_ref.dtype)
    orws_ref[:, 0:w] = r.astype(orws_ref.dtype)
    orws_ref[:, w:2 * w] = v.astype(orws_ref.dtype)
    orws_ref[:, 2 * w:3 * w] = (-kk).astype(orws_ref.dtype)
    obg_ref[:, 0:w] = (_dot_sel(r * kd_sum * rk_ref[...], seg) * v).astype(obg_ref.dtype)
    obg_ref[:, w:2 * w] = _dot3(_sigmoid(gl), g2_ref[...]).astype(obg_ref.dtype)


def _even_prep(proj, nctx_t, conv, gate_b, mu, w0, w2cat, a0, a2cat, g2, kk, ka, rk, seg64):
    b, t, _ = proj.shape
    tm = ROW_TILE
    nt = t // tm
    rh = tm // HALO
    last_h = t // HALO - 1
    cur = lambda c: (lambda bb, i: (bb, i, c))
    prv = lambda c: (lambda bb, i: (bb, jnp.maximum(i * rh - 1, 0), c))
    nxt = lambda c: (lambda bb, i: (bb, jnp.minimum((i + 1) * rh, last_h), c))
    w = RW_WIDTH
    return pl.pallas_call(
        functools.partial(_even_prep_body, nctx_t, nt),
        grid=(b, nt),
        in_specs=[
            pl.BlockSpec((None, tm, 2048), cur(0)),
            pl.BlockSpec((None, HALO, 2048), prv(0)),
            pl.BlockSpec((None, HALO, 2048), nxt(0)),
            pl.BlockSpec((None, tm, 1024), cur(2)),
            pl.BlockSpec((None, HALO, 1024), prv(2)),
            pl.BlockSpec((None, HALO, 1024), nxt(2)),
            _const_spec(conv.shape), _const_spec(gate_b.shape), _const_spec(mu.shape),
            _const_spec(w0.shape), _const_spec(w2cat.shape), _const_spec(a0.shape),
            _const_spec(a2cat.shape), _const_spec(g2.shape), _const_spec(kk.shape),
            _const_spec(ka.shape), _const_spec(rk.shape), _const_spec(seg64.shape),
        ],
        out_specs=[
            pl.BlockSpec((None, tm, 1024), cur(0)),
            pl.BlockSpec((None, tm, LANES), cur(0)),
            pl.BlockSpec((None, tm, 3 * w), cur(0)),
            pl.BlockSpec((2, None, tm, w), lambda bb, i: (0, bb, i, 0)),
            pl.BlockSpec((2, None, tm, 2 * w), lambda bb, i: (0, bb, i, 0)),
            pl.BlockSpec((None, tm, 2 * w), cur(0)),
        ],
        out_shape=[
            jax.ShapeDtypeStruct((b, t, 1024), BF16),
            jax.ShapeDtypeStruct((b, t, LANES), F32),
            jax.ShapeDtypeStruct((b, t, 3 * w), BF16),
            jax.ShapeDtypeStruct((2, b, t, w), F32),
            jax.ShapeDtypeStruct((2, b, t, 2 * w), BF16),
            jax.ShapeDtypeStruct((b, t, 2 * w), BF16),
        ],
        compiler_params=_params(("parallel", "parallel")),
        name="even_prep",
    )(proj, proj, proj, proj, proj, proj, conv, gate_b, mu, w0, w2cat, a0, a2cat, g2, kk, ka, rk, seg64)


def _mlstm_body(q_ref, k_ref, v_ref, gc_ref, gr_ref, o_ref, c_scr, n_scr, m_scr):
    d = pl.program_id(0)
    j = pl.program_id(1)
    n_batch, n_rows = q_ref.shape[0], q_ref.shape[1]

    @pl.when(j == 0)
    def _():
        c_scr[...] = jnp.zeros_like(c_scr)
        n_scr[...] = jnp.zeros_like(n_scr)
        m_scr[...] = jnp.zeros_like(m_scr)

    mask = _order_mask(n_rows, 1 - 2 * d, strict=False)
    mf = mask.astype(BF16)
    gc = [gc_ref[bi] for bi in range(n_batch)]
    gr = [gr_ref[bi] for bi in range(n_batch)]
    b_col = [_sel_dot(mf, x) for x in gc]
    b_row = [_dot_sel_nt(x, mf) for x in gr]
    tot_col = [jnp.sum(x, axis=0, keepdims=True) for x in gc]
    units = [(bi, h) for bi in range(n_batch) for h in range(ML_HEADS)]
    idx = range(len(units))
    sls = [slice(h * ML_DIM, (h + 1) * ML_DIM) for _, h in units]
    fcol = [ML_HEADS + h for _, h in units]
    pick_row = lax.broadcasted_iota(jnp.int32, (2 * ML_HEADS, LANES), 0)
    pick = [(pick_row == c).astype(BF16) for c in range(2 * ML_HEADS)]
    ones_cols = jnp.ones((n_rows, LANES), BF16)
    q = [_bf(q_ref[bi, :, sls[u]]) for u, (bi, _) in enumerate(units)]
    kf = [k_ref[bi, :, sls[u]].astype(F32) * (ML_DIM ** -0.5) for u, (bi, _) in enumerate(units)]
    k = [_bf(x) for x in kf]
    vf = [v_ref[bi, :, sls[u]].astype(F32) for u, (bi, _) in enumerate(units)]
    c_prev = [c_scr[u] for u in idx]
    n_prev = [n_scr[u][0:1, :] for u in idx]
    m_prev = [m_scr[u][0:1, :] for u in idx]
    qk = [_dot_nt(q[u], k[u]) for u in idx]
    qc = [_dot_nt(q[u], jnp.concatenate([_bf(c_prev[u]), _bf(jnp.broadcast_to(n_prev[u], (LANES, ML_DIM)))], axis=0))
          for u in idx]
    ic = [_dot_sel(gc[bi], pick[h]) for bi, h in units]
    bc = [_dot_sel(b_col[bi], pick[fcol[u]]) for u, (bi, _) in enumerate(units)]
    b_tot = [tot_col[bi][:, fcol[u]:fcol[u] + 1] for u, (bi, _) in enumerate(units)]
    logw = [b_tot[u] - bc[u] + ic[u] for u in idx]
    m_new = [jnp.maximum(b_tot[u] + m_prev[u], jnp.max(logw[u], axis=0, keepdims=True)) for u in idx]
    wgt = [jnp.exp(logw[u] - m_new[u]) for u in idx]
    decay = [jnp.exp(b_tot[u] + m_prev[u] - m_new[u]) for u in idx]
    kv = [_dot_tn(_bf(vf[u] * wgt[u]), k[u]) for u in idx]
    s, m_t, w_inter = [], [], []
    for u, (bi, h) in enumerate(units):
        logd = jnp.where(mask, bc[u] - b_row[bi][fcol[u]:fcol[u] + 1, :] + gr[bi][h:h + 1, :], -1e30)
        m_inter = bc[u] + m_prev[u]
        m_t.append(jnp.maximum(m_inter, jnp.max(logd, axis=-1, keepdims=True)))
        s.append(qk[u] * jnp.exp(logd - m_t[u]))
        w_inter.append(jnp.exp(m_inter - m_t[u]))
    sv = [_dot(_bf(s[u]), jnp.concatenate([_bf(vf[u]), ones_cols], axis=1)) for u in idx]
    for u, (bi, _) in enumerate(units):
        num = sv[u][:, :ML_DIM] + w_inter[u] * qc[u][:, :ML_DIM]
        den = sv[u][:, ML_DIM:] + w_inter[u] * qc[u][:, ML_DIM:]
        o_ref[bi, :, sls[u]] = num / jnp.maximum(jnp.abs(den), jnp.exp(-m_t[u]))
    for u in idx:
        c_scr[u] = decay[u] * c_prev[u] + kv[u]
        n_scr[u] = jnp.broadcast_to(decay[u] * n_prev[u] + jnp.sum(wgt[u] * kf[u], axis=0, keepdims=True),
                                    (SUBLANES, ML_DIM))
        m_scr[u] = jnp.broadcast_to(m_new[u], (SUBLANES, LANES))


def _mlstm(qk, proj, gcol, grow, n_ctx):
    b, t, _ = qk.shape
    n_rows = ML_CHUNK
    nc, nctx = t // n_rows, n_ctx // n_rows
    ci = lambda d, j: _chunk_index(d, j, nctx, nc)
    return pl.pallas_call(
        _mlstm_body,
        grid=(2, nc),
        in_specs=[
            pl.BlockSpec((b, n_rows, ML_WIDTH), lambda d, j: (0, ci(d, j), 0)),
            pl.BlockSpec((b, n_rows, ML_WIDTH), lambda d, j: (0, ci(d, j), 1)),
            pl.BlockSpec((b, n_rows, ML_WIDTH), lambda d, j: (0, ci(d, j), 6)),
            pl.BlockSpec((None, b, n_rows, 2 * ML_HEADS), lambda d, j: (d, 0, ci(d, j), 0)),
            pl.BlockSpec((None, b, 2 * ML_HEADS, n_rows), lambda d, j: (d, 0, 0, ci(d, j))),
        ],
        out_specs=pl.BlockSpec((None, b, n_rows, ML_WIDTH), lambda d, j: (d, 0, ci(d, j), 0)),
        out_shape=jax.ShapeDtypeStruct((2, b, t, ML_WIDTH), F32),
        scratch_shapes=[
            pltpu.VMEM((b * ML_HEADS, ML_DIM, ML_DIM), F32),
            pltpu.VMEM((b * ML_HEADS, SUBLANES, ML_DIM), F32),
            pltpu.VMEM((b * ML_HEADS, SUBLANES, LANES), F32),
        ],
        compiler_params=_params(("parallel", "arbitrary")),
        name="mlstm",
    )(qk, qk, proj, gcol, grow)


def _rwkv_body(r_ref, v_ref, a_ref, lw_ref, k_ref, b_ref, o_ref, h_scr):
    d = pl.program_id(0)
    j = pl.program_id(1)
    n_rows = r_ref.shape[1]
    two = 2 * n_rows
    shift = n_rows.bit_length() - 1

    @pl.when(j == 0)
    def _():
        h_scr[...] = jnp.zeros_like(h_scr)

    sgn = 1 - 2 * d
    cum = _order_mask(n_rows, sgn, strict=False).astype(BF16)
    row = lax.broadcasted_iota(jnp.int32, (two, two), 0)
    col = lax.broadcasted_iota(jnp.int32, (two, two), 1)
    same = jnp.right_shift(row, shift) == jnp.right_shift(col, shift)
    diff = (jnp.bitwise_and(row, n_rows - 1) - jnp.bitwise_and(col, n_rows - 1)) * sgn
    m_strict = jnp.logical_and(same, diff > 0)
    m_incl = jnp.logical_and(same, diff >= 0)
    eye = (row == col).astype(F32)
    head0 = lax.broadcasted_iota(jnp.int32, (1, LANES), 1) < RW_HEAD_DIM

    def stack(x):
        return jnp.concatenate([jnp.where(head0, x, 0.0), jnp.where(head0, 0.0, x)], axis=0)

    def fold(x):
        return x[:n_rows] + x[n_rows:]

    n_batch = r_ref.shape[0]
    groups = [(bi, slice(p * LANES, (p + 1) * LANES)) for bi in range(n_batch) for p in range(RW_WIDTH // LANES)]
    x_s, r_s, v_s, bh_s, kh_s, bi_s, ki_s, e_tot = [], [], [], [], [], [], [], []
    for bi in range(n_batch):
        lw = lw_ref[bi]
        g = _sel_dot(cum, lw)
        g_tot = jnp.sum(lw, axis=0, keepdims=True)
        inv = jnp.exp(-g)
        to_end = jnp.exp(g_tot - g)
        at = a_ref[bi].astype(F32) * jnp.exp(g - lw)
        rt = r_ref[bi].astype(F32) * jnp.exp(g)
        b_in, k_in, vv = b_ref[bi].astype(F32), k_ref[bi].astype(F32), v_ref[bi].astype(F32)
        for gb, sl in groups:
            if gb != bi:
                continue
            x_s.append(stack(at[:, sl]))
            r_s.append(stack(rt[:, sl]))
            v_s.append(stack(vv[:, sl]))
            bh_s.append(stack((b_in * to_end)[:, sl]))
            kh_s.append(stack((k_in * to_end)[:, sl]))
            bi_s.append(stack((b_in * inv)[:, sl]))
            ki_s.append(stack((k_in * inv)[:, sl]))
            e_tot.append(jnp.exp(g_tot)[:, sl])
    big = [_mm_nt(jnp.concatenate([x_s[p], r_s[p]], axis=0), jnp.concatenate([bi_s[p], ki_s[p]], axis=0))
           for p in range(len(groups))]
    a_ab = [jnp.where(m_strict, m[:two, :two], 0.0) for m in big]
    a_rb = [jnp.where(m_incl, m[two:, :two], 0.0) for m in big]
    a_k = [jnp.concatenate([jnp.where(m_strict, m[:two, two:], 0.0), jnp.where(m_incl, m[two:, two:], 0.0)], axis=0)
           for m in big]
    av = [_mm(a_k[p], v_s[p]) for p in range(len(groups))]
    t_inv = [eye + m for m in a_ab]
    pw = [_mm(m, m) for m in a_ab]
    for level in range(1, shift):
        last = level == shift - 1
        res = [_mm(t if last else jnp.concatenate([t, m], axis=0), m) for t, m in zip(t_inv, pw)]
        t_inv = [t + r[:two] for t, r in zip(t_inv, res)]
        if not last:
            pw = [r[two:] for r in res]
    wu = [_mm(t_inv[p], jnp.concatenate([x_s[p], av[p][:two]], axis=1)) for p in range(len(groups))]
    rb = [_mm(a_rb[p], wu[p]) for p in range(len(groups))]
    upd = [_mm_tn(wu[p], bh_s[p]) for p in range(len(groups))]
    gt2 = [_mm_tn(v_s[p], kh_s[p]) for p in range(len(groups))]
    h_old = [h_scr[p] for p in range(len(groups))]
    for p, (bi, sl) in enumerate(groups):
        q_eff = fold(r_s[p] + rb[p][:, :LANES])
        y_in = fold(rb[p][:, LANES:] + av[p][two:])
        o_ref[bi, :, sl] = _mm_nt(q_eff, h_old[p]) + y_in
    for p in range(len(groups)):
        h_scr[p] = e_tot[p] * h_old[p] + _mm(h_old[p], upd[p][:LANES]) + upd[p][LANES:] + gt2[p]


def _rwkv(rws, lw, kb, n_ctx):
    b, t, _ = rws.shape
    n_rows = RW_CHUNK
    nc, nctx = t // n_rows, n_ctx // n_rows
    ci = lambda d, j: _chunk_index(d, j, nctx, nc)
    w = RW_WIDTH
    shared = lambda c: pl.BlockSpec((b, n_rows, w), lambda d, j: (0, ci(d, j), c))
    per_dir = lambda c: pl.BlockSpec((None, b, n_rows, w), lambda d, j: (d, 0, ci(d, j), c))
    return pl.pallas_call(
        _rwkv_body,
        grid=(2, nc),
        in_specs=[shared(0), shared(1), shared(2), per_dir(0), per_dir(0), per_dir(1)],
        out_specs=pl.BlockSpec((None, b, n_rows, w), lambda d, j: (d, 0, ci(d, j), 0)),
        out_shape=jax.ShapeDtypeStruct((2, b, t, w), F32),
        scratch_shapes=[pltpu.VMEM((b * w // LANES, LANES, LANES), F32)],
        compiler_params=_params(("parallel", "arbitrary")),
        name="rwkv7",
    )(rws, rws, rws, lw, kb, kb)


def _tail(x, mix, outw_ref, w1_ref, w2_ref, ng_ref, ga_ref, shm_ref, scm_ref, gm_ref):
    x1 = x + ga_ref[...] * _rms(_dot(_bf(mix), outw_ref[...]), ng_ref[1:2, :])
    h = _bf(_rms(x1, ng_ref[2:3, :]) * (1.0 + scm_ref[...]) + shm_ref[...])
    chunks = [slice(c * FF_CHUNK, (c + 1) * FF_CHUNK) for c in range(D_FF // FF_CHUNK)]
    u = [jnp.maximum(_dot(h, w1_ref[:, cols]), 0.0) for cols in chunks]
    dd = _dot(_bf(u[0] * u[0]), w2_ref[chunks[0], :])
    for uc, cols in zip(u[1:], chunks[1:]):
        dd = dd + _dot(_bf(uc * uc), w2_ref[cols, :])
    return x1 + gm_ref[...] * _rms(dd, ng_ref[3:4, :])


def _even_post_body(x_ref, hm_ref, mo_ref, yrw_ref, bg_ref, mln_ref, lnw_ref, lnb_ref, s128_ref, s64_ref,
                    outw_ref, w1_ref, w2_ref, ng_ref, ga_ref, shm_ref, scm_ref, gm_ref, o_ref):
    w = RW_WIDTH
    ym = hm_ref[0] + hm_ref[1]
    ym = (ym * lax.rsqrt(_dot_sel(ym * ym, s128_ref[...]) + EPS) * mln_ref[...]
          * _sigmoid(mo_ref[...].astype(F32)))
    y = yrw_ref[0] + yrw_ref[1]
    yc = y - _dot_sel(y, s64_ref[...])
    y = (yc * lax.rsqrt(_dot_sel(yc * yc, s64_ref[...]) + RW_GN_EPS) * lnw_ref[...] + lnb_ref[...]
         + bg_ref[:, 0:w].astype(F32))
    mix = jnp.concatenate([ym, y * bg_ref[:, w:2 * w].astype(F32)], axis=-1)
    o_ref[...] = _tail(x_ref[...], mix, outw_ref, w1_ref, w2_ref, ng_ref, ga_ref, shm_ref, scm_ref, gm_ref)


def _tail_specs(seg):
    d = D_MODEL
    return [
        _const_spec((d, d)), _const_spec((d, D_FF)), _const_spec((D_FF, d)), _const_spec((4, d)),
        _mod_spec(2, seg), _mod_spec(3, seg), _mod_spec(4, seg), _mod_spec(5, seg),
    ]


def _even_post(xall, hm, proj, yrw, bg, mln, lnw, lnb, s128, s64, outw, w1, w2, ng, mods, nctx_t):
    b, t, d = xall.shape
    tm = ROW_TILE
    seg = lambda i: (i >= nctx_t).astype(jnp.int32)
    w = RW_WIDTH
    return pl.pallas_call(
        _even_post_body,
        grid=(b, t // tm),
        in_specs=[
            pl.BlockSpec((None, tm, d), lambda bb, i: (bb, i, 0)),
            pl.BlockSpec((2, None, tm, w), lambda bb, i: (0, bb, i, 0)),
            pl.BlockSpec((None, tm, w), lambda bb, i: (bb, i, 7)),
            pl.BlockSpec((2, None, tm, w), lambda bb, i: (0, bb, i, 0)),
            pl.BlockSpec((None, tm, 2 * w), lambda bb, i: (bb, i, 0)),
            _const_spec((1, w)), _const_spec((1, w)), _const_spec((1, w)),
            _const_spec((w, w)), _const_spec((w, w)),
        ] + _tail_specs(seg),
        out_specs=pl.BlockSpec((None, tm, d), lambda bb, i: (bb, i, 0)),
        out_shape=jax.ShapeDtypeStruct((b, t, d), F32),
        compiler_params=_params(("parallel", "parallel")),
        name="even_post",
    )(xall, hm, proj, yrw, bg, mln, lnw, lnb, s128, s64, outw, w1, w2, ng, mods, mods, mods, mods)


def _hgrn_body(q_ref, f_ref, i_ref, lb_ref, o_ref, s_scr):
    d = pl.program_id(0)
    j = pl.program_id(1)
    n_batch, n_rows = q_ref.shape[0], q_ref.shape[1]
    n_sub = n_rows // HG_CHUNK

    @pl.when(j == 0)
    def _():
        s_scr[...] = jnp.zeros_like(s_scr)

    mask = _order_mask(HG_CHUNK, 1 - 2 * d, strict=False)
    mf = mask.astype(BF16)
    lb = lb_ref[...]
    heads = range(HG_HEADS)
    sls = [slice(h * HG_DIM, (h + 1) * HG_DIM) for h in heads]
    units = [(bi, h) for bi in range(n_batch) for h in heads]
    rows, q_in, e_tot, q_mids, k_mids, k_outs, vvs = [], [], [], [], [], [], []
    for sc in range(n_sub):
        off = pl.multiple_of(jnp.where(d == 0, sc, n_sub - 1 - sc) * HG_CHUNK, HG_CHUNK)
        rows.append(pl.ds(off, HG_CHUNK))
        for lst in (q_in, e_tot, q_mids, k_mids, k_outs, vvs):
            lst.append([])
        for bi in range(n_batch):
            qa = q_ref[bi, rows[sc], :].astype(F32)
            pre = f_ref[bi, rows[sc], :].astype(F32)
            vv = i_ref[bi, rows[sc], :]
            q = qa * _sigmoid(qa) * (HG_DIM ** -0.5)
            e_abs = jnp.exp(-jnp.abs(pre))
            big = 1.0 / (1.0 + e_abs)
            small = e_abs * big
            log_f = jnp.log(lb + (1.0 - lb) * jnp.where(pre >= 0.0, big, small))
            kf = (1.0 - lb) * jnp.where(pre >= 0.0, small, big)
            bc = _sel_dot(mf, log_f)
            tot = jnp.sum(log_f, axis=0, keepdims=True)
            half = 0.5 * tot
            e_half = jnp.exp(half)
            q_mid_f = q * jnp.exp(bc - half)
            k_mid_f = kf * jnp.exp(half - bc)
            q_mid = _bf(q_mid_f)
            k_mid = _bf(k_mid_f)
            k_out = _bf(k_mid_f * e_half)
            q_all = _bf(q_mid_f * e_half)
            e_all = e_half * e_half
            q_mids[sc] += [q_mid[:, sl] for sl in sls]
            k_mids[sc] += [k_mid[:, sl] for sl in sls]
            k_outs[sc] += [k_out[:, sl] for sl in sls]
            vvs[sc] += [vv[:, sl] for sl in sls]
            q_in[sc] += [q_all[:, sl] for sl in sls]
            e_tot[sc] += [e_all[:, sl] for sl in sls]
    n_units = len(units)
    att = [[_dot_nt(q_mids[sc][u], k_mids[sc][u]) for u in range(n_units)] for sc in range(n_sub)]
    kv = [[_dot_tn(vvs[sc][u], k_outs[sc][u]) for u in range(n_units)] for sc in range(n_sub)]
    intra = [[_dot(_bf(jnp.where(mask, att[sc][u], 0.0)), vvs[sc][u]) for u in range(n_units)] for sc in range(n_sub)]
    s_t = [s_scr[u] for u in range(len(units))]
    for sc in range(n_sub):
        for u, (bi, h) in enumerate(units):
            o_ref[bi, rows[sc], sls[h]] = intra[sc][u] + _dot_nt(q_in[sc][u], _bf(s_t[u]))
        s_t = [e_tot[sc][u] * s_t[u] + kv[sc][u] for u in range(len(units))]
    for u in range(len(units)):
        s_scr[u] = s_t[u]


def _hgrn(proj, lb, n_ctx):
    b, t, _ = proj.shape
    n_rows = HG_BLOCK
    nc, nctx = t // n_rows, n_ctx // n_rows
    ci = lambda d, j: _chunk_index(d, j, nctx, nc)
    w = HG_WIDTH
    return pl.pallas_call(
        _hgrn_body,
        grid=(2, nc),
        in_specs=[
            pl.BlockSpec((b, n_rows, w), lambda d, j: (0, ci(d, j), 0)),
            pl.BlockSpec((b, n_rows, w), lambda d, j: (0, ci(d, j), 1 + d)),
            pl.BlockSpec((b, n_rows, w), lambda d, j: (0, ci(d, j), 3)),
            _const_spec((1, w)),
        ],
        out_specs=pl.BlockSpec((None, b, n_rows, w), lambda d, j: (d, 0, ci(d, j), 0)),
        out_shape=jax.ShapeDtypeStruct((2, b, t, w), F32),
        scratch_shapes=[pltpu.VMEM((b * HG_HEADS, HG_DIM, HG_DIM), F32)],
        compiler_params=_params(("parallel", "arbitrary")),
        name="hgrn2",
    )(proj, proj, proj, lb)


V_ROWS = MLA_V + SUBLANES


def _mla_kv_body(c_ref, s_ref, cos_ref, sin_ref, nrm_ref, wk_ref, wvt_ref, k_ref, vt_ref):
    blk = c_ref[...].astype(F32)
    kvn = _bf(_rms(blk[:, 0:MLA_KV_RANK], nrm_ref[...]))
    kn = _dot(kvn, wk_ref[...])
    rope = (blk * cos_ref[...] + s_ref[...].astype(F32) * sin_ref[...])[:, LANES:2 * LANES]
    ones = jnp.ones((SUBLANES, blk.shape[0]), BF16)
    for h in range(MLA_HEADS):
        k_ref[h] = jnp.concatenate([kn[:, h * MLA_NOPE:(h + 1) * MLA_NOPE], rope], axis=-1).astype(BF16)
        vt = _dot_nt(wvt_ref[h], kvn)
        vt_ref[h] = jnp.concatenate([vt.astype(BF16), ones], axis=0)


def _mla_kv(proj, cos, sin, nrm, wk, wvt):
    b, t, _ = proj.shape
    tm = ROW_TILE
    return pl.pallas_call(
        _mla_kv_body,
        grid=(b, t // tm),
        in_specs=[
            pl.BlockSpec((None, tm, 256), lambda bb, i: (bb, i, 11)),
            pl.BlockSpec((None, tm, 256), lambda bb, i: (bb, i, 12)),
            pl.BlockSpec((tm, 256), lambda bb, i: (i, 0)),
            pl.BlockSpec((tm, 256), lambda bb, i: (i, 0)),
            _const_spec((1, MLA_KV_RANK)),
            _const_spec(wk.shape),
            _const_spec(wvt.shape),
        ],
        out_specs=[
            pl.BlockSpec((None, MLA_HEADS, tm, 256), lambda bb, i: (bb, 0, i, 0)),
            pl.BlockSpec((None, MLA_HEADS, V_ROWS, tm), lambda bb, i: (bb, 0, 0, i)),
        ],
        out_shape=[
            jax.ShapeDtypeStruct((b, MLA_HEADS, t, 256), BF16),
            jax.ShapeDtypeStruct((b, MLA_HEADS, V_ROWS, t), BF16),
        ],
        compiler_params=_params(("parallel", "parallel")),
        name="mla_kv",
    )(proj, proj, cos, sin, nrm, wk, wvt)


def _mla_q_body(c_ref, cos_ref, sin_ref, nrm_ref, w1_ref, w2_ref, q_ref):
    cn = _bf(_rms(c_ref[...].astype(F32), nrm_ref[...]))
    qa = _dot(cn, w1_ref[...])
    qb = _dot(cn, w2_ref[...])
    scale = (MLA_NOPE + MLA_ROPE) ** -0.5 * float(np.log2(np.e))
    for h in range(MLA_HEADS):
        a = qa[:, h * 256:(h + 1) * 256]
        rope = a * cos_ref[...] + qb[:, h * 256:(h + 1) * 256] * sin_ref[...]
        q = jnp.concatenate([a[:, 0:MLA_NOPE], rope[:, MLA_NOPE:]], axis=-1) * scale
        q_ref[h] = q.astype(BF16)


def _mla_q(proj, cos, sin, nrm, w1, w2, n_ctx):
    b, t, _ = proj.shape
    tm = ROW_TILE
    n = t - n_ctx
    off = n_ctx // tm
    return pl.pallas_call(
        _mla_q_body,
        grid=(b, n // tm),
        in_specs=[
            pl.BlockSpec((None, tm, 256), lambda bb, i: (bb, i + off, 10)),
            pl.BlockSpec((tm, 256), lambda bb, i: (i + off, 0)),
            pl.BlockSpec((tm, 256), lambda bb, i: (i + off, 0)),
            _const_spec((1, MLA_Q_RANK)),
            _const_spec(w1.shape), _const_spec(w2.shape),
        ],
        out_specs=pl.BlockSpec((None, MLA_HEADS, tm, 256), lambda bb, i: (bb, 0, i, 0)),
        out_shape=jax.ShapeDtypeStruct((b, MLA_HEADS, n, 256), BF16),
        compiler_params=_params(("parallel", "parallel")),
        name="mla_q",
    )(proj, cos, sin, nrm, w1, w2)


def _attn_body(q_ref, k_ref, vt_ref, o_ref):
    k = k_ref[...]
    vt = vt_ref[...]
    subs = [slice(i * ATTN_SUB, (i + 1) * ATTN_SUB) for i in range(q_ref.shape[0] // ATTN_SUB)]
    st = [_dot_nt(k, q_ref[sl, :]) for sl in subs]
    pt = [_bf(jnp.exp2(s - jnp.max(s, axis=0, keepdims=True))) for s in st]
    ot = [_dot(vt, p) for p in pt]
    for sl, o in zip(subs, ot):
        o_ref[sl, :] = (o[0:MLA_V] / o[MLA_V:MLA_V + 1]).T


def _attention(q, k, vt):
    b, h, n, dq = q.shape
    t = k.shape[2]
    tq = ATTN_TQ
    return pl.pallas_call(
        _attn_body,
        grid=(b, h, n // tq),
        in_specs=[
            pl.BlockSpec((None, None, tq, dq), lambda bb, hh, i: (bb, hh, i, 0)),
            pl.BlockSpec((None, None, t, dq), lambda bb, hh, i: (bb, hh, 0, 0)),
            pl.BlockSpec((None, None, V_ROWS, t), lambda bb, hh, i: (bb, hh, 0, 0)),
        ],
        out_specs=pl.BlockSpec((None, tq, MLA_V), lambda bb, hh, i: (bb, i, hh)),
        out_shape=jax.ShapeDtypeStruct((b, n, h * MLA_V), F32),
        compiler_params=_params(("parallel", "parallel", "parallel")),
        name="mla_attention",
    )(q, k, vt)


def _odd_post_body(x_ref, o_ref_in, hg_ref, att_ref, hgn_ref, s128_ref,
                   outw_ref, w1_ref, w2_ref, ng_ref, ga_ref, shm_ref, scm_ref, gm_ref, o_ref):
    y = o_ref_in[0, 0] + o_ref_in[1, 0]
    g = hg_ref[0].astype(F32)
    y = y * lax.rsqrt(_dot_sel(y * y, s128_ref[...]) + EPS) * hgn_ref[...] * (g * _sigmoid(g))
    mix = jnp.concatenate([y, att_ref[...]], axis=-1)
    o_ref[...] = _tail(x_ref[0], mix, outw_ref, w1_ref, w2_ref, ng_ref, ga_ref, shm_ref, scm_ref, gm_ref)


def _odd_post(xall, ohg, proj, att, hgn, s128, outw, w1, w2, ng, mods, n_ctx):
    b, t, d = xall.shape
    tm = LAT_TILE
    n = t - n_ctx
    seg = lambda i: 1
    w = HG_WIDTH
    el = lambda *shape: tuple(pl.Element(s) for s in shape)
    row0 = lambda i: pl.multiple_of(n_ctx + i * tm, int(np.gcd(n_ctx, tm)))
    return pl.pallas_call(
        _odd_post_body,
        grid=(b, n // tm),
        in_specs=[
            pl.BlockSpec(el(1, tm, d), lambda bb, i: (bb, row0(i), 0)),
            pl.BlockSpec(el(2, 1, tm, w), lambda bb, i: (0, bb, row0(i), 0)),
            pl.BlockSpec(el(1, tm, w), lambda bb, i: (bb, row0(i), 4 * w)),
            pl.BlockSpec((None, tm, w), lambda bb, i: (bb, i, 0)),
            _const_spec((1, w)), _const_spec((w, w)),
        ] + _tail_specs(seg),
        out_specs=pl.BlockSpec((None, tm, d), lambda bb, i: (bb, i, 0)),
        out_shape=jax.ShapeDtypeStruct((b, n, d), F32),
        compiler_params=_params(("parallel", "parallel")),
        name="odd_post",
    )(xall, ohg, proj, att, hgn, s128, outw, w1, w2, ng, mods, mods, mods, mods)


def _block_mean_matrix(width, group):
    idx = np.arange(width) // group
    return jnp.asarray((idx[:, None] == idx[None, :]).astype(np.float32) / group)


def _rope_tables(n, n_ctx):
    quarter = MLA_ROPE // 4
    inv = ROPE_BASE ** (-jnp.arange(quarter, dtype=F32) / quarter)
    tok = jnp.arange(n)
    ang_r = (tok // GRID_W).astype(F32)[:, None] * inv[None, :]
    ang_c = (tok % GRID_W).astype(F32)[:, None] * inv[None, :]
    cos = jnp.concatenate([jnp.cos(ang_r)] * 2 + [jnp.cos(ang_c)] * 2, axis=-1)
    sin = jnp.concatenate([-jnp.sin(ang_r), jnp.sin(ang_r), -jnp.sin(ang_c), jnp.sin(ang_c)], axis=-1)
    cos = jnp.concatenate([jnp.ones((n_ctx, MLA_ROPE), F32), cos], axis=0)
    sin = jnp.concatenate([jnp.zeros((n_ctx, MLA_ROPE), F32), sin], axis=0)
    pad = lambda a: jnp.pad(a, ((0, 0), (LANES, 256 - LANES - MLA_ROPE)))
    return pad(cos), pad(sin)


_ROPE_SWAP = np.concatenate([np.arange(16, 32), np.arange(0, 16), np.arange(48, 64), np.arange(32, 48)])


def kernel(x, c, ctx, c_ctx, ada_w, ada_b, norm_g, out_w, mlp_w1, mlp_w2, hg_lb, ev_w_in, ml_conv, ml_gate_b, ml_norm, rw_mu, rw_w0, rw_w2, rw_a0, rw_a2, rw_g2, rw_kk, rw_ka, rw_rk, rw_ln_w, rw_ln_b, od_w_in, hg_norm, mla_q_norm, mla_w_qb, mla_kv_norm, mla_w_kvb):
    b, n, d = x.shape
    n_ctx = ctx.shape[1]
    t = n_ctx + n
    assert d == D_MODEL and b + 1 <= SUBLANES
    assert n_ctx % ROW_TILE == 0 and n % ROW_TILE == 0 and n % ATTN_TQ == 0
    assert n_ctx % ML_CHUNK == 0 and n % ML_CHUNK == 0 and n_ctx % HG_BLOCK == 0 and n % HG_BLOCK == 0
    nctx_t = n_ctx // ROW_TILE
    row1 = lambda a: a.reshape(1, -1)

    c8 = jnp.concatenate([c, c_ctx[None], jnp.zeros((SUBLANES - b - 1, d), F32)], axis=0)
    mod = _ada_mod(c8, ada_w, ada_b).reshape(-1, SUBLANES, N_MOD, d)

    def layer_mods(l):
        lat = mod[l, :b]
        cm = jnp.broadcast_to(mod[l, b][None], lat.shape)
        return jnp.stack([cm, lat], axis=1)[:, :, :, None, :]

    xall = jnp.concatenate([ctx, x], axis=1)
    s128 = _bf(_block_mean_matrix(512, 128))
    s64 = _bf(_block_mean_matrix(512, 64))

    l, e = 0, 0
    mods = layer_mods(l)
    wi = ev_w_in[e]
    w_pad = jnp.concatenate([wi[:, 2064:3984], wi[:, 2048:2064], jnp.zeros((d, LANES - 16), F32), wi[:, 0:2048]], axis=1)
    proj = _inproj(xall, row1(norm_g[l, 0]), mods, _bf(w_pad), nctx_t)

    gate_b = jnp.pad(ml_gate_b[e].reshape(1, 16), ((0, 0), (0, LANES - 16)))
    zeros64 = jnp.zeros((64, RW_WIDTH), F32)
    w2cat = jnp.stack([jnp.concatenate([rw_w2[e, 0], zeros64], 0), jnp.concatenate([zeros64, rw_w2[e, 1]], 0)])
    a2cat = jnp.stack([jnp.concatenate([rw_a2[e, 0], zeros64], 0), jnp.concatenate([zeros64, rw_a2[e, 1]], 0)])
    qk, gates, rws, rw_lw, rw_kb, bg = _even_prep(
        proj, nctx_t, ml_conv[e], gate_b, row1(rw_mu[e]), rw_w0[e], w2cat, rw_a0[e], a2cat, rw_g2[e],
        row1(rw_kk[e]), row1(rw_ka[e]), row1(rw_rk[e]), s64 * RW_HEAD_DIM)

    g4 = gates[:, :, :16].reshape(b, t, 4, ML_HEADS)
    gcol = jnp.stack([g4[:, :, 0:2].reshape(b, t, 8), g4[:, :, 2:4].reshape(b, t, 8)], axis=0)
    grow = jnp.swapaxes(gcol, 2, 3)
    hm = _mlstm(qk, proj, gcol, grow, n_ctx)
    yrw = _rwkv(rws, rw_lw, rw_kb, n_ctx)
    xall = _even_post(xall, hm, proj, yrw, bg, row1(ml_norm[e]), row1(rw_ln_w[e]), row1(rw_ln_b[e]), s128, s64,
                      _bf(out_w[l]), _bf(mlp_w1[l]), _bf(mlp_w2[l]), norm_g[l], mods, nctx_t)

    l, o = 1, 0
    mods = layer_mods(l)
    lb_all = jax.nn.softmax(hg_lb.astype(F32), axis=0)
    lb_all = jnp.cumsum(lb_all, axis=0) - lb_all[0]
    wi = od_w_in[o]
    kr = wi[:, 2944:3008]
    z = lambda k: jnp.zeros((d, k), F32)
    w_pad = jnp.concatenate([wi[:, 0:2944], kr, z(64), z(LANES), kr[:, _ROPE_SWAP], z(64)], axis=1)
    proj = _inproj(xall, row1(norm_g[l, 0]), mods, _bf(w_pad), nctx_t)

    ohg = _hgrn(proj, row1(lb_all[l]), n_ctx)

    cos, sin = _rope_tables(n, n_ctx)
    wkv = mla_w_kvb[o].reshape(MLA_KV_RANK, MLA_HEADS, MLA_NOPE + MLA_V)
    wk = wkv[:, :, :MLA_NOPE].reshape(MLA_KV_RANK, MLA_HEADS * MLA_NOPE)
    wvt = jnp.transpose(wkv[:, :, MLA_NOPE:], (1, 2, 0))
    k_all, v_all = _mla_kv(proj, cos, sin, row1(mla_kv_norm[o]), _bf(wk), _bf(wvt))
    wq = mla_w_qb[o].reshape(MLA_Q_RANK, MLA_HEADS, MLA_NOPE + MLA_ROPE)
    zq = lambda k: jnp.zeros((MLA_Q_RANK, MLA_HEADS, k), F32)
    wq1 = jnp.concatenate([wq, zq(64)], axis=-1).reshape(MLA_Q_RANK, -1)
    wq2 = jnp.concatenate([zq(MLA_NOPE), wq[:, :, MLA_NOPE:][:, :, _ROPE_SWAP], zq(64)], axis=-1).reshape(MLA_Q_RANK, -1)
    q_all = _mla_q(proj, cos, sin, row1(mla_q_norm[o]), _bf(wq1), _bf(wq2), n_ctx)
    att = _attention(q_all, k_all, v_all)

    return _odd_post(xall, ohg, proj, att, row1(hg_norm[o]), s128,
                     _bf(out_w[l]), _bf(mlp_w1[l]), _bf(mlp_w2[l]), norm_g[l], mods, n_ctx)
```

```python
import functools

import numpy as np
import jax
import jax.numpy as jnp
from jax import lax
from jax.experimental import pallas as pl
from jax.experimental.pallas import tpu as pltpu

F32 = jnp.float32
BF16 = jnp.bfloat16
HI = lax.Precision.HIGHEST

D_MODEL = 1024
GRID_W = 64
D_FF = 4 * D_MODEL
N_MOD = 6
EPS = 1e-6
ML_WIDTH = 512
ML_HEADS = 4
ML_DIM = 128
RW_WIDTH = 512
RW_HEAD_DIM = 64
RW_GN_EPS = 64e-5
RW_FEAT = 1920
HG_WIDTH = 512
HG_HEADS = 4
HG_DIM = 128
MLA_HEADS = 4
MLA_NOPE = 128
MLA_ROPE = 64
MLA_V = 128
MLA_Q_RANK = 256
MLA_KV_RANK = 128
ROPE_BASE = 10000.0

LANES = 128
SUBLANES = 8
HALO = 16
VMEM_LIMIT_BYTES = 56 * 2**20

ROW_TILE = 256
LAT_TILE = 512
FF_CHUNK = 1024
ML_CHUNK = 128
RW_CHUNK = 64
HG_BLOCK = 128
HG_CHUNK = 32
ATTN_TQ = 512
ATTN_SUB = 256


def _dot(a, b, prec=None):
    return jnp.dot(a, b, preferred_element_type=F32, precision=prec)


def _dot_nt(a, b, prec=None):
    return lax.dot_general(a, b, (((1,), (1,)), ((), ())), preferred_element_type=F32, precision=prec)


def _dot_tn(a, b, prec=None):
    return lax.dot_general(a, b, (((0,), (0,)), ((), ())), preferred_element_type=F32, precision=prec)


def _bf(x):
    return x.astype(BF16)


def _mm(a, b):
    return _dot(_bf(a), _bf(b))


def _mm_nt(a, b):
    return _dot_nt(_bf(a), _bf(b))


def _mm_tn(a, b):
    return _dot_tn(_bf(a), _bf(b))


def _split(x):
    hi = _bf(x)
    return hi, _bf(x - hi.astype(F32))


def _sel_dot(sel, x):
    hi, lo = _split(x)
    return _dot(sel, hi) + _dot(sel, lo)


def _dot_sel(x, sel):
    hi, lo = _split(x)
    return _dot(hi, sel) + _dot(lo, sel)


def _dot_sel_nt(x, sel):
    hi, lo = _split(x)
    return _dot_nt(hi, sel) + _dot_nt(lo, sel)


def _dot3(a, b):
    ah, al = _split(a)
    bh, bl = _split(b)
    return _dot(ah, bh) + _dot(ah, bl) + _dot(al, bh)


def _sigmoid(x):
    return 1.0 / (1.0 + jnp.exp(-x))


def _log_sigmoid(x):
    return jnp.minimum(x, 0.0) - jnp.log(1.0 + jnp.exp(-jnp.abs(x)))


def _softplus(x):
    return jnp.maximum(x, 0.0) + jnp.log(1.0 + jnp.exp(-jnp.abs(x)))


def _rms(x, g):
    return x * lax.rsqrt(jnp.mean(x * x, axis=-1, keepdims=True) + EPS) * g


def _order_mask(n, sgn, strict):
    row = lax.broadcasted_iota(jnp.int32, (n, n), 0)
    col = lax.broadcasted_iota(jnp.int32, (n, n), 1)
    diff = (row - col) * sgn
    return diff > 0 if strict else diff >= 0


def _chunk_index(d, j, nctx, nc):
    bwd = jnp.where(j < nctx, nctx - 1 - j, nc - 1 - (j - nctx))
    return jnp.where(d == 0, j, bwd)


def _params(sem):
    return pltpu.CompilerParams(dimension_semantics=sem, vmem_limit_bytes=VMEM_LIMIT_BYTES)


def _const_spec(shape):
    nd = len(shape)
    return pl.BlockSpec(shape, lambda *_: (0,) * nd, pipeline_mode=pl.Buffered(1))


def _mod_spec(k, seg_of):
    return pl.BlockSpec((None, None, None, 1, D_MODEL), lambda b, i: (b, seg_of(i), k, 0, 0))


def _ada_body(c_ref, w_ref, b_ref, o_ref):
    c = c_ref[...]
    o_ref[...] = _dot(c * _sigmoid(c), w_ref[...], HI) + b_ref[...]


def _ada_mod(c8, ada_w, ada_b):
    depth, d, n = ada_w.shape
    tn = 1024
    return pl.pallas_call(
        _ada_body,
        grid=(depth, n // tn),
        in_specs=[
            _const_spec((SUBLANES, d)),
            pl.BlockSpec((None, d, tn), lambda l, j: (l, 0, j)),
            pl.BlockSpec((None, 1, tn), lambda l, j: (l, 0, j)),
        ],
        out_specs=pl.BlockSpec((None, SUBLANES, tn), lambda l, j: (l, 0, j)),
        out_shape=jax.ShapeDtypeStruct((depth, SUBLANES, n), F32),
        compiler_params=_params(("parallel", "parallel")),
        name="ada_mod",
    )(c8, ada_w, ada_b.reshape(depth, 1, n))


EV_SHIFTED = 3072


def _even_in_body(nctx_t, nt, x_ref, xp_ref, xn_ref, g_ref, sh_ref, sc_ref, w_ref,
                  conv_ref, gb_ref, mu_ref, w0_ref, w2_ref, a0_ref, a2_ref, g2_ref,
                  kk_ref, ka_ref, rk_ref, seg_ref,
                  ovo_ref, oqk_ref, og_ref, orws_ref, olw_ref, okb_ref, obg_ref):
    i = pl.program_id(1)
    has_prev = jnp.logical_and(i != 0, i != nctx_t).astype(F32)
    has_next = jnp.logical_and(i != nctx_t - 1, i != nt - 1).astype(F32)
    tm = x_ref.shape[0]
    ext = tm + 2 * HALO
    row = lax.broadcasted_iota(jnp.int32, (tm, 1), 0)
    keep_dn = jnp.where(row == 0, has_prev, 1.0)
    keep_up = jnp.where(row == tm - 1, has_next, 1.0)

    def normed(xx):
        return _bf(_rms(xx, g_ref[...]) * (1.0 + sc_ref[...]) + sh_ref[...])

    h = normed(x_ref[...])
    ovo_ref[...] = _dot(h, w_ref[:, EV_SHIFTED:]).astype(ovo_ref.dtype)
    hcat = jnp.concatenate([normed(xp_ref[...]), h, normed(xn_ref[...])], axis=0)
    proj = _dot(hcat, w_ref[:, 0:EV_SHIFTED])

    def neighbours(cols):
        blk = proj[:, cols]
        dn = pltpu.roll(blk, 1, 0)[HALO:HALO + tm] * keep_dn
        up = pltpu.roll(blk, ext - 1, 0)[HALO:HALO + tm] * keep_up
        return blk[HALO:HALO + tm], dn, up

    qk, dn, up = neighbours(slice(2048, 3072))
    oqk_ref[...] = (dn * conv_ref[0:1, :] + qk * conv_ref[1:2, :] + up * conv_ref[2:3, :]).astype(oqk_ref.dtype)

    gv = proj[HALO:HALO + tm, RW_FEAT:RW_FEAT + LANES] + gb_ref[...]
    lane = lax.broadcasted_iota(jnp.int32, (1, LANES), 1)
    is_f = jnp.logical_and(jnp.bitwise_and(jnp.right_shift(lane, 2), 1) == 1, lane < 16)
    og_ref[...] = jnp.where(is_f, _log_sigmoid(gv), gv)

    cur, dn, up = neighbours(slice(0, RW_FEAT))
    rf = cur + (0.5 * (dn + up) - cur) * mu_ref[...]
    w = RW_WIDTH
    r, k, v = rf[:, 0:w], rf[:, w:2 * w], rf[:, 2 * w:3 * w]
    wl = rf[:, 3 * w:3 * w + LANES]
    al = rf[:, 3 * w + LANES:3 * w + 2 * LANES]
    gl = rf[:, 3 * w + 2 * LANES:3 * w + 3 * LANES]
    seg = seg_ref[...]
    kk = k * kk_ref[...]
    kk = kk * lax.rsqrt(_dot_sel(kk * kk, seg) + 1e-12)
    twl = jnp.tanh(wl)
    kd_sum = jnp.zeros_like(k)
    for d in range(2):
        w_log = -_softplus(-(w0_ref[d:d + 1, :] + _dot3(twl, w2_ref[d]))) - 0.5
        a = _sigmoid(a0_ref[d:d + 1, :] + _dot3(al, a2_ref[d]))
        kd = k * (1.0 + (a - 1.0) * ka_ref[...])
        kd_sum = kd_sum + kd
        olw_ref[d] = -jnp.exp(w_log)
        okb_ref[d, :, 0:w] = kd.astype(okb_ref.dtype)
        okb_ref[d, :, w:2 * w] = (kk * a).astype(okb_ref.dtype)
    orws_ref[:, 0:w] = r.astype(orws_ref.dtype)
    orws_ref[:, w:2 * w] = v.astype(orws_ref.dtype)
    orws_ref[:, 2 * w:3 * w] = (-kk).astype(orws_ref.dtype)
    obg_ref[:, 0:w] = (_dot_sel(r * kd_sum * rk_ref[...], seg) * v).astype(obg_ref.dtype)
    obg_ref[:, w:2 * w] = _dot3(_sigmoid(gl), g2_ref[...]).astype(obg_ref.dtype)


def _even_in(xall, g, mods, w_in, nctx_t, conv, gate_b, mu, w0, w2cat, a0, a2cat, g2, kk, ka, rk, seg64):
    b, t, d = xall.shape
    tm = ROW_TILE
    nt = t // tm
    rh = tm // HALO
    last_h = t // HALO - 1
    seg = lambda i: (i >= nctx_t).astype(jnp.int32)
    cur = lambda c: (lambda bb, i: (bb, i, c))
    w = RW_WIDTH
    return pl.pallas_call(
        functools.partial(_even_in_body, nctx_t, nt),
        grid=(b, nt),
        in_specs=[
            pl.BlockSpec((None, tm, d), cur(0)),
            pl.BlockSpec((None, HALO, d), lambda bb, i: (bb, jnp.maximum(i * rh - 1, 0), 0)),
            pl.BlockSpec((None, HALO, d), lambda bb, i: (bb, jnp.minimum((i + 1) * rh, last_h), 0)),
            _const_spec((1, d)), _mod_spec(0, seg), _mod_spec(1, seg), _const_spec(w_in.shape),
            _const_spec(conv.shape), _const_spec(gate_b.shape), _const_spec(mu.shape),
            _const_spec(w0.shape), _const_spec(w2cat.shape), _const_spec(a0.shape),
            _const_spec(a2cat.shape), _const_spec(g2.shape), _const_spec(kk.shape),
            _const_spec(ka.shape), _const_spec(rk.shape), _const_spec(seg64.shape),
        ],
        out_specs=[
            pl.BlockSpec((None, tm, 1024), cur(0)),
            pl.BlockSpec((None, tm, 1024), cur(0)),
            pl.BlockSpec((None, tm, LANES), cur(0)),
            pl.BlockSpec((None, tm, 3 * w), cur(0)),
            pl.BlockSpec((2, None, tm, w), lambda bb, i: (0, bb, i, 0)),
            pl.BlockSpec((2, None, tm, 2 * w), lambda bb, i: (0, bb, i, 0)),
            pl.BlockSpec((None, tm, 2 * w), cur(0)),
        ],
        out_shape=[
            jax.ShapeDtypeStruct((b, t, 1024), BF16),
            jax.ShapeDtypeStruct((b, t, 1024), BF16),
            jax.ShapeDtypeStruct((b, t, LANES), F32),
            jax.ShapeDtypeStruct((b, t, 3 * w), BF16),
            jax.ShapeDtypeStruct((2, b, t, w), F32),
            jax.ShapeDtypeStruct((2, b, t, 2 * w), BF16),
            jax.ShapeDtypeStruct((b, t, 2 * w), BF16),
        ],
        compiler_params=_params(("parallel", "parallel")),
        name="even_in",
    )(xall, xall, xall, g, mods, mods, w_in, conv, gate_b, mu, w0, w2cat, a0, a2cat, g2, kk, ka, rk, seg64)


def _mlstm_body(q_ref, k_ref, v_ref, gc_ref, gr_ref, o_ref, c_scr, n_scr, m_scr):
    d = pl.program_id(0)
    j = pl.program_id(1)
    n_batch, n_rows = q_ref.shape[0], q_ref.shape[1]

    @pl.when(j == 0)
    def _():
        c_scr[...] = jnp.zeros_like(c_scr)
        n_scr[...] = jnp.zeros_like(n_scr)
        m_scr[...] = jnp.zeros_like(m_scr)

    mask = _order_mask(n_rows, 1 - 2 * d, strict=False)
    mf = mask.astype(BF16)
    gc = [gc_ref[bi] for bi in range(n_batch)]
    gr = [gr_ref[bi] for bi in range(n_batch)]
    b_col = [_sel_dot(mf, x) for x in gc]
    b_row = [_dot_sel_nt(x, mf) for x in gr]
    tot_col = [jnp.sum(x, axis=0, keepdims=True) for x in gc]
    units = [(bi, h) for bi in range(n_batch) for h in range(ML_HEADS)]
    idx = range(len(units))
    sls = [slice(h * ML_DIM, (h + 1) * ML_DIM) for _, h in units]
    fcol = [ML_HEADS + h for _, h in units]
    pick_row = lax.broadcasted_iota(jnp.int32, (2 * ML_HEADS, LANES), 0)
    pick = [(pick_row == c).astype(BF16) for c in range(2 * ML_HEADS)]
    ones_cols = jnp.ones((n_rows, LANES), BF16)
    q = [_bf(q_ref[bi, :, sls[u]]) for u, (bi, _) in enumerate(units)]
    kf = [k_ref[bi, :, sls[u]].astype(F32) * (ML_DIM ** -0.5) for u, (bi, _) in enumerate(units)]
    k = [_bf(x) for x in kf]
    vf = [v_ref[bi, :, sls[u]].astype(F32) for u, (bi, _) in enumerate(units)]
    c_prev = [c_scr[u] for u in idx]
    n_prev = [n_scr[u][0:1, :] for u in idx]
    m_prev = [m_scr[u][0:1, :] for u in idx]
    qk = [_dot_nt(q[u], k[u]) for u in idx]
    qc = [_dot_nt(q[u], jnp.concatenate([_bf(c_prev[u]), _bf(jnp.broadcast_to(n_prev[u], (LANES, ML_DIM)))], axis=0))
          for u in idx]
    ic = [_dot_sel(gc[bi], pick[h]) for bi, h in units]
    bc = [_dot_sel(b_col[bi], pick[fcol[u]]) for u, (bi, _) in enumerate(units)]
    b_tot = [tot_col[bi][:, fcol[u]:fcol[u] + 1] for u, (bi, _) in enumerate(units)]
    logw = [b_tot[u] - bc[u] + ic[u] for u in idx]
    m_new = [jnp.maximum(b_tot[u] + m_prev[u], jnp.max(logw[u], axis=0, keepdims=True)) for u in idx]
    wgt = [jnp.exp(logw[u] - m_new[u]) for u in idx]
    decay = [jnp.exp(b_tot[u] + m_prev[u] - m_new[u]) for u in idx]
    kv = [_dot_tn(_bf(vf[u] * wgt[u]), k[u]) for u in idx]
    s, m_t, w_inter = [], [], []
    for u, (bi, h) in enumerate(units):
        logd = jnp.where(mask, bc[u] - b_row[bi][fcol[u]:fcol[u] + 1, :] + gr[bi][h:h + 1, :], -1e30)
        m_inter = bc[u] + m_prev[u]
        m_t.append(jnp.maximum(m_inter, jnp.max(logd, axis=-1, keepdims=True)))
        s.append(qk[u] * jnp.exp(logd - m_t[u]))
        w_inter.append(jnp.exp(m_inter - m_t[u]))
    sv = [_dot(_bf(s[u]), jnp.concatenate([_bf(vf[u]), ones_cols], axis=1)) for u in idx]
    for u, (bi, _) in enumerate(units):
        num = sv[u][:, :ML_DIM] + w_inter[u] * qc[u][:, :ML_DIM]
        den = sv[u][:, ML_DIM:] + w_inter[u] * qc[u][:, ML_DIM:]
        o_ref[bi, :, sls[u]] = num / jnp.maximum(jnp.abs(den), jnp.exp(-m_t[u]))
    for u in idx:
        c_scr[u] = decay[u] * c_prev[u] + kv[u]
        n_scr[u] = jnp.broadcast_to(decay[u] * n_prev[u] + jnp.sum(wgt[u] * kf[u], axis=0, keepdims=True),
                                    (SUBLANES, ML_DIM))
        m_scr[u] = jnp.broadcast_to(m_new[u], (SUBLANES, LANES))


def _mlstm(qk, proj, gcol, grow, n_ctx):
    b, t, _ = qk.shape
    n_rows = ML_CHUNK
    nc, nctx = t // n_rows, n_ctx // n_rows
    ci = lambda d, j: _chunk_index(d, j, nctx, nc)
    return pl.pallas_call(
        _mlstm_body,
        grid=(2, nc),
        in_specs=[
            pl.BlockSpec((b, n_rows, ML_WIDTH), lambda d, j: (0, ci(d, j), 0)),
            pl.BlockSpec((b, n_rows, ML_WIDTH), lambda d, j: (0, ci(d, j), 1)),
            pl.BlockSpec((b, n_rows, ML_WIDTH), lambda d, j: (0, ci(d, j), 0)),
            pl.BlockSpec((None, b, n_rows, 2 * ML_HEADS), lambda d, j: (d, 0, ci(d, j), 0)),
            pl.BlockSpec((None, b, 2 * ML_HEADS, n_rows), lambda d, j: (d, 0, 0, ci(d, j))),
        ],
        out_specs=pl.BlockSpec((None, b, n_rows, ML_WIDTH), lambda d, j: (d, 0, ci(d, j), 0)),
        out_shape=jax.ShapeDtypeStruct((2, b, t, ML_WIDTH), F32),
        scratch_shapes=[
            pltpu.VMEM((b * ML_HEADS, ML_DIM, ML_DIM), F32),
            pltpu.VMEM((b * ML_HEADS, SUBLANES, ML_DIM), F32),
            pltpu.VMEM((b * ML_HEADS, SUBLANES, LANES), F32),
        ],
        compiler_params=_params(("parallel", "arbitrary")),
        name="mlstm",
    )(qk, qk, proj, gcol, grow)


def _rwkv_body(r_ref, v_ref, a_ref, lw_ref, k_ref, b_ref, o_ref, h_scr):
    d = pl.program_id(0)
    j = pl.program_id(1)
    n_rows = r_ref.shape[1]
    two = 2 * n_rows
    shift = n_rows.bit_length() - 1

    @pl.when(j == 0)
    def _():
        h_scr[...] = jnp.zeros_like(h_scr)

    sgn = 1 - 2 * d
    cum = _order_mask(n_rows, sgn, strict=False).astype(BF16)
    row = lax.broadcasted_iota(jnp.int32, (two, two), 0)
    col = lax.broadcasted_iota(jnp.int32, (two, two), 1)
    same = jnp.right_shift(row, shift) == jnp.right_shift(col, shift)
    diff = (jnp.bitwise_and(row, n_rows - 1) - jnp.bitwise_and(col, n_rows - 1)) * sgn
    m_strict = jnp.logical_and(same, diff > 0)
    m_incl = jnp.logical_and(same, diff >= 0)
    eye = (row == col).astype(F32)
    head0 = lax.broadcasted_iota(jnp.int32, (1, LANES), 1) < RW_HEAD_DIM

    def stack(x):
        return jnp.concatenate([jnp.where(head0, x, 0.0), jnp.where(head0, 0.0, x)], axis=0)

    def fold(x):
        return x[:n_rows] + x[n_rows:]

    n_batch = r_ref.shape[0]
    groups = [(bi, slice(p * LANES, (p + 1) * LANES)) for bi in range(n_batch) for p in range(RW_WIDTH // LANES)]
    x_s, r_s, v_s, bh_s, kh_s, bi_s, ki_s, e_tot = [], [], [], [], [], [], [], []
    for bi in range(n_batch):
        lw = lw_ref[bi]
        g = _sel_dot(cum, lw)
        g_tot = jnp.sum(lw, axis=0, keepdims=True)
        inv = jnp.exp(-g)
        to_end = jnp.exp(g_tot - g)
        at = a_ref[bi].astype(F32) * jnp.exp(g - lw)
        rt = r_ref[bi].astype(F32) * jnp.exp(g)
        b_in, k_in, vv = b_ref[bi].astype(F32), k_ref[bi].astype(F32), v_ref[bi].astype(F32)
        for gb, sl in groups:
            if gb != bi:
                continue
            x_s.append(stack(at[:, sl]))
            r_s.append(stack(rt[:, sl]))
            v_s.append(stack(vv[:, sl]))
            bh_s.append(stack((b_in * to_end)[:, sl]))
            kh_s.append(stack((k_in * to_end)[:, sl]))
            bi_s.append(stack((b_in * inv)[:, sl]))
            ki_s.append(stack((k_in * inv)[:, sl]))
            e_tot.append(jnp.exp(g_tot)[:, sl])
    big = [_mm_nt(jnp.concatenate([x_s[p], r_s[p]], axis=0), jnp.concatenate([bi_s[p], ki_s[p]], axis=0))
           for p in range(len(groups))]
    a_ab = [jnp.where(m_strict, m[:two, :two], 0.0) for m in big]
    a_rb = [jnp.where(m_incl, m[two:, :two], 0.0) for m in big]
    a_k = [jnp.concatenate([jnp.where(m_strict, m[:two, two:], 0.0), jnp.where(m_incl, m[two:, two:], 0.0)], axis=0)
           for m in big]
    av = [_mm(a_k[p], v_s[p]) for p in range(len(groups))]
    t_inv = [eye + m for m in a_ab]
    pw = [_mm(m, m) for m in a_ab]
    for level in range(1, shift):
        last = level == shift - 1
        res = [_mm(t if last else jnp.concatenate([t, m], axis=0), m) for t, m in zip(t_inv, pw)]
        t_inv = [t + r[:two] for t, r in zip(t_inv, res)]
        if not last:
            pw = [r[two:] for r in res]
    wu = [_mm(t_inv[p], jnp.concatenate([x_s[p], av[p][:two]], axis=1)) for p in range(len(groups))]
    rb = [_mm(a_rb[p], wu[p]) for p in range(len(groups))]
    upd = [_mm_tn(wu[p], bh_s[p]) for p in range(len(groups))]
    gt2 = [_mm_tn(v_s[p], kh_s[p]) for p in range(len(groups))]
    h_old = [h_scr[p] for p in range(len(groups))]
    for p, (bi, sl) in enumerate(groups):
        q_eff = fold(r_s[p] + rb[p][:, :LANES])
        y_in = fold(rb[p][:, LANES:] + av[p][two:])
        o_ref[bi, :, sl] = _mm_nt(q_eff, h_old[p]) + y_in
    for p in range(len(groups)):
        h_scr[p] = e_tot[p] * h_old[p] + _mm(h_old[p], upd[p][:LANES]) + upd[p][LANES:] + gt2[p]


def _rwkv(rws, lw, kb, n_ctx):
    b, t, _ = rws.shape
    n_rows = RW_CHUNK
    nc, nctx = t // n_rows, n_ctx // n_rows
    ci = lambda d, j: _chunk_index(d, j, nctx, nc)
    w = RW_WIDTH
    shared = lambda c: pl.BlockSpec((b, n_rows, w), lambda d, j: (0, ci(d, j), c))
    per_dir = lambda c: pl.BlockSpec((None, b, n_rows, w), lambda d, j: (d, 0, ci(d, j), c))
    return pl.pallas_call(
        _rwkv_body,
        grid=(2, nc),
        in_specs=[shared(0), shared(1), shared(2), per_dir(0), per_dir(0), per_dir(1)],
        out_specs=pl.BlockSpec((None, b, n_rows, w), lambda d, j: (d, 0, ci(d, j), 0)),
        out_shape=jax.ShapeDtypeStruct((2, b, t, w), F32),
        scratch_shapes=[pltpu.VMEM((b * w // LANES, LANES, LANES), F32)],
        compiler_params=_params(("parallel", "arbitrary")),
        name="rwkv7",
    )(rws, rws, rws, lw, kb, kb)


def _tail(x, mix, outw_ref, w1_ref, w2_ref, ng_ref, ga_ref, shm_ref, scm_ref, gm_ref):
    x1 = x + ga_ref[...] * _rms(_dot(_bf(mix), outw_ref[...]), ng_ref[1:2, :])
    h = _bf(_rms(x1, ng_ref[2:3, :]) * (1.0 + scm_ref[...]) + shm_ref[...])
    chunks = [slice(c * FF_CHUNK, (c + 1) * FF_CHUNK) for c in range(D_FF // FF_CHUNK)]
    u = [jnp.maximum(_dot(h, w1_ref[:, cols]), 0.0) for cols in chunks]
    dd = _dot(_bf(u[0] * u[0]), w2_ref[chunks[0], :])
    for uc, cols in zip(u[1:], chunks[1:]):
        dd = dd + _dot(_bf(uc * uc), w2_ref[cols, :])
    return x1 + gm_ref[...] * _rms(dd, ng_ref[3:4, :])


def _even_post_body(x_ref, hm_ref, mo_ref, yrw_ref, bg_ref, mln_ref, lnw_ref, lnb_ref, s128_ref, s64_ref,
                    outw_ref, w1_ref, w2_ref, ng_ref, ga_ref, shm_ref, scm_ref, gm_ref, o_ref):
    w = RW_WIDTH
    ym = hm_ref[0] + hm_ref[1]
    ym = (ym * lax.rsqrt(_dot_sel(ym * ym, s128_ref[...]) + EPS) * mln_ref[...]
          * _sigmoid(mo_ref[...].astype(F32)))
    y = yrw_ref[0] + yrw_ref[1]
    yc = y - _dot_sel(y, s64_ref[...])
    y = (yc * lax.rsqrt(_dot_sel(yc * yc, s64_ref[...]) + RW_GN_EPS) * lnw_ref[...] + lnb_ref[...]
         + bg_ref[:, 0:w].astype(F32))
    mix = jnp.concatenate([ym, y * bg_ref[:, w:2 * w].astype(F32)], axis=-1)
    o_ref[...] = _tail(x_ref[...], mix, outw_ref, w1_ref, w2_ref, ng_ref, ga_ref, shm_ref, scm_ref, gm_ref)


def _tail_specs(seg):
    d = D_MODEL
    return [
        _const_spec((d, d)), _const_spec((d, D_FF)), _const_spec((D_FF, d)), _const_spec((4, d)),
        _mod_spec(2, seg), _mod_spec(3, seg), _mod_spec(4, seg), _mod_spec(5, seg),
    ]


def _even_post(xall, hm, proj, yrw, bg, mln, lnw, lnb, s128, s64, outw, w1, w2, ng, mods, nctx_t):
    b, t, d = xall.shape
    tm = ROW_TILE
    seg = lambda i: (i >= nctx_t).astype(jnp.int32)
    w = RW_WIDTH
    return pl.pallas_call(
        _even_post_body,
        grid=(b, t // tm),
        in_specs=[
            pl.BlockSpec((None, tm, d), lambda bb, i: (bb, i, 0)),
            pl.BlockSpec((2, None, tm, w), lambda bb, i: (0, bb, i, 0)),
            pl.BlockSpec((None, tm, w), lambda bb, i: (bb, i, 1)),
            pl.BlockSpec((2, None, tm, w), lambda bb, i: (0, bb, i, 0)),
            pl.BlockSpec((None, tm, 2 * w), lambda bb, i: (bb, i, 0)),
            _const_spec((1, w)), _const_spec((1, w)), _const_spec((1, w)),
            _const_spec((w, w)), _const_spec((w, w)),
        ] + _tail_specs(seg),
        out_specs=pl.BlockSpec((None, tm, d), lambda bb, i: (bb, i, 0)),
        out_shape=jax.ShapeDtypeStruct((b, t, d), F32),
        compiler_params=_params(("parallel", "parallel")),
        name="even_post",
    )(xall, hm, proj, yrw, bg, mln, lnw, lnb, s128, s64, outw, w1, w2, ng, mods, mods, mods, mods)


def _hgrn_body(q_ref, f_ref, i_ref, lb_ref, o_ref, s_scr):
    d = pl.program_id(0)
    j = pl.program_id(1)
    n_batch, n_rows = q_ref.shape[0], q_ref.shape[1]
    n_sub = n_rows // HG_CHUNK

    @pl.when(j == 0)
    def _():
        s_scr[...] = jnp.zeros_like(s_scr)

    mask = _order_mask(HG_CHUNK, 1 - 2 * d, strict=False)
    mf = mask.astype(BF16)
    lb = lb_ref[...]
    heads = range(HG_HEADS)
    sls = [slice(h * HG_DIM, (h + 1) * HG_DIM) for h in heads]
    units = [(bi, h) for bi in range(n_batch) for h in heads]
    rows, q_in, e_tot, q_mids, k_mids, k_outs, vvs = [], [], [], [], [], [], []
    for sc in range(n_sub):
        off = pl.multiple_of(jnp.where(d == 0, sc, n_sub - 1 - sc) * HG_CHUNK, HG_CHUNK)
        rows.append(pl.ds(off, HG_CHUNK))
        for lst in (q_in, e_tot, q_mids, k_mids, k_outs, vvs):
            lst.append([])
        for bi in range(n_batch):
            qa = q_ref[bi, rows[sc], :].astype(F32)
            pre = f_ref[bi, rows[sc], :].astype(F32)
            vv = i_ref[bi, rows[sc], :]
            q = qa * _sigmoid(qa) * (HG_DIM ** -0.5)
            e_abs = jnp.exp(-jnp.abs(pre))
            big = 1.0 / (1.0 + e_abs)
            small = e_abs * big
            log_f = jnp.log(lb + (1.0 - lb) * jnp.where(pre >= 0.0, big, small))
            kf = (1.0 - lb) * jnp.where(pre >= 0.0, small, big)
            bc = _sel_dot(mf, log_f)
            tot = jnp.sum(log_f, axis=0, keepdims=True)
            half = 0.5 * tot
            e_half = jnp.exp(half)
            q_mid_f = q * jnp.exp(bc - half)
            k_mid_f = kf * jnp.exp(half - bc)
            q_mid = _bf(q_mid_f)
            k_mid = _bf(k_mid_f)
            k_out = _bf(k_mid_f * e_half)
            q_all = _bf(q_mid_f * e_half)
            e_all = e_half * e_half
            q_mids[sc] += [q_mid[:, sl] for sl in sls]
            k_mids[sc] += [k_mid[:, sl] for sl in sls]
            k_outs[sc] += [k_out[:, sl] for sl in sls]
            vvs[sc] += [vv[:, sl] for sl in sls]
            q_in[sc] += [q_all[:, sl] for sl in sls]
            e_tot[sc] += [e_all[:, sl] for sl in sls]
    n_units = len(units)
    att = [[_dot_nt(q_mids[sc][u], k_mids[sc][u]) for u in range(n_units)] for sc in range(n_sub)]
    kv = [[_dot_tn(vvs[sc][u], k_outs[sc][u]) for u in range(n_units)] for sc in range(n_sub)]
    intra = [[_dot(_bf(jnp.where(mask, att[sc][u], 0.0)), vvs[sc][u]) for u in range(n_units)] for sc in range(n_sub)]
    s_t = [s_scr[u] for u in range(len(units))]
    for sc in range(n_sub):
        for u, (bi, h) in enumerate(units):
            o_ref[bi, rows[sc], sls[h]] = intra[sc][u] + _dot_nt(q_in[sc][u], _bf(s_t[u]))
        s_t = [e_tot[sc][u] * s_t[u] + kv[sc][u] for u in range(len(units))]
    for u in range(len(units)):
        s_scr[u] = s_t[u]


def _hgrn(proj, lb, n_ctx):
    b, t, _ = proj.shape
    n_rows = HG_BLOCK
    nc, nctx = t // n_rows, n_ctx // n_rows
    ci = lambda d, j: _chunk_index(d, j, nctx, nc)
    w = HG_WIDTH
    return pl.pallas_call(
        _hgrn_body,
        grid=(2, nc),
        in_specs=[
            pl.BlockSpec((b, n_rows, w), lambda d, j: (0, ci(d, j), 0)),
            pl.BlockSpec((b, n_rows, w), lambda d, j: (0, ci(d, j), 1 + d)),
            pl.BlockSpec((b, n_rows, w), lambda d, j: (0, ci(d, j), 3)),
            _const_spec((1, w)),
        ],
        out_specs=pl.BlockSpec((None, b, n_rows, w), lambda d, j: (d, 0, ci(d, j), 0)),
        out_shape=jax.ShapeDtypeStruct((2, b, t, w), F32),
        scratch_shapes=[pltpu.VMEM((b * HG_HEADS, HG_DIM, HG_DIM), F32)],
        compiler_params=_params(("parallel", "arbitrary")),
        name="hgrn2",
    )(proj, proj, proj, lb)


V_ROWS = MLA_V + SUBLANES


OD_MAIN = 5 * HG_WIDTH


def _odd_in_body(x_ref, g_ref, sh_ref, sc_ref, w_ref, cos_ref, sin_ref, kvn_ref, qn_ref,
                 wk_ref, wvt_ref, wq1_ref, wq2_ref, proj_ref, k_ref, vt_ref, q_ref):
    h = _bf(_rms(x_ref[...], g_ref[...]) * (1.0 + sc_ref[...]) + sh_ref[...])
    proj = _dot(h, w_ref[...])
    proj_ref[...] = proj[:, 0:OD_MAIN].astype(proj_ref.dtype)
    cq = proj[:, OD_MAIN:OD_MAIN + MLA_Q_RANK]
    base = OD_MAIN + MLA_Q_RANK
    ckv = proj[:, base:base + MLA_KV_RANK]
    cos, sin = cos_ref[...], sin_ref[...]
    k_rope = proj[:, base + LANES:base + 2 * LANES] * cos + proj[:, base + 2 * LANES:base + 3 * LANES] * sin
    kvn = _bf(_rms(ckv, kvn_ref[...]))
    kn = _dot(kvn, wk_ref[...])
    ones = jnp.ones((SUBLANES, proj.shape[0]), BF16)
    cn = _bf(_rms(cq, qn_ref[...]))
    qa = _dot(cn, wq1_ref[...])
    qb = _dot(cn, wq2_ref[...])
    scale = (MLA_NOPE + MLA_ROPE) ** -0.5 * float(np.log2(np.e))
    for hd in range(MLA_HEADS):
        k_ref[hd] = jnp.concatenate([kn[:, hd * MLA_NOPE:(hd + 1) * MLA_NOPE], k_rope], axis=-1).astype(BF16)
        vt = _dot_nt(wvt_ref[hd], kvn)
        vt_ref[hd] = jnp.concatenate([vt.astype(BF16), ones], axis=0)
        a = qa[:, hd * 256:(hd + 1) * 256]
        q_rope = a[:, MLA_NOPE:] * cos + qb[:, hd * LANES:(hd + 1) * LANES] * sin
        q_ref[hd] = (jnp.concatenate([a[:, 0:MLA_NOPE], q_rope], axis=-1) * scale).astype(BF16)


def _odd_in(xall, g, mods, w_in, nctx_t, cos, sin, kv_norm, q_norm, wk, wvt, wq1, wq2):
    b, t, d = xall.shape
    tm = ROW_TILE
    n = t - nctx_t * tm
    seg = lambda i: (i >= nctx_t).astype(jnp.int32)
    q_blk = lambda bb, i: (bb, 0, jnp.maximum(i - nctx_t, 0), 0)
    return pl.pallas_call(
        _odd_in_body,
        grid=(b, t // tm),
        in_specs=[
            pl.BlockSpec((None, tm, d), lambda bb, i: (bb, i, 0)),
            _const_spec((1, d)), _mod_spec(0, seg), _mod_spec(1, seg), _const_spec(w_in.shape),
            pl.BlockSpec((tm, LANES), lambda bb, i: (i, 0)),
            pl.BlockSpec((tm, LANES), lambda bb, i: (i, 0)),
            _const_spec((1, MLA_KV_RANK)), _const_spec((1, MLA_Q_RANK)),
            _const_spec(wk.shape), _const_spec(wvt.shape), _const_spec(wq1.shape), _const_spec(wq2.shape),
        ],
        out_specs=[
            pl.BlockSpec((None, tm, OD_MAIN), lambda bb, i: (bb, i, 0)),
            pl.BlockSpec((None, MLA_HEADS, tm, 256), lambda bb, i: (bb, 0, i, 0)),
            pl.BlockSpec((None, MLA_HEADS, V_ROWS, tm), lambda bb, i: (bb, 0, 0, i)),
            pl.BlockSpec((None, MLA_HEADS, tm, 256), q_blk),
        ],
        out_shape=[
            jax.ShapeDtypeStruct((b, t, OD_MAIN), BF16),
            jax.ShapeDtypeStruct((b, MLA_HEADS, t, 256), BF16),
            jax.ShapeDtypeStruct((b, MLA_HEADS, V_ROWS, t), BF16),
            jax.ShapeDtypeStruct((b, MLA_HEADS, n, 256), BF16),
        ],
        compiler_params=_params(("parallel", "arbitrary")),
        name="odd_in",
    )(xall, g, mods, mods, w_in, cos, sin, kv_norm, q_norm, wk, wvt, wq1, wq2)


def _attn_body(q_ref, k_ref, vt_ref, o_ref):
    k = k_ref[...]
    vt = vt_ref[...]
    subs = [slice(i * ATTN_SUB, (i + 1) * ATTN_SUB) for i in range(q_ref.shape[0] // ATTN_SUB)]
    st = [_dot_nt(k, q_ref[sl, :]) for sl in subs]
    pt = [_bf(jnp.exp2(s - jnp.max(s, axis=0, keepdims=True))) for s in st]
    ot = [_dot(vt, p) for p in pt]
    for sl, o in zip(subs, ot):
        o_ref[sl, :] = (o[0:MLA_V] / o[MLA_V:MLA_V + 1]).T


def _attention(q, k, vt):
    b, h, n, dq = q.shape
    t = k.shape[2]
    tq = ATTN_TQ
    return pl.pallas_call(
        _attn_body,
        grid=(b, h, n // tq),
        in_specs=[
            pl.BlockSpec((None, None, tq, dq), lambda bb, hh, i: (bb, hh, i, 0)),
            pl.BlockSpec((None, None, t, dq), lambda bb, hh, i: (bb, hh, 0, 0)),
            pl.BlockSpec((None, None, V_ROWS, t), lambda bb, hh, i: (bb, hh, 0, 0)),
        ],
        out_specs=pl.BlockSpec((None, tq, MLA_V), lambda bb, hh, i: (bb, i, hh)),
        out_shape=jax.ShapeDtypeStruct((b, n, h * MLA_V), F32),
        compiler_params=_params(("parallel", "parallel", "parallel")),
        name="mla_attention",
    )(q, k, vt)


def _odd_post_body(x_ref, o_ref_in, hg_ref, att_ref, hgn_ref, s128_ref,
                   outw_ref, w1_ref, w2_ref, ng_ref, ga_ref, shm_ref, scm_ref, gm_ref, o_ref):
    y = o_ref_in[0, 0] + o_ref_in[1, 0]
    g = hg_ref[0].astype(F32)
    y = y * lax.rsqrt(_dot_sel(y * y, s128_ref[...]) + EPS) * hgn_ref[...] * (g * _sigmoid(g))
    mix = jnp.concatenate([y, att_ref[...]], axis=-1)
    o_ref[...] = _tail(x_ref[0], mix, outw_ref, w1_ref, w2_ref, ng_ref, ga_ref, shm_ref, scm_ref, gm_ref)


def _odd_post(xall, ohg, proj, att, hgn, s128, outw, w1, w2, ng, mods, n_ctx):
    b, t, d = xall.shape
    tm = LAT_TILE
    n = t - n_ctx
    seg = lambda i: 1
    w = HG_WIDTH
    el = lambda *shape: tuple(pl.Element(s) for s in shape)
    row0 = lambda i: pl.multiple_of(n_ctx + i * tm, int(np.gcd(n_ctx, tm)))
    return pl.pallas_call(
        _odd_post_body,
        grid=(b, n // tm),
        in_specs=[
            pl.BlockSpec(el(1, tm, d), lambda bb, i: (bb, row0(i), 0)),
            pl.BlockSpec(el(2, 1, tm, w), lambda bb, i: (0, bb, row0(i), 0)),
            pl.BlockSpec(el(1, tm, w), lambda bb, i: (bb, row0(i), 4 * w)),
            pl.BlockSpec((None, tm, w), lambda bb, i: (bb, i, 0)),
            _const_spec((1, w)), _const_spec((w, w)),
        ] + _tail_specs(seg),
        out_specs=pl.BlockSpec((None, tm, d), lambda bb, i: (bb, i, 0)),
        out_shape=jax.ShapeDtypeStruct((b, n, d), F32),
        compiler_params=_params(("parallel", "parallel")),
        name="odd_post",
    )(xall, ohg, proj, att, hgn, s128, outw, w1, w2, ng, mods, mods, mods, mods)


def _block_mean_matrix(width, group):
    idx = np.arange(width) // group
    return jnp.asarray((idx[:, None] == idx[None, :]).astype(np.float32) / group)


def _rope_tables(n, n_ctx):
    quarter = MLA_ROPE // 4
    inv = ROPE_BASE ** (-jnp.arange(quarter, dtype=F32) / quarter)
    tok = jnp.arange(n)
    ang_r = (tok // GRID_W).astype(F32)[:, None] * inv[None, :]
    ang_c = (tok % GRID_W).astype(F32)[:, None] * inv[None, :]
    cos = jnp.concatenate([jnp.cos(ang_r)] * 2 + [jnp.cos(ang_c)] * 2, axis=-1)
    sin = jnp.concatenate([-jnp.sin(ang_r), jnp.sin(ang_r), -jnp.sin(ang_c), jnp.sin(ang_c)], axis=-1)
    cos = jnp.concatenate([jnp.ones((n_ctx, MLA_ROPE), F32), cos], axis=0)
    sin = jnp.concatenate([jnp.zeros((n_ctx, MLA_ROPE), F32), sin], axis=0)
    pad = lambda a: jnp.pad(a, ((0, 0), (0, LANES - MLA_ROPE)))
    return pad(cos), pad(sin)


_ROPE_SWAP = np.concatenate([np.arange(16, 32), np.arange(0, 16), np.arange(48, 64), np.arange(32, 48)])


def kernel(x, c, ctx, c_ctx, ada_w, ada_b, norm_g, out_w, mlp_w1, mlp_w2, hg_lb, ev_w_in, ml_conv, ml_gate_b, ml_norm, rw_mu, rw_w0, rw_w2, rw_a0, rw_a2, rw_g2, rw_kk, rw_ka, rw_rk, rw_ln_w, rw_ln_b, od_w_in, hg_norm, mla_q_norm, mla_w_qb, mla_kv_norm, mla_w_kvb):
    b, n, d = x.shape
    n_ctx = ctx.shape[1]
    t = n_ctx + n
    assert d == D_MODEL and b + 1 <= SUBLANES
    assert n_ctx % ROW_TILE == 0 and n % ROW_TILE == 0 and n % ATTN_TQ == 0
    assert n_ctx % ML_CHUNK == 0 and n % ML_CHUNK == 0 and n_ctx % HG_BLOCK == 0 and n % HG_BLOCK == 0
    nctx_t = n_ctx // ROW_TILE
    row1 = lambda a: a.reshape(1, -1)

    c8 = jnp.concatenate([c, c_ctx[None], jnp.zeros((SUBLANES - b - 1, d), F32)], axis=0)
    mod = _ada_mod(c8, ada_w, ada_b).reshape(-1, SUBLANES, N_MOD, d)

    def layer_mods(l):
        lat = mod[l, :b]
        cm = jnp.broadcast_to(mod[l, b][None], lat.shape)
        return jnp.stack([cm, lat], axis=1)[:, :, :, None, :]

    xall = jnp.concatenate([ctx, x], axis=1)
    s128 = _bf(_block_mean_matrix(512, 128))
    s64 = _bf(_block_mean_matrix(512, 64))

    l, e = 0, 0
    mods = layer_mods(l)
    wi = ev_w_in[e]
    w_pad = jnp.concatenate([wi[:, 2064:3984], wi[:, 2048:2064], jnp.zeros((d, LANES - 16), F32), wi[:, 0:2048]], axis=1)
    gate_b = jnp.pad(ml_gate_b[e].reshape(1, 16), ((0, 0), (0, LANES - 16)))
    zeros64 = jnp.zeros((64, RW_WIDTH), F32)
    w2cat = jnp.stack([jnp.concatenate([rw_w2[e, 0], zeros64], 0), jnp.concatenate([zeros64, rw_w2[e, 1]], 0)])
    a2cat = jnp.stack([jnp.concatenate([rw_a2[e, 0], zeros64], 0), jnp.concatenate([zeros64, rw_a2[e, 1]], 0)])
    proj, qk, gates, rws, rw_lw, rw_kb, bg = _even_in(
        xall, row1(norm_g[l, 0]), mods, _bf(w_pad), nctx_t,
        ml_conv[e], gate_b, row1(rw_mu[e]), rw_w0[e], w2cat, rw_a0[e], a2cat, rw_g2[e],
        row1(rw_kk[e]), row1(rw_ka[e]), row1(rw_rk[e]), s64 * RW_HEAD_DIM)

    g4 = gates[:, :, :16].reshape(b, t, 4, ML_HEADS)
    gcol = jnp.stack([g4[:, :, 0:2].reshape(b, t, 8), g4[:, :, 2:4].reshape(b, t, 8)], axis=0)
    grow = jnp.swapaxes(gcol, 2, 3)
    hm = _mlstm(qk, proj, gcol, grow, n_ctx)
    yrw = _rwkv(rws, rw_lw, rw_kb, n_ctx)
    xall = _even_post(xall, hm, proj, yrw, bg, row1(ml_norm[e]), row1(rw_ln_w[e]), row1(rw_ln_b[e]), s128, s64,
                      _bf(out_w[l]), _bf(mlp_w1[l]), _bf(mlp_w2[l]), norm_g[l], mods, nctx_t)

    l, o = 1, 0
    mods = layer_mods(l)
    lb_all = jax.nn.softmax(hg_lb.astype(F32), axis=0)
    lb_all = jnp.cumsum(lb_all, axis=0) - lb_all[0]
    wi = od_w_in[o]
    kr = wi[:, 2944:3008]
    z = lambda k: jnp.zeros((d, k), F32)
    w_pad = jnp.concatenate([wi[:, 0:2944], kr, z(64), kr[:, _ROPE_SWAP], z(64)], axis=1)
    cos, sin = _rope_tables(n, n_ctx)
    wkv = mla_w_kvb[o].reshape(MLA_KV_RANK, MLA_HEADS, MLA_NOPE + MLA_V)
    wk = wkv[:, :, :MLA_NOPE].reshape(MLA_KV_RANK, MLA_HEADS * MLA_NOPE)
    wvt = jnp.transpose(wkv[:, :, MLA_NOPE:], (1, 2, 0))
    wq = mla_w_qb[o].reshape(MLA_Q_RANK, MLA_HEADS, MLA_NOPE + MLA_ROPE)
    zq = lambda k: jnp.zeros((MLA_Q_RANK, MLA_HEADS, k), F32)
    wq1 = jnp.concatenate([wq, zq(64)], axis=-1).reshape(MLA_Q_RANK, -1)
    wq2 = jnp.concatenate([wq[:, :, MLA_NOPE:][:, :, _ROPE_SWAP], zq(64)], axis=-1).reshape(MLA_Q_RANK, -1)
    proj, k_all, v_all, q_all = _odd_in(
        xall, row1(norm_g[l, 0]), mods, _bf(w_pad), nctx_t, cos, sin, row1(mla_kv_norm[o]), row1(mla_q_norm[o]),
        _bf(wk), _bf(wvt), _bf(wq1), _bf(wq2))

    ohg = _hgrn(proj, row1(lb_all[l]), n_ctx)
    att = _attention(q_all, k_all, v_all)

    return _odd_post(xall, ohg, proj, att, row1(hg_norm[o]), s128,
                     _bf(out_w[l]), _bf(mlp_w1[l]), _bf(mlp_w2[l]), norm_g[l], mods, n_ctx)
```

```python
import functools

import numpy as np
import jax
import jax.numpy as jnp
from jax import lax
from jax.experimental import pallas as pl
from jax.experimental.pallas import tpu as pltpu

F32 = jnp.float32
BF16 = jnp.bfloat16
HI = lax.Precision.HIGHEST

D_MODEL = 1024
GRID_W = 64
D_FF = 4 * D_MODEL
N_MOD = 6
EPS = 1e-6
ML_WIDTH = 512
ML_HEADS = 4
ML_DIM = 128
RW_WIDTH = 512
RW_HEAD_DIM = 64
RW_GN_EPS = 64e-5
RW_FEAT = 1920
HG_WIDTH = 512
HG_HEADS = 4
HG_DIM = 128
MLA_HEADS = 4
MLA_NOPE = 128
MLA_ROPE = 64
MLA_V = 128
MLA_Q_RANK = 256
MLA_KV_RANK = 128
ROPE_BASE = 10000.0

LANES = 128
SUBLANES = 8
HALO = 16
VMEM_LIMIT_BYTES = 56 * 2**20

ROW_TILE = 256
LAT_TILE = 512
FF_CHUNK = 1024
ML_CHUNK = 128
RW_CHUNK = 64
HG_BLOCK = 128
HG_CHUNK = 32
ATTN_TQ = 512
ATTN_SUB = 256


def _dot(a, b, prec=None):
    return jnp.dot(a, b, preferred_element_type=F32, precision=prec)


def _dot_nt(a, b, prec=None):
    return lax.dot_general(a, b, (((1,), (1,)), ((), ())), preferred_element_type=F32, precision=prec)


def _dot_tn(a, b, prec=None):
    return lax.dot_general(a, b, (((0,), (0,)), ((), ())), preferred_element_type=F32, precision=prec)


def _bf(x):
    return x.astype(BF16)


def _mm(a, b):
    return _dot(_bf(a), _bf(b))


def _mm_nt(a, b):
    return _dot_nt(_bf(a), _bf(b))


def _mm_tn(a, b):
    return _dot_tn(_bf(a), _bf(b))


def _split(x):
    hi = _bf(x)
    return hi, _bf(x - hi.astype(F32))


def _sel_dot(sel, x):
    hi, lo = _split(x)
    return _dot(sel, hi) + _dot(sel, lo)


def _dot_sel(x, sel):
    hi, lo = _split(x)
    return _dot(hi, sel) + _dot(lo, sel)


def _dot_sel_nt(x, sel):
    hi, lo = _split(x)
    return _dot_nt(hi, sel) + _dot_nt(lo, sel)


def _dot3(a, b):
    ah, al = _split(a)
    bh, bl = _split(b)
    return _dot(ah, bh) + _dot(ah, bl) + _dot(al, bh)


def _group_mean(x, group):
    low = lax.broadcasted_iota(jnp.int32, (1, LANES), 1) < group
    outs = []
    for c in range(x.shape[1] // LANES):
        blk = x[:, c * LANES:(c + 1) * LANES]
        total = jnp.sum(blk, axis=-1, keepdims=True)
        if group == LANES:
            outs.append(jnp.broadcast_to(total, blk.shape))
        else:
            first = jnp.sum(jnp.where(low, blk, 0.0), axis=-1, keepdims=True)
            outs.append(jnp.where(low, first, total - first))
    return jnp.concatenate(outs, axis=-1) * (1.0 / group)


def _sigmoid(x):
    return 1.0 / (1.0 + jnp.exp(-x))


def _log_sigmoid(x):
    return jnp.minimum(x, 0.0) - jnp.log(1.0 + jnp.exp(-jnp.abs(x)))


def _softplus(x):
    return jnp.maximum(x, 0.0) + jnp.log(1.0 + jnp.exp(-jnp.abs(x)))


def _rms(x, g):
    return x * lax.rsqrt(jnp.mean(x * x, axis=-1, keepdims=True) + EPS) * g


def _order_mask(n, sgn, strict):
    row = lax.broadcasted_iota(jnp.int32, (n, n), 0)
    col = lax.broadcasted_iota(jnp.int32, (n, n), 1)
    diff = (row - col) * sgn
    return diff > 0 if strict else diff >= 0


def _chunk_index(d, j, nctx, nc):
    bwd = jnp.where(j < nctx, nctx - 1 - j, nc - 1 - (j - nctx))
    return jnp.where(d == 0, j, bwd)


def _params(sem):
    return pltpu.CompilerParams(dimension_semantics=sem, vmem_limit_bytes=VMEM_LIMIT_BYTES)


def _const_spec(shape):
    nd = len(shape)
    return pl.BlockSpec(shape, lambda *_: (0,) * nd, pipeline_mode=pl.Buffered(1))


def _mod_spec(k, seg_of):
    return pl.BlockSpec((None, None, None, 1, D_MODEL), lambda b, i: (b, seg_of(i), k, 0, 0))


def _ada_body(c_ref, w_ref, b_ref, o_ref):
    c = c_ref[...]
    o_ref[...] = _dot(c * _sigmoid(c), w_ref[...], HI) + b_ref[...]


def _ada_mod(c8, ada_w, ada_b):
    depth, d, n = ada_w.shape
    tn = 1024
    return pl.pallas_call(
        _ada_body,
        grid=(depth, n // tn),
        in_specs=[
            _const_spec((SUBLANES, d)),
            pl.BlockSpec((None, d, tn), lambda l, j: (l, 0, j)),
            pl.BlockSpec((None, 1, tn), lambda l, j: (l, 0, j)),
        ],
        out_specs=pl.BlockSpec((None, SUBLANES, tn), lambda l, j: (l, 0, j)),
        out_shape=jax.ShapeDtypeStruct((depth, SUBLANES, n), F32),
        compiler_params=_params(("parallel", "parallel")),
        name="ada_mod",
    )(c8, ada_w, ada_b.reshape(depth, 1, n))


EV_SHIFTED = 3072


def _even_in_body(nt, c_ref, x_ref, xp_ref, xn_ref, g_ref, sh_ref, sc_ref, w_ref,
                  conv_ref, gb_ref, mu_ref, w0_ref, w2_ref, a0_ref, a2_ref, g2_ref,
                  kk_ref, ka_ref, rk_ref,
                  ovo_ref, oqk_ref, og_ref, ogt_ref, orws_ref, olw_ref, okb_ref, obg_ref):
    i = pl.program_id(1)
    has_prev = (i > 1).astype(F32)
    has_next = jnp.logical_and(i != 0, i != nt - 1).astype(F32)
    tm = x_ref.shape[0]
    ext = tm + 2 * HALO
    row = lax.broadcasted_iota(jnp.int32, (tm, 1), 0)
    keep_dn = jnp.where(row == 0, has_prev, 1.0)
    keep_up = jnp.where(row == tm - 1, has_next, 1.0)

    def normed(xx):
        return _bf(_rms(xx, g_ref[...]) * (1.0 + sc_ref[...]) + sh_ref[...])

    h = normed(jnp.where(i == 0, c_ref[...], x_ref[...]))
    ovo_ref[...] = _dot(h, w_ref[:, EV_SHIFTED:]).astype(ovo_ref.dtype)
    hcat = jnp.concatenate([normed(xp_ref[...]), h, normed(xn_ref[...])], axis=0)
    proj = _dot(hcat, w_ref[:, 0:EV_SHIFTED])

    def neighbours(cols):
        blk = proj[:, cols]
        dn = pltpu.roll(blk, 1, 0)[HALO:HALO + tm] * keep_dn
        up = pltpu.roll(blk, ext - 1, 0)[HALO:HALO + tm] * keep_up
        return blk[HALO:HALO + tm], dn, up

    qk, dn, up = neighbours(slice(2048, 3072))
    oqk_ref[...] = (dn * conv_ref[0:1, :] + qk * conv_ref[1:2, :] + up * conv_ref[2:3, :]).astype(oqk_ref.dtype)

    gv = proj[HALO:HALO + tm, RW_FEAT:RW_FEAT + LANES] + gb_ref[...]
    lane = lax.broadcasted_iota(jnp.int32, (1, LANES), 1)
    is_f = jnp.logical_and(jnp.bitwise_and(jnp.right_shift(lane, 2), 1) == 1, lane < 16)
    gates = jnp.where(is_f, _log_sigmoid(gv), gv)
    og_ref[...] = gates
    ogt_ref[...] = gates.T[0:4 * ML_HEADS, :]

    cur, dn, up = neighbours(slice(0, RW_FEAT))
    rf = cur + (0.5 * (dn + up) - cur) * mu_ref[...]
    w = RW_WIDTH
    r, k, v = rf[:, 0:w], rf[:, w:2 * w], rf[:, 2 * w:3 * w]
    wl = rf[:, 3 * w:3 * w + LANES]
    al = rf[:, 3 * w + LANES:3 * w + 2 * LANES]
    gl = rf[:, 3 * w + 2 * LANES:3 * w + 3 * LANES]
    kk = k * kk_ref[...]
    kk = kk * lax.rsqrt(_group_mean(kk * kk, RW_HEAD_DIM) * RW_HEAD_DIM + 1e-12)
    twl = jnp.tanh(wl)
    kd_sum = jnp.zeros_like(k)
    for d in range(2):
        w_log = -_softplus(-(w0_ref[d:d + 1, :] + _dot3(twl, w2_ref[d]))) - 0.5
        a = _sigmoid(a0_ref[d:d + 1, :] + _dot3(al, a2_ref[d]))
        kd = k * (1.0 + (a - 1.0) * ka_ref[...])
        kd_sum = kd_sum + kd
        olw_ref[d] = -jnp.exp(w_log)
        okb_ref[d, :, 0:w] = kd.astype(okb_ref.dtype)
        okb_ref[d, :, w:2 * w] = (kk * a).astype(okb_ref.dtype)
    orws_ref[:, 0:w] = r.astype(orws_ref.dtype)
    orws_ref[:, w:2 * w] = v.astype(orws_ref.dtype)
    orws_ref[:, 2 * w:3 * w] = (-kk).astype(orws_ref.dtype)
    bonus = _group_mean(r * kd_sum * rk_ref[...], RW_HEAD_DIM) * RW_HEAD_DIM * v
    obg_ref[:, 0:w] = bonus.astype(obg_ref.dtype)
    obg_ref[:, w:2 * w] = _dot3(_sigmoid(gl), g2_ref[...]).astype(obg_ref.dtype)


def _even_in(ctx, x, g, mods, w_in, conv, gate_b, mu, w0, w2cat, a0, a2cat, g2, kk, ka, rk):
    b, n, d = x.shape
    tm = ROW_TILE
    assert ctx.shape[1] == tm
    t = n + tm
    nt = t // tm
    rh = tm // HALO
    last_h = n // HALO - 1
    seg = lambda i: (i >= 1).astype(jnp.int32)
    cur = lambda c: (lambda bb, i: (bb, i, c))
    lat = lambda i: jnp.maximum(i - 1, 0)
    w = RW_WIDTH
    return pl.pallas_call(
        functools.partial(_even_in_body, nt),
        grid=(b, nt),
        in_specs=[
            pl.BlockSpec((None, tm, d), lambda bb, i: (bb, 0, 0)),
            pl.BlockSpec((None, tm, d), lambda bb, i: (bb, lat(i), 0)),
            pl.BlockSpec((None, HALO, d), lambda bb, i: (bb, jnp.maximum(lat(i) * rh - 1, 0), 0)),
            pl.BlockSpec((None, HALO, d), lambda bb, i: (bb, jnp.minimum((lat(i) + 1) * rh, last_h), 0)),
            _const_spec((1, d)), _mod_spec(0, seg), _mod_spec(1, seg), _const_spec(w_in.shape),
            _const_spec(conv.shape), _const_spec(gate_b.shape), _const_spec(mu.shape),
            _const_spec(w0.shape), _const_spec(w2cat.shape), _const_spec(a0.shape),
            _const_spec(a2cat.shape), _const_spec(g2.shape), _const_spec(kk.shape),
            _const_spec(ka.shape), _const_spec(rk.shape),
        ],
        out_specs=[
            pl.BlockSpec((None, tm, 1024), cur(0)),
            pl.BlockSpec((None, tm, 1024), cur(0)),
            pl.BlockSpec((None, tm, LANES), cur(0)),
            pl.BlockSpec((None, 4 * ML_HEADS, tm), lambda bb, i: (bb, 0, i)),
            pl.BlockSpec((None, tm, 3 * w), cur(0)),
            pl.BlockSpec((2, None, tm, w), lambda bb, i: (0, bb, i, 0)),
            pl.BlockSpec((2, None, tm, 2 * w), lambda bb, i: (0, bb, i, 0)),
            pl.BlockSpec((None, tm, 2 * w), cur(0)),
        ],
        out_shape=[
            jax.ShapeDtypeStruct((b, t, 1024), BF16),
            jax.ShapeDtypeStruct((b, t, 1024), BF16),
            jax.ShapeDtypeStruct((b, t, LANES), F32),
            jax.ShapeDtypeStruct((b, 4 * ML_HEADS, t), F32),
            jax.ShapeDtypeStruct((b, t, 3 * w), BF16),
            jax.ShapeDtypeStruct((2, b, t, w), F32),
            jax.ShapeDtypeStruct((2, b, t, 2 * w), BF16),
            jax.ShapeDtypeStruct((b, t, 2 * w), BF16),
        ],
        compiler_params=_params(("parallel", "parallel")),
        name="even_in",
    )(ctx, x, x, x, g, mods, mods, w_in, conv, gate_b, mu, w0, w2cat, a0, a2cat, g2, kk, ka, rk)


def _mlstm_body(q_ref, k_ref, v_ref, gc_ref, gr_ref, o_ref, c_scr, n_scr, m_scr):
    d = pl.program_id(0)
    j = pl.program_id(1)
    n_batch, n_rows = q_ref.shape[0], q_ref.shape[1]

    @pl.when(j == 0)
    def _():
        c_scr[...] = jnp.zeros_like(c_scr)
        n_scr[...] = jnp.zeros_like(n_scr)
        m_scr[...] = jnp.zeros_like(m_scr)

    mask = _order_mask(n_rows, 1 - 2 * d, strict=False)
    mf = mask.astype(BF16)
    gc = [gc_ref[bi] for bi in range(n_batch)]
    gr = [gr_ref[bi] for bi in range(n_batch)]
    b_col = [_sel_dot(mf, x) for x in gc]
    b_row = [_dot_sel_nt(x, mf) for x in gr]
    row_id = lax.broadcasted_iota(jnp.int32, (n_rows, 1), 0)
    is_last = row_id == (n_rows - 1) * (1 - d)
    units = [(bi, h) for bi in range(n_batch) for h in range(ML_HEADS)]
    idx = range(len(units))
    sls = [slice(h * ML_DIM, (h + 1) * ML_DIM) for _, h in units]
    fcol = [ML_HEADS + h for _, h in units]
    pick_row = lax.broadcasted_iota(jnp.int32, (LANES, LANES), 0)
    pick = [(pick_row == 2 * ML_HEADS * d + c).astype(BF16) for c in range(2 * ML_HEADS)]
    ones_cols = jnp.ones((n_rows, LANES), BF16)
    q = [_bf(q_ref[bi, :, sls[u]]) for u, (bi, _) in enumerate(units)]
    kf = [k_ref[bi, :, sls[u]].astype(F32) * (ML_DIM ** -0.5) for u, (bi, _) in enumerate(units)]
    k = [_bf(x) for x in kf]
    vf = [v_ref[bi, :, sls[u]].astype(F32) for u, (bi, _) in enumerate(units)]
    c_prev = [c_scr[u] for u in idx]
    n_prev = [n_scr[u][0:1, :] for u in idx]
    m_prev = [m_scr[u][0:1, :] for u in idx]
    qk = [_dot_nt(q[u], k[u]) for u in idx]
    qc = [_dot_nt(q[u], jnp.concatenate([_bf(c_prev[u]), _bf(jnp.broadcast_to(n_prev[u], (LANES, ML_DIM)))], axis=0))
          for u in idx]
    ic = [_dot_sel(gc[bi], pick[h]) for bi, h in units]
    bc = [_dot_sel(b_col[bi], pick[fcol[u]]) for u, (bi, _) in enumerate(units)]
    b_tot = [jnp.sum(jnp.where(is_last, bc[u], 0.0), axis=0, keepdims=True) for u in idx]
    logw = [b_tot[u] - bc[u] + ic[u] for u in idx]
    m_new = [jnp.maximum(b_tot[u] + m_prev[u], jnp.max(logw[u], axis=0, keepdims=True)) for u in idx]
    wgt = [jnp.exp(logw[u] - m_new[u]) for u in idx]
    decay = [jnp.exp(b_tot[u] + m_prev[u] - m_new[u]) for u in idx]
    kv = [_dot_tn(_bf(vf[u] * wgt[u]), k[u]) for u in idx]
    s, m_t, w_inter = [], [], []
    for u, (bi, h) in enumerate(units):
        logd = jnp.where(mask, bc[u] - b_row[bi][fcol[u]:fcol[u] + 1, :] + gr[bi][h:h + 1, :], -1e30)
        m_inter = bc[u] + m_prev[u]
        m_t.append(jnp.maximum(m_inter, jnp.max(logd, axis=-1, keepdims=True)))
        s.append(qk[u] * jnp.exp(logd - m_t[u]))
        w_inter.append(jnp.exp(m_inter - m_t[u]))
    sv = [_dot(_bf(s[u]), jnp.concatenate([_bf(vf[u]), ones_cols], axis=1)) for u in idx]
    for u, (bi, _) in enumerate(units):
        num = sv[u][:, :ML_DIM] + w_inter[u] * qc[u][:, :ML_DIM]
        den = sv[u][:, ML_DIM:] + w_inter[u] * qc[u][:, ML_DIM:]
        o_ref[bi, :, sls[u]] = num / jnp.maximum(jnp.abs(den), jnp.exp(-m_t[u]))
    for u in idx:
        c_scr[u] = decay[u] * c_prev[u] + kv[u]
        n_scr[u] = jnp.broadcast_to(decay[u] * n_prev[u] + jnp.sum(wgt[u] * kf[u], axis=0, keepdims=True),
                                    (SUBLANES, ML_DIM))
        m_scr[u] = jnp.broadcast_to(m_new[u], (SUBLANES, LANES))


def _mlstm(qk, proj, gates, gates_t, n_ctx):
    b, t, _ = qk.shape
    n_rows = ML_CHUNK
    nc, nctx = t // n_rows, n_ctx // n_rows
    ci = lambda d, j: _chunk_index(d, j, nctx, nc)
    return pl.pallas_call(
        _mlstm_body,
        grid=(2, nc),
        in_specs=[
            pl.BlockSpec((b, n_rows, ML_WIDTH), lambda d, j: (0, ci(d, j), 0)),
            pl.BlockSpec((b, n_rows, ML_WIDTH), lambda d, j: (0, ci(d, j), 1)),
            pl.BlockSpec((b, n_rows, ML_WIDTH), lambda d, j: (0, ci(d, j), 0)),
            pl.BlockSpec((b, n_rows, LANES), lambda d, j: (0, ci(d, j), 0)),
            pl.BlockSpec((b, 2 * ML_HEADS, n_rows), lambda d, j: (0, d, ci(d, j))),
        ],
        out_specs=pl.BlockSpec((None, b, n_rows, ML_WIDTH), lambda d, j: (d, 0, ci(d, j), 0)),
        out_shape=jax.ShapeDtypeStruct((2, b, t, ML_WIDTH), F32),
        scratch_shapes=[
            pltpu.VMEM((b * ML_HEADS, ML_DIM, ML_DIM), F32),
            pltpu.VMEM((b * ML_HEADS, SUBLANES, ML_DIM), F32),
            pltpu.VMEM((b * ML_HEADS, SUBLANES, LANES), F32),
        ],
        compiler_params=_params(("parallel", "arbitrary")),
        name="mlstm",
    )(qk, qk, proj, gates, gates_t)


def _rwkv_body(r_ref, v_ref, a_ref, lw_ref, k_ref, b_ref, o_ref, h_scr):
    d = pl.program_id(0)
    j = pl.program_id(1)
    n_rows = r_ref.shape[1]
    two = 2 * n_rows
    shift = n_rows.bit_length() - 1

    @pl.when(j == 0)
    def _():
        h_scr[...] = jnp.zeros_like(h_scr)

    sgn = 1 - 2 * d
    cum = _order_mask(n_rows, sgn, strict=False).astype(BF16)
    row = lax.broadcasted_iota(jnp.int32, (two, two), 0)
    col = lax.broadcasted_iota(jnp.int32, (two, two), 1)
    same = jnp.right_shift(row, shift) == jnp.right_shift(col, shift)
    diff = (jnp.bitwise_and(row, n_rows - 1) - jnp.bitwise_and(col, n_rows - 1)) * sgn
    m_strict = jnp.logical_and(same, diff > 0)
    m_incl = jnp.logical_and(same, diff >= 0)
    eye = (row == col).astype(F32)
    head0 = lax.broadcasted_iota(jnp.int32, (1, LANES), 1) < RW_HEAD_DIM

    def stack(x):
        return jnp.concatenate([jnp.where(head0, x, 0.0), jnp.where(head0, 0.0, x)], axis=0)

    def fold(x):
        return x[:n_rows] + x[n_rows:]

    n_batch = r_ref.shape[0]
    groups = [(bi, slice(p * LANES, (p + 1) * LANES)) for bi in range(n_batch) for p in range(RW_WIDTH // LANES)]
    x_s, r_s, v_s, bh_s, kh_s, bi_s, ki_s, e_tot = [], [], [], [], [], [], [], []
    for bi in range(n_batch):
        lw = lw_ref[bi]
        g = _sel_dot(cum, lw)
        g_tot = jnp.sum(lw, axis=0, keepdims=True)
        inv = jnp.exp(-g)
        to_end = jnp.exp(g_tot - g)
        at = a_ref[bi].astype(F32) * jnp.exp(g - lw)
        rt = r_ref[bi].astype(F32) * jnp.exp(g)
        b_in, k_in, vv = b_ref[bi].astype(F32), k_ref[bi].astype(F32), v_ref[bi].astype(F32)
        for gb, sl in groups:
            if gb != bi:
                continue
            x_s.append(stack(at[:, sl]))
            r_s.append(stack(rt[:, sl]))
            v_s.append(stack(vv[:, sl]))
            bh_s.append(stack((b_in * to_end)[:, sl]))
            kh_s.append(stack((k_in * to_end)[:, sl]))
            bi_s.append(stack((b_in * inv)[:, sl]))
            ki_s.append(stack((k_in * inv)[:, sl]))
            e_tot.append(jnp.exp(g_tot)[:, sl])
    big = [_mm_nt(jnp.concatenate([x_s[p], r_s[p]], axis=0), jnp.concatenate([bi_s[p], ki_s[p]], axis=0))
           for p in range(len(groups))]
    a_ab = [jnp.where(m_strict, m[:two, :two], 0.0) for m in big]
    a_rb = [jnp.where(m_incl, m[two:, :two], 0.0) for m in big]
    a_k = [jnp.concatenate([jnp.where(m_strict, m[:two, two:], 0.0), jnp.where(m_incl, m[two:, two:], 0.0)], axis=0)
           for m in big]
    av = [_mm(a_k[p], v_s[p]) for p in range(len(groups))]
    t_inv = [eye + m for m in a_ab]
    pw = [_mm(m, m) for m in a_ab]
    for level in range(1, shift):
        last = level == shift - 1
        res = [_mm(t if last else jnp.concatenate([t, m], axis=0), m) for t, m in zip(t_inv, pw)]
        t_inv = [t + r[:two] for t, r in zip(t_inv, res)]
        if not last:
            pw = [r[two:] for r in res]
    wu = [_mm(t_inv[p], jnp.concatenate([x_s[p], av[p][:two]], axis=1)) for p in range(len(groups))]
    rb = [_mm(a_rb[p], wu[p]) for p in range(len(groups))]
    upd = [_mm_tn(wu[p], bh_s[p]) for p in range(len(groups))]
    gt2 = [_mm_tn(v_s[p], kh_s[p]) for p in range(len(groups))]
    h_old = [h_scr[p] for p in range(len(groups))]
    for p, (bi, sl) in enumerate(groups):
        q_eff = fold(r_s[p] + rb[p][:, :LANES])
        y_in = fold(rb[p][:, LANES:] + av[p][two:])
        o_ref[bi, :, sl] = _mm_nt(q_eff, h_old[p]) + y_in
    for p in range(len(groups)):
        h_scr[p] = e_tot[p] * h_old[p] + _mm(h_old[p], upd[p][:LANES]) + upd[p][LANES:] + gt2[p]


def _rwkv(rws, lw, kb, n_ctx):
    b, t, _ = rws.shape
    n_rows = RW_CHUNK
    nc, nctx = t // n_rows, n_ctx // n_rows
    ci = lambda d, j: _chunk_index(d, j, nctx, nc)
    w = RW_WIDTH
    shared = lambda c: pl.BlockSpec((b, n_rows, w), lambda d, j: (0, ci(d, j), c))
    per_dir = lambda c: pl.BlockSpec((None, b, n_rows, w), lambda d, j: (d, 0, ci(d, j), c))
    return pl.pallas_call(
        _rwkv_body,
        grid=(2, nc),
        in_specs=[shared(0), shared(1), shared(2), per_dir(0), per_dir(0), per_dir(1)],
        out_specs=pl.BlockSpec((None, b, n_rows, w), lambda d, j: (d, 0, ci(d, j), 0)),
        out_shape=jax.ShapeDtypeStruct((2, b, t, w), F32),
        scratch_shapes=[pltpu.VMEM((b * w // LANES, LANES, LANES), F32)],
        compiler_params=_params(("parallel", "arbitrary")),
        name="rwkv7",
    )(rws, rws, rws, lw, kb, kb)


def _tail(x, mix, outw_ref, w1_ref, w2_ref, ng_ref, ga_ref, shm_ref, scm_ref, gm_ref):
    x1 = x + ga_ref[...] * _rms(_dot(_bf(mix), outw_ref[...]), ng_ref[1:2, :])
    h = _bf(_rms(x1, ng_ref[2:3, :]) * (1.0 + scm_ref[...]) + shm_ref[...])
    chunks = [slice(c * FF_CHUNK, (c + 1) * FF_CHUNK) for c in range(D_FF // FF_CHUNK)]
    u = [jnp.maximum(_dot(h, w1_ref[:, cols]), 0.0) for cols in chunks]
    dd = _dot(_bf(u[0] * u[0]), w2_ref[chunks[0], :])
    for uc, cols in zip(u[1:], chunks[1:]):
        dd = dd + _dot(_bf(uc * uc), w2_ref[cols, :])
    return x1 + gm_ref[...] * _rms(dd, ng_ref[3:4, :])


def _even_post_body(c_ref, x_ref, hm_ref, mo_ref, yrw_ref, bg_ref, mln_ref, lnw_ref, lnb_ref,
                    outw_ref, w1_ref, w2_ref, ng_ref, ga_ref, shm_ref, scm_ref, gm_ref, o_ref):
    w = RW_WIDTH
    x_in = jnp.where(pl.program_id(1) == 0, c_ref[...], x_ref[...])
    ym = hm_ref[0] + hm_ref[1]
    ym = (ym * lax.rsqrt(_group_mean(ym * ym, ML_DIM) + EPS) * mln_ref[...]
          * _sigmoid(mo_ref[...].astype(F32)))
    y = yrw_ref[0] + yrw_ref[1]
    yc = y - _group_mean(y, RW_HEAD_DIM)
    y = (yc * lax.rsqrt(_group_mean(yc * yc, RW_HEAD_DIM) + RW_GN_EPS) * lnw_ref[...] + lnb_ref[...]
         + bg_ref[:, 0:w].astype(F32))
    mix = jnp.concatenate([ym, y * bg_ref[:, w:2 * w].astype(F32)], axis=-1)
    o_ref[...] = _tail(x_in, mix, outw_ref, w1_ref, w2_ref, ng_ref, ga_ref, shm_ref, scm_ref, gm_ref)


def _tail_specs(seg, layer):
    d = D_MODEL
    pick = lambda *shape: pl.BlockSpec((None,) + shape, lambda *_: (layer, 0, 0), pipeline_mode=pl.Buffered(1))
    return [
        pick(d, d), pick(d, D_FF), pick(D_FF, d), pick(4, d),
        _mod_spec(2, seg), _mod_spec(3, seg), _mod_spec(4, seg), _mod_spec(5, seg),
    ]


def _even_post(ctx, x, hm, proj, yrw, bg, mln, lnw, lnb, outw, w1, w2, ng, mods, layer):
    b, n, d = x.shape
    tm = ROW_TILE
    assert ctx.shape[1] == tm
    t = n + tm
    seg = lambda i: (i >= 1).astype(jnp.int32)
    w = RW_WIDTH
    return pl.pallas_call(
        _even_post_body,
        grid=(b, t // tm),
        in_specs=[
            pl.BlockSpec((None, tm, d), lambda bb, i: (bb, 0, 0)),
            pl.BlockSpec((None, tm, d), lambda bb, i: (bb, jnp.maximum(i - 1, 0), 0)),
            pl.BlockSpec((2, None, tm, w), lambda bb, i: (0, bb, i, 0)),
            pl.BlockSpec((None, tm, w), lambda bb, i: (bb, i, 1)),
            pl.BlockSpec((2, None, tm, w), lambda bb, i: (0, bb, i, 0)),
            pl.BlockSpec((None, tm, 2 * w), lambda bb, i: (bb, i, 0)),
            _const_spec((1, w)), _const_spec((1, w)), _const_spec((1, w)),
        ] + _tail_specs(seg, layer),
        out_specs=pl.BlockSpec((None, tm, d), lambda bb, i: (bb, i, 0)),
        out_shape=jax.ShapeDtypeStruct((b, t, d), F32),
        compiler_params=_params(("parallel", "parallel")),
        name="even_post",
    )(ctx, x, hm, proj, yrw, bg, mln, lnw, lnb, outw, w1, w2, ng, mods, mods, mods, mods)


def _hgrn_body(q_ref, f_ref, i_ref, lb_ref, o_ref, s_scr):
    d = pl.program_id(0)
    j = pl.program_id(1)
    n_batch, n_rows = q_ref.shape[0], q_ref.shape[1]
    n_sub = n_rows // HG_CHUNK

    @pl.when(j == 0)
    def _():
        s_scr[...] = jnp.zeros_like(s_scr)

    mask = _order_mask(HG_CHUNK, 1 - 2 * d, strict=False)
    mf = mask.astype(BF16)
    lb = lb_ref[...]
    heads = range(HG_HEADS)
    sls = [slice(h * HG_DIM, (h + 1) * HG_DIM) for h in heads]
    units = [(bi, h) for bi in range(n_batch) for h in heads]
    rows, q_in, e_tot, q_mids, k_mids, k_outs, vvs = [], [], [], [], [], [], []
    for sc in range(n_sub):
        off = pl.multiple_of(jnp.where(d == 0, sc, n_sub - 1 - sc) * HG_CHUNK, HG_CHUNK)
        rows.append(pl.ds(off, HG_CHUNK))
        for lst in (q_in, e_tot, q_mids, k_mids, k_outs, vvs):
            lst.append([])
        for bi in range(n_batch):
            qa = q_ref[bi, rows[sc], :].astype(F32)
            pre = f_ref[bi, rows[sc], :].astype(F32)
            vv = i_ref[bi, rows[sc], :]
            q = qa * _sigmoid(qa) * (HG_DIM ** -0.5)
            e_abs = jnp.exp(-jnp.abs(pre))
            big = 1.0 / (1.0 + e_abs)
            small = e_abs * big
            log_f = jnp.log(lb + (1.0 - lb) * jnp.where(pre >= 0.0, big, small))
            kf = (1.0 - lb) * jnp.where(pre >= 0.0, small, big)
            bc = _sel_dot(mf, log_f)
            tot = jnp.sum(log_f, axis=0, keepdims=True)
            half = 0.5 * tot
            e_half = jnp.exp(half)
            q_mid_f = q * jnp.exp(bc - half)
            k_mid_f = kf * jnp.exp(half - bc)
            q_mid = _bf(q_mid_f)
            k_mid = _bf(k_mid_f)
            k_out = _bf(k_mid_f * e_half)
            q_all = _bf(q_mid_f * e_half)
            e_all = e_half * e_half
            q_mids[sc] += [q_mid[:, sl] for sl in sls]
            k_mids[sc] += [k_mid[:, sl] for sl in sls]
            k_outs[sc] += [k_out[:, sl] for sl in sls]
            vvs[sc] += [vv[:, sl] for sl in sls]
            q_in[sc] += [q_all[:, sl] for sl in sls]
            e_tot[sc] += [e_all[:, sl] for sl in sls]
    n_units = len(units)
    att = [[_dot_nt(q_mids[sc][u], k_mids[sc][u]) for u in range(n_units)] for sc in range(n_sub)]
    kv = [[_dot_tn(vvs[sc][u], k_outs[sc][u]) for u in range(n_units)] for sc in range(n_sub)]
    intra = [[_dot(_bf(jnp.where(mask, att[sc][u], 0.0)), vvs[sc][u]) for u in range(n_units)] for sc in range(n_sub)]
    s_t = [s_scr[u] for u in range(len(units))]
    for sc in range(n_sub):
        for u, (bi, h) in enumerate(units):
            o_ref[bi, rows[sc], sls[h]] = intra[sc][u] + _dot_nt(q_in[sc][u], _bf(s_t[u]))
        s_t = [e_tot[sc][u] * s_t[u] + kv[sc][u] for u in range(len(units))]
    for u in range(len(units)):
        s_scr[u] = s_t[u]


def _hgrn(proj, lb, n_ctx):
    b, t, _ = proj.shape
    n_rows = HG_BLOCK
    nc, nctx = t // n_rows, n_ctx // n_rows
    ci = lambda d, j: _chunk_index(d, j, nctx, nc)
    w = HG_WIDTH
    return pl.pallas_call(
        _hgrn_body,
        grid=(2, nc),
        in_specs=[
            pl.BlockSpec((b, n_rows, w), lambda d, j: (0, ci(d, j), 0)),
            pl.BlockSpec((b, n_rows, w), lambda d, j: (0, ci(d, j), 1 + d)),
            pl.BlockSpec((b, n_rows, w), lambda d, j: (0, ci(d, j), 3)),
            _const_spec((1, w)),
        ],
        out_specs=pl.BlockSpec((None, b, n_rows, w), lambda d, j: (d, 0, ci(d, j), 0)),
        out_shape=jax.ShapeDtypeStruct((2, b, t, w), F32),
        scratch_shapes=[pltpu.VMEM((b * HG_HEADS, HG_DIM, HG_DIM), F32)],
        compiler_params=_params(("parallel", "arbitrary")),
        name="hgrn2",
    )(proj, proj, proj, lb)


V_ROWS = MLA_V + SUBLANES
OD_MAIN = 5 * HG_WIDTH


def _odd_in_body(x_ref, g_ref, sh_ref, sc_ref, w_ref, cos_ref, sin_ref, kvn_ref, qn_ref,
                 wk_ref, wvt_ref, wq1_ref, wq2_ref, proj_ref, k_ref, vt_ref, q_ref):
    h = _bf(_rms(x_ref[...], g_ref[...]) * (1.0 + sc_ref[...]) + sh_ref[...])
    proj = _dot(h, w_ref[...])
    proj_ref[...] = proj[:, 0:OD_MAIN].astype(proj_ref.dtype)
    cq = proj[:, OD_MAIN:OD_MAIN + MLA_Q_RANK]
    base = OD_MAIN + MLA_Q_RANK
    ckv = proj[:, base:base + MLA_KV_RANK]
    cos, sin = cos_ref[...], sin_ref[...]
    k_rope = proj[:, base + LANES:base + 2 * LANES] * cos + proj[:, base + 2 * LANES:base + 3 * LANES] * sin
    kvn = _bf(_rms(ckv, kvn_ref[...]))
    kn = _dot(kvn, wk_ref[...])
    ones = jnp.ones((SUBLANES, proj.shape[0]), BF16)
    cn = _bf(_rms(cq, qn_ref[...]))
    qa = _dot(cn, wq1_ref[...])
    qb = _dot(cn, wq2_ref[...])
    scale = (MLA_NOPE + MLA_ROPE) ** -0.5 * float(np.log2(np.e))
    for hd in range(MLA_HEADS):
        k_ref[hd] = jnp.concatenate([kn[:, hd * MLA_NOPE:(hd + 1) * MLA_NOPE], k_rope], axis=-1).astype(BF16)
        vt = _dot_nt(wvt_ref[hd], kvn)
        vt_ref[hd] = jnp.concatenate([vt.astype(BF16), ones], axis=0)
        a = qa[:, hd * 256:(hd + 1) * 256]
        q_rope = a[:, MLA_NOPE:] * cos + qb[:, hd * LANES:(hd + 1) * LANES] * sin
        q_ref[hd] = (jnp.concatenate([a[:, 0:MLA_NOPE], q_rope], axis=-1) * scale).astype(BF16)


def _odd_in(xall, g, mods, w_in, nctx_t, cos, sin, kv_norm, q_norm, wk, wvt, wq1, wq2):
    b, t, d = xall.shape
    tm = ROW_TILE
    n = t - nctx_t * tm
    seg = lambda i: (i >= nctx_t).astype(jnp.int32)
    q_blk = lambda bb, i: (bb, 0, jnp.maximum(i - nctx_t, 0), 0)
    return pl.pallas_call(
        _odd_in_body,
        grid=(b, t // tm),
        in_specs=[
            pl.BlockSpec((None, tm, d), lambda bb, i: (bb, i, 0)),
            _const_spec((1, d)), _mod_spec(0, seg), _mod_spec(1, seg), _const_spec(w_in.shape),
            pl.BlockSpec((tm, LANES), lambda bb, i: (i, 0)),
            pl.BlockSpec((tm, LANES), lambda bb, i: (i, 0)),
            _const_spec((1, MLA_KV_RANK)), _const_spec((1, MLA_Q_RANK)),
            _const_spec(wk.shape), _const_spec(wvt.shape), _const_spec(wq1.shape), _const_spec(wq2.shape),
        ],
        out_specs=[
            pl.BlockSpec((None, tm, OD_MAIN), lambda bb, i: (bb, i, 0)),
            pl.BlockSpec((None, MLA_HEADS, tm, 256), lambda bb, i: (bb, 0, i, 0)),
            pl.BlockSpec((None, MLA_HEADS, V_ROWS, tm), lambda bb, i: (bb, 0, 0, i)),
            pl.BlockSpec((None, MLA_HEADS, tm, 256), q_blk),
        ],
        out_shape=[
            jax.ShapeDtypeStruct((b, t, OD_MAIN), BF16),
            jax.ShapeDtypeStruct((b, MLA_HEADS, t, 256), BF16),
            jax.ShapeDtypeStruct((b, MLA_HEADS, V_ROWS, t), BF16),
            jax.ShapeDtypeStruct((b, MLA_HEADS, n, 256), BF16),
        ],
        compiler_params=_params(("parallel", "arbitrary")),
        name="odd_in",
    )(xall, g, mods, mods, w_in, cos, sin, kv_norm, q_norm, wk, wvt, wq1, wq2)


def _attn_body(q_ref, k_ref, vt_ref, o_ref):
    k = k_ref[...]
    vt = vt_ref[...]
    subs = [slice(i * ATTN_SUB, (i + 1) * ATTN_SUB) for i in range(q_ref.shape[0] // ATTN_SUB)]
    st = [_dot_nt(k, q_ref[sl, :]) for sl in subs]
    pt = [_bf(jnp.exp2(s - jnp.max(s, axis=0, keepdims=True))) for s in st]
    ot = [_dot(vt, p) for p in pt]
    for sl, o in zip(subs, ot):
        o_ref[sl, :] = (o[0:MLA_V] / o[MLA_V:MLA_V + 1]).T


def _attention(q, k, vt):
    b, h, n, dq = q.shape
    t = k.shape[2]
    tq = ATTN_TQ
    return pl.pallas_call(
        _attn_body,
        grid=(b, h, n // tq),
        in_specs=[
            pl.BlockSpec((None, None, tq, dq), lambda bb, hh, i: (bb, hh, i, 0)),
            pl.BlockSpec((None, None, t, dq), lambda bb, hh, i: (bb, hh, 0, 0)),
            pl.BlockSpec((None, None, V_ROWS, t), lambda bb, hh, i: (bb, hh, 0, 0)),
        ],
        out_specs=pl.BlockSpec((None, tq, MLA_V), lambda bb, hh, i: (bb, i, hh)),
        out_shape=jax.ShapeDtypeStruct((b, n, h * MLA_V), F32),
        compiler_params=_params(("parallel", "parallel", "parallel")),
        name="mla_attention",
    )(q, k, vt)


def _odd_post_body(x_ref, o_ref_in, hg_ref, att_ref, hgn_ref,
                   outw_ref, w1_ref, w2_ref, ng_ref, ga_ref, shm_ref, scm_ref, gm_ref, o_ref):
    y = o_ref_in[0, 0] + o_ref_in[1, 0]
    g = hg_ref[0].astype(F32)
    y = y * lax.rsqrt(_group_mean(y * y, HG_DIM) + EPS) * hgn_ref[...] * (g * _sigmoid(g))
    mix = jnp.concatenate([y, att_ref[...]], axis=-1)
    o_ref[...] = _tail(x_ref[0], mix, outw_ref, w1_ref, w2_ref, ng_ref, ga_ref, shm_ref, scm_ref, gm_ref)


def _odd_post(xall, ohg, proj, att, hgn, outw, w1, w2, ng, mods, n_ctx, layer):
    b, t, d = xall.shape
    tm = LAT_TILE
    n = t - n_ctx
    seg = lambda i: 1
    w = HG_WIDTH
    el = lambda *shape: tuple(pl.Element(s) for s in shape)
    row0 = lambda i: pl.multiple_of(n_ctx + i * tm, int(np.gcd(n_ctx, tm)))
    return pl.pallas_call(
        _odd_post_body,
        grid=(b, n // tm),
        in_specs=[
            pl.BlockSpec(el(1, tm, d), lambda bb, i: (bb, row0(i), 0)),
            pl.BlockSpec(el(2, 1, tm, w), lambda bb, i: (0, bb, row0(i), 0)),
            pl.BlockSpec(el(1, tm, w), lambda bb, i: (bb, row0(i), 4 * w)),
            pl.BlockSpec((None, tm, w), lambda bb, i: (bb, i, 0)),
            _const_spec((1, w)),
        ] + _tail_specs(seg, layer),
        out_specs=pl.BlockSpec((None, tm, d), lambda bb, i: (bb, i, 0)),
        out_shape=jax.ShapeDtypeStruct((b, n, d), F32),
        compiler_params=_params(("parallel", "parallel")),
        name="odd_post",
    )(xall, ohg, proj, att, hgn, outw, w1, w2, ng, mods, mods, mods, mods)


def _rope_tables(n, n_ctx):
    quarter = MLA_ROPE // 4
    inv = ROPE_BASE ** (-jnp.arange(quarter, dtype=F32) / quarter)
    rows = n // GRID_W
    ang_r = jnp.arange(rows, dtype=F32)[:, None] * inv[None, :]
    ang_c = jnp.arange(GRID_W, dtype=F32)[:, None] * inv[None, :]
    by_row = lambda a: jnp.repeat(a, GRID_W, axis=0)
    by_col = lambda a: jnp.tile(a, (rows, 1))
    cos_r, sin_r, cos_c, sin_c = by_row(jnp.cos(ang_r)), by_row(jnp.sin(ang_r)), by_col(jnp.cos(ang_c)), by_col(jnp.sin(ang_c))
    cos = jnp.concatenate([cos_r, cos_r, cos_c, cos_c], axis=-1)
    sin = jnp.concatenate([-sin_r, sin_r, -sin_c, sin_c], axis=-1)
    cos = jnp.concatenate([jnp.ones((n_ctx, MLA_ROPE), F32), cos], axis=0)
    sin = jnp.concatenate([jnp.zeros((n_ctx, MLA_ROPE), F32), sin], axis=0)
    pad = lambda a: jnp.pad(a, ((0, 0), (0, LANES - MLA_ROPE)))
    return pad(cos), pad(sin)


_ROPE_SWAP = np.concatenate([np.arange(16, 32), np.arange(0, 16), np.arange(48, 64), np.arange(32, 48)])


def kernel(x, c, ctx, c_ctx, ada_w, ada_b, norm_g, out_w, mlp_w1, mlp_w2, hg_lb, ev_w_in, ml_conv, ml_gate_b, ml_norm, rw_mu, rw_w0, rw_w2, rw_a0, rw_a2, rw_g2, rw_kk, rw_ka, rw_rk, rw_ln_w, rw_ln_b, od_w_in, hg_norm, mla_q_norm, mla_w_qb, mla_kv_norm, mla_w_kvb):
    b, n, d = x.shape
    n_ctx = ctx.shape[1]
    t = n_ctx + n
    assert d == D_MODEL and b + 1 <= SUBLANES
    assert n_ctx % ROW_TILE == 0 and n % ROW_TILE == 0 and n % ATTN_TQ == 0
    assert n_ctx % ML_CHUNK == 0 and n % ML_CHUNK == 0 and n_ctx % HG_BLOCK == 0 and n % HG_BLOCK == 0
    nctx_t = n_ctx // ROW_TILE
    row1 = lambda a: a.reshape(1, -1)

    c8 = jnp.concatenate([c, c_ctx[None], jnp.zeros((SUBLANES - b - 1, d), F32)], axis=0)
    mod = _ada_mod(c8, ada_w, ada_b).reshape(-1, SUBLANES, N_MOD, d)

    def layer_mods(l):
        lat = mod[l, :b]
        cm = jnp.broadcast_to(mod[l, b][None], lat.shape)
        return jnp.stack([cm, lat], axis=1)[:, :, :, None, :]

    outw_bf, w1_bf, w2_bf = _bf(out_w), _bf(mlp_w1), _bf(mlp_w2)

    l, e = 0, 0
    mods = layer_mods(l)
    wi = ev_w_in[e]
    w_pad = jnp.concatenate([wi[:, 2064:3984], wi[:, 2048:2064], jnp.zeros((d, LANES - 16), F32), wi[:, 0:2048]], axis=1)
    gate_b = jnp.pad(ml_gate_b[e].reshape(1, 16), ((0, 0), (0, LANES - 16)))
    zeros64 = jnp.zeros((64, RW_WIDTH), F32)
    w2cat = jnp.stack([jnp.concatenate([rw_w2[e, 0], zeros64], 0), jnp.concatenate([zeros64, rw_w2[e, 1]], 0)])
    a2cat = jnp.stack([jnp.concatenate([rw_a2[e, 0], zeros64], 0), jnp.concatenate([zeros64, rw_a2[e, 1]], 0)])
    proj, qk, gates, gates_t, rws, rw_lw, rw_kb, bg = _even_in(
        ctx, x, row1(norm_g[l, 0]), mods, _bf(w_pad),
        ml_conv[e], gate_b, row1(rw_mu[e]), rw_w0[e], w2cat, rw_a0[e], a2cat, rw_g2[e],
        row1(rw_kk[e]), row1(rw_ka[e]), row1(rw_rk[e]))

    hm = _mlstm(qk, proj, gates, gates_t, n_ctx)
    yrw = _rwkv(rws, rw_lw, rw_kb, n_ctx)
    xall = _even_post(ctx, x, hm, proj, yrw, bg, row1(ml_norm[e]), row1(rw_ln_w[e]), row1(rw_ln_b[e]),
                      outw_bf, w1_bf, w2_bf, norm_g, mods, l)

    l, o = 1, 0
    mods = layer_mods(l)
    lb_all = jax.nn.softmax(hg_lb.astype(F32), axis=0)
    lb_all = jnp.cumsum(lb_all, axis=0) - lb_all[0]
    wi = od_w_in[o]
    kr = wi[:, 2944:3008]
    z = lambda k: jnp.zeros((d, k), F32)
    w_pad = jnp.concatenate([wi[:, 0:2944], kr, z(64), kr[:, _ROPE_SWAP], z(64)], axis=1)
    cos, sin = _rope_tables(n, n_ctx)
    wkv = mla_w_kvb[o].reshape(MLA_KV_RANK, MLA_HEADS, MLA_NOPE + MLA_V)
    wk = wkv[:, :, :MLA_NOPE].reshape(MLA_KV_RANK, MLA_HEADS * MLA_NOPE)
    wvt = jnp.transpose(wkv[:, :, MLA_NOPE:], (1, 2, 0))
    wq = mla_w_qb[o].reshape(MLA_Q_RANK, MLA_HEADS, MLA_NOPE + MLA_ROPE)
    zq = lambda k: jnp.zeros((MLA_Q_RANK, MLA_HEADS, k), F32)
    wq1 = jnp.concatenate([wq, zq(64)], axis=-1).reshape(MLA_Q_RANK, -1)
    wq2 = jnp.concatenate([wq[:, :, MLA_NOPE:][:, :, _ROPE_SWAP], zq(64)], axis=-1).reshape(MLA_Q_RANK, -1)
    proj, k_all, v_all, q_all = _odd_in(
        xall, row1(norm_g[l, 0]), mods, _bf(w_pad), nctx_t, cos, sin, row1(mla_kv_norm[o]), row1(mla_q_norm[o]),
        _bf(wk), _bf(wvt), _bf(wq1), _bf(wq2))

    ohg = _hgrn(proj, row1(lb_all[l]), n_ctx)
    att = _attention(q_all, k_all, v_all)

    return _odd_post(xall, ohg, proj, att, row1(hg_norm[o]), outw_bf, w1_bf, w2_bf, norm_g, mods, n_ctx, l)
```

```python
import functools

import numpy as np
import jax
import jax.numpy as jnp
from jax import lax
from jax.experimental import pallas as pl
from jax.experimental.pallas import tpu as pltpu

F32 = jnp.float32
BF16 = jnp.bfloat16
HI = lax.Precision.HIGHEST

D_MODEL = 1024
GRID_W = 64
D_FF = 4 * D_MODEL
N_MOD = 6
EPS = 1e-6
ML_WIDTH = 512
ML_HEADS = 4
ML_DIM = 128
RW_WIDTH = 512
RW_HEAD_DIM = 64
RW_GN_EPS = 64e-5
RW_FEAT = 1920
HG_WIDTH = 512
HG_HEADS = 4
HG_DIM = 128
MLA_HEADS = 4
MLA_NOPE = 128
MLA_ROPE = 64
MLA_V = 128
MLA_Q_RANK = 256
MLA_KV_RANK = 128
ROPE_BASE = 10000.0

LANES = 128
SUBLANES = 8
HALO = 16
VMEM_LIMIT_BYTES = 56 * 2**20

ROW_TILE = 256
LAT_TILE = 512
FF_CHUNK = 1024
ML_CHUNK = 128
RW_CHUNK = 64
RW_BLOCK = 256
HG_BLOCK = 256
HG_CHUNK = 32
ATTN_TQ = 512
ATTN_SUB = 256


def _dot(a, b, prec=None):
    return jnp.dot(a, b, preferred_element_type=F32, precision=prec)


def _dot_nt(a, b, prec=None):
    return lax.dot_general(a, b, (((1,), (1,)), ((), ())), preferred_element_type=F32, precision=prec)


def _dot_tn(a, b, prec=None):
    return lax.dot_general(a, b, (((0,), (0,)), ((), ())), preferred_element_type=F32, precision=prec)


def _bf(x):
    return x.astype(BF16)


def _mm(a, b):
    return _dot(_bf(a), _bf(b))


def _mm_nt(a, b):
    return _dot_nt(_bf(a), _bf(b))


def _mm_tn(a, b):
    return _dot_tn(_bf(a), _bf(b))


def _split(x):
    hi = _bf(x)
    return hi, _bf(x - hi.astype(F32))


def _sel_dot(sel, x):
    hi, lo = _split(x)
    return _dot(sel, hi) + _dot(sel, lo)


def _dot_sel(x, sel):
    hi, lo = _split(x)
    return _dot(hi, sel) + _dot(lo, sel)


def _dot_sel_nt(x, sel):
    hi, lo = _split(x)
    return _dot_nt(hi, sel) + _dot_nt(lo, sel)


def _dot3(a, b):
    ah, al = _split(a)
    bh, bl = _split(b)
    return _dot(ah, bh) + _dot(ah, bl) + _dot(al, bh)


def _group_mean(x, group):
    low = lax.broadcasted_iota(jnp.int32, (1, LANES), 1) < group
    outs = []
    for c in range(x.shape[1] // LANES):
        blk = x[:, c * LANES:(c + 1) * LANES]
        total = jnp.sum(blk, axis=-1, keepdims=True)
        if group == LANES:
            outs.append(jnp.broadcast_to(total, blk.shape))
        else:
            first = jnp.sum(jnp.where(low, blk, 0.0), axis=-1, keepdims=True)
            outs.append(jnp.where(low, first, total - first))
    return jnp.concatenate(outs, axis=-1) * (1.0 / group)


def _sigmoid(x):
    return 1.0 / (1.0 + jnp.exp(-x))


def _log_sigmoid(x):
    return jnp.minimum(x, 0.0) - jnp.log(1.0 + jnp.exp(-jnp.abs(x)))


def _softplus(x):
    return jnp.maximum(x, 0.0) + jnp.log(1.0 + jnp.exp(-jnp.abs(x)))


def _rms(x, g):
    return x * lax.rsqrt(jnp.mean(x * x, axis=-1, keepdims=True) + EPS) * g


def _order_mask(n, sgn, strict):
    row = lax.broadcasted_iota(jnp.int32, (n, n), 0)
    col = lax.broadcasted_iota(jnp.int32, (n, n), 1)
    diff = (row - col) * sgn
    return diff > 0 if strict else diff >= 0


def _chunk_index(d, j, nctx, nc):
    bwd = jnp.where(j < nctx, nctx - 1 - j, nc - 1 - (j - nctx))
    return jnp.where(d == 0, j, bwd)


def _params(sem):
    return pltpu.CompilerParams(dimension_semantics=sem, vmem_limit_bytes=VMEM_LIMIT_BYTES)


def _const_spec(shape):
    nd = len(shape)
    return pl.BlockSpec(shape, lambda *_: (0,) * nd, pipeline_mode=pl.Buffered(1))


def _mod_spec(k, seg_of):
    return pl.BlockSpec((None, None, None, 1, D_MODEL), lambda b, i: (b, seg_of(i), k, 0, 0))


def _ada_body(c_ref, w_ref, b_ref, o_ref):
    c = c_ref[...]
    o_ref[...] = _dot(c * _sigmoid(c), w_ref[...], HI) + b_ref[...]


def _ada_mod(c8, ada_w, ada_b):
    depth, d, n = ada_w.shape
    tn = 1024
    return pl.pallas_call(
        _ada_body,
        grid=(depth, n // tn),
        in_specs=[
            _const_spec((SUBLANES, d)),
            pl.BlockSpec((None, d, tn), lambda l, j: (l, 0, j)),
            pl.BlockSpec((None, 1, tn), lambda l, j: (l, 0, j)),
        ],
        out_specs=pl.BlockSpec((None, SUBLANES, tn), lambda l, j: (l, 0, j)),
        out_shape=jax.ShapeDtypeStruct((depth, SUBLANES, n), F32),
        compiler_params=_params(("parallel", "parallel")),
        name="ada_mod",
    )(c8, ada_w, ada_b.reshape(depth, 1, n))


EV_SHIFTED = 3072


def _even_in_body(nt, c_ref, x_ref, xp_ref, xn_ref, g_ref, sh_ref, sc_ref, w_ref,
                  conv_ref, gb_ref, mu_ref, w0_ref, w2_ref, a0_ref, a2_ref, g2_ref,
                  kk_ref, ka_ref, rk_ref,
                  ovo_ref, oqk_ref, og_ref, ogt_ref, orws_ref, olw_ref, okb_ref, obg_ref):
    i = pl.program_id(1)
    has_prev = (i > 1).astype(F32)
    has_next = jnp.logical_and(i != 0, i != nt - 1).astype(F32)
    tm = x_ref.shape[0]
    ext = tm + 2 * HALO
    row = lax.broadcasted_iota(jnp.int32, (tm, 1), 0)
    keep_dn = jnp.where(row == 0, has_prev, 1.0)
    keep_up = jnp.where(row == tm - 1, has_next, 1.0)

    def normed(xx):
        return _bf(_rms(xx, g_ref[...]) * (1.0 + sc_ref[...]) + sh_ref[...])

    h = normed(jnp.where(i == 0, c_ref[...], x_ref[...]))
    ovo_ref[...] = _dot(h, w_ref[:, EV_SHIFTED:]).astype(ovo_ref.dtype)
    hcat = jnp.concatenate([normed(xp_ref[...]), h, normed(xn_ref[...])], axis=0)
    proj = _dot(hcat, w_ref[:, 0:EV_SHIFTED])

    def neighbours(cols):
        blk = proj[:, cols]
        dn = pltpu.roll(blk, 1, 0)[HALO:HALO + tm] * keep_dn
        up = pltpu.roll(blk, ext - 1, 0)[HALO:HALO + tm] * keep_up
        return blk[HALO:HALO + tm], dn, up

    qk, dn, up = neighbours(slice(2048, 3072))
    oqk_ref[...] = (dn * conv_ref[0:1, :] + qk * conv_ref[1:2, :] + up * conv_ref[2:3, :]).astype(oqk_ref.dtype)

    gv = proj[HALO:HALO + tm, RW_FEAT:RW_FEAT + LANES] + gb_ref[...]
    lane = lax.broadcasted_iota(jnp.int32, (1, LANES), 1)
    is_f = jnp.logical_and(jnp.bitwise_and(jnp.right_shift(lane, 2), 1) == 1, lane < 16)
    gates = jnp.where(is_f, _log_sigmoid(gv), gv)
    og_ref[...] = gates
    ogt_ref[...] = gates.T[0:4 * ML_HEADS, :]

    cur, dn, up = neighbours(slice(0, RW_FEAT))
    rf = cur + (0.5 * (dn + up) - cur) * mu_ref[...]
    w = RW_WIDTH
    r, k, v = rf[:, 0:w], rf[:, w:2 * w], rf[:, 2 * w:3 * w]
    wl = rf[:, 3 * w:3 * w + LANES]
    al = rf[:, 3 * w + LANES:3 * w + 2 * LANES]
    gl = rf[:, 3 * w + 2 * LANES:3 * w + 3 * LANES]
    kk = k * kk_ref[...]
    kk = kk * lax.rsqrt(_group_mean(kk * kk, RW_HEAD_DIM) * RW_HEAD_DIM + 1e-12)
    twl = jnp.tanh(wl)
    kd_sum = jnp.zeros_like(k)
    for d in range(2):
        w_log = -_softplus(-(w0_ref[d:d + 1, :] + _dot3(twl, w2_ref[d]))) - 0.5
        a = _sigmoid(a0_ref[d:d + 1, :] + _dot3(al, a2_ref[d]))
        kd = k * (1.0 + (a - 1.0) * ka_ref[...])
        kd_sum = kd_sum + kd
        olw_ref[d] = -jnp.exp(w_log)
        okb_ref[d, :, 0:w] = kd.astype(okb_ref.dtype)
        okb_ref[d, :, w:2 * w] = (kk * a).astype(okb_ref.dtype)
    orws_ref[:, 0:w] = r.astype(orws_ref.dtype)
    orws_ref[:, w:2 * w] = v.astype(orws_ref.dtype)
    orws_ref[:, 2 * w:3 * w] = (-kk).astype(orws_ref.dtype)
    bonus = _group_mean(r * kd_sum * rk_ref[...], RW_HEAD_DIM) * RW_HEAD_DIM * v
    obg_ref[:, 0:w] = bonus.astype(obg_ref.dtype)
    obg_ref[:, w:2 * w] = _dot3(_sigmoid(gl), g2_ref[...]).astype(obg_ref.dtype)


def _even_in(ctx, x, g, mods, w_in, conv, gate_b, mu, w0, w2cat, a0, a2cat, g2, kk, ka, rk):
    b, n, d = x.shape
    tm = ROW_TILE
    assert ctx.shape[1] == tm
    t = n + tm
    nt = t // tm
    rh = tm // HALO
    last_h = n // HALO - 1
    seg = lambda i: (i >= 1).astype(jnp.int32)
    cur = lambda c: (lambda bb, i: (bb, i, c))
    lat = lambda i: jnp.maximum(i - 1, 0)
    w = RW_WIDTH
    return pl.pallas_call(
        functools.partial(_even_in_body, nt),
        grid=(b, nt),
        in_specs=[
            pl.BlockSpec((None, tm, d), lambda bb, i: (bb, 0, 0)),
            pl.BlockSpec((None, tm, d), lambda bb, i: (bb, lat(i), 0)),
            pl.BlockSpec((None, HALO, d), lambda bb, i: (bb, jnp.maximum(lat(i) * rh - 1, 0), 0)),
            pl.BlockSpec((None, HALO, d), lambda bb, i: (bb, jnp.minimum((lat(i) + 1) * rh, last_h), 0)),
            _const_spec((1, d)), _mod_spec(0, seg), _mod_spec(1, seg), _const_spec(w_in.shape),
            _const_spec(conv.shape), _const_spec(gate_b.shape), _const_spec(mu.shape),
            _const_spec(w0.shape), _const_spec(w2cat.shape), _const_spec(a0.shape),
            _const_spec(a2cat.shape), _const_spec(g2.shape), _const_spec(kk.shape),
            _const_spec(ka.shape), _const_spec(rk.shape),
        ],
        out_specs=[
            pl.BlockSpec((None, tm, 1024), cur(0)),
            pl.BlockSpec((None, tm, 1024), cur(0)),
            pl.BlockSpec((None, tm, LANES), cur(0)),
            pl.BlockSpec((None, 4 * ML_HEADS, tm), lambda bb, i: (bb, 0, i)),
            pl.BlockSpec((None, tm, 3 * w), cur(0)),
            pl.BlockSpec((2, None, tm, w), lambda bb, i: (0, bb, i, 0)),
            pl.BlockSpec((2, None, tm, 2 * w), lambda bb, i: (0, bb, i, 0)),
            pl.BlockSpec((None, tm, 2 * w), cur(0)),
        ],
        out_shape=[
            jax.ShapeDtypeStruct((b, t, 1024), BF16),
            jax.ShapeDtypeStruct((b, t, 1024), BF16),
            jax.ShapeDtypeStruct((b, t, LANES), F32),
            jax.ShapeDtypeStruct((b, 4 * ML_HEADS, t), F32),
            jax.ShapeDtypeStruct((b, t, 3 * w), BF16),
            jax.ShapeDtypeStruct((2, b, t, w), F32),
            jax.ShapeDtypeStruct((2, b, t, 2 * w), BF16),
            jax.ShapeDtypeStruct((b, t, 2 * w), BF16),
        ],
        compiler_params=_params(("parallel", "parallel")),
        name="even_in",
    )(ctx, x, x, x, g, mods, mods, w_in, conv, gate_b, mu, w0, w2cat, a0, a2cat, g2, kk, ka, rk)


def _mlstm_body(q_ref, k_ref, v_ref, gc_ref, gr_ref, o_ref, c_scr, n_scr, m_scr):
    d = pl.program_id(0)
    j = pl.program_id(1)
    n_batch, n_rows = q_ref.shape[0], q_ref.shape[1]

    @pl.when(j == 0)
    def _():
        c_scr[...] = jnp.zeros_like(c_scr)
        n_scr[...] = jnp.zeros_like(n_scr)
        m_scr[...] = jnp.zeros_like(m_scr)

    mask = _order_mask(n_rows, 1 - 2 * d, strict=False)
    mf = mask.astype(BF16)
    gc = [gc_ref[bi] for bi in range(n_batch)]
    gr = [gr_ref[bi] for bi in range(n_batch)]
    b_col = [_sel_dot(mf, x) for x in gc]
    b_row = [_dot_sel_nt(x, mf) for x in gr]
    row_id = lax.broadcasted_iota(jnp.int32, (n_rows, 1), 0)
    is_last = row_id == (n_rows - 1) * (1 - d)
    units = [(bi, h) for bi in range(n_batch) for h in range(ML_HEADS)]
    idx = range(len(units))
    sls = [slice(h * ML_DIM, (h + 1) * ML_DIM) for _, h in units]
    fcol = [ML_HEADS + h for _, h in units]
    pick_row = lax.broadcasted_iota(jnp.int32, (LANES, LANES), 0)
    pick = [(pick_row == 2 * ML_HEADS * d + c).astype(BF16) for c in range(2 * ML_HEADS)]
    ones_cols = jnp.ones((n_rows, LANES), BF16)
    q = [_bf(q_ref[bi, :, sls[u]]) for u, (bi, _) in enumerate(units)]
    kf = [k_ref[bi, :, sls[u]].astype(F32) * (ML_DIM ** -0.5) for u, (bi, _) in enumerate(units)]
    k = [_bf(x) for x in kf]
    vf = [v_ref[bi, :, sls[u]].astype(F32) for u, (bi, _) in enumerate(units)]
    c_prev = [c_scr[u] for u in idx]
    n_prev = [n_scr[u][0:1, :] for u in idx]
    m_prev = [m_scr[u][0:1, :] for u in idx]
    qk = [_dot_nt(q[u], k[u]) for u in idx]
    qc = [_dot_nt(q[u], jnp.concatenate([_bf(c_prev[u]), _bf(jnp.broadcast_to(n_prev[u], (LANES, ML_DIM)))], axis=0))
          for u in idx]
    ic = [_dot_sel(gc[bi], pick[h]) for bi, h in units]
    bc = [_dot_sel(b_col[bi], pick[fcol[u]]) for u, (bi, _) in enumerate(units)]
    b_tot = [jnp.sum(jnp.where(is_last, bc[u], 0.0), axis=0, keepdims=True) for u in idx]
    logw = [b_tot[u] - bc[u] + ic[u] for u in idx]
    m_new = [jnp.maximum(b_tot[u] + m_prev[u], jnp.max(logw[u], axis=0, keepdims=True)) for u in idx]
    wgt = [jnp.exp(logw[u] - m_new[u]) for u in idx]
    decay = [jnp.exp(b_tot[u] + m_prev[u] - m_new[u]) for u in idx]
    kv = [_dot_tn(_bf(vf[u] * wgt[u]), k[u]) for u in idx]
    s, m_t, w_inter = [], [], []
    for u, (bi, h) in enumerate(units):
        logd = jnp.where(mask, bc[u] - b_row[bi][fcol[u]:fcol[u] + 1, :] + gr[bi][h:h + 1, :], -1e30)
        m_inter = bc[u] + m_prev[u]
        m_t.append(jnp.maximum(m_inter, jnp.max(logd, axis=-1, keepdims=True)))
        s.append(qk[u] * jnp.exp(logd - m_t[u]))
        w_inter.append(jnp.exp(m_inter - m_t[u]))
    sv = [_dot(_bf(s[u]), jnp.concatenate([_bf(vf[u]), ones_cols], axis=1)) for u in idx]
    for u, (bi, _) in enumerate(units):
        num = sv[u][:, :ML_DIM] + w_inter[u] * qc[u][:, :ML_DIM]
        den = sv[u][:, ML_DIM:] + w_inter[u] * qc[u][:, ML_DIM:]
        o_ref[bi, :, sls[u]] = num / jnp.maximum(jnp.abs(den), jnp.exp(-m_t[u]))
    for u in idx:
        c_scr[u] = decay[u] * c_prev[u] + kv[u]
        n_scr[u] = jnp.broadcast_to(decay[u] * n_prev[u] + jnp.sum(wgt[u] * kf[u], axis=0, keepdims=True),
                                    (SUBLANES, ML_DIM))
        m_scr[u] = jnp.broadcast_to(m_new[u], (SUBLANES, LANES))


def _mlstm(qk, proj, gates, gates_t, n_ctx):
    b, t, _ = qk.shape
    n_rows = ML_CHUNK
    nc, nctx = t // n_rows, n_ctx // n_rows
    ci = lambda d, j: _chunk_index(d, j, nctx, nc)
    return pl.pallas_call(
        _mlstm_body,
        grid=(2, nc),
        in_specs=[
            pl.BlockSpec((b, n_rows, ML_WIDTH), lambda d, j: (0, ci(d, j), 0)),
            pl.BlockSpec((b, n_rows, ML_WIDTH), lambda d, j: (0, ci(d, j), 1)),
            pl.BlockSpec((b, n_rows, ML_WIDTH), lambda d, j: (0, ci(d, j), 0)),
            pl.BlockSpec((b, n_rows, LANES), lambda d, j: (0, ci(d, j), 0)),
            pl.BlockSpec((b, 2 * ML_HEADS, n_rows), lambda d, j: (0, d, ci(d, j))),
        ],
        out_specs=pl.BlockSpec((None, b, n_rows, ML_WIDTH), lambda d, j: (d, 0, ci(d, j), 0)),
        out_shape=jax.ShapeDtypeStruct((2, b, t, ML_WIDTH), F32),
        scratch_shapes=[
            pltpu.VMEM((b * ML_HEADS, ML_DIM, ML_DIM), F32),
            pltpu.VMEM((b * ML_HEADS, SUBLANES, ML_DIM), F32),
            pltpu.VMEM((b * ML_HEADS, SUBLANES, LANES), F32),
        ],
        compiler_params=_params(("parallel", "arbitrary")),
        name="mlstm",
    )(qk, qk, proj, gates, gates_t)


def _rwkv_body(r_ref, v_ref, a_ref, lw_ref, k_ref, b_ref, o_ref, h_scr):
    d = pl.program_id(0)
    j = pl.program_id(1)
    n_rows = RW_CHUNK
    two = 2 * n_rows
    shift = n_rows.bit_length() - 1

    @pl.when(j == 0)
    def _():
        h_scr[...] = jnp.zeros_like(h_scr)

    sgn = 1 - 2 * d
    cum = _order_mask(n_rows, sgn, strict=False).astype(BF16)
    row = lax.broadcasted_iota(jnp.int32, (two, two), 0)
    col = lax.broadcasted_iota(jnp.int32, (two, two), 1)
    same = jnp.right_shift(row, shift) == jnp.right_shift(col, shift)
    diff = (jnp.bitwise_and(row, n_rows - 1) - jnp.bitwise_and(col, n_rows - 1)) * sgn
    m_strict = jnp.logical_and(same, diff > 0)
    m_incl = jnp.logical_and(same, diff >= 0)
    eye = (row == col).astype(F32)
    head0 = lax.broadcasted_iota(jnp.int32, (1, LANES), 1) < RW_HEAD_DIM

    def stack(x):
        return jnp.concatenate([jnp.where(head0, x, 0.0), jnp.where(head0, 0.0, x)], axis=0)

    def fold(x):
        return x[:n_rows] + x[n_rows:]

    n_batch = r_ref.shape[0]
    n_sub = r_ref.shape[1] // n_rows
    lane_groups = [slice(p * LANES, (p + 1) * LANES) for p in range(RW_WIDTH // LANES)]
    rows = [pl.ds(pl.multiple_of(jnp.where(d == 0, sc, n_sub - 1 - sc) * n_rows, n_rows), n_rows)
            for sc in range(n_sub)]
    groups = [(sc, bi, p) for sc in range(n_sub) for bi in range(n_batch) for p in range(len(lane_groups))]
    x_s, r_s, v_s, bh_s, kh_s, bi_s, ki_s, e_tot = [], [], [], [], [], [], [], []
    for sc in range(n_sub):
        for bi in range(n_batch):
            lw = lw_ref[bi, rows[sc], :]
            g = _sel_dot(cum, lw)
            g_tot = jnp.sum(lw, axis=0, keepdims=True)
            inv = jnp.exp(-g)
            to_end = jnp.exp(g_tot - g)
            at = a_ref[bi, rows[sc], :].astype(F32) * jnp.exp(g - lw)
            rt = r_ref[bi, rows[sc], :].astype(F32) * jnp.exp(g)
            b_in = b_ref[bi, rows[sc], :].astype(F32)
            k_in = k_ref[bi, rows[sc], :].astype(F32)
            vv = v_ref[bi, rows[sc], :].astype(F32)
            for sl in lane_groups:
                x_s.append(stack(at[:, sl]))
                r_s.append(stack(rt[:, sl]))
                v_s.append(stack(vv[:, sl]))
                bh_s.append(stack((b_in * to_end)[:, sl]))
                kh_s.append(stack((k_in * to_end)[:, sl]))
                bi_s.append(stack((b_in * inv)[:, sl]))
                ki_s.append(stack((k_in * inv)[:, sl]))
                e_tot.append(jnp.exp(g_tot)[:, sl])
    big = [_mm_nt(jnp.concatenate([x_s[p], r_s[p]], axis=0), jnp.concatenate([bi_s[p], ki_s[p]], axis=0))
           for p in range(len(groups))]
    a_ab = [jnp.where(m_strict, m[:two, :two], 0.0) for m in big]
    a_rb = [jnp.where(m_incl, m[two:, :two], 0.0) for m in big]
    a_k = [jnp.concatenate([jnp.where(m_strict, m[:two, two:], 0.0), jnp.where(m_incl, m[two:, two:], 0.0)], axis=0)
           for m in big]
    av = [_mm(a_k[p], v_s[p]) for p in range(len(groups))]
    t_inv = [eye + m for m in a_ab]
    pw = [_mm(m, m) for m in a_ab]
    for level in range(1, shift):
        last = level == shift - 1
        res = [_mm(t if last else jnp.concatenate([t, m], axis=0), m) for t, m in zip(t_inv, pw)]
        t_inv = [t + r[:two] for t, r in zip(t_inv, res)]
        if not last:
            pw = [r[two:] for r in res]
    wu = [_mm(t_inv[p], jnp.concatenate([x_s[p], av[p][:two]], axis=1)) for p in range(len(groups))]
    rb = [_mm(a_rb[p], wu[p]) for p in range(len(groups))]
    upd = [_mm_tn(wu[p], bh_s[p]) for p in range(len(groups))]
    gt2 = [_mm_tn(v_s[p], kh_s[p]) for p in range(len(groups))]
    q_eff = [fold(r_s[p] + rb[p][:, :LANES]) for p in range(len(groups))]
    y_in = [fold(rb[p][:, LANES:] + av[p][two:]) for p in range(len(groups))]
    per_chunk = n_batch * len(lane_groups)
    h_t = [h_scr[u] for u in range(per_chunk)]
    for sc in range(n_sub):
        ids = range(sc * per_chunk, (sc + 1) * per_chunk)
        for u, p in enumerate(ids):
            _, bi, lg = groups[p]
            o_ref[bi, rows[sc], lane_groups[lg]] = _mm_nt(q_eff[p], h_t[u]) + y_in[p]
        h_t = [e_tot[p] * h_t[u] + _mm(h_t[u], upd[p][:LANES]) + upd[p][LANES:] + gt2[p] for u, p in enumerate(ids)]
    for u in range(per_chunk):
        h_scr[u] = h_t[u]


def _rwkv(rws, lw, kb, n_ctx):
    b, t, _ = rws.shape
    n_rows = RW_BLOCK
    nc, nctx = t // n_rows, n_ctx // n_rows
    ci = lambda d, j: _chunk_index(d, j, nctx, nc)
    w = RW_WIDTH
    shared = lambda c: pl.BlockSpec((b, n_rows, w), lambda d, j: (0, ci(d, j), c))
    per_dir = lambda c: pl.BlockSpec((None, b, n_rows, w), lambda d, j: (d, 0, ci(d, j), c))
    return pl.pallas_call(
        _rwkv_body,
        grid=(2, nc),
        in_specs=[shared(0), shared(1), shared(2), per_dir(0), per_dir(0), per_dir(1)],
        out_specs=pl.BlockSpec((None, b, n_rows, w), lambda d, j: (d, 0, ci(d, j), 0)),
        out_shape=jax.ShapeDtypeStruct((2, b, t, w), F32),
        scratch_shapes=[pltpu.VMEM((b * w // LANES, LANES, LANES), F32)],
        compiler_params=_params(("parallel", "arbitrary")),
        name="rwkv7",
    )(rws, rws, rws, lw, kb, kb)


def _tail(x, mix, outw_ref, w1_ref, w2_ref, ng_ref, ga_ref, shm_ref, scm_ref, gm_ref):
    x1 = x + ga_ref[...] * _rms(_dot(_bf(mix), outw_ref[...]), ng_ref[1:2, :])
    h = _bf(_rms(x1, ng_ref[2:3, :]) * (1.0 + scm_ref[...]) + shm_ref[...])
    chunks = [slice(c * FF_CHUNK, (c + 1) * FF_CHUNK) for c in range(D_FF // FF_CHUNK)]
    u = [jnp.maximum(_dot(h, w1_ref[:, cols]), 0.0) for cols in chunks]
    dd = _dot(_bf(u[0] * u[0]), w2_ref[chunks[0], :])
    for uc, cols in zip(u[1:], chunks[1:]):
        dd = dd + _dot(_bf(uc * uc), w2_ref[cols, :])
    return x1 + gm_ref[...] * _rms(dd, ng_ref[3:4, :])


def _even_post_body(c_ref, x_ref, hm_ref, mo_ref, yrw_ref, bg_ref, mln_ref, lnw_ref, lnb_ref,
                    outw_ref, w1_ref, w2_ref, ng_ref, ga_ref, shm_ref, scm_ref, gm_ref, o_ref):
    w = RW_WIDTH
    x_in = jnp.where(pl.program_id(1) == 0, c_ref[...], x_ref[...])
    ym = hm_ref[0] + hm_ref[1]
    ym = (ym * lax.rsqrt(_group_mean(ym * ym, ML_DIM) + EPS) * mln_ref[...]
          * _sigmoid(mo_ref[...].astype(F32)))
    y = yrw_ref[0] + yrw_ref[1]
    yc = y - _group_mean(y, RW_HEAD_DIM)
    y = (yc * lax.rsqrt(_group_mean(yc * yc, RW_HEAD_DIM) + RW_GN_EPS) * lnw_ref[...] + lnb_ref[...]
         + bg_ref[:, 0:w].astype(F32))
    mix = jnp.concatenate([ym, y * bg_ref[:, w:2 * w].astype(F32)], axis=-1)
    o_ref[...] = _tail(x_in, mix, outw_ref, w1_ref, w2_ref, ng_ref, ga_ref, shm_ref, scm_ref, gm_ref)


def _tail_specs(seg, layer):
    d = D_MODEL
    pick = lambda *shape: pl.BlockSpec((None,) + shape, lambda *_: (layer, 0, 0), pipeline_mode=pl.Buffered(1))
    return [
        pick(d, d), pick(d, D_FF), pick(D_FF, d), pick(4, d),
        _mod_spec(2, seg), _mod_spec(3, seg), _mod_spec(4, seg), _mod_spec(5, seg),
    ]


def _even_post(ctx, x, hm, proj, yrw, bg, mln, lnw, lnb, outw, w1, w2, ng, mods, layer):
    b, n, d = x.shape
    tm = ROW_TILE
    assert ctx.shape[1] == tm
    t = n + tm
    seg = lambda i: (i >= 1).astype(jnp.int32)
    w = RW_WIDTH
    return pl.pallas_call(
        _even_post_body,
        grid=(b, t // tm),
        in_specs=[
            pl.BlockSpec((None, tm, d), lambda bb, i: (bb, 0, 0)),
            pl.BlockSpec((None, tm, d), lambda bb, i: (bb, jnp.maximum(i - 1, 0), 0)),
            pl.BlockSpec((2, None, tm, w), lambda bb, i: (0, bb, i, 0)),
            pl.BlockSpec((None, tm, w), lambda bb, i: (bb, i, 1)),
            pl.BlockSpec((2, None, tm, w), lambda bb, i: (0, bb, i, 0)),
            pl.BlockSpec((None, tm, 2 * w), lambda bb, i: (bb, i, 0)),
            _const_spec((1, w)), _const_spec((1, w)), _const_spec((1, w)),
        ] + _tail_specs(seg, layer),
        out_specs=pl.BlockSpec((None, tm, d), lambda bb, i: (bb, i, 0)),
        out_shape=jax.ShapeDtypeStruct((b, t, d), F32),
        compiler_params=_params(("parallel", "parallel")),
        name="even_post",
    )(ctx, x, hm, proj, yrw, bg, mln, lnw, lnb, outw, w1, w2, ng, mods, mods, mods, mods)


def _hgrn_body(q_ref, f_ref, i_ref, lb_ref, o_ref, s_scr):
    d = pl.program_id(0)
    j = pl.program_id(1)
    n_batch, n_rows = q_ref.shape[0], q_ref.shape[1]
    n_sub = n_rows // HG_CHUNK

    @pl.when(j == 0)
    def _():
        s_scr[...] = jnp.zeros_like(s_scr)

    mask = _order_mask(HG_CHUNK, 1 - 2 * d, strict=False)
    mf = mask.astype(BF16)
    lb = lb_ref[...]
    heads = range(HG_HEADS)
    sls = [slice(h * HG_DIM, (h + 1) * HG_DIM) for h in heads]
    units = [(bi, h) for bi in range(n_batch) for h in heads]
    n_units = len(units)
    s_t = [s_scr[u] for u in range(n_units)]
    rows = [pl.ds(pl.multiple_of(jnp.where(d == 0, sc, n_sub - 1 - sc) * HG_CHUNK, HG_CHUNK), HG_CHUNK)
            for sc in range(n_sub)]
    loaded = [[(q_ref[bi, r, :], f_ref[bi, r, :], i_ref[bi, r, :]) for bi in range(n_batch)] for r in rows]
    q_in, e_tot, q_mids, k_mids, k_outs, vvs = [], [], [], [], [], []

    def prepare(sc):
        for lst in (q_in, e_tot, q_mids, k_mids, k_outs, vvs):
            lst.append([])
        for bi in range(n_batch):
            qa = loaded[sc][bi][0].astype(F32)
            pre = loaded[sc][bi][1].astype(F32)
            vv = loaded[sc][bi][2]
            q = qa * _sigmoid(qa) * (HG_DIM ** -0.5)
            e_abs = jnp.exp(-jnp.abs(pre))
            big = 1.0 / (1.0 + e_abs)
            small = e_abs * big
            log_f = jnp.log(lb + (1.0 - lb) * jnp.where(pre >= 0.0, big, small))
            kf = (1.0 - lb) * jnp.where(pre >= 0.0, small, big)
            bc = _sel_dot(mf, log_f)
            tot = jnp.sum(log_f, axis=0, keepdims=True)
            half = 0.5 * tot
            e_half = jnp.exp(half)
            q_mid_f = q * jnp.exp(bc - half)
            k_mid_f = kf * jnp.exp(half - bc)
            q_mid = _bf(q_mid_f)
            k_mid = _bf(k_mid_f)
            k_out = _bf(k_mid_f * e_half)
            q_all = _bf(q_mid_f * e_half)
            e_all = e_half * e_half
            q_mids[sc] += [q_mid[:, sl] for sl in sls]
            k_mids[sc] += [k_mid[:, sl] for sl in sls]
            k_outs[sc] += [k_out[:, sl] for sl in sls]
            vvs[sc] += [vv[:, sl] for sl in sls]
            q_in[sc] += [q_all[:, sl] for sl in sls]
            e_tot[sc] += [e_all[:, sl] for sl in sls]

    prepare(0)
    for sc in range(n_sub):
        if sc + 1 < n_sub:
            prepare(sc + 1)
        att = [_dot_nt(q_mids[sc][u], k_mids[sc][u]) for u in range(n_units)]
        kv = [_dot_tn(vvs[sc][u], k_outs[sc][u]) for u in range(n_units)]
        intra = [_dot(_bf(jnp.where(mask, att[u], 0.0)), vvs[sc][u]) for u in range(n_units)]
        for u, (bi, h) in enumerate(units):
            o_ref[bi, rows[sc], sls[h]] = intra[u] + _dot_nt(q_in[sc][u], _bf(s_t[u]))
        s_t = [e_tot[sc][u] * s_t[u] + kv[u] for u in range(n_units)]
    for u in range(n_units):
        s_scr[u] = s_t[u]


def _hgrn(proj, lb, n_ctx):
    b, t, _ = proj.shape
    n_rows = HG_BLOCK
    nc, nctx = t // n_rows, n_ctx // n_rows
    ci = lambda d, j: _chunk_index(d, j, nctx, nc)
    w = HG_WIDTH
    return pl.pallas_call(
        _hgrn_body,
        grid=(2, nc),
        in_specs=[
            pl.BlockSpec((b, n_rows, w), lambda d, j: (0, ci(d, j), 0)),
            pl.BlockSpec((b, n_rows, w), lambda d, j: (0, ci(d, j), 1 + d)),
            pl.BlockSpec((b, n_rows, w), lambda d, j: (0, ci(d, j), 3)),
            _const_spec((1, w)),
        ],
        out_specs=pl.BlockSpec((None, b, n_rows, w), lambda d, j: (d, 0, ci(d, j), 0)),
        out_shape=jax.ShapeDtypeStruct((2, b, t, w), F32),
        scratch_shapes=[pltpu.VMEM((b * HG_HEADS, HG_DIM, HG_DIM), F32)],
        compiler_params=_params(("parallel", "arbitrary")),
        name="hgrn2",
    )(proj, proj, proj, lb)


V_ROWS = MLA_V + SUBLANES
OD_MAIN = 5 * HG_WIDTH


def _odd_in_body(x_ref, g_ref, sh_ref, sc_ref, w_ref, cos_ref, sin_ref, kvn_ref, qn_ref,
                 wk_ref, wvt_ref, wq1_ref, wq2_ref, proj_ref, k_ref, vt_ref, q_ref):
    h = _bf(_rms(x_ref[...], g_ref[...]) * (1.0 + sc_ref[...]) + sh_ref[...])
    mla = _dot(h, w_ref[:, OD_MAIN:])
    proj_ref[...] = _dot(h, w_ref[:, 0:OD_MAIN]).astype(proj_ref.dtype)
    cq = mla[:, 0:MLA_Q_RANK]
    base = MLA_Q_RANK
    ckv = mla[:, base:base + MLA_KV_RANK]
    cos, sin = cos_ref[...], sin_ref[...]
    k_rope = mla[:, base + LANES:base + 2 * LANES] * cos + mla[:, base + 2 * LANES:base + 3 * LANES] * sin
    kvn = _bf(_rms(ckv, kvn_ref[...]))
    kn = _dot(kvn, wk_ref[...])
    ones = jnp.ones((SUBLANES, mla.shape[0]), BF16)
    cn = _bf(_rms(cq, qn_ref[...]))
    qa = _dot(cn, wq1_ref[...])
    qb = _dot(cn, wq2_ref[...])
    scale = (MLA_NOPE + MLA_ROPE) ** -0.5 * float(np.log2(np.e))
    for hd in range(MLA_HEADS):
        k_ref[hd] = jnp.concatenate([kn[:, hd * MLA_NOPE:(hd + 1) * MLA_NOPE], k_rope], axis=-1).astype(BF16)
        vt = _dot_nt(wvt_ref[hd], kvn)
        vt_ref[hd] = jnp.concatenate([vt.astype(BF16), ones], axis=0)
        a = qa[:, hd * 256:(hd + 1) * 256]
        q_rope = a[:, MLA_NOPE:] * cos + qb[:, hd * LANES:(hd + 1) * LANES] * sin
        q_ref[hd] = (jnp.concatenate([a[:, 0:MLA_NOPE], q_rope], axis=-1) * scale).astype(BF16)


def _odd_in(xall, g, mods, w_in, nctx_t, cos, sin, kv_norm, q_norm, wk, wvt, wq1, wq2):
    b, t, d = xall.shape
    tm = ROW_TILE
    n = t - nctx_t * tm
    seg = lambda i: (i >= nctx_t).astype(jnp.int32)
    q_blk = lambda bb, i: (bb, 0, jnp.maximum(i - nctx_t, 0), 0)
    return pl.pallas_call(
        _odd_in_body,
        grid=(b, t // tm),
        in_specs=[
            pl.BlockSpec((None, tm, d), lambda bb, i: (bb, i, 0)),
            _const_spec((1, d)), _mod_spec(0, seg), _mod_spec(1, seg), _const_spec(w_in.shape),
            pl.BlockSpec((tm, LANES), lambda bb, i: (i, 0)),
            pl.BlockSpec((tm, LANES), lambda bb, i: (i, 0)),
            _const_spec((1, MLA_KV_RANK)), _const_spec((1, MLA_Q_RANK)),
            _const_spec(wk.shape), _const_spec(wvt.shape), _const_spec(wq1.shape), _const_spec(wq2.shape),
        ],
        out_specs=[
            pl.BlockSpec((None, tm, OD_MAIN), lambda bb, i: (bb, i, 0)),
            pl.BlockSpec((None, MLA_HEADS, tm, 256), lambda bb, i: (bb, 0, i, 0)),
            pl.BlockSpec((None, MLA_HEADS, V_ROWS, tm), lambda bb, i: (bb, 0, 0, i)),
            pl.BlockSpec((None, MLA_HEADS, tm, 256), q_blk),
        ],
        out_shape=[
            jax.ShapeDtypeStruct((b, t, OD_MAIN), BF16),
            jax.ShapeDtypeStruct((b, MLA_HEADS, t, 256), BF16),
            jax.ShapeDtypeStruct((b, MLA_HEADS, V_ROWS, t), BF16),
            jax.ShapeDtypeStruct((b, MLA_HEADS, n, 256), BF16),
        ],
        compiler_params=_params(("parallel", "arbitrary")),
        name="odd_in",
    )(xall, g, mods, mods, w_in, cos, sin, kv_norm, q_norm, wk, wvt, wq1, wq2)


def _attn_body(q_ref, k_ref, vt_ref, o_ref):
    k = k_ref[...]
    vt = vt_ref[...]
    subs = [slice(i * ATTN_SUB, (i + 1) * ATTN_SUB) for i in range(q_ref.shape[0] // ATTN_SUB)]
    st = [_dot_nt(k, q_ref[sl, :]) for sl in subs]
    pt = [_bf(jnp.exp2(s - jnp.max(s, axis=0, keepdims=True))) for s in st]
    ot = [_dot(vt, p) for p in pt]
    for sl, o in zip(subs, ot):
        o_ref[sl, :] = (o[0:MLA_V] / o[MLA_V:MLA_V + 1]).T


def _attention(q, k, vt):
    b, h, n, dq = q.shape
    t = k.shape[2]
    tq = ATTN_TQ
    return pl.pallas_call(
        _attn_body,
        grid=(b, h, n // tq),
        in_specs=[
            pl.BlockSpec((None, None, tq, dq), lambda bb, hh, i: (bb, hh, i, 0)),
            pl.BlockSpec((None, None, t, dq), lambda bb, hh, i: (bb, hh, 0, 0)),
            pl.BlockSpec((None, None, V_ROWS, t), lambda bb, hh, i: (bb, hh, 0, 0)),
        ],
        out_specs=pl.BlockSpec((None, tq, MLA_V), lambda bb, hh, i: (bb, i, hh)),
        out_shape=jax.ShapeDtypeStruct((b, n, h * MLA_V), F32),
        compiler_params=_params(("parallel", "parallel", "parallel")),
        name="mla_attention",
    )(q, k, vt)


def _odd_post_body(x_ref, o_ref_in, hg_ref, att_ref, hgn_ref,
                   outw_ref, w1_ref, w2_ref, ng_ref, ga_ref, shm_ref, scm_ref, gm_ref, o_ref):
    y = o_ref_in[0, 0] + o_ref_in[1, 0]
    g = hg_ref[0].astype(F32)
    y = y * lax.rsqrt(_group_mean(y * y, HG_DIM) + EPS) * hgn_ref[...] * (g * _sigmoid(g))
    mix = jnp.concatenate([y, att_ref[...]], axis=-1)
    o_ref[...] = _tail(x_ref[0], mix, outw_ref, w1_ref, w2_ref, ng_ref, ga_ref, shm_ref, scm_ref, gm_ref)


def _odd_post(xall, ohg, proj, att, hgn, outw, w1, w2, ng, mods, n_ctx, layer):
    b, t, d = xall.shape
    tm = LAT_TILE
    n = t - n_ctx
    seg = lambda i: 1
    w = HG_WIDTH
    el = lambda *shape: tuple(pl.Element(s) for s in shape)
    row0 = lambda i: pl.multiple_of(n_ctx + i * tm, int(np.gcd(n_ctx, tm)))
    return pl.pallas_call(
        _odd_post_body,
        grid=(b, n // tm),
        in_specs=[
            pl.BlockSpec(el(1, tm, d), lambda bb, i: (bb, row0(i), 0)),
            pl.BlockSpec(el(2, 1, tm, w), lambda bb, i: (0, bb, row0(i), 0)),
            pl.BlockSpec(el(1, tm, w), lambda bb, i: (bb, row0(i), 4 * w)),
            pl.BlockSpec((None, tm, w), lambda bb, i: (bb, i, 0)),
            _const_spec((1, w)),
        ] + _tail_specs(seg, layer),
        out_specs=pl.BlockSpec((None, tm, d), lambda bb, i: (bb, i, 0)),
        out_shape=jax.ShapeDtypeStruct((b, n, d), F32),
        compiler_params=_params(("parallel", "parallel")),
        name="odd_post",
    )(xall, ohg, proj, att, hgn, outw, w1, w2, ng, mods, mods, mods, mods)


def _rope_tables(n, n_ctx):
    quarter = MLA_ROPE // 4
    inv = ROPE_BASE ** (-jnp.arange(quarter, dtype=F32) / quarter)
    rows = n // GRID_W
    ang_r = jnp.arange(rows, dtype=F32)[:, None] * inv[None, :]
    ang_c = jnp.arange(GRID_W, dtype=F32)[:, None] * inv[None, :]
    by_row = lambda a: jnp.repeat(a, GRID_W, axis=0)
    by_col = lambda a: jnp.tile(a, (rows, 1))
    cos_r, sin_r, cos_c, sin_c = by_row(jnp.cos(ang_r)), by_row(jnp.sin(ang_r)), by_col(jnp.cos(ang_c)), by_col(jnp.sin(ang_c))
    cos = jnp.concatenate([cos_r, cos_r, cos_c, cos_c], axis=-1)
    sin = jnp.concatenate([-sin_r, sin_r, -sin_c, sin_c], axis=-1)
    cos = jnp.concatenate([jnp.ones((n_ctx, MLA_ROPE), F32), cos], axis=0)
    sin = jnp.concatenate([jnp.zeros((n_ctx, MLA_ROPE), F32), sin], axis=0)
    pad = lambda a: jnp.pad(a, ((0, 0), (0, LANES - MLA_ROPE)))
    return pad(cos), pad(sin)


_ROPE_SWAP = np.concatenate([np.arange(16, 32), np.arange(0, 16), np.arange(48, 64), np.arange(32, 48)])


def kernel(x, c, ctx, c_ctx, ada_w, ada_b, norm_g, out_w, mlp_w1, mlp_w2, hg_lb, ev_w_in, ml_conv, ml_gate_b, ml_norm, rw_mu, rw_w0, rw_w2, rw_a0, rw_a2, rw_g2, rw_kk, rw_ka, rw_rk, rw_ln_w, rw_ln_b, od_w_in, hg_norm, mla_q_norm, mla_w_qb, mla_kv_norm, mla_w_kvb):
    b, n, d = x.shape
    n_ctx = ctx.shape[1]
    t = n_ctx + n
    assert d == D_MODEL and b + 1 <= SUBLANES
    assert n_ctx % ROW_TILE == 0 and n % ROW_TILE == 0 and n % ATTN_TQ == 0
    assert n_ctx % ML_CHUNK == 0 and n % ML_CHUNK == 0 and n_ctx % HG_BLOCK == 0 and n % HG_BLOCK == 0
    nctx_t = n_ctx // ROW_TILE
    row1 = lambda a: a.reshape(1, -1)

    c8 = jnp.concatenate([c, c_ctx[None], jnp.zeros((SUBLANES - b - 1, d), F32)], axis=0)
    mod = _ada_mod(c8, ada_w, ada_b).reshape(-1, SUBLANES, N_MOD, d)

    def layer_mods(l):
        lat = mod[l, :b]
        cm = jnp.broadcast_to(mod[l, b][None], lat.shape)
        return jnp.stack([cm, lat], axis=1)[:, :, :, None, :]

    outw_bf, w1_bf, w2_bf = _bf(out_w), _bf(mlp_w1), _bf(mlp_w2)

    l, e = 0, 0
    mods = layer_mods(l)
    wi = ev_w_in[e]
    w_pad = jnp.concatenate([wi[:, 2064:3984], wi[:, 2048:2064], jnp.zeros((d, LANES - 16), F32), wi[:, 0:2048]], axis=1)
    gate_b = jnp.pad(ml_gate_b[e].reshape(1, 16), ((0, 0), (0, LANES - 16)))
    zeros64 = jnp.zeros((64, RW_WIDTH), F32)
    w2cat = jnp.stack([jnp.concatenate([rw_w2[e, 0], zeros64], 0), jnp.concatenate([zeros64, rw_w2[e, 1]], 0)])
    a2cat = jnp.stack([jnp.concatenate([rw_a2[e, 0], zeros64], 0), jnp.concatenate([zeros64, rw_a2[e, 1]], 0)])
    proj, qk, gates, gates_t, rws, rw_lw, rw_kb, bg = _even_in(
        ctx, x, row1(norm_g[l, 0]), mods, _bf(w_pad),
        ml_conv[e], gate_b, row1(rw_mu[e]), rw_w0[e], w2cat, rw_a0[e], a2cat, rw_g2[e],
        row1(rw_kk[e]), row1(rw_ka[e]), row1(rw_rk[e]))

    hm = _mlstm(qk, proj, gates, gates_t, n_ctx)
    yrw = _rwkv(rws, rw_lw, rw_kb, n_ctx)
    xall = _even_post(ctx, x, hm, proj, yrw, bg, row1(ml_norm[e]), row1(rw_ln_w[e]), row1(rw_ln_b[e]),
                      outw_bf, w1_bf, w2_bf, norm_g, mods, l)

    l, o = 1, 0
    mods = layer_mods(l)
    lb_all = jax.nn.softmax(hg_lb.astype(F32), axis=0)
    lb_all = jnp.cumsum(lb_all, axis=0) - lb_all[0]
    wi = od_w_in[o]
    kr = wi[:, 2944:3008]
    z = lambda k: jnp.zeros((d, k), F32)
    w_pad = jnp.concatenate([wi[:, 0:2944], kr, z(64), kr[:, _ROPE_SWAP], z(64)], axis=1)
    cos, sin = _rope_tables(n, n_ctx)
    wkv = mla_w_kvb[o].reshape(MLA_KV_RANK, MLA_HEADS, MLA_NOPE + MLA_V)
    wk = wkv[:, :, :MLA_NOPE].reshape(MLA_KV_RANK, MLA_HEADS * MLA_NOPE)
    wvt = jnp.transpose(wkv[:, :, MLA_NOPE:], (1, 2, 0))
    wq = mla_w_qb[o].reshape(MLA_Q_RANK, MLA_HEADS, MLA_NOPE + MLA_ROPE)
    zq = lambda k: jnp.zeros((MLA_Q_RANK, MLA_HEADS, k), F32)
    wq1 = jnp.concatenate([wq, zq(64)], axis=-1).reshape(MLA_Q_RANK, -1)
    wq2 = jnp.concatenate([wq[:, :, MLA_NOPE:][:, :, _ROPE_SWAP], zq(64)], axis=-1).reshape(MLA_Q_RANK, -1)
    proj, k_all, v_all, q_all = _odd_in(
        xall, row1(norm_g[l, 0]), mods, _bf(w_pad), nctx_t, cos, sin, row1(mla_kv_norm[o]), row1(mla_q_norm[o]),
        _bf(wk), _bf(wvt), _bf(wq1), _bf(wq2))

    ohg = _hgrn(proj, row1(lb_all[l]), n_ctx)
    att = _attention(q_all, k_all, v_all)

    return _odd_post(xall, ohg, proj, att, row1(hg_norm[o]), outw_bf, w1_bf, w2_bf, norm_g, mods, n_ctx, l)
```

```python
import functools

import numpy as np
import jax
import jax.numpy as jnp
from jax import lax
from jax.experimental import pallas as pl
from jax.experimental.pallas import tpu as pltpu

F32 = jnp.float32
BF16 = jnp.bfloat16

D_MODEL = 1024
GRID_W = 64
D_FF = 4 * D_MODEL
N_MOD = 6
EPS = 1e-6
ML_WIDTH = 512
ML_HEADS = 4
ML_DIM = 128
RW_WIDTH = 512
RW_HEAD_DIM = 64
RW_GN_EPS = 64e-5
RW_FEAT = 1920
HG_WIDTH = 512
HG_HEADS = 4
HG_DIM = 128
MLA_HEADS = 4
MLA_NOPE = 128
MLA_ROPE = 64
MLA_V = 128
MLA_Q_RANK = 256
MLA_KV_RANK = 128
ROPE_BASE = 10000.0

LANES = 128
SUBLANES = 8
HALO = 16
VMEM_LIMIT_BYTES = 56 * 2**20

ROW_TILE = 256
LAT_TILE = 512
FF_CHUNK = 1024
ML_CHUNK = 128
ML_BLOCK = 256
RW_CHUNK = 64
RW_BLOCK = 256
HG_BLOCK = 256
HG_CHUNK = 32
ATTN_TQ = 512
ATTN_SUB = 256


def _dot(a, b, prec=None):
    return jnp.dot(a, b, preferred_element_type=F32, precision=prec)


def _dot_nt(a, b, prec=None):
    return lax.dot_general(a, b, (((1,), (1,)), ((), ())), preferred_element_type=F32, precision=prec)


def _dot_tn(a, b, prec=None):
    return lax.dot_general(a, b, (((0,), (0,)), ((), ())), preferred_element_type=F32, precision=prec)


def _bf(x):
    return x.astype(BF16)


def _mm(a, b):
    return _dot(_bf(a), _bf(b))


def _mm_nt(a, b):
    return _dot_nt(_bf(a), _bf(b))


def _mm_tn(a, b):
    return _dot_tn(_bf(a), _bf(b))


def _split(x):
    hi = _bf(x)
    return hi, _bf(x - hi.astype(F32))


def _sel_dot(sel, x):
    hi, lo = _split(x)
    return _dot(sel, hi) + _dot(sel, lo)


def _dot_sel(x, sel):
    hi, lo = _split(x)
    return _dot(hi, sel) + _dot(lo, sel)


def _dot_sel_nt(x, sel):
    hi, lo = _split(x)
    return _dot_nt(hi, sel) + _dot_nt(lo, sel)


def _dot3(a, b):
    ah, al = _split(a)
    bh, bl = _split(b)
    return _dot(ah, bh) + _dot(ah, bl) + _dot(al, bh)


def _group_mean(x, group):
    low = lax.broadcasted_iota(jnp.int32, (1, LANES), 1) < group
    outs = []
    for c in range(x.shape[1] // LANES):
        blk = x[:, c * LANES:(c + 1) * LANES]
        total = jnp.sum(blk, axis=-1, keepdims=True)
        if group == LANES:
            outs.append(jnp.broadcast_to(total, blk.shape))
        else:
            first = jnp.sum(jnp.where(low, blk, 0.0), axis=-1, keepdims=True)
            outs.append(jnp.where(low, first, total - first))
    return jnp.concatenate(outs, axis=-1) * (1.0 / group)


def _sigmoid(x):
    return 1.0 / (1.0 + jnp.exp(-x))


def _log_sigmoid(x):
    return jnp.minimum(x, 0.0) - jnp.log(1.0 + jnp.exp(-jnp.abs(x)))


def _softplus(x):
    return jnp.maximum(x, 0.0) + jnp.log(1.0 + jnp.exp(-jnp.abs(x)))


def _rms(x, g):
    return x * lax.rsqrt(jnp.mean(x * x, axis=-1, keepdims=True) + EPS) * g


def _order_mask(n, sgn, strict):
    row = lax.broadcasted_iota(jnp.int32, (n, n), 0)
    col = lax.broadcasted_iota(jnp.int32, (n, n), 1)
    diff = (row - col) * sgn
    return diff > 0 if strict else diff >= 0


def _chunk_index(d, j, nctx, nc):
    bwd = jnp.where(j < nctx, nctx - 1 - j, nc - 1 - (j - nctx))
    return jnp.where(d == 0, j, bwd)


def _params(sem):
    return pltpu.CompilerParams(dimension_semantics=sem, vmem_limit_bytes=VMEM_LIMIT_BYTES)


def _const_spec(shape):
    nd = len(shape)
    return pl.BlockSpec(shape, lambda *_: (0,) * nd, pipeline_mode=pl.Buffered(1))


def _mod_spec(k, seg_of):
    return pl.BlockSpec((None, None, None, 1, D_MODEL), lambda b, i: (b, seg_of(i), k, 0, 0))


def _ada_body(c_ref, w_ref, b_ref, o_ref):
    c = c_ref[...]
    o_ref[...] = _dot3(c * _sigmoid(c), w_ref[...]) + b_ref[...]


def _ada_mod(c8, ada_w, ada_b):
    depth, d, n = ada_w.shape
    tn = 1024
    return pl.pallas_call(
        _ada_body,
        grid=(depth, n // tn),
        in_specs=[
            _const_spec((SUBLANES, d)),
            pl.BlockSpec((None, d, tn), lambda l, j: (l, 0, j)),
            pl.BlockSpec((None, 1, tn), lambda l, j: (l, 0, j)),
        ],
        out_specs=pl.BlockSpec((None, SUBLANES, tn), lambda l, j: (l, 0, j)),
        out_shape=jax.ShapeDtypeStruct((depth, SUBLANES, n), F32),
        compiler_params=_params(("parallel", "parallel")),
        name="ada_mod",
    )(c8, ada_w, ada_b.reshape(depth, 1, n))


EV_SHIFTED = 3072


def _even_in_body(nt, c_ref, x_ref, xp_ref, xn_ref, g_ref, sh_ref, sc_ref, w_ref,
                  conv_ref, gb_ref, mu_ref, w0_ref, w2_ref, a0_ref, a2_ref, g2_ref,
                  kk_ref, ka_ref, rk_ref,
                  ovo_ref, oqk_ref, og_ref, ogt_ref, orws_ref, olw_ref, okb_ref, obg_ref):
    i = pl.program_id(1)
    has_prev = (i > 1).astype(F32)
    has_next = jnp.logical_and(i != 0, i != nt - 1).astype(F32)
    tm = x_ref.shape[0]
    ext = tm + 2 * HALO
    row = lax.broadcasted_iota(jnp.int32, (tm, 1), 0)
    keep_dn = jnp.where(row == 0, has_prev, 1.0)
    keep_up = jnp.where(row == tm - 1, has_next, 1.0)

    def normed(xx):
        return _bf(_rms(xx, g_ref[...]) * (1.0 + sc_ref[...]) + sh_ref[...])

    h = normed(jnp.where(i == 0, c_ref[...], x_ref[...]))
    ovo_ref[...] = _dot(h, w_ref[:, EV_SHIFTED:]).astype(ovo_ref.dtype)
    hcat = jnp.concatenate([normed(xp_ref[...]), h, normed(xn_ref[...])], axis=0)
    proj = _dot(hcat, w_ref[:, 0:EV_SHIFTED])

    def neighbours(cols):
        blk = proj[:, cols]
        dn = pltpu.roll(blk, 1, 0)[HALO:HALO + tm] * keep_dn
        up = pltpu.roll(blk, ext - 1, 0)[HALO:HALO + tm] * keep_up
        return blk[HALO:HALO + tm], dn, up

    qk, dn, up = neighbours(slice(2048, 3072))
    oqk_ref[...] = (dn * conv_ref[0:1, :] + qk * conv_ref[1:2, :] + up * conv_ref[2:3, :]).astype(oqk_ref.dtype)

    gv = proj[HALO:HALO + tm, RW_FEAT:RW_FEAT + LANES] + gb_ref[...]
    lane = lax.broadcasted_iota(jnp.int32, (1, LANES), 1)
    is_f = jnp.logical_and(jnp.bitwise_and(jnp.right_shift(lane, 2), 1) == 1, lane < 16)
    gates = jnp.where(is_f, _log_sigmoid(gv), gv)
    og_ref[...] = gates
    ogt_ref[...] = gates.T[0:4 * ML_HEADS, :]

    cur, dn, up = neighbours(slice(0, RW_FEAT))
    rf = cur + (0.5 * (dn + up) - cur) * mu_ref[...]
    w = RW_WIDTH
    r, k, v = rf[:, 0:w], rf[:, w:2 * w], rf[:, 2 * w:3 * w]
    wl = rf[:, 3 * w:3 * w + LANES]
    al = rf[:, 3 * w + LANES:3 * w + 2 * LANES]
    gl = rf[:, 3 * w + 2 * LANES:3 * w + 3 * LANES]
    kk = k * kk_ref[...]
    kk = kk * lax.rsqrt(_group_mean(kk * kk, RW_HEAD_DIM) * RW_HEAD_DIM + 1e-12)
    twl = jnp.tanh(wl)
    kd_sum = jnp.zeros_like(k)
    for d in range(2):
        w_log = -_softplus(-(w0_ref[d:d + 1, :] + _dot3(twl, w2_ref[d]))) - 0.5
        a = _sigmoid(a0_ref[d:d + 1, :] + _dot3(al, a2_ref[d]))
        kd = k * (1.0 + (a - 1.0) * ka_ref[...])
        kd_sum = kd_sum + kd
        olw_ref[d] = -jnp.exp(w_log)
        okb_ref[d, :, 0:w] = kd.astype(okb_ref.dtype)
        okb_ref[d, :, w:2 * w] = (kk * a).astype(okb_ref.dtype)
    orws_ref[:, 0:w] = r.astype(orws_ref.dtype)
    orws_ref[:, w:2 * w] = v.astype(orws_ref.dtype)
    orws_ref[:, 2 * w:3 * w] = (-kk).astype(orws_ref.dtype)
    bonus = _group_mean(r * kd_sum * rk_ref[...], RW_HEAD_DIM) * RW_HEAD_DIM * v
    obg_ref[:, 0:w] = bonus.astype(obg_ref.dtype)
    obg_ref[:, w:2 * w] = _dot3(_sigmoid(gl), g2_ref[...]).astype(obg_ref.dtype)


def _even_in(ctx, x, g, mods, w_in, conv, gate_b, mu, w0, w2cat, a0, a2cat, g2, kk, ka, rk):
    b, n, d = x.shape
    tm = ROW_TILE
    assert ctx.shape[1] == tm
    t = n + tm
    nt = t // tm
    rh = tm // HALO
    last_h = n // HALO - 1
    seg = lambda i: (i >= 1).astype(jnp.int32)
    cur = lambda c: (lambda bb, i: (bb, i, c))
    lat = lambda i: jnp.maximum(i - 1, 0)
    w = RW_WIDTH
    return pl.pallas_call(
        functools.partial(_even_in_body, nt),
        grid=(b, nt),
        in_specs=[
            pl.BlockSpec((None, tm, d), lambda bb, i: (bb, 0, 0)),
            pl.BlockSpec((None, tm, d), lambda bb, i: (bb, lat(i), 0)),
            pl.BlockSpec((None, HALO, d), lambda bb, i: (bb, jnp.maximum(lat(i) * rh - 1, 0), 0)),
            pl.BlockSpec((None, HALO, d), lambda bb, i: (bb, jnp.minimum((lat(i) + 1) * rh, last_h), 0)),
            _const_spec((1, d)), _mod_spec(0, seg), _mod_spec(1, seg), _const_spec(w_in.shape),
            _const_spec(conv.shape), _const_spec(gate_b.shape), _const_spec(mu.shape),
            _const_spec(w0.shape), _const_spec(w2cat.shape), _const_spec(a0.shape),
            _const_spec(a2cat.shape), _const_spec(g2.shape), _const_spec(kk.shape),
            _const_spec(ka.shape), _const_spec(rk.shape),
        ],
        out_specs=[
            pl.BlockSpec((None, tm, 1024), cur(0)),
            pl.BlockSpec((None, tm, 1024), cur(0)),
            pl.BlockSpec((None, tm, LANES), cur(0)),
            pl.BlockSpec((None, 4 * ML_HEADS, tm), lambda bb, i: (bb, 0, i)),
            pl.BlockSpec((None, tm, 3 * w), cur(0)),
            pl.BlockSpec((2, None, tm, w), lambda bb, i: (0, bb, i, 0)),
            pl.BlockSpec((2, None, tm, 2 * w), lambda bb, i: (0, bb, i, 0)),
            pl.BlockSpec((None, tm, 2 * w), cur(0)),
        ],
        out_shape=[
            jax.ShapeDtypeStruct((b, t, 1024), BF16),
            jax.ShapeDtypeStruct((b, t, 1024), BF16),
            jax.ShapeDtypeStruct((b, t, LANES), F32),
            jax.ShapeDtypeStruct((b, 4 * ML_HEADS, t), F32),
            jax.ShapeDtypeStruct((b, t, 3 * w), BF16),
            jax.ShapeDtypeStruct((2, b, t, w), F32),
            jax.ShapeDtypeStruct((2, b, t, 2 * w), BF16),
            jax.ShapeDtypeStruct((b, t, 2 * w), BF16),
        ],
        compiler_params=_params(("parallel", "parallel")),
        name="even_in",
    )(ctx, x, x, x, g, mods, mods, w_in, conv, gate_b, mu, w0, w2cat, a0, a2cat, g2, kk, ka, rk)


def _mlstm_body(q_ref, k_ref, v_ref, gc_ref, gr_ref, o_ref, c_scr, n_scr, m_scr):
    d = pl.program_id(0)
    j = pl.program_id(1)
    n_batch = q_ref.shape[0]
    n_rows = ML_CHUNK
    n_sub = q_ref.shape[1] // n_rows

    @pl.when(j == 0)
    def _():
        c_scr[...] = jnp.zeros_like(c_scr)
        n_scr[...] = jnp.zeros_like(n_scr)
        m_scr[...] = jnp.zeros_like(m_scr)

    mask = _order_mask(n_rows, 1 - 2 * d, strict=False)
    mf = mask.astype(BF16)
    row_id = lax.broadcasted_iota(jnp.int32, (n_rows, 1), 0)
    is_last = row_id == (n_rows - 1) * (1 - d)
    pick_row = lax.broadcasted_iota(jnp.int32, (LANES, LANES), 0)
    pick = [(pick_row == 2 * ML_HEADS * d + c).astype(BF16) for c in range(2 * ML_HEADS)]
    ones_cols = jnp.ones((n_rows, LANES), BF16)

    rows = [pl.ds(pl.multiple_of(jnp.where(d == 0, sc, n_sub - 1 - sc) * n_rows, n_rows), n_rows)
            for sc in range(n_sub)]
    heads = [slice(h * ML_DIM, (h + 1) * ML_DIM) for h in range(ML_HEADS)]
    units = [(sc, bi, h) for sc in range(n_sub) for bi in range(n_batch) for h in range(ML_HEADS)]
    idx = range(len(units))
    carried = [bi * ML_HEADS + h for _, bi, h in units]
    gc, b_col, gr, b_row = {}, {}, {}, {}
    for bi in range(n_batch):
        gr_all = gr_ref[bi]
        for sc in range(n_sub):
            gc[sc, bi] = gc_ref[bi, rows[sc], :]
            parts = [gr_all[:, c * n_rows:(c + 1) * n_rows] for c in (sc, n_sub - 1 - sc)]
            gr[sc, bi] = jnp.where(d == 0, parts[0], parts[1])
            b_col[sc, bi] = _sel_dot(mf, gc[sc, bi])
            b_row[sc, bi] = _dot_sel_nt(gr[sc, bi], mf)
    q_f = [q_ref[bi, rows[sc], heads[h]] for sc, bi, h in units]
    q = [_bf(x) for x in q_f]
    kf = [k_ref[bi, rows[sc], heads[h]].astype(F32) * (ML_DIM ** -0.5) for sc, bi, h in units]
    k = [_bf(x) for x in kf]
    vf = [v_ref[bi, rows[sc], heads[h]].astype(F32) for sc, bi, h in units]
    qk = [_dot_nt(q[u], k[u]) for u in idx]
    ic = [_dot_sel(gc[sc, bi], pick[h]) for sc, bi, h in units]
    bc = [_dot_sel(b_col[sc, bi], pick[ML_HEADS + h]) for sc, bi, h in units]
    b_tot = [jnp.sum(jnp.where(is_last, bc[u], 0.0), axis=0, keepdims=True) for u in idx]
    logw = [b_tot[u] - bc[u] + ic[u] for u in idx]
    logw_max = [jnp.max(logw[u], axis=0, keepdims=True) for u in idx]
    logd = [jnp.where(mask, bc[u] - b_row[sc, bi][ML_HEADS + h:ML_HEADS + h + 1, :] + gr[sc, bi][h:h + 1, :], -1e30)
            for u, (sc, bi, h) in enumerate(units)]
    logd_max = [jnp.max(x, axis=-1, keepdims=True) for x in logd]
    m_run = [m_scr[c][0:1, :] for c in range(n_batch * ML_HEADS)]
    wgt, decay, s, m_t, w_inter = [], [], [], [], []
    for u in idx:
        m_prev = m_run[carried[u]]
        m_new = jnp.maximum(b_tot[u] + m_prev, logw_max[u])
        wgt.append(jnp.exp(logw[u] - m_new))
        decay.append(jnp.exp(b_tot[u] + m_prev - m_new))
        m_inter = bc[u] + m_prev
        m_t.append(jnp.maximum(m_inter, logd_max[u]))
        s.append(qk[u] * jnp.exp(logd[u] - m_t[u]))
        w_inter.append(jnp.exp(m_inter - m_t[u]))
        m_run[carried[u]] = m_new
    kv = [_dot_tn(_bf(vf[u] * wgt[u]), k[u]) for u in idx]
    sv = [_dot(_bf(s[u]), jnp.concatenate([_bf(vf[u]), ones_cols], axis=1)) for u in idx]
    c_run = [c_scr[c] for c in range(n_batch * ML_HEADS)]
    n_run = [n_scr[c][0:1, :] for c in range(n_batch * ML_HEADS)]
    per_chunk = n_batch * ML_HEADS
    for sc in range(n_sub):
        ids = range(sc * per_chunk, (sc + 1) * per_chunk)
        qc = [_dot_nt(q[u], jnp.concatenate([_bf(c_run[carried[u]]),
                                             _bf(jnp.broadcast_to(n_run[carried[u]], (LANES, ML_DIM)))], axis=0))
              for u in ids]
        for u, x in zip(ids, qc):
            _, bi, h = units[u]
            num = sv[u][:, :ML_DIM] + w_inter[u] * x[:, :ML_DIM]
            den = sv[u][:, ML_DIM:] + w_inter[u] * x[:, ML_DIM:]
            o_ref[bi, rows[sc], heads[h]] = num / jnp.maximum(jnp.abs(den), jnp.exp(-m_t[u]))
        for u in ids:
            c = carried[u]
            c_run[c] = decay[u] * c_run[c] + kv[u]
            n_run[c] = decay[u] * n_run[c] + jnp.sum(wgt[u] * kf[u], axis=0, keepdims=True)
    for c in range(per_chunk):
        c_scr[c] = c_run[c]
        n_scr[c] = jnp.broadcast_to(n_run[c], (SUBLANES, ML_DIM))
        m_scr[c] = jnp.broadcast_to(m_run[c], (SUBLANES, LANES))


def _mlstm(qk, proj, gates, gates_t, n_ctx):
    b, t, _ = qk.shape
    n_rows = ML_BLOCK
    nc, nctx = t // n_rows, n_ctx // n_rows
    ci = lambda d, j: _chunk_index(d, j, nctx, nc)
    return pl.pallas_call(
        _mlstm_body,
        grid=(2, nc),
        in_specs=[
            pl.BlockSpec((b, n_rows, ML_WIDTH), lambda d, j: (0, ci(d, j), 0)),
            pl.BlockSpec((b, n_rows, ML_WIDTH), lambda d, j: (0, ci(d, j), 1)),
            pl.BlockSpec((b, n_rows, ML_WIDTH), lambda d, j: (0, ci(d, j), 0)),
            pl.BlockSpec((b, n_rows, LANES), lambda d, j: (0, ci(d, j), 0)),
            pl.BlockSpec((b, 2 * ML_HEADS, n_rows), lambda d, j: (0, d, ci(d, j))),
        ],
        out_specs=pl.BlockSpec((None, b, n_rows, ML_WIDTH), lambda d, j: (d, 0, ci(d, j), 0)),
        out_shape=jax.ShapeDtypeStruct((2, b, t, ML_WIDTH), F32),
        scratch_shapes=[
            pltpu.VMEM((b * ML_HEADS, ML_DIM, ML_DIM), F32),
            pltpu.VMEM((b * ML_HEADS, SUBLANES, ML_DIM), F32),
            pltpu.VMEM((b * ML_HEADS, SUBLANES, LANES), F32),
        ],
        compiler_params=_params(("parallel", "arbitrary")),
        name="mlstm",
    )(qk, qk, proj, gates, gates_t)


def _rwkv_body(r_ref, v_ref, a_ref, lw_ref, k_ref, b_ref, o_ref, h_scr):
    d = pl.program_id(0)
    j = pl.program_id(1)
    n_rows = RW_CHUNK
    two = 2 * n_rows
    shift = n_rows.bit_length() - 1

    @pl.when(j == 0)
    def _():
        h_scr[...] = jnp.zeros_like(h_scr)

    sgn = 1 - 2 * d
    cum = _order_mask(n_rows, sgn, strict=False).astype(BF16)
    row = lax.broadcasted_iota(jnp.int32, (two, two), 0)
    col = lax.broadcasted_iota(jnp.int32, (two, two), 1)
    same = jnp.right_shift(row, shift) == jnp.right_shift(col, shift)
    diff = (jnp.bitwise_and(row, n_rows - 1) - jnp.bitwise_and(col, n_rows - 1)) * sgn
    m_strict = jnp.logical_and(same, diff > 0)
    m_incl = jnp.logical_and(same, diff >= 0)
    eye = (row == col).astype(F32)
    head0 = lax.broadcasted_iota(jnp.int32, (1, LANES), 1) < RW_HEAD_DIM

    def stack(x):
        return jnp.concatenate([jnp.where(head0, x, 0.0), jnp.where(head0, 0.0, x)], axis=0)

    def fold(x):
        return x[:n_rows] + x[n_rows:]

    n_batch = r_ref.shape[0]
    n_sub = r_ref.shape[1] // n_rows
    lane_groups = [slice(p * LANES, (p + 1) * LANES) for p in range(RW_WIDTH // LANES)]
    rows = [pl.ds(pl.multiple_of(jnp.where(d == 0, sc, n_sub - 1 - sc) * n_rows, n_rows), n_rows)
            for sc in range(n_sub)]
    groups = [(sc, bi, p) for sc in range(n_sub) for bi in range(n_batch) for p in range(len(lane_groups))]
    x_s, r_s, v_s, bh_s, kh_s, bi_s, ki_s, e_tot = [], [], [], [], [], [], [], []
    for sc in range(n_sub):
        for bi in range(n_batch):
            lw = lw_ref[bi, rows[sc], :]
            g = _sel_dot(cum, lw)
            g_tot = jnp.sum(lw, axis=0, keepdims=True)
            inv = jnp.exp(-g)
            to_end = jnp.exp(g_tot - g)
            at = a_ref[bi, rows[sc], :].astype(F32) * jnp.exp(g - lw)
            rt = r_ref[bi, rows[sc], :].astype(F32) * jnp.exp(g)
            b_in = b_ref[bi, rows[sc], :].astype(F32)
            k_in = k_ref[bi, rows[sc], :].astype(F32)
            vv = v_ref[bi, rows[sc], :].astype(F32)
            for sl in lane_groups:
                x_s.append(stack(at[:, sl]))
                r_s.append(stack(rt[:, sl]))
                v_s.append(stack(vv[:, sl]))
                bh_s.append(stack((b_in * to_end)[:, sl]))
                kh_s.append(stack((k_in * to_end)[:, sl]))
                bi_s.append(stack((b_in * inv)[:, sl]))
                ki_s.append(stack((k_in * inv)[:, sl]))
                e_tot.append(jnp.exp(g_tot)[:, sl])
    big = [_mm_nt(jnp.concatenate([x_s[p], r_s[p]], axis=0), jnp.concatenate([bi_s[p], ki_s[p]], axis=0))
           for p in range(len(groups))]
    a_ab = [jnp.where(m_strict, m[:two, :two], 0.0) for m in big]
    a_rb = [jnp.where(m_incl, m[two:, :two], 0.0) for m in big]
    a_k = [jnp.concatenate([jnp.where(m_strict, m[:two, two:], 0.0), jnp.where(m_incl, m[two:, two:], 0.0)], axis=0)
           for m in big]
    av = [_mm(a_k[p], v_s[p]) for p in range(len(groups))]
    t_inv = [eye + m for m in a_ab]
    pw = [_mm(m, m) for m in a_ab]
    for level in range(1, shift):
        last = level == shift - 1
        res = [_mm(t if last else jnp.concatenate([t, m], axis=0), m) for t, m in zip(t_inv, pw)]
        t_inv = [t + r[:two] for t, r in zip(t_inv, res)]
        if not last:
            pw = [r[two:] for r in res]
    wu = [_mm(t_inv[p], jnp.concatenate([x_s[p], av[p][:two]], axis=1)) for p in range(len(groups))]
    rb = [_mm(a_rb[p], wu[p]) for p in range(len(groups))]
    upd = [_mm_tn(wu[p], bh_s[p]) for p in range(len(groups))]
    gt2 = [_mm_tn(v_s[p], kh_s[p]) for p in range(len(groups))]
    q_eff = [fold(r_s[p] + rb[p][:, :LANES]) for p in range(len(groups))]
    y_in = [fold(rb[p][:, LANES:] + av[p][two:]) for p in range(len(groups))]
    per_chunk = n_batch * len(lane_groups)
    h_t = [h_scr[u] for u in range(per_chunk)]
    for sc in range(n_sub):
        ids = range(sc * per_chunk, (sc + 1) * per_chunk)
        for u, p in enumerate(ids):
            _, bi, lg = groups[p]
            o_ref[bi, rows[sc], lane_groups[lg]] = _mm_nt(q_eff[p], h_t[u]) + y_in[p]
        h_t = [e_tot[p] * h_t[u] + _mm(h_t[u], upd[p][:LANES]) + upd[p][LANES:] + gt2[p] for u, p in enumerate(ids)]
    for u in range(per_chunk):
        h_scr[u] = h_t[u]


def _rwkv(rws, lw, kb, n_ctx):
    b, t, _ = rws.shape
    n_rows = RW_BLOCK
    nc, nctx = t // n_rows, n_ctx // n_rows
    ci = lambda d, j: _chunk_index(d, j, nctx, nc)
    w = RW_WIDTH
    shared = lambda c: pl.BlockSpec((b, n_rows, w), lambda d, j: (0, ci(d, j), c))
    per_dir = lambda c: pl.BlockSpec((None, b, n_rows, w), lambda d, j: (d, 0, ci(d, j), c))
    return pl.pallas_call(
        _rwkv_body,
        grid=(2, nc),
        in_specs=[shared(0), shared(1), shared(2), per_dir(0), per_dir(0), per_dir(1)],
        out_specs=pl.BlockSpec((None, b, n_rows, w), lambda d, j: (d, 0, ci(d, j), 0)),
        out_shape=jax.ShapeDtypeStruct((2, b, t, w), F32),
        scratch_shapes=[pltpu.VMEM((b * w // LANES, LANES, LANES), F32)],
        compiler_params=_params(("parallel", "arbitrary")),
        name="rwkv7",
    )(rws, rws, rws, lw, kb, kb)


def _tail(x, mix, outw_ref, w1_ref, w2_ref, ng_ref, ga_ref, shm_ref, scm_ref, gm_ref):
    x1 = x + ga_ref[...] * _rms(_dot(_bf(mix), outw_ref[...]), ng_ref[1:2, :])
    h = _bf(_rms(x1, ng_ref[2:3, :]) * (1.0 + scm_ref[...]) + shm_ref[...])
    chunks = [slice(c * FF_CHUNK, (c + 1) * FF_CHUNK) for c in range(D_FF // FF_CHUNK)]
    u = [jnp.maximum(_dot(h, w1_ref[:, cols]), 0.0) for cols in chunks]
    dd = _dot(_bf(u[0] * u[0]), w2_ref[chunks[0], :])
    for uc, cols in zip(u[1:], chunks[1:]):
        dd = dd + _dot(_bf(uc * uc), w2_ref[cols, :])
    return x1 + gm_ref[...] * _rms(dd, ng_ref[3:4, :])


def _even_post_body(x_ref, hm_ref, mo_ref, yrw_ref, bg_ref, mln_ref, lnw_ref, lnb_ref,
                    outw_ref, w1_ref, w2_ref, ng_ref, ga_ref, shm_ref, scm_ref, gm_ref, o_ref):
    w = RW_WIDTH
    ym = hm_ref[0, 0] + hm_ref[1, 0]
    ym = (ym * lax.rsqrt(_group_mean(ym * ym, ML_DIM) + EPS) * mln_ref[...]
          * _sigmoid(mo_ref[0].astype(F32)))
    y = yrw_ref[0, 0] + yrw_ref[1, 0]
    yc = y - _group_mean(y, RW_HEAD_DIM)
    y = (yc * lax.rsqrt(_group_mean(yc * yc, RW_HEAD_DIM) + RW_GN_EPS) * lnw_ref[...] + lnb_ref[...]
         + bg_ref[0, :, 0:w].astype(F32))
    mix = jnp.concatenate([ym, y * bg_ref[0, :, w:2 * w].astype(F32)], axis=-1)
    o_ref[...] = _tail(x_ref[...], mix, outw_ref, w1_ref, w2_ref, ng_ref, ga_ref, shm_ref, scm_ref, gm_ref)


def _tail_specs(seg, layer):
    d = D_MODEL
    pick = lambda *shape: pl.BlockSpec((None,) + shape, lambda *_: (layer, 0, 0), pipeline_mode=pl.Buffered(1))
    return [
        pick(d, d), pick(d, D_FF), pick(D_FF, d), pick(4, d),
        _mod_spec(2, seg), _mod_spec(3, seg), _mod_spec(4, seg), _mod_spec(5, seg),
    ]


def _even_post(x_seg, first_row, segment, tm, hm, proj, yrw, bg, mln, lnw, lnb, outw, w1, w2, ng, mods, layer):
    b, rows, d = x_seg.shape
    seg = lambda i: segment
    w = RW_WIDTH
    el = lambda *shape: tuple(pl.Element(s) for s in shape)
    row0 = lambda i: pl.multiple_of(first_row + i * tm, int(np.gcd(first_row, tm)))
    return pl.pallas_call(
        _even_post_body,
        grid=(b, rows // tm),
        in_specs=[
            pl.BlockSpec((None, tm, d), lambda bb, i: (bb, i, 0)),
            pl.BlockSpec(el(2, 1, tm, w), lambda bb, i: (0, bb, row0(i), 0)),
            pl.BlockSpec(el(1, tm, w), lambda bb, i: (bb, row0(i), w)),
            pl.BlockSpec(el(2, 1, tm, w), lambda bb, i: (0, bb, row0(i), 0)),
            pl.BlockSpec(el(1, tm, 2 * w), lambda bb, i: (bb, row0(i), 0)),
            _const_spec((1, w)), _const_spec((1, w)), _const_spec((1, w)),
        ] + _tail_specs(seg, layer),
        out_specs=pl.BlockSpec((None, tm, d), lambda bb, i: (bb, i, 0)),
        out_shape=jax.ShapeDtypeStruct((b, rows, d), F32),
        compiler_params=_params(("parallel", "parallel")),
        name="even_post",
    )(x_seg, hm, proj, yrw, bg, mln, lnw, lnb, outw, w1, w2, ng, mods, mods, mods, mods)


def _hgrn_body(q_ref, f_ref, i_ref, lb_ref, o_ref, s_scr):
    d = pl.program_id(0)
    j = pl.program_id(1)
    n_batch, n_rows = q_ref.shape[0], q_ref.shape[1]
    n_sub = n_rows // HG_CHUNK

    @pl.when(j == 0)
    def _():
        s_scr[...] = jnp.zeros_like(s_scr)

    mask = _order_mask(HG_CHUNK, 1 - 2 * d, strict=False)
    mf = mask.astype(BF16)
    lb = lb_ref[...]
    heads = range(HG_HEADS)
    sls = [slice(h * HG_DIM, (h + 1) * HG_DIM) for h in heads]
    units = [(bi, h) for bi in range(n_batch) for h in heads]
    n_units = len(units)
    s_t = [s_scr[u] for u in range(n_units)]
    rows = [pl.ds(pl.multiple_of(jnp.where(d == 0, sc, n_sub - 1 - sc) * HG_CHUNK, HG_CHUNK), HG_CHUNK)
            for sc in range(n_sub)]
    loaded = [[(q_ref[bi, r, :], f_ref[bi, r, :], i_ref[bi, r, :]) for bi in range(n_batch)] for r in rows]
    q_in, e_tot, q_mids, k_mids, k_outs, vvs = [], [], [], [], [], []

    def prepare(sc):
        for lst in (q_in, e_tot, q_mids, k_mids, k_outs, vvs):
            lst.append([])
        for bi in range(n_batch):
            qa = loaded[sc][bi][0].astype(F32)
            pre = loaded[sc][bi][1].astype(F32)
            vv = loaded[sc][bi][2]
            q = qa * _sigmoid(qa) * (HG_DIM ** -0.5)
            e_abs = jnp.exp(-jnp.abs(pre))
            big = 1.0 / (1.0 + e_abs)
            small = e_abs * big
            log_f = jnp.log(lb + (1.0 - lb) * jnp.where(pre >= 0.0, big, small))
            kf = (1.0 - lb) * jnp.where(pre >= 0.0, small, big)
            bc = _sel_dot(mf, log_f)
            tot = jnp.sum(log_f, axis=0, keepdims=True)
            half = 0.5 * tot
            e_half = jnp.exp(half)
            q_mid_f = q * jnp.exp(bc - half)
            k_mid_f = kf * jnp.exp(half - bc)
            q_mid = _bf(q_mid_f)
            k_mid = _bf(k_mid_f)
            k_out = _bf(k_mid_f * e_half)
            q_all = _bf(q_mid_f * e_half)
            e_all = e_half * e_half
            q_mids[sc] += [q_mid[:, sl] for sl in sls]
            k_mids[sc] += [k_mid[:, sl] for sl in sls]
            k_outs[sc] += [k_out[:, sl] for sl in sls]
            vvs[sc] += [vv[:, sl] for sl in sls]
            q_in[sc] += [q_all[:, sl] for sl in sls]
            e_tot[sc] += [e_all[:, sl] for sl in sls]

    prepare(0)
    for sc in range(n_sub):
        if sc + 1 < n_sub:
            prepare(sc + 1)
        att = [_dot_nt(q_mids[sc][u], k_mids[sc][u]) for u in range(n_units)]
        kv = [_dot_tn(vvs[sc][u], k_outs[sc][u]) for u in range(n_units)]
        intra = [_dot(_bf(jnp.where(mask, att[u], 0.0)), vvs[sc][u]) for u in range(n_units)]
        for u, (bi, h) in enumerate(units):
            o_ref[bi, rows[sc], sls[h]] = intra[u] + _dot_nt(q_in[sc][u], _bf(s_t[u]))
        s_t = [e_tot[sc][u] * s_t[u] + kv[u] for u in range(n_units)]
    for u in range(n_units):
        s_scr[u] = s_t[u]


def _hgrn(proj, lb, n_ctx):
    b, t, _ = proj.shape
    n_rows = HG_BLOCK
    nc, nctx = t // n_rows, n_ctx // n_rows
    ci = lambda d, j: _chunk_index(d, j, nctx, nc)
    w = HG_WIDTH
    return pl.pallas_call(
        _hgrn_body,
        grid=(2, nc),
        in_specs=[
            pl.BlockSpec((b, n_rows, w), lambda d, j: (0, ci(d, j), 0)),
            pl.BlockSpec((b, n_rows, w), lambda d, j: (0, ci(d, j), 1 + d)),
            pl.BlockSpec((b, n_rows, w), lambda d, j: (0, ci(d, j), 3)),
            _const_spec((1, w)),
        ],
        out_specs=pl.BlockSpec((None, b, n_rows, w), lambda d, j: (d, 0, ci(d, j), 0)),
        out_shape=jax.ShapeDtypeStruct((2, b, t, w), F32),
        scratch_shapes=[pltpu.VMEM((b * HG_HEADS, HG_DIM, HG_DIM), F32)],
        compiler_params=_params(("parallel", "arbitrary")),
        name="hgrn2",
    )(proj, proj, proj, lb)


V_ROWS = MLA_V + SUBLANES
OD_MAIN = 5 * HG_WIDTH


def _odd_in_body(c_ref, x_ref, g_ref, sh_ref, sc_ref, w_ref, cos_ref, sin_ref, kvn_ref, qn_ref,
                 wk_ref, wvt_ref, wq1_ref, wq2_ref, proj_ref, k_ref, vt_ref, q_ref):
    x_in = jnp.where(pl.program_id(1) == 0, c_ref[...], x_ref[...])
    h = _bf(_rms(x_in, g_ref[...]) * (1.0 + sc_ref[...]) + sh_ref[...])
    mla = _dot(h, w_ref[:, OD_MAIN:])
    proj_ref[...] = _dot(h, w_ref[:, 0:OD_MAIN]).astype(proj_ref.dtype)
    cq = mla[:, 0:MLA_Q_RANK]
    base = MLA_Q_RANK
    ckv = mla[:, base:base + MLA_KV_RANK]
    cos, sin = cos_ref[...], sin_ref[...]
    k_rope = mla[:, base + LANES:base + 2 * LANES] * cos + mla[:, base + 2 * LANES:base + 3 * LANES] * sin
    kvn = _bf(_rms(ckv, kvn_ref[...]))
    kn = _dot(kvn, wk_ref[...])
    ones = jnp.ones((SUBLANES, mla.shape[0]), BF16)
    cn = _bf(_rms(cq, qn_ref[...]))
    qa = _dot(cn, wq1_ref[...])
    qb = _dot(cn, wq2_ref[...])
    scale = (MLA_NOPE + MLA_ROPE) ** -0.5 * float(np.log2(np.e))
    for hd in range(MLA_HEADS):
        k_ref[hd] = jnp.concatenate([kn[:, hd * MLA_NOPE:(hd + 1) * MLA_NOPE], k_rope], axis=-1).astype(BF16)
        vt = _dot_nt(wvt_ref[hd], kvn)
        vt_ref[hd] = jnp.concatenate([vt.astype(BF16), ones], axis=0)
        a = qa[:, hd * 256:(hd + 1) * 256]
        q_rope = a[:, MLA_NOPE:] * cos + qb[:, hd * LANES:(hd + 1) * LANES] * sin
        q_ref[hd] = (jnp.concatenate([a[:, 0:MLA_NOPE], q_rope], axis=-1) * scale).astype(BF16)


def _odd_in(xc, x, g, mods, w_in, cos, sin, kv_norm, q_norm, wk, wvt, wq1, wq2):
    b, n, d = x.shape
    tm = ROW_TILE
    assert xc.shape[1] == tm
    t = n + tm
    seg = lambda i: (i >= 1).astype(jnp.int32)
    q_blk = lambda bb, i: (bb, 0, jnp.maximum(i - 1, 0), 0)
    return pl.pallas_call(
        _odd_in_body,
        grid=(b, t // tm),
        in_specs=[
            pl.BlockSpec((None, tm, d), lambda bb, i: (bb, 0, 0)),
            pl.BlockSpec((None, tm, d), lambda bb, i: (bb, jnp.maximum(i - 1, 0), 0)),
            _const_spec((1, d)), _mod_spec(0, seg), _mod_spec(1, seg), _const_spec(w_in.shape),
            pl.BlockSpec((tm, LANES), lambda bb, i: (i, 0)),
            pl.BlockSpec((tm, LANES), lambda bb, i: (i, 0)),
            _const_spec((1, MLA_KV_RANK)), _const_spec((1, MLA_Q_RANK)),
            _const_spec(wk.shape), _const_spec(wvt.shape), _const_spec(wq1.shape), _const_spec(wq2.shape),
        ],
        out_specs=[
            pl.BlockSpec((None, tm, OD_MAIN), lambda bb, i: (bb, i, 0)),
            pl.BlockSpec((None, MLA_HEADS, tm, 256), lambda bb, i: (bb, 0, i, 0)),
            pl.BlockSpec((None, MLA_HEADS, V_ROWS, tm), lambda bb, i: (bb, 0, 0, i)),
            pl.BlockSpec((None, MLA_HEADS, tm, 256), q_blk),
        ],
        out_shape=[
            jax.ShapeDtypeStruct((b, t, OD_MAIN), BF16),
            jax.ShapeDtypeStruct((b, MLA_HEADS, t, 256), BF16),
            jax.ShapeDtypeStruct((b, MLA_HEADS, V_ROWS, t), BF16),
            jax.ShapeDtypeStruct((b, MLA_HEADS, n, 256), BF16),
        ],
        compiler_params=_params(("parallel", "arbitrary")),
        name="odd_in",
    )(xc, x, g, mods, mods, w_in, cos, sin, kv_norm, q_norm, wk, wvt, wq1, wq2)


def _attn_body(q_ref, k_ref, vt_ref, o_ref):
    k = k_ref[...]
    vt = vt_ref[...]
    subs = [slice(i * ATTN_SUB, (i + 1) * ATTN_SUB) for i in range(q_ref.shape[0] // ATTN_SUB)]
    st = [_dot_nt(k, q_ref[sl, :]) for sl in subs]
    pt = [_bf(jnp.exp2(s - jnp.max(s, axis=0, keepdims=True))) for s in st]
    ot = [_dot(vt, p) for p in pt]
    for sl, o in zip(subs, ot):
        o_ref[sl, :] = (o[0:MLA_V] / o[MLA_V:MLA_V + 1]).T


def _attention(q, k, vt):
    b, h, n, dq = q.shape
    t = k.shape[2]
    tq = ATTN_TQ
    return pl.pallas_call(
        _attn_body,
        grid=(b, h, n // tq),
        in_specs=[
            pl.BlockSpec((None, None, tq, dq), lambda bb, hh, i: (bb, hh, i, 0)),
            pl.BlockSpec((None, None, t, dq), lambda bb, hh, i: (bb, hh, 0, 0)),
            pl.BlockSpec((None, None, V_ROWS, t), lambda bb, hh, i: (bb, hh, 0, 0)),
        ],
        out_specs=pl.BlockSpec((None, tq, MLA_V), lambda bb, hh, i: (bb, i, hh)),
        out_shape=jax.ShapeDtypeStruct((b, n, h * MLA_V), F32),
        compiler_params=_params(("parallel", "parallel", "parallel")),
        name="mla_attention",
    )(q, k, vt)


def _odd_post_body(x_ref, o_ref_in, hg_ref, att_ref, hgn_ref,
                   outw_ref, w1_ref, w2_ref, ng_ref, ga_ref, shm_ref, scm_ref, gm_ref, o_ref):
    y = o_ref_in[0, 0] + o_ref_in[1, 0]
    g = hg_ref[0].astype(F32)
    y = y * lax.rsqrt(_group_mean(y * y, HG_DIM) + EPS) * hgn_ref[...] * (g * _sigmoid(g))
    mix = jnp.concatenate([y, att_ref[...]], axis=-1)
    o_ref[...] = _tail(x_ref[...], mix, outw_ref, w1_ref, w2_ref, ng_ref, ga_ref, shm_ref, scm_ref, gm_ref)


def _odd_post(x, ohg, proj, att, hgn, outw, w1, w2, ng, mods, n_ctx, layer):
    b, n, d = x.shape
    tm = LAT_TILE
    seg = lambda i: 1
    w = HG_WIDTH
    el = lambda *shape: tuple(pl.Element(s) for s in shape)
    row0 = lambda i: pl.multiple_of(n_ctx + i * tm, int(np.gcd(n_ctx, tm)))
    return pl.pallas_call(
        _odd_post_body,
        grid=(b, n // tm),
        in_specs=[
            pl.BlockSpec((None, tm, d), lambda bb, i: (bb, i, 0)),
            pl.BlockSpec(el(2, 1, tm, w), lambda bb, i: (0, bb, row0(i), 0)),
            pl.BlockSpec(el(1, tm, w), lambda bb, i: (bb, row0(i), 4 * w)),
            pl.BlockSpec((None, tm, w), lambda bb, i: (bb, i, 0)),
            _const_spec((1, w)),
        ] + _tail_specs(seg, layer),
        out_specs=pl.BlockSpec((None, tm, d), lambda bb, i: (bb, i, 0)),
        out_shape=jax.ShapeDtypeStruct((b, n, d), F32),
        compiler_params=_params(("parallel", "parallel")),
        name="odd_post",
    )(x, ohg, proj, att, hgn, outw, w1, w2, ng, mods, mods, mods, mods)


def _rope_tables(n, n_ctx):
    quarter = MLA_ROPE // 4
    inv = ROPE_BASE ** (-jnp.arange(quarter, dtype=F32) / quarter)
    rows = n // GRID_W
    ang_r = jnp.arange(rows, dtype=F32)[:, None] * inv[None, :]
    ang_c = jnp.arange(GRID_W, dtype=F32)[:, None] * inv[None, :]
    by_row = lambda a: jnp.repeat(a, GRID_W, axis=0)
    by_col = lambda a: jnp.tile(a, (rows, 1))
    cos_r, sin_r, cos_c, sin_c = by_row(jnp.cos(ang_r)), by_row(jnp.sin(ang_r)), by_col(jnp.cos(ang_c)), by_col(jnp.sin(ang_c))
    cos = jnp.concatenate([cos_r, cos_r, cos_c, cos_c], axis=-1)
    sin = jnp.concatenate([-sin_r, sin_r, -sin_c, sin_c], axis=-1)
    cos = jnp.concatenate([jnp.ones((n_ctx, MLA_ROPE), F32), cos], axis=0)
    sin = jnp.concatenate([jnp.zeros((n_ctx, MLA_ROPE), F32), sin], axis=0)
    pad = lambda a: jnp.pad(a, ((0, 0), (0, LANES - MLA_ROPE)))
    return pad(cos), pad(sin)


_ROPE_SWAP = np.concatenate([np.arange(16, 32), np.arange(0, 16), np.arange(48, 64), np.arange(32, 48)])


def kernel(x, c, ctx, c_ctx, ada_w, ada_b, norm_g, out_w, mlp_w1, mlp_w2, hg_lb, ev_w_in, ml_conv, ml_gate_b, ml_norm, rw_mu, rw_w0, rw_w2, rw_a0, rw_a2, rw_g2, rw_kk, rw_ka, rw_rk, rw_ln_w, rw_ln_b, od_w_in, hg_norm, mla_q_norm, mla_w_qb, mla_kv_norm, mla_w_kvb):
    b, n, d = x.shape
    n_ctx = ctx.shape[1]
    assert d == D_MODEL and b + 1 <= SUBLANES
    assert n_ctx == ROW_TILE and n % LAT_TILE == 0 and n % ATTN_TQ == 0
    assert all(n_ctx % blk == 0 and n % blk == 0 for blk in (ML_BLOCK, RW_BLOCK, HG_BLOCK))
    row1 = lambda a: a.reshape(1, -1)

    c8 = jnp.concatenate([c, c_ctx[None], jnp.zeros((SUBLANES - b - 1, d), F32)], axis=0)
    mod = _ada_mod(c8, ada_w, ada_b).reshape(-1, SUBLANES, N_MOD, d)

    def layer_mods(l):
        lat = mod[l, :b]
        cm = jnp.broadcast_to(mod[l, b][None], lat.shape)
        return jnp.stack([cm, lat], axis=1)[:, :, :, None, :]

    outw_bf, w1_bf, w2_bf = _bf(out_w), _bf(mlp_w1), _bf(mlp_w2)

    l, e = 0, 0
    mods = layer_mods(l)
    wi = ev_w_in[e]
    w_pad = jnp.concatenate([wi[:, 2064:3984], wi[:, 2048:2064], jnp.zeros((d, LANES - 16), F32), wi[:, 0:2048]], axis=1)
    gate_b = jnp.pad(ml_gate_b[e].reshape(1, 16), ((0, 0), (0, LANES - 16)))
    zeros64 = jnp.zeros((64, RW_WIDTH), F32)
    w2cat = jnp.stack([jnp.concatenate([rw_w2[e, 0], zeros64], 0), jnp.concatenate([zeros64, rw_w2[e, 1]], 0)])
    a2cat = jnp.stack([jnp.concatenate([rw_a2[e, 0], zeros64], 0), jnp.concatenate([zeros64, rw_a2[e, 1]], 0)])
    proj, qk, gates, gates_t, rws, rw_lw, rw_kb, bg = _even_in(
        ctx, x, row1(norm_g[l, 0]), mods, _bf(w_pad),
        ml_conv[e], gate_b, row1(rw_mu[e]), rw_w0[e], w2cat, rw_a0[e], a2cat, rw_g2[e],
        row1(rw_kk[e]), row1(rw_ka[e]), row1(rw_rk[e]))

    hm = _mlstm(qk, proj, gates, gates_t, n_ctx)
    yrw = _rwkv(rws, rw_lw, rw_kb, n_ctx)
    post_args = (hm, proj, yrw, bg, row1(ml_norm[e]), row1(rw_ln_w[e]), row1(rw_ln_b[e]),
                 outw_bf, w1_bf, w2_bf, norm_g, mods, l)
    xc1 = _even_post(ctx, 0, 0, ROW_TILE, *post_args)
    x1 = _even_post(x, n_ctx, 1, LAT_TILE, *post_args)

    l, o = 1, 0
    mods = layer_mods(l)
    lb_all = jax.nn.softmax(hg_lb.astype(F32), axis=0)
    lb_all = jnp.cumsum(lb_all, axis=0) - lb_all[0]
    wi = od_w_in[o]
    kr = wi[:, 2944:3008]
    z = lambda k: jnp.zeros((d, k), F32)
    w_pad = jnp.concatenate([wi[:, 0:2944], kr, z(64), kr[:, _ROPE_SWAP], z(64)], axis=1)
    cos, sin = _rope_tables(n, n_ctx)
    wkv = mla_w_kvb[o].reshape(MLA_KV_RANK, MLA_HEADS, MLA_NOPE + MLA_V)
    wk = wkv[:, :, :MLA_NOPE].reshape(MLA_KV_RANK, MLA_HEADS * MLA_NOPE)
    wvt = jnp.transpose(wkv[:, :, MLA_NOPE:], (1, 2, 0))
    wq = mla_w_qb[o].reshape(MLA_Q_RANK, MLA_HEADS, MLA_NOPE + MLA_ROPE)
    zq = lambda k: jnp.zeros((MLA_Q_RANK, MLA_HEADS, k), F32)
    wq1 = jnp.concatenate([wq, zq(64)], axis=-1).reshape(MLA_Q_RANK, -1)
    wq2 = jnp.concatenate([wq[:, :, MLA_NOPE:][:, :, _ROPE_SWAP], zq(64)], axis=-1).reshape(MLA_Q_RANK, -1)
    proj, k_all, v_all, q_all = _odd_in(
        xc1, x1, row1(norm_g[l, 0]), mods, _bf(w_pad), cos, sin, row1(mla_kv_norm[o]), row1(mla_q_norm[o]),
        _bf(wk), _bf(wvt), _bf(wq1), _bf(wq2))

    ohg = _hgrn(proj, row1(lb_all[l]), n_ctx)
    att = _attention(q_all, k_all, v_all)

    return _odd_post(x1, ohg, proj, att, row1(hg_norm[o]), outw_bf, w1_bf, w2_bf, norm_g, mods, n_ctx, l)
```

```python
import functools

import numpy as np
import jax
import jax.numpy as jnp
from jax import lax
from jax.experimental import pallas as pl
from jax.experimental.pallas import tpu as pltpu

F32 = jnp.float32
BF16 = jnp.bfloat16

D_MODEL = 1024
GRID_W = 64
D_FF = 4 * D_MODEL
N_MOD = 6
EPS = 1e-6
ML_WIDTH = 512
ML_HEADS = 4
ML_DIM = 128
RW_WIDTH = 512
RW_HEAD_DIM = 64
RW_GN_EPS = 64e-5
RW_FEAT = 1920
HG_WIDTH = 512
HG_HEADS = 4
HG_DIM = 128
MLA_HEADS = 4
MLA_NOPE = 128
MLA_ROPE = 64
MLA_V = 128
MLA_Q_RANK = 256
MLA_KV_RANK = 128
ROPE_BASE = 10000.0

LANES = 128
SUBLANES = 8
HALO = 16
VMEM_LIMIT_BYTES = 56 * 2**20

ROW_TILE = 256
LAT_TILE = 512
FF_CHUNK = 1024
ML_CHUNK = 128
ML_BLOCK = 256
RW_CHUNK = 64
RW_BLOCK = 256
HG_BLOCK = 256
HG_CHUNK = 32
ATTN_TQ = 512
ATTN_SUB = 256


def _dot(a, b, prec=None):
    return jnp.dot(a, b, preferred_element_type=F32, precision=prec)


def _dot_nt(a, b, prec=None):
    return lax.dot_general(a, b, (((1,), (1,)), ((), ())), preferred_element_type=F32, precision=prec)


def _dot_tn(a, b, prec=None):
    return lax.dot_general(a, b, (((0,), (0,)), ((), ())), preferred_element_type=F32, precision=prec)


def _bf(x):
    return x.astype(BF16)


def _mm(a, b):
    return _dot(_bf(a), _bf(b))


def _mm_nt(a, b):
    return _dot_nt(_bf(a), _bf(b))


def _mm_tn(a, b):
    return _dot_tn(_bf(a), _bf(b))


def _split(x):
    hi = _bf(x)
    return hi, _bf(x - hi.astype(F32))


def _sel_dot(sel, x):
    hi, lo = _split(x)
    return _dot(sel, hi) + _dot(sel, lo)


def _dot_sel(x, sel):
    hi, lo = _split(x)
    return _dot(hi, sel) + _dot(lo, sel)


def _dot_sel_nt(x, sel):
    hi, lo = _split(x)
    return _dot_nt(hi, sel) + _dot_nt(lo, sel)


def _dot3(a, b):
    ah, al = _split(a)
    bh, bl = _split(b)
    return _dot(ah, bh) + _dot(ah, bl) + _dot(al, bh)


def _group_mean(x, group):
    low = lax.broadcasted_iota(jnp.int32, (1, LANES), 1) < group
    outs = []
    for c in range(x.shape[1] // LANES):
        blk = x[:, c * LANES:(c + 1) * LANES]
        total = jnp.sum(blk, axis=-1, keepdims=True)
        if group == LANES:
            outs.append(jnp.broadcast_to(total, blk.shape))
        else:
            first = jnp.sum(jnp.where(low, blk, 0.0), axis=-1, keepdims=True)
            outs.append(jnp.where(low, first, total - first))
    return jnp.concatenate(outs, axis=-1) * (1.0 / group)


def _sigmoid(x):
    return 1.0 / (1.0 + jnp.exp(-x))


def _log_sigmoid(x):
    return jnp.minimum(x, 0.0) - jnp.log(1.0 + jnp.exp(-jnp.abs(x)))


def _softplus(x):
    return jnp.maximum(x, 0.0) + jnp.log(1.0 + jnp.exp(-jnp.abs(x)))


def _rms(x, g):
    return x * lax.rsqrt(jnp.mean(x * x, axis=-1, keepdims=True) + EPS) * g


def _order_mask(n, sgn, strict):
    row = lax.broadcasted_iota(jnp.int32, (n, n), 0)
    col = lax.broadcasted_iota(jnp.int32, (n, n), 1)
    diff = (row - col) * sgn
    return diff > 0 if strict else diff >= 0


def _chunk_index(d, j, nctx, nc):
    bwd = jnp.where(j < nctx, nctx - 1 - j, nc - 1 - (j - nctx))
    return jnp.where(d == 0, j, bwd)


def _params(sem):
    return pltpu.CompilerParams(dimension_semantics=sem, vmem_limit_bytes=VMEM_LIMIT_BYTES)


def _const_spec(shape):
    nd = len(shape)
    return pl.BlockSpec(shape, lambda *_: (0,) * nd, pipeline_mode=pl.Buffered(1))


def _mod_spec(k, seg_of):
    return pl.BlockSpec((None, None, None, 1, D_MODEL), lambda b, i: (b, seg_of(i), k, 0, 0))


def _ada_body(c_ref, w_ref, b_ref, o_ref):
    c = c_ref[...]
    o_ref[...] = _dot3(c * _sigmoid(c), w_ref[...]) + b_ref[...]


def _ada_mod(c8, ada_w, ada_b):
    depth, d, n = ada_w.shape
    tn = 1024
    return pl.pallas_call(
        _ada_body,
        grid=(depth, n // tn),
        in_specs=[
            _const_spec((SUBLANES, d)),
            pl.BlockSpec((None, d, tn), lambda l, j: (l, 0, j)),
            pl.BlockSpec((None, 1, tn), lambda l, j: (l, 0, j)),
        ],
        out_specs=pl.BlockSpec((None, SUBLANES, tn), lambda l, j: (l, 0, j)),
        out_shape=jax.ShapeDtypeStruct((depth, SUBLANES, n), F32),
        compiler_params=_params(("parallel", "parallel")),
        name="ada_mod",
    )(c8, ada_w, ada_b.reshape(depth, 1, n))


EV_SHIFTED = 3072


def _even_in_body(nt, c_ref, x_ref, xp_ref, xn_ref, g_ref, sh_ref, sc_ref, w_ref,
                  conv_ref, gb_ref, mu_ref, w0_ref, w2_ref, a0_ref, a2_ref, g2_ref,
                  kk_ref, ka_ref, rk_ref,
                  ovo_ref, oqk_ref, og_ref, ogt_ref, orws_ref, olw_ref, okb_ref, obg_ref):
    i = pl.program_id(1)
    has_prev = (i > 1).astype(F32)
    has_next = jnp.logical_and(i != 0, i != nt - 1).astype(F32)
    tm = x_ref.shape[0]
    ext = tm + 2 * HALO
    row = lax.broadcasted_iota(jnp.int32, (tm, 1), 0)
    keep_dn = jnp.where(row == 0, has_prev, 1.0)
    keep_up = jnp.where(row == tm - 1, has_next, 1.0)

    def normed(xx):
        return _bf(_rms(xx, g_ref[...]) * (1.0 + sc_ref[...]) + sh_ref[...])

    h = normed(jnp.where(i == 0, c_ref[...], x_ref[...]))
    ovo_ref[...] = _dot(h, w_ref[:, EV_SHIFTED:]).astype(ovo_ref.dtype)
    hcat = jnp.concatenate([normed(xp_ref[...]), h, normed(xn_ref[...])], axis=0)
    proj = _dot(hcat, w_ref[:, 0:EV_SHIFTED])

    def neighbours(cols):
        blk = proj[:, cols]
        dn = pltpu.roll(blk, 1, 0)[HALO:HALO + tm] * keep_dn
        up = pltpu.roll(blk, ext - 1, 0)[HALO:HALO + tm] * keep_up
        return blk[HALO:HALO + tm], dn, up

    qk, dn, up = neighbours(slice(2048, 3072))
    oqk_ref[...] = (dn * conv_ref[0:1, :] + qk * conv_ref[1:2, :] + up * conv_ref[2:3, :]).astype(oqk_ref.dtype)

    gv = proj[HALO:HALO + tm, RW_FEAT:RW_FEAT + LANES] + gb_ref[...]
    lane = lax.broadcasted_iota(jnp.int32, (1, LANES), 1)
    is_f = jnp.logical_and(jnp.bitwise_and(jnp.right_shift(lane, 2), 1) == 1, lane < 16)
    gates = jnp.where(is_f, _log_sigmoid(gv), gv)
    og_ref[...] = gates
    ogt_ref[...] = gates.T[0:4 * ML_HEADS, :]

    cur, dn, up = neighbours(slice(0, RW_FEAT))
    rf = cur + (0.5 * (dn + up) - cur) * mu_ref[...]
    w = RW_WIDTH
    r, k, v = rf[:, 0:w], rf[:, w:2 * w], rf[:, 2 * w:3 * w]
    wl = rf[:, 3 * w:3 * w + LANES]
    al = rf[:, 3 * w + LANES:3 * w + 2 * LANES]
    gl = rf[:, 3 * w + 2 * LANES:3 * w + 3 * LANES]
    kk = k * kk_ref[...]
    kk = kk * lax.rsqrt(_group_mean(kk * kk, RW_HEAD_DIM) * RW_HEAD_DIM + 1e-12)
    twl = jnp.tanh(wl)
    kd_sum = jnp.zeros_like(k)
    for d in range(2):
        w_log = -_softplus(-(w0_ref[d:d + 1, :] + _dot_sel(twl, w2_ref[d]))) - 0.5
        a = _sigmoid(a0_ref[d:d + 1, :] + _dot_sel(al, a2_ref[d]))
        kd = k * (1.0 + (a - 1.0) * ka_ref[...])
        kd_sum = kd_sum + kd
        olw_ref[d] = -jnp.exp(w_log)
        okb_ref[d, :, 0:w] = kd.astype(okb_ref.dtype)
        okb_ref[d, :, w:2 * w] = (kk * a).astype(okb_ref.dtype)
    orws_ref[:, 0:w] = r.astype(orws_ref.dtype)
    orws_ref[:, w:2 * w] = v.astype(orws_ref.dtype)
    orws_ref[:, 2 * w:3 * w] = (-kk).astype(orws_ref.dtype)
    bonus = _group_mean(r * kd_sum * rk_ref[...], RW_HEAD_DIM) * RW_HEAD_DIM * v
    obg_ref[:, 0:w] = bonus.astype(obg_ref.dtype)
    obg_ref[:, w:2 * w] = _dot(_bf(_sigmoid(gl)), g2_ref[...]).astype(obg_ref.dtype)


def _even_in(ctx, x, g, mods, w_in, conv, gate_b, mu, w0, w2cat, a0, a2cat, g2, kk, ka, rk):
    b, n, d = x.shape
    tm = ROW_TILE
    assert ctx.shape[1] == tm
    t = n + tm
    nt = t // tm
    rh = tm // HALO
    last_h = n // HALO - 1
    seg = lambda i: (i >= 1).astype(jnp.int32)
    cur = lambda c: (lambda bb, i: (bb, i, c))
    lat = lambda i: jnp.maximum(i - 1, 0)
    w = RW_WIDTH
    return pl.pallas_call(
        functools.partial(_even_in_body, nt),
        grid=(b, nt),
        in_specs=[
            pl.BlockSpec((None, tm, d), lambda bb, i: (bb, 0, 0)),
            pl.BlockSpec((None, tm, d), lambda bb, i: (bb, lat(i), 0)),
            pl.BlockSpec((None, HALO, d), lambda bb, i: (bb, jnp.maximum(lat(i) * rh - 1, 0), 0)),
            pl.BlockSpec((None, HALO, d), lambda bb, i: (bb, jnp.minimum((lat(i) + 1) * rh, last_h), 0)),
            _const_spec((1, d)), _mod_spec(0, seg), _mod_spec(1, seg), _const_spec(w_in.shape),
            _const_spec(conv.shape), _const_spec(gate_b.shape), _const_spec(mu.shape),
            _const_spec(w0.shape), _const_spec(w2cat.shape), _const_spec(a0.shape),
            _const_spec(a2cat.shape), _const_spec(g2.shape), _const_spec(kk.shape),
            _const_spec(ka.shape), _const_spec(rk.shape),
        ],
        out_specs=[
            pl.BlockSpec((None, tm, 1024), cur(0)),
            pl.BlockSpec((None, tm, 1024), cur(0)),
            pl.BlockSpec((None, tm, LANES), cur(0)),
            pl.BlockSpec((None, 4 * ML_HEADS, tm), lambda bb, i: (bb, 0, i)),
            pl.BlockSpec((None, tm, 3 * w), cur(0)),
            pl.BlockSpec((2, None, tm, w), lambda bb, i: (0, bb, i, 0)),
            pl.BlockSpec((2, None, tm, 2 * w), lambda bb, i: (0, bb, i, 0)),
            pl.BlockSpec((None, tm, 2 * w), cur(0)),
        ],
        out_shape=[
            jax.ShapeDtypeStruct((b, t, 1024), BF16),
            jax.ShapeDtypeStruct((b, t, 1024), BF16),
            jax.ShapeDtypeStruct((b, t, LANES), F32),
            jax.ShapeDtypeStruct((b, 4 * ML_HEADS, t), F32),
            jax.ShapeDtypeStruct((b, t, 3 * w), BF16),
            jax.ShapeDtypeStruct((2, b, t, w), F32),
            jax.ShapeDtypeStruct((2, b, t, 2 * w), BF16),
            jax.ShapeDtypeStruct((b, t, 2 * w), BF16),
        ],
        compiler_params=_params(("parallel", "parallel")),
        name="even_in",
    )(ctx, x, x, x, g, mods, mods, w_in, conv, gate_b, mu, w0, w2cat, a0, a2cat, g2, kk, ka, rk)


def _mlstm_body(q_ref, k_ref, v_ref, gc_ref, gr_ref, o_ref, c_scr, n_scr, m_scr):
    d = pl.program_id(0)
    j = pl.program_id(1)
    n_batch = q_ref.shape[0]
    n_rows = ML_CHUNK
    n_sub = q_ref.shape[1] // n_rows

    @pl.when(j == 0)
    def _():
        c_scr[...] = jnp.zeros_like(c_scr)
        n_scr[...] = jnp.zeros_like(n_scr)
        m_scr[...] = jnp.zeros_like(m_scr)

    mask = _order_mask(n_rows, 1 - 2 * d, strict=False)
    mf = mask.astype(BF16)
    row_id = lax.broadcasted_iota(jnp.int32, (n_rows, 1), 0)
    is_last = row_id == (n_rows - 1) * (1 - d)
    pick_row = lax.broadcasted_iota(jnp.int32, (LANES, LANES), 0)
    pick = [(pick_row == 2 * ML_HEADS * d + c).astype(BF16) for c in range(2 * ML_HEADS)]
    ones_cols = jnp.ones((n_rows, LANES), BF16)

    rows = [pl.ds(pl.multiple_of(jnp.where(d == 0, sc, n_sub - 1 - sc) * n_rows, n_rows), n_rows)
            for sc in range(n_sub)]
    heads = [slice(h * ML_DIM, (h + 1) * ML_DIM) for h in range(ML_HEADS)]
    units = [(sc, bi, h) for sc in range(n_sub) for bi in range(n_batch) for h in range(ML_HEADS)]
    idx = range(len(units))
    carried = [bi * ML_HEADS + h for _, bi, h in units]
    gc, b_col, gr, b_row = {}, {}, {}, {}
    for bi in range(n_batch):
        gr_all = gr_ref[bi]
        for sc in range(n_sub):
            gc[sc, bi] = gc_ref[bi, rows[sc], :]
            parts = [gr_all[:, c * n_rows:(c + 1) * n_rows] for c in (sc, n_sub - 1 - sc)]
            gr[sc, bi] = jnp.where(d == 0, parts[0], parts[1])
            b_col[sc, bi] = _sel_dot(mf, gc[sc, bi])
            b_row[sc, bi] = _dot_sel_nt(gr[sc, bi], mf)
    q_f = [q_ref[bi, rows[sc], heads[h]] for sc, bi, h in units]
    q = [_bf(x) for x in q_f]
    kf = [k_ref[bi, rows[sc], heads[h]].astype(F32) * (ML_DIM ** -0.5) for sc, bi, h in units]
    k = [_bf(x) for x in kf]
    vf = [v_ref[bi, rows[sc], heads[h]].astype(F32) for sc, bi, h in units]
    qk = [_dot_nt(q[u], k[u]) for u in idx]
    ic = [_dot_sel(gc[sc, bi], pick[h]) for sc, bi, h in units]
    bc = [_dot_sel(b_col[sc, bi], pick[ML_HEADS + h]) for sc, bi, h in units]
    b_tot = [jnp.sum(jnp.where(is_last, bc[u], 0.0), axis=0, keepdims=True) for u in idx]
    logw = [b_tot[u] - bc[u] + ic[u] for u in idx]
    logw_max = [jnp.max(logw[u], axis=0, keepdims=True) for u in idx]
    logd = [jnp.where(mask, bc[u] - b_row[sc, bi][ML_HEADS + h:ML_HEADS + h + 1, :] + gr[sc, bi][h:h + 1, :], -1e30)
            for u, (sc, bi, h) in enumerate(units)]
    logd_max = [jnp.max(x, axis=-1, keepdims=True) for x in logd]
    m_run = [m_scr[c][0:1, :] for c in range(n_batch * ML_HEADS)]
    wgt, decay, s, m_t, w_inter = [], [], [], [], []
    for u in idx:
        m_prev = m_run[carried[u]]
        m_new = jnp.maximum(b_tot[u] + m_prev, logw_max[u])
        wgt.append(jnp.exp(logw[u] - m_new))
        decay.append(jnp.exp(b_tot[u] + m_prev - m_new))
        m_inter = bc[u] + m_prev
        m_t.append(jnp.maximum(m_inter, logd_max[u]))
        s.append(qk[u] * jnp.exp(logd[u] - m_t[u]))
        w_inter.append(jnp.exp(m_inter - m_t[u]))
        m_run[carried[u]] = m_new
    kv = [_dot_tn(_bf(vf[u] * wgt[u]), k[u]) for u in idx]
    sv = [_dot(_bf(s[u]), jnp.concatenate([_bf(vf[u]), ones_cols], axis=1)) for u in idx]
    c_run = [c_scr[c] for c in range(n_batch * ML_HEADS)]
    n_run = [n_scr[c][0:1, :] for c in range(n_batch * ML_HEADS)]
    per_chunk = n_batch * ML_HEADS
    for sc in range(n_sub):
        ids = range(sc * per_chunk, (sc + 1) * per_chunk)
        qc = [_dot_nt(q[u], jnp.concatenate([_bf(c_run[carried[u]]),
                                             _bf(jnp.broadcast_to(n_run[carried[u]], (LANES, ML_DIM)))], axis=0))
              for u in ids]
        for u, x in zip(ids, qc):
            _, bi, h = units[u]
            num = sv[u][:, :ML_DIM] + w_inter[u] * x[:, :ML_DIM]
            den = sv[u][:, ML_DIM:] + w_inter[u] * x[:, ML_DIM:]
            o_ref[bi, rows[sc], heads[h]] = num / jnp.maximum(jnp.abs(den), jnp.exp(-m_t[u]))
        for u in ids:
            c = carried[u]
            c_run[c] = decay[u] * c_run[c] + kv[u]
            n_run[c] = decay[u] * n_run[c] + jnp.sum(wgt[u] * kf[u], axis=0, keepdims=True)
    for c in range(per_chunk):
        c_scr[c] = c_run[c]
        n_scr[c] = jnp.broadcast_to(n_run[c], (SUBLANES, ML_DIM))
        m_scr[c] = jnp.broadcast_to(m_run[c], (SUBLANES, LANES))


def _mlstm(qk, proj, gates, gates_t, n_ctx):
    b, t, _ = qk.shape
    n_rows = ML_BLOCK
    nc, nctx = t // n_rows, n_ctx // n_rows
    ci = lambda d, j: _chunk_index(d, j, nctx, nc)
    return pl.pallas_call(
        _mlstm_body,
        grid=(2, nc),
        in_specs=[
            pl.BlockSpec((b, n_rows, ML_WIDTH), lambda d, j: (0, ci(d, j), 0)),
            pl.BlockSpec((b, n_rows, ML_WIDTH), lambda d, j: (0, ci(d, j), 1)),
            pl.BlockSpec((b, n_rows, ML_WIDTH), lambda d, j: (0, ci(d, j), 0)),
            pl.BlockSpec((b, n_rows, LANES), lambda d, j: (0, ci(d, j), 0)),
            pl.BlockSpec((b, 2 * ML_HEADS, n_rows), lambda d, j: (0, d, ci(d, j))),
        ],
        out_specs=pl.BlockSpec((None, b, n_rows, ML_WIDTH), lambda d, j: (d, 0, ci(d, j), 0)),
        out_shape=jax.ShapeDtypeStruct((2, b, t, ML_WIDTH), F32),
        scratch_shapes=[
            pltpu.VMEM((b * ML_HEADS, ML_DIM, ML_DIM), F32),
            pltpu.VMEM((b * ML_HEADS, SUBLANES, ML_DIM), F32),
            pltpu.VMEM((b * ML_HEADS, SUBLANES, LANES), F32),
        ],
        compiler_params=_params(("parallel", "arbitrary")),
        name="mlstm",
    )(qk, qk, proj, gates, gates_t)


def _rwkv_body(r_ref, v_ref, a_ref, lw_ref, k_ref, b_ref, o_ref, h_scr):
    d = pl.program_id(0)
    j = pl.program_id(1)
    n_rows = RW_CHUNK
    two = 2 * n_rows
    shift = n_rows.bit_length() - 1

    @pl.when(j == 0)
    def _():
        h_scr[...] = jnp.zeros_like(h_scr)

    sgn = 1 - 2 * d
    cum = _order_mask(n_rows, sgn, strict=False).astype(BF16)
    row = lax.broadcasted_iota(jnp.int32, (two, two), 0)
    col = lax.broadcasted_iota(jnp.int32, (two, two), 1)
    same = jnp.right_shift(row, shift) == jnp.right_shift(col, shift)
    diff = (jnp.bitwise_and(row, n_rows - 1) - jnp.bitwise_and(col, n_rows - 1)) * sgn
    m_strict = jnp.logical_and(same, diff > 0)
    m_incl = jnp.logical_and(same, diff >= 0)
    eye = (row == col).astype(F32)
    head0 = lax.broadcasted_iota(jnp.int32, (1, LANES), 1) < RW_HEAD_DIM

    def stack(x):
        return jnp.concatenate([jnp.where(head0, x, 0.0), jnp.where(head0, 0.0, x)], axis=0)

    def fold(x):
        return x[:n_rows] + x[n_rows:]

    n_batch = r_ref.shape[0]
    n_sub = r_ref.shape[1] // n_rows
    lane_groups = [slice(p * LANES, (p + 1) * LANES) for p in range(RW_WIDTH // LANES)]
    rows = [pl.ds(pl.multiple_of(jnp.where(d == 0, sc, n_sub - 1 - sc) * n_rows, n_rows), n_rows)
            for sc in range(n_sub)]
    groups = [(sc, bi, p) for sc in range(n_sub) for bi in range(n_batch) for p in range(len(lane_groups))]
    x_s, r_s, v_s, bh_s, kh_s, bi_s, ki_s, e_tot = [], [], [], [], [], [], [], []
    for sc in range(n_sub):
        for bi in range(n_batch):
            lw = lw_ref[bi, rows[sc], :]
            g = _sel_dot(cum, lw)
            g_tot = jnp.sum(lw, axis=0, keepdims=True)
            inv = jnp.exp(-g)
            to_end = jnp.exp(g_tot - g)
            at = a_ref[bi, rows[sc], :].astype(F32) * jnp.exp(g - lw)
            rt = r_ref[bi, rows[sc], :].astype(F32) * jnp.exp(g)
            b_in = b_ref[bi, rows[sc], :].astype(F32)
            k_in = k_ref[bi, rows[sc], :].astype(F32)
            vv = v_ref[bi, rows[sc], :].astype(F32)
            for sl in lane_groups:
                x_s.append(stack(at[:, sl]))
                r_s.append(stack(rt[:, sl]))
                v_s.append(stack(vv[:, sl]))
                bh_s.append(stack((b_in * to_end)[:, sl]))
                kh_s.append(stack((k_in * to_end)[:, sl]))
                bi_s.append(stack((b_in * inv)[:, sl]))
                ki_s.append(stack((k_in * inv)[:, sl]))
                e_tot.append(jnp.exp(g_tot)[:, sl])
    big = [_mm_nt(jnp.concatenate([x_s[p], r_s[p]], axis=0), jnp.concatenate([bi_s[p], ki_s[p]], axis=0))
           for p in range(len(groups))]
    a_ab = [jnp.where(m_strict, m[:two, :two], 0.0) for m in big]
    a_rb = [jnp.where(m_incl, m[two:, :two], 0.0) for m in big]
    a_k = [jnp.concatenate([jnp.where(m_strict, m[:two, two:], 0.0), jnp.where(m_incl, m[two:, two:], 0.0)], axis=0)
           for m in big]
    av = [_mm(a_k[p], v_s[p]) for p in range(len(groups))]
    t_inv = [eye + m for m in a_ab]
    pw = [_mm(m, m) for m in a_ab]
    for level in range(1, shift):
        last = level == shift - 1
        res = [_mm(t if last else jnp.concatenate([t, m], axis=0), m) for t, m in zip(t_inv, pw)]
        t_inv = [t + r[:two] for t, r in zip(t_inv, res)]
        if not last:
            pw = [r[two:] for r in res]
    wu = [_mm(t_inv[p], jnp.concatenate([x_s[p], av[p][:two]], axis=1)) for p in range(len(groups))]
    rb = [_mm(a_rb[p], wu[p]) for p in range(len(groups))]
    upd = [_mm_tn(wu[p], bh_s[p]) for p in range(len(groups))]
    gt2 = [_mm_tn(v_s[p], kh_s[p]) for p in range(len(groups))]
    q_eff = [fold(r_s[p] + rb[p][:, :LANES]) for p in range(len(groups))]
    y_in = [fold(rb[p][:, LANES:] + av[p][two:]) for p in range(len(groups))]
    per_chunk = n_batch * len(lane_groups)
    h_t = [h_scr[u] for u in range(per_chunk)]
    for sc in range(n_sub):
        ids = range(sc * per_chunk, (sc + 1) * per_chunk)
        for u, p in enumerate(ids):
            _, bi, lg = groups[p]
            o_ref[bi, rows[sc], lane_groups[lg]] = _mm_nt(q_eff[p], h_t[u]) + y_in[p]
        h_t = [e_tot[p] * h_t[u] + _mm(h_t[u], upd[p][:LANES]) + upd[p][LANES:] + gt2[p] for u, p in enumerate(ids)]
    for u in range(per_chunk):
        h_scr[u] = h_t[u]


def _rwkv(rws, lw, kb, n_ctx):
    b, t, _ = rws.shape
    n_rows = RW_BLOCK
    nc, nctx = t // n_rows, n_ctx // n_rows
    ci = lambda d, j: _chunk_index(d, j, nctx, nc)
    w = RW_WIDTH
    shared = lambda c: pl.BlockSpec((b, n_rows, w), lambda d, j: (0, ci(d, j), c))
    per_dir = lambda c: pl.BlockSpec((None, b, n_rows, w), lambda d, j: (d, 0, ci(d, j), c))
    return pl.pallas_call(
        _rwkv_body,
        grid=(2, nc),
        in_specs=[shared(0), shared(1), shared(2), per_dir(0), per_dir(0), per_dir(1)],
        out_specs=pl.BlockSpec((None, b, n_rows, w), lambda d, j: (d, 0, ci(d, j), 0)),
        out_shape=jax.ShapeDtypeStruct((2, b, t, w), F32),
        scratch_shapes=[pltpu.VMEM((b * w // LANES, LANES, LANES), F32)],
        compiler_params=_params(("parallel", "arbitrary")),
        name="rwkv7",
    )(rws, rws, rws, lw, kb, kb)


def _tail(x, mix, outw_ref, w1_ref, w2_ref, ng_ref, ga_ref, shm_ref, scm_ref, gm_ref):
    x1 = x + ga_ref[...] * _rms(_dot(_bf(mix), outw_ref[...]), ng_ref[1:2, :])
    h = _bf(_rms(x1, ng_ref[2:3, :]) * (1.0 + scm_ref[...]) + shm_ref[...])
    chunks = [slice(c * FF_CHUNK, (c + 1) * FF_CHUNK) for c in range(D_FF // FF_CHUNK)]
    u = [jnp.maximum(_dot(h, w1_ref[:, cols]), 0.0) for cols in chunks]
    dd = _dot(_bf(u[0] * u[0]), w2_ref[chunks[0], :])
    for uc, cols in zip(u[1:], chunks[1:]):
        dd = dd + _dot(_bf(uc * uc), w2_ref[cols, :])
    return x1 + gm_ref[...] * _rms(dd, ng_ref[3:4, :])


def _even_post_body(x_ref, hm_ref, mo_ref, yrw_ref, bg_ref, mln_ref, lnw_ref, lnb_ref,
                    outw_ref, w1_ref, w2_ref, ng_ref, ga_ref, shm_ref, scm_ref, gm_ref, o_ref):
    w = RW_WIDTH
    ym = hm_ref[0, 0] + hm_ref[1, 0]
    ym = (ym * lax.rsqrt(_group_mean(ym * ym, ML_DIM) + EPS) * mln_ref[...]
          * _sigmoid(mo_ref[0].astype(F32)))
    y = yrw_ref[0, 0] + yrw_ref[1, 0]
    yc = y - _group_mean(y, RW_HEAD_DIM)
    y = (yc * lax.rsqrt(_group_mean(yc * yc, RW_HEAD_DIM) + RW_GN_EPS) * lnw_ref[...] + lnb_ref[...]
         + bg_ref[0, :, 0:w].astype(F32))
    mix = jnp.concatenate([ym, y * bg_ref[0, :, w:2 * w].astype(F32)], axis=-1)
    o_ref[...] = _tail(x_ref[...], mix, outw_ref, w1_ref, w2_ref, ng_ref, ga_ref, shm_ref, scm_ref, gm_ref)


def _tail_specs(seg, layer):
    d = D_MODEL
    pick = lambda *shape: pl.BlockSpec((None,) + shape, lambda *_: (layer, 0, 0), pipeline_mode=pl.Buffered(1))
    return [
        pick(d, d), pick(d, D_FF), pick(D_FF, d), pick(4, d),
        _mod_spec(2, seg), _mod_spec(3, seg), _mod_spec(4, seg), _mod_spec(5, seg),
    ]


def _even_post(x_seg, first_row, segment, tm, hm, proj, yrw, bg, mln, lnw, lnb, outw, w1, w2, ng, mods, layer):
    b, rows, d = x_seg.shape
    seg = lambda i: segment
    w = RW_WIDTH
    el = lambda *shape: tuple(pl.Element(s) for s in shape)
    row0 = lambda i: pl.multiple_of(first_row + i * tm, int(np.gcd(first_row, tm)))
    return pl.pallas_call(
        _even_post_body,
        grid=(b, rows // tm),
        in_specs=[
            pl.BlockSpec((None, tm, d), lambda bb, i: (bb, i, 0)),
            pl.BlockSpec(el(2, 1, tm, w), lambda bb, i: (0, bb, row0(i), 0)),
            pl.BlockSpec(el(1, tm, w), lambda bb, i: (bb, row0(i), w)),
            pl.BlockSpec(el(2, 1, tm, w), lambda bb, i: (0, bb, row0(i), 0)),
            pl.BlockSpec(el(1, tm, 2 * w), lambda bb, i: (bb, row0(i), 0)),
            _const_spec((1, w)), _const_spec((1, w)), _const_spec((1, w)),
        ] + _tail_specs(seg, layer),
        out_specs=pl.BlockSpec((None, tm, d), lambda bb, i: (bb, i, 0)),
        out_shape=jax.ShapeDtypeStruct((b, rows, d), F32),
        compiler_params=_params(("parallel", "parallel")),
        name="even_post",
    )(x_seg, hm, proj, yrw, bg, mln, lnw, lnb, outw, w1, w2, ng, mods, mods, mods, mods)


def _hgrn_body(q_ref, f_ref, i_ref, lb_ref, o_ref, s_scr):
    d = pl.program_id(0)
    j = pl.program_id(1)
    n_batch, n_rows = q_ref.shape[0], q_ref.shape[1]
    n_sub = n_rows // HG_CHUNK

    @pl.when(j == 0)
    def _():
        s_scr[...] = jnp.zeros_like(s_scr)

    mask = _order_mask(HG_CHUNK, 1 - 2 * d, strict=False)
    mf = mask.astype(BF16)
    lb = lb_ref[...]
    heads = range(HG_HEADS)
    sls = [slice(h * HG_DIM, (h + 1) * HG_DIM) for h in heads]
    units = [(bi, h) for bi in range(n_batch) for h in heads]
    n_units = len(units)
    s_t = [s_scr[u] for u in range(n_units)]
    rows = [pl.ds(pl.multiple_of(jnp.where(d == 0, sc, n_sub - 1 - sc) * HG_CHUNK, HG_CHUNK), HG_CHUNK)
            for sc in range(n_sub)]
    loaded = [[(q_ref[bi, r, :], f_ref[bi, r, :], i_ref[bi, r, :]) for bi in range(n_batch)] for r in rows]
    q_in, e_tot, q_mids, k_mids, k_outs, vvs = [], [], [], [], [], []

    def prepare(sc):
        for lst in (q_in, e_tot, q_mids, k_mids, k_outs, vvs):
            lst.append([])
        for bi in range(n_batch):
            qa = loaded[sc][bi][0].astype(F32)
            pre = loaded[sc][bi][1].astype(F32)
            vv = loaded[sc][bi][2]
            q = qa * _sigmoid(qa) * (HG_DIM ** -0.5)
            e_abs = jnp.exp(-jnp.abs(pre))
            big = 1.0 / (1.0 + e_abs)
            small = e_abs * big
            log_f = jnp.log(lb + (1.0 - lb) * jnp.where(pre >= 0.0, big, small))
            kf = (1.0 - lb) * jnp.where(pre >= 0.0, small, big)
            bc = _sel_dot(mf, log_f)
            tot = jnp.sum(log_f, axis=0, keepdims=True)
            half = 0.5 * tot
            e_half = jnp.exp(half)
            q_mid_f = q * jnp.exp(bc - half)
            k_mid_f = kf * jnp.exp(half - bc)
            q_mid = _bf(q_mid_f)
            k_mid = _bf(k_mid_f)
            k_out = _bf(k_mid_f * e_half)
            q_all = _bf(q_mid_f * e_half)
            e_all = e_half * e_half
            q_mids[sc] += [q_mid[:, sl] for sl in sls]
            k_mids[sc] += [k_mid[:, sl] for sl in sls]
            k_outs[sc] += [k_out[:, sl] for sl in sls]
            vvs[sc] += [vv[:, sl] for sl in sls]
            q_in[sc] += [q_all[:, sl] for sl in sls]
            e_tot[sc] += [e_all[:, sl] for sl in sls]

    prepare(0)
    for sc in range(n_sub):
        if sc + 1 < n_sub:
            prepare(sc + 1)
        att = [_dot_nt(q_mids[sc][u], k_mids[sc][u]) for u in range(n_units)]
        kv = [_dot_tn(vvs[sc][u], k_outs[sc][u]) for u in range(n_units)]
        intra = [_dot(_bf(jnp.where(mask, att[u], 0.0)), vvs[sc][u]) for u in range(n_units)]
        for u, (bi, h) in enumerate(units):
            o_ref[bi, rows[sc], sls[h]] = intra[u] + _dot_nt(q_in[sc][u], _bf(s_t[u]))
        s_t = [e_tot[sc][u] * s_t[u] + kv[u] for u in range(n_units)]
    for u in range(n_units):
        s_scr[u] = s_t[u]


def _hgrn(proj, lb, n_ctx):
    b, t, _ = proj.shape
    n_rows = HG_BLOCK
    nc, nctx = t // n_rows, n_ctx // n_rows
    ci = lambda d, j: _chunk_index(d, j, nctx, nc)
    w = HG_WIDTH
    return pl.pallas_call(
        _hgrn_body,
        grid=(2, nc),
        in_specs=[
            pl.BlockSpec((b, n_rows, w), lambda d, j: (0, ci(d, j), 0)),
            pl.BlockSpec((b, n_rows, w), lambda d, j: (0, ci(d, j), 1 + d)),
            pl.BlockSpec((b, n_rows, w), lambda d, j: (0, ci(d, j), 3)),
            _const_spec((1, w)),
        ],
        out_specs=pl.BlockSpec((None, b, n_rows, w), lambda d, j: (d, 0, ci(d, j), 0)),
        out_shape=jax.ShapeDtypeStruct((2, b, t, w), F32),
        scratch_shapes=[pltpu.VMEM((b * HG_HEADS, HG_DIM, HG_DIM), F32)],
        compiler_params=_params(("parallel", "arbitrary")),
        name="hgrn2",
    )(proj, proj, proj, lb)


V_ROWS = MLA_V + SUBLANES
OD_MAIN = 5 * HG_WIDTH


def _odd_in_body(c_ref, x_ref, g_ref, sh_ref, sc_ref, w_ref, cos_ref, sin_ref, kvn_ref, qn_ref,
                 wk_ref, wvt_ref, wq1_ref, wq2_ref, proj_ref, k_ref, vt_ref, q_ref):
    x_in = jnp.where(pl.program_id(1) == 0, c_ref[...], x_ref[...])
    h = _bf(_rms(x_in, g_ref[...]) * (1.0 + sc_ref[...]) + sh_ref[...])
    mla = _dot(h, w_ref[:, OD_MAIN:])
    proj_ref[...] = _dot(h, w_ref[:, 0:OD_MAIN]).astype(proj_ref.dtype)
    cq = mla[:, 0:MLA_Q_RANK]
    base = MLA_Q_RANK
    ckv = mla[:, base:base + MLA_KV_RANK]
    cos, sin = cos_ref[...], sin_ref[...]
    k_rope = mla[:, base + LANES:base + 2 * LANES] * cos + mla[:, base + 2 * LANES:base + 3 * LANES] * sin
    kvn = _bf(_rms(ckv, kvn_ref[...]))
    kn = _dot(kvn, wk_ref[...])
    ones = jnp.ones((SUBLANES, mla.shape[0]), BF16)
    cn = _bf(_rms(cq, qn_ref[...]))
    qa = _dot(cn, wq1_ref[...])
    qb = _dot(cn, wq2_ref[...])
    scale = (MLA_NOPE + MLA_ROPE) ** -0.5 * float(np.log2(np.e))
    for hd in range(MLA_HEADS):
        k_ref[hd] = jnp.concatenate([kn[:, hd * MLA_NOPE:(hd + 1) * MLA_NOPE], k_rope], axis=-1).astype(BF16)
        vt = _dot_nt(wvt_ref[hd], kvn)
        vt_ref[hd] = jnp.concatenate([vt.astype(BF16), ones], axis=0)
        a = qa[:, hd * 256:(hd + 1) * 256]
        q_rope = a[:, MLA_NOPE:] * cos + qb[:, hd * LANES:(hd + 1) * LANES] * sin
        q_ref[hd] = (jnp.concatenate([a[:, 0:MLA_NOPE], q_rope], axis=-1) * scale).astype(BF16)


def _odd_in(xc, x, g, mods, w_in, cos, sin, kv_norm, q_norm, wk, wvt, wq1, wq2):
    b, n, d = x.shape
    tm = ROW_TILE
    assert xc.shape[1] == tm
    t = n + tm
    seg = lambda i: (i >= 1).astype(jnp.int32)
    q_blk = lambda bb, i: (bb, 0, jnp.maximum(i - 1, 0), 0)
    return pl.pallas_call(
        _odd_in_body,
        grid=(b, t // tm),
        in_specs=[
            pl.BlockSpec((None, tm, d), lambda bb, i: (bb, 0, 0)),
            pl.BlockSpec((None, tm, d), lambda bb, i: (bb, jnp.maximum(i - 1, 0), 0)),
            _const_spec((1, d)), _mod_spec(0, seg), _mod_spec(1, seg), _const_spec(w_in.shape),
            pl.BlockSpec((tm, LANES), lambda bb, i: (i, 0)),
            pl.BlockSpec((tm, LANES), lambda bb, i: (i, 0)),
            _const_spec((1, MLA_KV_RANK)), _const_spec((1, MLA_Q_RANK)),
            _const_spec(wk.shape), _const_spec(wvt.shape), _const_spec(wq1.shape), _const_spec(wq2.shape),
        ],
        out_specs=[
            pl.BlockSpec((None, tm, OD_MAIN), lambda bb, i: (bb, i, 0)),
            pl.BlockSpec((None, MLA_HEADS, tm, 256), lambda bb, i: (bb, 0, i, 0)),
            pl.BlockSpec((None, MLA_HEADS, V_ROWS, tm), lambda bb, i: (bb, 0, 0, i)),
            pl.BlockSpec((None, MLA_HEADS, tm, 256), q_blk),
        ],
        out_shape=[
            jax.ShapeDtypeStruct((b, t, OD_MAIN), BF16),
            jax.ShapeDtypeStruct((b, MLA_HEADS, t, 256), BF16),
            jax.ShapeDtypeStruct((b, MLA_HEADS, V_ROWS, t), BF16),
            jax.ShapeDtypeStruct((b, MLA_HEADS, n, 256), BF16),
        ],
        compiler_params=_params(("parallel", "arbitrary")),
        name="odd_in",
    )(xc, x, g, mods, mods, w_in, cos, sin, kv_norm, q_norm, wk, wvt, wq1, wq2)


def _attn_body(q_ref, k_ref, vt_ref, o_ref):
    k = k_ref[...]
    vt = vt_ref[...]
    subs = [slice(i * ATTN_SUB, (i + 1) * ATTN_SUB) for i in range(q_ref.shape[0] // ATTN_SUB)]
    st = [_dot_nt(k, q_ref[sl, :]) for sl in subs]
    pt = [_bf(jnp.exp2(s - jnp.max(s, axis=0, keepdims=True))) for s in st]
    ot = [_dot(vt, p) for p in pt]
    for sl, o in zip(subs, ot):
        o_ref[sl, :] = (o[0:MLA_V] / o[MLA_V:MLA_V + 1]).T


def _attention(q, k, vt):
    b, h, n, dq = q.shape
    t = k.shape[2]
    tq = ATTN_TQ
    return pl.pallas_call(
        _attn_body,
        grid=(b, h, n // tq),
        in_specs=[
            pl.BlockSpec((None, None, tq, dq), lambda bb, hh, i: (bb, hh, i, 0)),
            pl.BlockSpec((None, None, t, dq), lambda bb, hh, i: (bb, hh, 0, 0)),
            pl.BlockSpec((None, None, V_ROWS, t), lambda bb, hh, i: (bb, hh, 0, 0)),
        ],
        out_specs=pl.BlockSpec((None, tq, MLA_V), lambda bb, hh, i: (bb, i, hh)),
        out_shape=jax.ShapeDtypeStruct((b, n, h * MLA_V), F32),
        compiler_params=_params(("parallel", "parallel", "parallel")),
        name="mla_attention",
    )(q, k, vt)


def _odd_post_body(x_ref, o_ref_in, hg_ref, att_ref, hgn_ref,
                   outw_ref, w1_ref, w2_ref, ng_ref, ga_ref, shm_ref, scm_ref, gm_ref, o_ref):
    y = o_ref_in[0, 0] + o_ref_in[1, 0]
    g = hg_ref[0].astype(F32)
    y = y * lax.rsqrt(_group_mean(y * y, HG_DIM) + EPS) * hgn_ref[...] * (g * _sigmoid(g))
    mix = jnp.concatenate([y, att_ref[...]], axis=-1)
    o_ref[...] = _tail(x_ref[...], mix, outw_ref, w1_ref, w2_ref, ng_ref, ga_ref, shm_ref, scm_ref, gm_ref)


def _odd_post(x, ohg, proj, att, hgn, outw, w1, w2, ng, mods, n_ctx, layer):
    b, n, d = x.shape
    tm = LAT_TILE
    seg = lambda i: 1
    w = HG_WIDTH
    el = lambda *shape: tuple(pl.Element(s) for s in shape)
    row0 = lambda i: pl.multiple_of(n_ctx + i * tm, int(np.gcd(n_ctx, tm)))
    return pl.pallas_call(
        _odd_post_body,
        grid=(b, n // tm),
        in_specs=[
            pl.BlockSpec((None, tm, d), lambda bb, i: (bb, i, 0)),
            pl.BlockSpec(el(2, 1, tm, w), lambda bb, i: (0, bb, row0(i), 0)),
            pl.BlockSpec(el(1, tm, w), lambda bb, i: (bb, row0(i), 4 * w)),
            pl.BlockSpec((None, tm, w), lambda bb, i: (bb, i, 0)),
            _const_spec((1, w)),
        ] + _tail_specs(seg, layer),
        out_specs=pl.BlockSpec((None, tm, d), lambda bb, i: (bb, i, 0)),
        out_shape=jax.ShapeDtypeStruct((b, n, d), F32),
        compiler_params=_params(("parallel", "parallel")),
        name="odd_post",
    )(x, ohg, proj, att, hgn, outw, w1, w2, ng, mods, mods, mods, mods)


def _rope_tables(n, n_ctx):
    quarter = MLA_ROPE // 4
    inv = ROPE_BASE ** (-jnp.arange(quarter, dtype=F32) / quarter)
    rows = n // GRID_W
    ang_r = jnp.arange(rows, dtype=F32)[:, None] * inv[None, :]
    ang_c = jnp.arange(GRID_W, dtype=F32)[:, None] * inv[None, :]
    by_row = lambda a: jnp.repeat(a, GRID_W, axis=0)
    by_col = lambda a: jnp.tile(a, (rows, 1))
    cos_r, sin_r, cos_c, sin_c = by_row(jnp.cos(ang_r)), by_row(jnp.sin(ang_r)), by_col(jnp.cos(ang_c)), by_col(jnp.sin(ang_c))
    cos = jnp.concatenate([cos_r, cos_r, cos_c, cos_c], axis=-1)
    sin = jnp.concatenate([-sin_r, sin_r, -sin_c, sin_c], axis=-1)
    cos = jnp.concatenate([jnp.ones((n_ctx, MLA_ROPE), F32), cos], axis=0)
    sin = jnp.concatenate([jnp.zeros((n_ctx, MLA_ROPE), F32), sin], axis=0)
    pad = lambda a: jnp.pad(a, ((0, 0), (0, LANES - MLA_ROPE)))
    return pad(cos), pad(sin)


_ROPE_SWAP = np.concatenate([np.arange(16, 32), np.arange(0, 16), np.arange(48, 64), np.arange(32, 48)])


def kernel(x, c, ctx, c_ctx, ada_w, ada_b, norm_g, out_w, mlp_w1, mlp_w2, hg_lb, ev_w_in, ml_conv, ml_gate_b, ml_norm, rw_mu, rw_w0, rw_w2, rw_a0, rw_a2, rw_g2, rw_kk, rw_ka, rw_rk, rw_ln_w, rw_ln_b, od_w_in, hg_norm, mla_q_norm, mla_w_qb, mla_kv_norm, mla_w_kvb):
    b, n, d = x.shape
    n_ctx = ctx.shape[1]
    assert d == D_MODEL and b + 1 <= SUBLANES
    assert n_ctx == ROW_TILE and n % LAT_TILE == 0 and n % ATTN_TQ == 0
    assert all(n_ctx % blk == 0 and n % blk == 0 for blk in (ML_BLOCK, RW_BLOCK, HG_BLOCK))
    row1 = lambda a: a.reshape(1, -1)

    c8 = jnp.concatenate([c, c_ctx[None], jnp.zeros((SUBLANES - b - 1, d), F32)], axis=0)
    mod = _ada_mod(c8, ada_w, ada_b).reshape(-1, SUBLANES, N_MOD, d)

    def layer_mods(l):
        lat = mod[l, :b]
        cm = jnp.broadcast_to(mod[l, b][None], lat.shape)
        return jnp.stack([cm, lat], axis=1)[:, :, :, None, :]

    outw_bf, w1_bf, w2_bf = _bf(out_w), _bf(mlp_w1), _bf(mlp_w2)

    l, e = 0, 0
    mods = layer_mods(l)
    wi = ev_w_in[e]
    w_pad = jnp.concatenate([wi[:, 2064:3984], wi[:, 2048:2064], jnp.zeros((d, LANES - 16), F32), wi[:, 0:2048]], axis=1)
    gate_b = jnp.pad(ml_gate_b[e].reshape(1, 16), ((0, 0), (0, LANES - 16)))
    zeros64 = jnp.zeros((64, RW_WIDTH), F32)
    w2cat = jnp.stack([jnp.concatenate([rw_w2[e, 0], zeros64], 0), jnp.concatenate([zeros64, rw_w2[e, 1]], 0)])
    a2cat = jnp.stack([jnp.concatenate([rw_a2[e, 0], zeros64], 0), jnp.concatenate([zeros64, rw_a2[e, 1]], 0)])
    proj, qk, gates, gates_t, rws, rw_lw, rw_kb, bg = _even_in(
        ctx, x, row1(norm_g[l, 0]), mods, _bf(w_pad),
        ml_conv[e], gate_b, row1(rw_mu[e]), rw_w0[e], _bf(w2cat), rw_a0[e], _bf(a2cat), _bf(rw_g2[e]),
        row1(rw_kk[e]), row1(rw_ka[e]), row1(rw_rk[e]))

    hm = _mlstm(qk, proj, gates, gates_t, n_ctx)
    yrw = _rwkv(rws, rw_lw, rw_kb, n_ctx)
    post_args = (hm, proj, yrw, bg, row1(ml_norm[e]), row1(rw_ln_w[e]), row1(rw_ln_b[e]),
                 outw_bf, w1_bf, w2_bf, norm_g, mods, l)
    xc1 = _even_post(ctx, 0, 0, ROW_TILE, *post_args)
    x1 = _even_post(x, n_ctx, 1, LAT_TILE, *post_args)

    l, o = 1, 0
    mods = layer_mods(l)
    lb_all = jax.nn.softmax(hg_lb.astype(F32), axis=0)
    lb_all = jnp.cumsum(lb_all, axis=0) - lb_all[0]
    wi = od_w_in[o]
    kr = wi[:, 2944:3008]
    z = lambda k: jnp.zeros((d, k), F32)
    w_pad = jnp.concatenate([wi[:, 0:2944], kr, z(64), kr[:, _ROPE_SWAP], z(64)], axis=1)
    cos, sin = _rope_tables(n, n_ctx)
    wkv = mla_w_kvb[o].reshape(MLA_KV_RANK, MLA_HEADS, MLA_NOPE + MLA_V)
    wk = wkv[:, :, :MLA_NOPE].reshape(MLA_KV_RANK, MLA_HEADS * MLA_NOPE)
    wvt = jnp.transpose(wkv[:, :, MLA_NOPE:], (1, 2, 0))
    wq = mla_w_qb[o].reshape(MLA_Q_RANK, MLA_HEADS, MLA_NOPE + MLA_ROPE)
    zq = lambda k: jnp.zeros((MLA_Q_RANK, MLA_HEADS, k), F32)
    wq1 = jnp.concatenate([wq, zq(64)], axis=-1).reshape(MLA_Q_RANK, -1)
    wq2 = jnp.concatenate([wq[:, :, MLA_NOPE:][:, :, _ROPE_SWAP], zq(64)], axis=-1).reshape(MLA_Q_RANK, -1)
    proj, k_all, v_all, q_all = _odd_in(
        xc1, x1, row1(norm_g[l, 0]), mods, _bf(w_pad), cos, sin, row1(mla_kv_norm[o]), row1(mla_q_norm[o]),
        _bf(wk), _bf(wvt), _bf(wq1), _bf(wq2))

    ohg = _hgrn(proj, row1(lb_all[l]), n_ctx)
    att = _attention(q_all, k_all, v_all)

    return _odd_post(x1, ohg, proj, att, row1(hg_norm[o]), outw_bf, w1_bf, w2_bf, norm_g, mods, n_ctx, l)
```

```python
import functools

import numpy as np
import jax
import jax.numpy as jnp
from jax import lax
from jax.experimental import pallas as pl
from jax.experimental.pallas import tpu as pltpu

F32 = jnp.float32
BF16 = jnp.bfloat16

D_MODEL = 1024
GRID_W = 64
D_FF = 4 * D_MODEL
N_MOD = 6
EPS = 1e-6
ML_WIDTH = 512
ML_HEADS = 4
ML_DIM = 128
RW_WIDTH = 512
RW_HEAD_DIM = 64
RW_GN_EPS = 64e-5
RW_FEAT = 1920
HG_WIDTH = 512
HG_HEADS = 4
HG_DIM = 128
MLA_HEADS = 4
MLA_NOPE = 128
MLA_ROPE = 64
MLA_V = 128
MLA_Q_RANK = 256
MLA_KV_RANK = 128
ROPE_BASE = 10000.0

LANES = 128
SUBLANES = 8
HALO = 16
VMEM_LIMIT_BYTES = 56 * 2**20

ROW_TILE = 256
LAT_TILE = 512
FF_CHUNK = 1024
ML_CHUNK = 128
ML_BLOCK = 256
RW_CHUNK = 64
RW_BLOCK = 256
HG_BLOCK = 256
HG_CHUNK = 32
ATTN_TQ = 512
ATTN_SUB = 256


def _dot(a, b, prec=None):
    return jnp.dot(a, b, preferred_element_type=F32, precision=prec)


def _dot_nt(a, b, prec=None):
    return lax.dot_general(a, b, (((1,), (1,)), ((), ())), preferred_element_type=F32, precision=prec)


def _dot_tn(a, b, prec=None):
    return lax.dot_general(a, b, (((0,), (0,)), ((), ())), preferred_element_type=F32, precision=prec)


def _bf(x):
    return x.astype(BF16)


def _mm(a, b):
    return _dot(_bf(a), _bf(b))


def _mm_nt(a, b):
    return _dot_nt(_bf(a), _bf(b))


def _mm_tn(a, b):
    return _dot_tn(_bf(a), _bf(b))


def _split(x):
    hi = _bf(x)
    return hi, _bf(x - hi.astype(F32))


def _sel_dot(sel, x):
    hi, lo = _split(x)
    return _dot(sel, hi) + _dot(sel, lo)


def _dot_sel(x, sel):
    hi, lo = _split(x)
    return _dot(hi, sel) + _dot(lo, sel)


def _dot_sel_nt(x, sel):
    hi, lo = _split(x)
    return _dot_nt(hi, sel) + _dot_nt(lo, sel)


def _dot3(a, b):
    ah, al = _split(a)
    bh, bl = _split(b)
    return _dot(ah, bh) + _dot(ah, bl) + _dot(al, bh)


def _group_mean(x, group):
    low = lax.broadcasted_iota(jnp.int32, (1, LANES), 1) < group
    outs = []
    for c in range(x.shape[1] // LANES):
        blk = x[:, c * LANES:(c + 1) * LANES]
        total = jnp.sum(blk, axis=-1, keepdims=True)
        if group == LANES:
            outs.append(jnp.broadcast_to(total, blk.shape))
        else:
            first = jnp.sum(jnp.where(low, blk, 0.0), axis=-1, keepdims=True)
            outs.append(jnp.where(low, first, total - first))
    return jnp.concatenate(outs, axis=-1) * (1.0 / group)


def _sigmoid(x):
    return 1.0 / (1.0 + jnp.exp(-x))


def _log_sigmoid(x):
    return jnp.minimum(x, 0.0) - jnp.log(1.0 + jnp.exp(-jnp.abs(x)))


def _softplus(x):
    return jnp.maximum(x, 0.0) + jnp.log(1.0 + jnp.exp(-jnp.abs(x)))


def _rms(x, g):
    return x * lax.rsqrt(jnp.mean(x * x, axis=-1, keepdims=True) + EPS) * g


def _order_mask(n, sgn, strict):
    row = lax.broadcasted_iota(jnp.int32, (n, n), 0)
    col = lax.broadcasted_iota(jnp.int32, (n, n), 1)
    diff = (row - col) * sgn
    return diff > 0 if strict else diff >= 0


def _chunk_index(d, j, nctx, nc):
    bwd = jnp.where(j < nctx, nctx - 1 - j, nc - 1 - (j - nctx))
    return jnp.where(d == 0, j, bwd)


def _params(sem):
    return pltpu.CompilerParams(dimension_semantics=sem, vmem_limit_bytes=VMEM_LIMIT_BYTES)


def _const_spec(shape):
    nd = len(shape)
    return pl.BlockSpec(shape, lambda *_: (0,) * nd, pipeline_mode=pl.Buffered(1))


def _mod_spec(k, seg_of):
    return pl.BlockSpec((None, None, None, 1, D_MODEL), lambda b, i: (b, seg_of(i), k, 0, 0))


def _ada_body(c_ref, w_ref, b_ref, o_ref):
    c = c_ref[...]
    o_ref[...] = _dot3(c * _sigmoid(c), w_ref[...]) + b_ref[...]


def _ada_mod(c8, ada_w, ada_b):
    depth, d, n = ada_w.shape
    tn = 1024
    return pl.pallas_call(
        _ada_body,
        grid=(depth, n // tn),
        in_specs=[
            _const_spec((SUBLANES, d)),
            pl.BlockSpec((None, d, tn), lambda l, j: (l, 0, j)),
            pl.BlockSpec((None, 1, tn), lambda l, j: (l, 0, j)),
        ],
        out_specs=pl.BlockSpec((None, SUBLANES, tn), lambda l, j: (l, 0, j)),
        out_shape=jax.ShapeDtypeStruct((depth, SUBLANES, n), F32),
        compiler_params=_params(("parallel", "parallel")),
        name="ada_mod",
    )(c8, ada_w, ada_b.reshape(depth, 1, n))


EV_SHIFTED = 3072


def _even_in_body(nt, c_ref, x_ref, xp_ref, xn_ref, g_ref, sh_ref, sc_ref, w_ref,
                  conv_ref, gb_ref, mu_ref, w0_ref, w2_ref, a0_ref, a2_ref, g2_ref,
                  kk_ref, ka_ref, rk_ref,
                  ovo_ref, oqk_ref, og_ref, ogt_ref, orws_ref, olw_ref, okb_ref, obg_ref):
    i = pl.program_id(1)
    has_prev = (i > 1).astype(F32)
    has_next = jnp.logical_and(i != 0, i != nt - 1).astype(F32)
    tm = x_ref.shape[0]
    ext = tm + 2 * HALO
    row = lax.broadcasted_iota(jnp.int32, (tm, 1), 0)
    keep_dn = jnp.where(row == 0, has_prev, 1.0)
    keep_up = jnp.where(row == tm - 1, has_next, 1.0)

    def normed(xx):
        return _bf(_rms(xx, g_ref[...]) * (1.0 + sc_ref[...]) + sh_ref[...])

    h = normed(jnp.where(i == 0, c_ref[...], x_ref[...]))
    ovo_ref[...] = _dot(h, w_ref[:, EV_SHIFTED:]).astype(ovo_ref.dtype)
    hcat = jnp.concatenate([normed(xp_ref[...]), h, normed(xn_ref[...])], axis=0)
    proj = _dot(hcat, w_ref[:, 0:EV_SHIFTED])

    def neighbours(cols):
        blk = proj[:, cols]
        dn = pltpu.roll(blk, 1, 0)[HALO:HALO + tm] * keep_dn
        up = pltpu.roll(blk, ext - 1, 0)[HALO:HALO + tm] * keep_up
        return blk[HALO:HALO + tm], dn, up

    qk, dn, up = neighbours(slice(2048, 3072))
    oqk_ref[...] = (dn * conv_ref[0:1, :] + qk * conv_ref[1:2, :] + up * conv_ref[2:3, :]).astype(oqk_ref.dtype)

    gv = proj[HALO:HALO + tm, RW_FEAT:RW_FEAT + LANES] + gb_ref[...]
    lane = lax.broadcasted_iota(jnp.int32, (1, LANES), 1)
    is_f = jnp.logical_and(jnp.bitwise_and(jnp.right_shift(lane, 2), 1) == 1, lane < 16)
    gates = jnp.where(is_f, _log_sigmoid(gv), gv)
    og_ref[...] = gates
    ogt_ref[...] = gates.T[0:4 * ML_HEADS, :]

    cur, dn, up = neighbours(slice(0, RW_FEAT))
    rf = cur + (0.5 * (dn + up) - cur) * mu_ref[...]
    w = RW_WIDTH
    r, k, v = rf[:, 0:w], rf[:, w:2 * w], rf[:, 2 * w:3 * w]
    wl = rf[:, 3 * w:3 * w + LANES]
    al = rf[:, 3 * w + LANES:3 * w + 2 * LANES]
    gl = rf[:, 3 * w + 2 * LANES:3 * w + 3 * LANES]
    kk = k * kk_ref[...]
    kk = kk * lax.rsqrt(_group_mean(kk * kk, RW_HEAD_DIM) * RW_HEAD_DIM + 1e-12)
    twl = jnp.tanh(wl)
    kd_sum = jnp.zeros_like(k)
    for d in range(2):
        w_log = -_softplus(-(w0_ref[d:d + 1, :] + _dot_sel(twl, w2_ref[d]))) - 0.5
        a = _sigmoid(a0_ref[d:d + 1, :] + _dot_sel(al, a2_ref[d]))
        kd = k * (1.0 + (a - 1.0) * ka_ref[...])
        kd_sum = kd_sum + kd
        olw_ref[d] = -jnp.exp(w_log)
        okb_ref[d, :, 0:w] = kd.astype(okb_ref.dtype)
        okb_ref[d, :, w:2 * w] = (kk * a).astype(okb_ref.dtype)
    orws_ref[:, 0:w] = r.astype(orws_ref.dtype)
    orws_ref[:, w:2 * w] = v.astype(orws_ref.dtype)
    orws_ref[:, 2 * w:3 * w] = (-kk).astype(orws_ref.dtype)
    bonus = _group_mean(r * kd_sum * rk_ref[...], RW_HEAD_DIM) * RW_HEAD_DIM * v
    obg_ref[:, 0:w] = bonus.astype(obg_ref.dtype)
    obg_ref[:, w:2 * w] = _dot(_bf(_sigmoid(gl)), g2_ref[...]).astype(obg_ref.dtype)


def _even_in(ctx, x, g, mods, w_in, conv, gate_b, mu, w0, w2cat, a0, a2cat, g2, kk, ka, rk):
    b, n, d = x.shape
    tm = ROW_TILE
    assert ctx.shape[1] == tm
    t = n + tm
    nt = t // tm
    rh = tm // HALO
    last_h = n // HALO - 1
    seg = lambda i: (i >= 1).astype(jnp.int32)
    cur = lambda c: (lambda bb, i: (bb, i, c))
    lat = lambda i: jnp.maximum(i - 1, 0)
    w = RW_WIDTH
    return pl.pallas_call(
        functools.partial(_even_in_body, nt),
        grid=(b, nt),
        in_specs=[
            pl.BlockSpec((None, tm, d), lambda bb, i: (bb, 0, 0)),
            pl.BlockSpec((None, tm, d), lambda bb, i: (bb, lat(i), 0)),
            pl.BlockSpec((None, HALO, d), lambda bb, i: (bb, jnp.maximum(lat(i) * rh - 1, 0), 0)),
            pl.BlockSpec((None, HALO, d), lambda bb, i: (bb, jnp.minimum((lat(i) + 1) * rh, last_h), 0)),
            _const_spec((1, d)), _mod_spec(0, seg), _mod_spec(1, seg), _const_spec(w_in.shape),
            _const_spec(conv.shape), _const_spec(gate_b.shape), _const_spec(mu.shape),
            _const_spec(w0.shape), _const_spec(w2cat.shape), _const_spec(a0.shape),
            _const_spec(a2cat.shape), _const_spec(g2.shape), _const_spec(kk.shape),
            _const_spec(ka.shape), _const_spec(rk.shape),
        ],
        out_specs=[
            pl.BlockSpec((None, tm, 1024), cur(0)),
            pl.BlockSpec((None, tm, 1024), cur(0)),
            pl.BlockSpec((None, tm, LANES), cur(0)),
            pl.BlockSpec((None, 4 * ML_HEADS, tm), lambda bb, i: (bb, 0, i)),
            pl.BlockSpec((None, tm, 3 * w), cur(0)),
            pl.BlockSpec((2, None, tm, w), lambda bb, i: (0, bb, i, 0)),
            pl.BlockSpec((2, None, tm, 2 * w), lambda bb, i: (0, bb, i, 0)),
            pl.BlockSpec((None, tm, 2 * w), cur(0)),
        ],
        out_shape=[
            jax.ShapeDtypeStruct((b, t, 1024), BF16),
            jax.ShapeDtypeStruct((b, t, 1024), BF16),
            jax.ShapeDtypeStruct((b, t, LANES), F32),
            jax.ShapeDtypeStruct((b, 4 * ML_HEADS, t), F32),
            jax.ShapeDtypeStruct((b, t, 3 * w), BF16),
            jax.ShapeDtypeStruct((2, b, t, w), F32),
            jax.ShapeDtypeStruct((2, b, t, 2 * w), BF16),
            jax.ShapeDtypeStruct((b, t, 2 * w), BF16),
        ],
        compiler_params=_params(("parallel", "parallel")),
        name="even_in",
    )(ctx, x, x, x, g, mods, mods, w_in, conv, gate_b, mu, w0, w2cat, a0, a2cat, g2, kk, ka, rk)


def _mlstm_steps(q_ref, k_ref, v_ref, gc_ref, gr_ref, o_ref, c_scr, n_scr, m_scr):
    d = pl.program_id(0)
    j = pl.program_id(1)
    n_batch = q_ref.shape[0]
    n_rows = ML_CHUNK
    n_sub = q_ref.shape[1] // n_rows

    @pl.when(j == 0)
    def _():
        c_scr[...] = jnp.zeros_like(c_scr)
        n_scr[...] = jnp.zeros_like(n_scr)
        m_scr[...] = jnp.zeros_like(m_scr)

    mask = _order_mask(n_rows, 1 - 2 * d, strict=False)
    mf = mask.astype(BF16)
    row_id = lax.broadcasted_iota(jnp.int32, (n_rows, 1), 0)
    is_last = row_id == (n_rows - 1) * (1 - d)
    pick_row = lax.broadcasted_iota(jnp.int32, (LANES, LANES), 0)
    pick = [(pick_row == 2 * ML_HEADS * d + c).astype(BF16) for c in range(2 * ML_HEADS)]
    ones_cols = jnp.ones((n_rows, LANES), BF16)

    rows = [pl.ds(pl.multiple_of(jnp.where(d == 0, sc, n_sub - 1 - sc) * n_rows, n_rows), n_rows)
            for sc in range(n_sub)]
    heads = [slice(h * ML_DIM, (h + 1) * ML_DIM) for h in range(ML_HEADS)]
    units = [(sc, bi, h) for sc in range(n_sub) for bi in range(n_batch) for h in range(ML_HEADS)]
    idx = range(len(units))
    carried = [bi * ML_HEADS + h for _, bi, h in units]
    gc, b_col, gr, b_row = {}, {}, {}, {}
    for bi in range(n_batch):
        gr_all = gr_ref[bi]
        for sc in range(n_sub):
            gc[sc, bi] = gc_ref[bi, rows[sc], :]
            parts = [gr_all[:, c * n_rows:(c + 1) * n_rows] for c in (sc, n_sub - 1 - sc)]
            gr[sc, bi] = jnp.where(d == 0, parts[0], parts[1])
            b_col[sc, bi] = _sel_dot(mf, gc[sc, bi])
            b_row[sc, bi] = _dot_sel_nt(gr[sc, bi], mf)
    q_f = [q_ref[bi, rows[sc], heads[h]] for sc, bi, h in units]
    q = [_bf(x) for x in q_f]
    kf = [k_ref[bi, rows[sc], heads[h]].astype(F32) * (ML_DIM ** -0.5) for sc, bi, h in units]
    k = [_bf(x) for x in kf]
    vf = [v_ref[bi, rows[sc], heads[h]].astype(F32) for sc, bi, h in units]
    qk = [_dot_nt(q[u], k[u]) for u in idx]
    ic = [_dot_sel(gc[sc, bi], pick[h]) for sc, bi, h in units]
    bc = [_dot_sel(b_col[sc, bi], pick[ML_HEADS + h]) for sc, bi, h in units]
    yield
    b_tot = [jnp.sum(jnp.where(is_last, bc[u], 0.0), axis=0, keepdims=True) for u in idx]
    logw = [b_tot[u] - bc[u] + ic[u] for u in idx]
    logw_max = [jnp.max(logw[u], axis=0, keepdims=True) for u in idx]
    logd = [jnp.where(mask, bc[u] - b_row[sc, bi][ML_HEADS + h:ML_HEADS + h + 1, :] + gr[sc, bi][h:h + 1, :], -1e30)
            for u, (sc, bi, h) in enumerate(units)]
    logd_max = [jnp.max(x, axis=-1, keepdims=True) for x in logd]
    m_run = [m_scr[c][0:1, :] for c in range(n_batch * ML_HEADS)]
    wgt, decay, s, m_t, w_inter = [], [], [], [], []
    for u in idx:
        m_prev = m_run[carried[u]]
        m_new = jnp.maximum(b_tot[u] + m_prev, logw_max[u])
        wgt.append(jnp.exp(logw[u] - m_new))
        decay.append(jnp.exp(b_tot[u] + m_prev - m_new))
        m_inter = bc[u] + m_prev
        m_t.append(jnp.maximum(m_inter, logd_max[u]))
        s.append(qk[u] * jnp.exp(logd[u] - m_t[u]))
        w_inter.append(jnp.exp(m_inter - m_t[u]))
        m_run[carried[u]] = m_new
    kv = [_dot_tn(_bf(vf[u] * wgt[u]), k[u]) for u in idx]
    sv = [_dot(_bf(s[u]), jnp.concatenate([_bf(vf[u]), ones_cols], axis=1)) for u in idx]
    yield
    c_run = [c_scr[c] for c in range(n_batch * ML_HEADS)]
    n_run = [n_scr[c][0:1, :] for c in range(n_batch * ML_HEADS)]
    per_chunk = n_batch * ML_HEADS
    for sc in range(n_sub):
        ids = range(sc * per_chunk, (sc + 1) * per_chunk)
        qc = [_dot_nt(q[u], jnp.concatenate([_bf(c_run[carried[u]]),
                                             _bf(jnp.broadcast_to(n_run[carried[u]], (LANES, ML_DIM)))], axis=0))
              for u in ids]
        for u, x in zip(ids, qc):
            _, bi, h = units[u]
            num = sv[u][:, :ML_DIM] + w_inter[u] * x[:, :ML_DIM]
            den = sv[u][:, ML_DIM:] + w_inter[u] * x[:, ML_DIM:]
            o_ref[bi, rows[sc], heads[h]] = num / jnp.maximum(jnp.abs(den), jnp.exp(-m_t[u]))
        for u in ids:
            c = carried[u]
            c_run[c] = decay[u] * c_run[c] + kv[u]
            n_run[c] = decay[u] * n_run[c] + jnp.sum(wgt[u] * kf[u], axis=0, keepdims=True)
    for c in range(per_chunk):
        c_scr[c] = c_run[c]
        n_scr[c] = jnp.broadcast_to(n_run[c], (SUBLANES, ML_DIM))
        m_scr[c] = jnp.broadcast_to(m_run[c], (SUBLANES, LANES))


def _rwkv_steps(r_ref, v_ref, a_ref, lw_ref, k_ref, b_ref, o_ref, h_scr):
    d = pl.program_id(0)
    j = pl.program_id(1)
    n_rows = RW_CHUNK
    two = 2 * n_rows
    shift = n_rows.bit_length() - 1

    @pl.when(j == 0)
    def _():
        h_scr[...] = jnp.zeros_like(h_scr)

    sgn = 1 - 2 * d
    cum = _order_mask(n_rows, sgn, strict=False).astype(BF16)
    row = lax.broadcasted_iota(jnp.int32, (two, two), 0)
    col = lax.broadcasted_iota(jnp.int32, (two, two), 1)
    same = jnp.right_shift(row, shift) == jnp.right_shift(col, shift)
    diff = (jnp.bitwise_and(row, n_rows - 1) - jnp.bitwise_and(col, n_rows - 1)) * sgn
    m_strict = jnp.logical_and(same, diff > 0)
    m_incl = jnp.logical_and(same, diff >= 0)
    eye = (row == col).astype(F32)
    head0 = lax.broadcasted_iota(jnp.int32, (1, LANES), 1) < RW_HEAD_DIM

    def stack(x):
        return jnp.concatenate([jnp.where(head0, x, 0.0), jnp.where(head0, 0.0, x)], axis=0)

    def fold(x):
        return x[:n_rows] + x[n_rows:]

    n_batch = r_ref.shape[0]
    n_sub = r_ref.shape[1] // n_rows
    lane_groups = [slice(p * LANES, (p + 1) * LANES) for p in range(RW_WIDTH // LANES)]
    rows = [pl.ds(pl.multiple_of(jnp.where(d == 0, sc, n_sub - 1 - sc) * n_rows, n_rows), n_rows)
            for sc in range(n_sub)]
    groups = [(sc, bi, p) for sc in range(n_sub) for bi in range(n_batch) for p in range(len(lane_groups))]
    x_s, r_s, v_s, bh_s, kh_s, bi_s, ki_s, e_tot = [], [], [], [], [], [], [], []
    for sc in range(n_sub):
        for bi in range(n_batch):
            lw = lw_ref[bi, rows[sc], :]
            g = _sel_dot(cum, lw)
            g_tot = jnp.sum(lw, axis=0, keepdims=True)
            inv = jnp.exp(-g)
            to_end = jnp.exp(g_tot - g)
            at = a_ref[bi, rows[sc], :].astype(F32) * jnp.exp(g - lw)
            rt = r_ref[bi, rows[sc], :].astype(F32) * jnp.exp(g)
            b_in = b_ref[bi, rows[sc], :].astype(F32)
            k_in = k_ref[bi, rows[sc], :].astype(F32)
            vv = v_ref[bi, rows[sc], :].astype(F32)
            for sl in lane_groups:
                x_s.append(stack(at[:, sl]))
                r_s.append(stack(rt[:, sl]))
                v_s.append(stack(vv[:, sl]))
                bh_s.append(stack((b_in * to_end)[:, sl]))
                kh_s.append(stack((k_in * to_end)[:, sl]))
                bi_s.append(stack((b_in * inv)[:, sl]))
                ki_s.append(stack((k_in * inv)[:, sl]))
                e_tot.append(jnp.exp(g_tot)[:, sl])
    big = [_mm_nt(jnp.concatenate([x_s[p], r_s[p]], axis=0), jnp.concatenate([bi_s[p], ki_s[p]], axis=0))
           for p in range(len(groups))]
    a_ab = [jnp.where(m_strict, m[:two, :two], 0.0) for m in big]
    a_rb = [jnp.where(m_incl, m[two:, :two], 0.0) for m in big]
    a_k = [jnp.concatenate([jnp.where(m_strict, m[:two, two:], 0.0), jnp.where(m_incl, m[two:, two:], 0.0)], axis=0)
           for m in big]
    av = [_mm(a_k[p], v_s[p]) for p in range(len(groups))]
    yield
    t_inv = [eye + m for m in a_ab]
    pw = [_mm(m, m) for m in a_ab]
    for level in range(1, shift):
        last = level == shift - 1
        res = [_mm(t if last else jnp.concatenate([t, m], axis=0), m) for t, m in zip(t_inv, pw)]
        t_inv = [t + r[:two] for t, r in zip(t_inv, res)]
        if not last:
            pw = [r[two:] for r in res]
    wu = [_mm(t_inv[p], jnp.concatenate([x_s[p], av[p][:two]], axis=1)) for p in range(len(groups))]
    rb = [_mm(a_rb[p], wu[p]) for p in range(len(groups))]
    upd = [_mm_tn(wu[p], bh_s[p]) for p in range(len(groups))]
    gt2 = [_mm_tn(v_s[p], kh_s[p]) for p in range(len(groups))]
    yield
    q_eff = [fold(r_s[p] + rb[p][:, :LANES]) for p in range(len(groups))]
    y_in = [fold(rb[p][:, LANES:] + av[p][two:]) for p in range(len(groups))]
    per_chunk = n_batch * len(lane_groups)
    h_t = [h_scr[u] for u in range(per_chunk)]
    for sc in range(n_sub):
        ids = range(sc * per_chunk, (sc + 1) * per_chunk)
        for u, p in enumerate(ids):
            _, bi, lg = groups[p]
            o_ref[bi, rows[sc], lane_groups[lg]] = _mm_nt(q_eff[p], h_t[u]) + y_in[p]
        h_t = [e_tot[p] * h_t[u] + _mm(h_t[u], upd[p][:LANES]) + upd[p][LANES:] + gt2[p] for u, p in enumerate(ids)]
    for u in range(per_chunk):
        h_scr[u] = h_t[u]


def _even_mix_body(q_ref, k_ref, v_ref, gc_ref, gr_ref, r_ref, rv_ref, a_ref, lw_ref, rk_ref, b_ref,
                   om_ref, or_ref, c_scr, n_scr, m_scr, h_scr):
    pending = [_mlstm_steps(q_ref, k_ref, v_ref, gc_ref, gr_ref, om_ref, c_scr, n_scr, m_scr),
               _rwkv_steps(r_ref, rv_ref, a_ref, lw_ref, rk_ref, b_ref, or_ref, h_scr)]
    while pending:
        for steps in list(pending):
            if next(steps, "done") == "done":
                pending.remove(steps)


def _even_mix(qk, proj, gates, gates_t, rws, lw, kb, n_ctx):
    b, t, _ = rws.shape
    assert ML_BLOCK == RW_BLOCK
    n_rows = RW_BLOCK
    nc, nctx = t // n_rows, n_ctx // n_rows
    ci = lambda d, j: _chunk_index(d, j, nctx, nc)
    w = RW_WIDTH
    shared = lambda c: pl.BlockSpec((b, n_rows, w), lambda d, j: (0, ci(d, j), c))
    per_dir = lambda c: pl.BlockSpec((None, b, n_rows, w), lambda d, j: (d, 0, ci(d, j), c))
    out = pl.BlockSpec((None, b, n_rows, w), lambda d, j: (d, 0, ci(d, j), 0))
    return pl.pallas_call(
        _even_mix_body,
        grid=(2, nc),
        in_specs=[
            shared(0), shared(1), shared(0),
            pl.BlockSpec((b, n_rows, LANES), lambda d, j: (0, ci(d, j), 0)),
            pl.BlockSpec((b, 2 * ML_HEADS, n_rows), lambda d, j: (0, d, ci(d, j))),
            shared(0), shared(1), shared(2), per_dir(0), per_dir(0), per_dir(1),
        ],
        out_specs=[out, out],
        out_shape=[jax.ShapeDtypeStruct((2, b, t, w), F32), jax.ShapeDtypeStruct((2, b, t, w), F32)],
        scratch_shapes=[
            pltpu.VMEM((b * ML_HEADS, ML_DIM, ML_DIM), F32),
            pltpu.VMEM((b * ML_HEADS, SUBLANES, ML_DIM), F32),
            pltpu.VMEM((b * ML_HEADS, SUBLANES, LANES), F32),
            pltpu.VMEM((b * w // LANES, LANES, LANES), F32),
        ],
        compiler_params=_params(("parallel", "arbitrary")),
        name="even_mix",
    )(qk, qk, proj, gates, gates_t, rws, rws, rws, lw, kb, kb)


def _tail(x, mix, outw_ref, w1_ref, w2_ref, ng_ref, ga_ref, shm_ref, scm_ref, gm_ref):
    x1 = x + ga_ref[...] * _rms(_dot(_bf(mix), outw_ref[...]), ng_ref[1:2, :])
    h = _bf(_rms(x1, ng_ref[2:3, :]) * (1.0 + scm_ref[...]) + shm_ref[...])
    chunks = [slice(c * FF_CHUNK, (c + 1) * FF_CHUNK) for c in range(D_FF // FF_CHUNK)]
    u = [jnp.maximum(_dot(h, w1_ref[:, cols]), 0.0) for cols in chunks]
    dd = _dot(_bf(u[0] * u[0]), w2_ref[chunks[0], :])
    for uc, cols in zip(u[1:], chunks[1:]):
        dd = dd + _dot(_bf(uc * uc), w2_ref[cols, :])
    return x1 + gm_ref[...] * _rms(dd, ng_ref[3:4, :])


def _even_post_body(x_ref, hm_ref, mo_ref, yrw_ref, bg_ref, mln_ref, lnw_ref, lnb_ref,
                    outw_ref, w1_ref, w2_ref, ng_ref, ga_ref, shm_ref, scm_ref, gm_ref, o_ref):
    w = RW_WIDTH
    ym = hm_ref[0, 0] + hm_ref[1, 0]
    ym = (ym * lax.rsqrt(_group_mean(ym * ym, ML_DIM) + EPS) * mln_ref[...]
          * _sigmoid(mo_ref[0].astype(F32)))
    y = yrw_ref[0, 0] + yrw_ref[1, 0]
    yc = y - _group_mean(y, RW_HEAD_DIM)
    y = (yc * lax.rsqrt(_group_mean(yc * yc, RW_HEAD_DIM) + RW_GN_EPS) * lnw_ref[...] + lnb_ref[...]
         + bg_ref[0, :, 0:w].astype(F32))
    mix = jnp.concatenate([ym, y * bg_ref[0, :, w:2 * w].astype(F32)], axis=-1)
    o_ref[...] = _tail(x_ref[...], mix, outw_ref, w1_ref, w2_ref, ng_ref, ga_ref, shm_ref, scm_ref, gm_ref)


def _tail_specs(seg, layer):
    d = D_MODEL
    pick = lambda *shape: pl.BlockSpec((None,) + shape, lambda *_: (layer, 0, 0), pipeline_mode=pl.Buffered(1))
    return [
        pick(d, d), pick(d, D_FF), pick(D_FF, d), pick(4, d),
        _mod_spec(2, seg), _mod_spec(3, seg), _mod_spec(4, seg), _mod_spec(5, seg),
    ]


def _even_post(x_seg, first_row, segment, tm, hm, proj, yrw, bg, mln, lnw, lnb, outw, w1, w2, ng, mods, layer):
    b, rows, d = x_seg.shape
    seg = lambda i: segment
    w = RW_WIDTH
    el = lambda *shape: tuple(pl.Element(s) for s in shape)
    row0 = lambda i: pl.multiple_of(first_row + i * tm, int(np.gcd(first_row, tm)))
    return pl.pallas_call(
        _even_post_body,
        grid=(b, rows // tm),
        in_specs=[
            pl.BlockSpec((None, tm, d), lambda bb, i: (bb, i, 0)),
            pl.BlockSpec(el(2, 1, tm, w), lambda bb, i: (0, bb, row0(i), 0)),
            pl.BlockSpec(el(1, tm, w), lambda bb, i: (bb, row0(i), w)),
            pl.BlockSpec(el(2, 1, tm, w), lambda bb, i: (0, bb, row0(i), 0)),
            pl.BlockSpec(el(1, tm, 2 * w), lambda bb, i: (bb, row0(i), 0)),
            _const_spec((1, w)), _const_spec((1, w)), _const_spec((1, w)),
        ] + _tail_specs(seg, layer),
        out_specs=pl.BlockSpec((None, tm, d), lambda bb, i: (bb, i, 0)),
        out_shape=jax.ShapeDtypeStruct((b, rows, d), F32),
        compiler_params=_params(("parallel", "parallel")),
        name="even_post",
    )(x_seg, hm, proj, yrw, bg, mln, lnw, lnb, outw, w1, w2, ng, mods, mods, mods, mods)


def _hgrn_body(q_ref, f_ref, i_ref, lb_ref, o_ref, s_scr):
    d = pl.program_id(0)
    j = pl.program_id(1)
    n_batch, n_rows = q_ref.shape[0], q_ref.shape[1]
    n_sub = n_rows // HG_CHUNK

    @pl.when(j == 0)
    def _():
        s_scr[...] = jnp.zeros_like(s_scr)

    mask = _order_mask(HG_CHUNK, 1 - 2 * d, strict=False)
    mf = mask.astype(BF16)
    lb = lb_ref[...]
    heads = range(HG_HEADS)
    sls = [slice(h * HG_DIM, (h + 1) * HG_DIM) for h in heads]
    units = [(bi, h) for bi in range(n_batch) for h in heads]
    n_units = len(units)
    s_t = [s_scr[u] for u in range(n_units)]
    rows = [pl.ds(pl.multiple_of(jnp.where(d == 0, sc, n_sub - 1 - sc) * HG_CHUNK, HG_CHUNK), HG_CHUNK)
            for sc in range(n_sub)]
    loaded = [[(q_ref[bi, r, :], f_ref[bi, r, :], i_ref[bi, r, :]) for bi in range(n_batch)] for r in rows]
    q_in, e_tot, q_mids, k_mids, k_outs, vvs = [], [], [], [], [], []

    def prepare(sc):
        for lst in (q_in, e_tot, q_mids, k_mids, k_outs, vvs):
            lst.append([])
        for bi in range(n_batch):
            qa = loaded[sc][bi][0].astype(F32)
            pre = loaded[sc][bi][1].astype(F32)
            vv = loaded[sc][bi][2]
            q = qa * _sigmoid(qa) * (HG_DIM ** -0.5)
            e_abs = jnp.exp(-jnp.abs(pre))
            big = 1.0 / (1.0 + e_abs)
            small = e_abs * big
            log_f = jnp.log(lb + (1.0 - lb) * jnp.where(pre >= 0.0, big, small))
            kf = (1.0 - lb) * jnp.where(pre >= 0.0, small, big)
            bc = _sel_dot(mf, log_f)
            tot = jnp.sum(log_f, axis=0, keepdims=True)
            half = 0.5 * tot
            e_half = jnp.exp(half)
            q_mid_f = q * jnp.exp(bc - half)
            k_mid_f = kf * jnp.exp(half - bc)
            q_mid = _bf(q_mid_f)
            k_mid = _bf(k_mid_f)
            k_out = _bf(k_mid_f * e_half)
            q_all = _bf(q_mid_f * e_half)
            e_all = e_half * e_half
            q_mids[sc] += [q_mid[:, sl] for sl in sls]
            k_mids[sc] += [k_mid[:, sl] for sl in sls]
            k_outs[sc] += [k_out[:, sl] for sl in sls]
            vvs[sc] += [vv[:, sl] for sl in sls]
            q_in[sc] += [q_all[:, sl] for sl in sls]
            e_tot[sc] += [e_all[:, sl] for sl in sls]

    prepare(0)
    for sc in range(n_sub):
        if sc + 1 < n_sub:
            prepare(sc + 1)
        att = [_dot_nt(q_mids[sc][u], k_mids[sc][u]) for u in range(n_units)]
        kv = [_dot_tn(vvs[sc][u], k_outs[sc][u]) for u in range(n_units)]
        intra = [_dot(_bf(jnp.where(mask, att[u], 0.0)), vvs[sc][u]) for u in range(n_units)]
        for u, (bi, h) in enumerate(units):
            o_ref[bi, rows[sc], sls[h]] = intra[u] + _dot_nt(q_in[sc][u], _bf(s_t[u]))
        s_t = [e_tot[sc][u] * s_t[u] + kv[u] for u in range(n_units)]
    for u in range(n_units):
        s_scr[u] = s_t[u]


def _hgrn(proj, lb, n_ctx):
    b, t, _ = proj.shape
    n_rows = HG_BLOCK
    nc, nctx = t // n_rows, n_ctx // n_rows
    ci = lambda d, j: _chunk_index(d, j, nctx, nc)
    w = HG_WIDTH
    return pl.pallas_call(
        _hgrn_body,
        grid=(2, nc),
        in_specs=[
            pl.BlockSpec((b, n_rows, w), lambda d, j: (0, ci(d, j), 0)),
            pl.BlockSpec((b, n_rows, w), lambda d, j: (0, ci(d, j), 1 + d)),
            pl.BlockSpec((b, n_rows, w), lambda d, j: (0, ci(d, j), 3)),
            _const_spec((1, w)),
        ],
        out_specs=pl.BlockSpec((None, b, n_rows, w), lambda d, j: (d, 0, ci(d, j), 0)),
        out_shape=jax.ShapeDtypeStruct((2, b, t, w), F32),
        scratch_shapes=[pltpu.VMEM((b * HG_HEADS, HG_DIM, HG_DIM), F32)],
        compiler_params=_params(("parallel", "arbitrary")),
        name="hgrn2",
    )(proj, proj, proj, lb)


V_ROWS = MLA_V + SUBLANES
OD_MAIN = 5 * HG_WIDTH


def _odd_in_body(c_ref, x_ref, g_ref, sh_ref, sc_ref, w_ref, cos_ref, sin_ref, kvn_ref, qn_ref,
                 wk_ref, wvt_ref, wq1_ref, wq2_ref, proj_ref, k_ref, vt_ref, q_ref):
    x_in = jnp.where(pl.program_id(1) == 0, c_ref[...], x_ref[...])
    h = _bf(_rms(x_in, g_ref[...]) * (1.0 + sc_ref[...]) + sh_ref[...])
    mla = _dot(h, w_ref[:, OD_MAIN:])
    proj_ref[...] = _dot(h, w_ref[:, 0:OD_MAIN]).astype(proj_ref.dtype)
    cq = mla[:, 0:MLA_Q_RANK]
    base = MLA_Q_RANK
    ckv = mla[:, base:base + MLA_KV_RANK]
    cos, sin = cos_ref[...], sin_ref[...]
    k_rope = mla[:, base + LANES:base + 2 * LANES] * cos + mla[:, base + 2 * LANES:base + 3 * LANES] * sin
    kvn = _bf(_rms(ckv, kvn_ref[...]))
    kn = _dot(kvn, wk_ref[...])
    ones = jnp.ones((SUBLANES, mla.shape[0]), BF16)
    cn = _bf(_rms(cq, qn_ref[...]))
    qa = _dot(cn, wq1_ref[...])
    qb = _dot(cn, wq2_ref[...])
    scale = (MLA_NOPE + MLA_ROPE) ** -0.5 * float(np.log2(np.e))
    for hd in range(MLA_HEADS):
        k_ref[hd] = jnp.concatenate([kn[:, hd * MLA_NOPE:(hd + 1) * MLA_NOPE], k_rope], axis=-1).astype(BF16)
        vt = _dot_nt(wvt_ref[hd], kvn)
        vt_ref[hd] = jnp.concatenate([vt.astype(BF16), ones], axis=0)
        a = qa[:, hd * 256:(hd + 1) * 256]
        q_rope = a[:, MLA_NOPE:] * cos + qb[:, hd * LANES:(hd + 1) * LANES] * sin
        q_ref[hd] = (jnp.concatenate([a[:, 0:MLA_NOPE], q_rope], axis=-1) * scale).astype(BF16)


def _odd_in(xc, x, g, mods, w_in, cos, sin, kv_norm, q_norm, wk, wvt, wq1, wq2):
    b, n, d = x.shape
    tm = ROW_TILE
    assert xc.shape[1] == tm
    t = n + tm
    seg = lambda i: (i >= 1).astype(jnp.int32)
    q_blk = lambda bb, i: (bb, 0, jnp.maximum(i - 1, 0), 0)
    return pl.pallas_call(
        _odd_in_body,
        grid=(b, t // tm),
        in_specs=[
            pl.BlockSpec((None, tm, d), lambda bb, i: (bb, 0, 0)),
            pl.BlockSpec((None, tm, d), lambda bb, i: (bb, jnp.maximum(i - 1, 0), 0)),
            _const_spec((1, d)), _mod_spec(0, seg), _mod_spec(1, seg), _const_spec(w_in.shape),
            pl.BlockSpec((tm, LANES), lambda bb, i: (i, 0)),
            pl.BlockSpec((tm, LANES), lambda bb, i: (i, 0)),
            _const_spec((1, MLA_KV_RANK)), _const_spec((1, MLA_Q_RANK)),
            _const_spec(wk.shape), _const_spec(wvt.shape), _const_spec(wq1.shape), _const_spec(wq2.shape),
        ],
        out_specs=[
            pl.BlockSpec((None, tm, OD_MAIN), lambda bb, i: (bb, i, 0)),
            pl.BlockSpec((None, MLA_HEADS, tm, 256), lambda bb, i: (bb, 0, i, 0)),
            pl.BlockSpec((None, MLA_HEADS, V_ROWS, tm), lambda bb, i: (bb, 0, 0, i)),
            pl.BlockSpec((None, MLA_HEADS, tm, 256), q_blk),
        ],
        out_shape=[
            jax.ShapeDtypeStruct((b, t, OD_MAIN), BF16),
            jax.ShapeDtypeStruct((b, MLA_HEADS, t, 256), BF16),
            jax.ShapeDtypeStruct((b, MLA_HEADS, V_ROWS, t), BF16),
            jax.ShapeDtypeStruct((b, MLA_HEADS, n, 256), BF16),
        ],
        compiler_params=_params(("parallel", "arbitrary")),
        name="odd_in",
    )(xc, x, g, mods, mods, w_in, cos, sin, kv_norm, q_norm, wk, wvt, wq1, wq2)


def _attn_body(q_ref, k_ref, vt_ref, o_ref):
    k = k_ref[...]
    vt = vt_ref[...]
    subs = [slice(i * ATTN_SUB, (i + 1) * ATTN_SUB) for i in range(q_ref.shape[0] // ATTN_SUB)]
    st = [_dot_nt(k, q_ref[sl, :]) for sl in subs]
    pt = [_bf(jnp.exp2(s - jnp.max(s, axis=0, keepdims=True))) for s in st]
    ot = [_dot(vt, p) for p in pt]
    for sl, o in zip(subs, ot):
        o_ref[sl, :] = (o[0:MLA_V] / o[MLA_V:MLA_V + 1]).T


def _attention(q, k, vt):
    b, h, n, dq = q.shape
    t = k.shape[2]
    tq = ATTN_TQ
    return pl.pallas_call(
        _attn_body,
        grid=(b, h, n // tq),
        in_specs=[
            pl.BlockSpec((None, None, tq, dq), lambda bb, hh, i: (bb, hh, i, 0)),
            pl.BlockSpec((None, None, t, dq), lambda bb, hh, i: (bb, hh, 0, 0)),
            pl.BlockSpec((None, None, V_ROWS, t), lambda bb, hh, i: (bb, hh, 0, 0)),
        ],
        out_specs=pl.BlockSpec((None, tq, MLA_V), lambda bb, hh, i: (bb, i, hh)),
        out_shape=jax.ShapeDtypeStruct((b, n, h * MLA_V), F32),
        compiler_params=_params(("parallel", "parallel", "parallel")),
        name="mla_attention",
    )(q, k, vt)


def _odd_post_body(x_ref, o_ref_in, hg_ref, att_ref, hgn_ref,
                   outw_ref, w1_ref, w2_ref, ng_ref, ga_ref, shm_ref, scm_ref, gm_ref, o_ref):
    y = o_ref_in[0, 0] + o_ref_in[1, 0]
    g = hg_ref[0].astype(F32)
    y = y * lax.rsqrt(_group_mean(y * y, HG_DIM) + EPS) * hgn_ref[...] * (g * _sigmoid(g))
    mix = jnp.concatenate([y, att_ref[...]], axis=-1)
    o_ref[...] = _tail(x_ref[...], mix, outw_ref, w1_ref, w2_ref, ng_ref, ga_ref, shm_ref, scm_ref, gm_ref)


def _odd_post(x, ohg, proj, att, hgn, outw, w1, w2, ng, mods, n_ctx, layer):
    b, n, d = x.shape
    tm = LAT_TILE
    seg = lambda i: 1
    w = HG_WIDTH
    el = lambda *shape: tuple(pl.Element(s) for s in shape)
    row0 = lambda i: pl.multiple_of(n_ctx + i * tm, int(np.gcd(n_ctx, tm)))
    return pl.pallas_call(
        _odd_post_body,
        grid=(b, n // tm),
        in_specs=[
            pl.BlockSpec((None, tm, d), lambda bb, i: (bb, i, 0)),
            pl.BlockSpec(el(2, 1, tm, w), lambda bb, i: (0, bb, row0(i), 0)),
            pl.BlockSpec(el(1, tm, w), lambda bb, i: (bb, row0(i), 4 * w)),
            pl.BlockSpec((None, tm, w), lambda bb, i: (bb, i, 0)),
            _const_spec((1, w)),
        ] + _tail_specs(seg, layer),
        out_specs=pl.BlockSpec((None, tm, d), lambda bb, i: (bb, i, 0)),
        out_shape=jax.ShapeDtypeStruct((b, n, d), F32),
        compiler_params=_params(("parallel", "parallel")),
        name="odd_post",
    )(x, ohg, proj, att, hgn, outw, w1, w2, ng, mods, mods, mods, mods)


def _rope_tables(n, n_ctx):
    quarter = MLA_ROPE // 4
    inv = ROPE_BASE ** (-jnp.arange(quarter, dtype=F32) / quarter)
    rows = n // GRID_W
    ang_r = jnp.arange(rows, dtype=F32)[:, None] * inv[None, :]
    ang_c = jnp.arange(GRID_W, dtype=F32)[:, None] * inv[None, :]
    by_row = lambda a: jnp.repeat(a, GRID_W, axis=0)
    by_col = lambda a: jnp.tile(a, (rows, 1))
    cos_r, sin_r, cos_c, sin_c = by_row(jnp.cos(ang_r)), by_row(jnp.sin(ang_r)), by_col(jnp.cos(ang_c)), by_col(jnp.sin(ang_c))
    cos = jnp.concatenate([cos_r, cos_r, cos_c, cos_c], axis=-1)
    sin = jnp.concatenate([-sin_r, sin_r, -sin_c, sin_c], axis=-1)
    cos = jnp.concatenate([jnp.ones((n_ctx, MLA_ROPE), F32), cos], axis=0)
    sin = jnp.concatenate([jnp.zeros((n_ctx, MLA_ROPE), F32), sin], axis=0)
    pad = lambda a: jnp.pad(a, ((0, 0), (0, LANES - MLA_ROPE)))
    return pad(cos), pad(sin)


_ROPE_SWAP = np.concatenate([np.arange(16, 32), np.arange(0, 16), np.arange(48, 64), np.arange(32, 48)])


def kernel(x, c, ctx, c_ctx, ada_w, ada_b, norm_g, out_w, mlp_w1, mlp_w2, hg_lb, ev_w_in, ml_conv, ml_gate_b, ml_norm, rw_mu, rw_w0, rw_w2, rw_a0, rw_a2, rw_g2, rw_kk, rw_ka, rw_rk, rw_ln_w, rw_ln_b, od_w_in, hg_norm, mla_q_norm, mla_w_qb, mla_kv_norm, mla_w_kvb):
    b, n, d = x.shape
    n_ctx = ctx.shape[1]
    assert d == D_MODEL and b + 1 <= SUBLANES
    assert n_ctx == ROW_TILE and n % LAT_TILE == 0 and n % ATTN_TQ == 0
    assert all(n_ctx % blk == 0 and n % blk == 0 for blk in (ML_BLOCK, RW_BLOCK, HG_BLOCK))
    row1 = lambda a: a.reshape(1, -1)

    c8 = jnp.concatenate([c, c_ctx[None], jnp.zeros((SUBLANES - b - 1, d), F32)], axis=0)
    mod = _ada_mod(c8, ada_w, ada_b).reshape(-1, SUBLANES, N_MOD, d)

    def layer_mods(l):
        lat = mod[l, :b]
        cm = jnp.broadcast_to(mod[l, b][None], lat.shape)
        return jnp.stack([cm, lat], axis=1)[:, :, :, None, :]

    outw_bf, w1_bf, w2_bf = _bf(out_w), _bf(mlp_w1), _bf(mlp_w2)

    l, e = 0, 0
    mods = layer_mods(l)
    wi = ev_w_in[e]
    w_pad = jnp.concatenate([wi[:, 2064:3984], wi[:, 2048:2064], jnp.zeros((d, LANES - 16), F32), wi[:, 0:2048]], axis=1)
    gate_b = jnp.pad(ml_gate_b[e].reshape(1, 16), ((0, 0), (0, LANES - 16)))
    zeros64 = jnp.zeros((64, RW_WIDTH), F32)
    w2cat = jnp.stack([jnp.concatenate([rw_w2[e, 0], zeros64], 0), jnp.concatenate([zeros64, rw_w2[e, 1]], 0)])
    a2cat = jnp.stack([jnp.concatenate([rw_a2[e, 0], zeros64], 0), jnp.concatenate([zeros64, rw_a2[e, 1]], 0)])
    proj, qk, gates, gates_t, rws, rw_lw, rw_kb, bg = _even_in(
        ctx, x, row1(norm_g[l, 0]), mods, _bf(w_pad),
        ml_conv[e], gate_b, row1(rw_mu[e]), rw_w0[e], _bf(w2cat), rw_a0[e], _bf(a2cat), _bf(rw_g2[e]),
        row1(rw_kk[e]), row1(rw_ka[e]), row1(rw_rk[e]))

    hm, yrw = _even_mix(qk, proj, gates, gates_t, rws, rw_lw, rw_kb, n_ctx)
    post_args = (hm, proj, yrw, bg, row1(ml_norm[e]), row1(rw_ln_w[e]), row1(rw_ln_b[e]),
                 outw_bf, w1_bf, w2_bf, norm_g, mods, l)
    xc1 = _even_post(ctx, 0, 0, ROW_TILE, *post_args)
    x1 = _even_post(x, n_ctx, 1, LAT_TILE, *post_args)

    l, o = 1, 0
    mods = layer_mods(l)
    lb_all = jax.nn.softmax(hg_lb.astype(F32), axis=0)
    lb_all = jnp.cumsum(lb_all, axis=0) - lb_all[0]
    wi = od_w_in[o]
    kr = wi[:, 2944:3008]
    z = lambda k: jnp.zeros((d, k), F32)
    w_pad = jnp.concatenate([wi[:, 0:2944], kr, z(64), kr[:, _ROPE_SWAP], z(64)], axis=1)
    cos, sin = _rope_tables(n, n_ctx)
    wkv = mla_w_kvb[o].reshape(MLA_KV_RANK, MLA_HEADS, MLA_NOPE + MLA_V)
    wk = wkv[:, :, :MLA_NOPE].reshape(MLA_KV_RANK, MLA_HEADS * MLA_NOPE)
    wvt = jnp.transpose(wkv[:, :, MLA_NOPE:], (1, 2, 0))
    wq = mla_w_qb[o].reshape(MLA_Q_RANK, MLA_HEADS, MLA_NOPE + MLA_ROPE)
    zq = lambda k: jnp.zeros((MLA_Q_RANK, MLA_HEADS, k), F32)
    wq1 = jnp.concatenate([wq, zq(64)], axis=-1).reshape(MLA_Q_RANK, -1)
    wq2 = jnp.concatenate([wq[:, :, MLA_NOPE:][:, :, _ROPE_SWAP], zq(64)], axis=-1).reshape(MLA_Q_RANK, -1)
    proj, k_all, v_all, q_all = _odd_in(
        xc1, x1, row1(norm_g[l, 0]), mods, _bf(w_pad), cos, sin, row1(mla_kv_norm[o]), row1(mla_q_norm[o]),
        _bf(wk), _bf(wvt), _bf(wq1), _bf(wq2))

    ohg = _hgrn(proj, row1(lb_all[l]), n_ctx)
    att = _attention(q_all, k_all, v_all)

    return _odd_post(x1, ohg, proj, att, row1(hg_norm[o]), outw_bf, w1_bf, w2_bf, norm_g, mods, n_ctx, l)
```

```python
import functools

import numpy as np
import jax
import jax.numpy as jnp
from jax import lax
from jax.experimental import pallas as pl
from jax.experimental.pallas import tpu as pltpu

F32 = jnp.float32
BF16 = jnp.bfloat16

D_MODEL = 1024
GRID_W = 64
D_FF = 4 * D_MODEL
N_MOD = 6
EPS = 1e-6
ML_WIDTH = 512
ML_HEADS = 4
ML_DIM = 128
RW_WIDTH = 512
RW_HEAD_DIM = 64
RW_GN_EPS = 64e-5
RW_FEAT = 1920
HG_WIDTH = 512
HG_HEADS = 4
HG_DIM = 128
MLA_HEADS = 4
MLA_NOPE = 128
MLA_ROPE = 64
MLA_V = 128
MLA_Q_RANK = 256
MLA_KV_RANK = 128
ROPE_BASE = 10000.0

LANES = 128
SUBLANES = 8
HALO = 16
VMEM_LIMIT_BYTES = 56 * 2**20

ROW_TILE = 256
LAT_TILE = 512
FF_CHUNK = 1024
ML_CHUNK = 128
ML_BLOCK = 256
RW_CHUNK = 64
RW_BLOCK = 256
HG_BLOCK = 256
HG_CHUNK = 32
ATTN_TQ = 512
ATTN_SUB = 256


def _dot(a, b, prec=None):
    return jnp.dot(a, b, preferred_element_type=F32, precision=prec)


def _dot_nt(a, b, prec=None):
    return lax.dot_general(a, b, (((1,), (1,)), ((), ())), preferred_element_type=F32, precision=prec)


def _dot_tn(a, b, prec=None):
    return lax.dot_general(a, b, (((0,), (0,)), ((), ())), preferred_element_type=F32, precision=prec)


def _bf(x):
    return x.astype(BF16)


def _mm(a, b):
    return _dot(_bf(a), _bf(b))


def _mm_nt(a, b):
    return _dot_nt(_bf(a), _bf(b))


def _mm_tn(a, b):
    return _dot_tn(_bf(a), _bf(b))


def _split(x):
    hi = _bf(x)
    return hi, _bf(x - hi.astype(F32))


def _sel_dot(sel, x):
    hi, lo = _split(x)
    return _dot(sel, hi) + _dot(sel, lo)


def _dot_sel(x, sel):
    hi, lo = _split(x)
    return _dot(hi, sel) + _dot(lo, sel)


def _dot_sel_nt(x, sel):
    hi, lo = _split(x)
    return _dot_nt(hi, sel) + _dot_nt(lo, sel)


def _dot3(a, b):
    ah, al = _split(a)
    bh, bl = _split(b)
    return _dot(ah, bh) + _dot(ah, bl) + _dot(al, bh)


def _group_mean(x, group):
    low = lax.broadcasted_iota(jnp.int32, (1, LANES), 1) < group
    outs = []
    for c in range(x.shape[1] // LANES):
        blk = x[:, c * LANES:(c + 1) * LANES]
        total = jnp.sum(blk, axis=-1, keepdims=True)
        if group == LANES:
            outs.append(jnp.broadcast_to(total, blk.shape))
        else:
            first = jnp.sum(jnp.where(low, blk, 0.0), axis=-1, keepdims=True)
            outs.append(jnp.where(low, first, total - first))
    return jnp.concatenate(outs, axis=-1) * (1.0 / group)


def _sigmoid(x):
    return 1.0 / (1.0 + jnp.exp(-x))


def _log_sigmoid(x):
    return jnp.minimum(x, 0.0) - jnp.log(1.0 + jnp.exp(-jnp.abs(x)))


def _softplus(x):
    return jnp.maximum(x, 0.0) + jnp.log(1.0 + jnp.exp(-jnp.abs(x)))


def _rms(x, g):
    return x * lax.rsqrt(jnp.mean(x * x, axis=-1, keepdims=True) + EPS) * g


def _order_mask(n, sgn, strict):
    row = lax.broadcasted_iota(jnp.int32, (n, n), 0)
    col = lax.broadcasted_iota(jnp.int32, (n, n), 1)
    diff = (row - col) * sgn
    return diff > 0 if strict else diff >= 0


def _chunk_index(d, j, nctx, nc):
    bwd = jnp.where(j < nctx, nctx - 1 - j, nc - 1 - (j - nctx))
    return jnp.where(d == 0, j, bwd)


def _params(sem):
    return pltpu.CompilerParams(dimension_semantics=sem, vmem_limit_bytes=VMEM_LIMIT_BYTES)


def _const_spec(shape):
    nd = len(shape)
    return pl.BlockSpec(shape, lambda *_: (0,) * nd, pipeline_mode=pl.Buffered(1))


def _mod_spec(k, seg_of):
    return pl.BlockSpec((None, None, None, 1, D_MODEL), lambda b, i: (b, seg_of(i), k, 0, 0))


def _ada_body(c_ref, w_ref, b_ref, o_ref):
    c = c_ref[...]
    o_ref[...] = _dot3(c * _sigmoid(c), w_ref[...]) + b_ref[...]


def _ada_mod(c8, ada_w, ada_b):
    depth, d, n = ada_w.shape
    tn = 1024
    return pl.pallas_call(
        _ada_body,
        grid=(depth, n // tn),
        in_specs=[
            _const_spec((SUBLANES, d)),
            pl.BlockSpec((None, d, tn), lambda l, j: (l, 0, j)),
            pl.BlockSpec((None, 1, tn), lambda l, j: (l, 0, j)),
        ],
        out_specs=pl.BlockSpec((None, SUBLANES, tn), lambda l, j: (l, 0, j)),
        out_shape=jax.ShapeDtypeStruct((depth, SUBLANES, n), F32),
        compiler_params=_params(("parallel", "parallel")),
        name="ada_mod",
    )(c8, ada_w, ada_b.reshape(depth, 1, n))


EV_SHIFTED = 3072


def _even_in_body(nt, c_ref, x_ref, xp_ref, xn_ref, g_ref, sh_ref, sc_ref, w_ref,
                  conv_ref, gb_ref, mu_ref, w0_ref, w2_ref, a0_ref, a2_ref, g2_ref,
                  kk_ref, ka_ref, rk_ref,
                  ovo_ref, oqk_ref, og_ref, ogt_ref, orws_ref, olw_ref, okb_ref, obg_ref):
    i = pl.program_id(0)
    has_prev = (i > 1).astype(F32)
    has_next = jnp.logical_and(i != 0, i != nt - 1).astype(F32)
    n_batch, tm = x_ref.shape[0], x_ref.shape[1]
    ext = tm + 2 * HALO
    row = lax.broadcasted_iota(jnp.int32, (tm, 1), 0)
    keep_dn = jnp.where(row == 0, has_prev, 1.0)
    keep_up = jnp.where(row == tm - 1, has_next, 1.0)

    def normed(xx, bi):
        return _bf(_rms(xx, g_ref[...]) * (1.0 + sc_ref[bi]) + sh_ref[bi])

    h = [normed(jnp.where(i == 0, c_ref[bi], x_ref[bi]), bi) for bi in range(n_batch)]
    vo = _dot(jnp.concatenate(h, axis=0), w_ref[:, EV_SHIFTED:])
    hcat = jnp.concatenate([part for bi in range(n_batch)
                            for part in (normed(xp_ref[bi], bi), h[bi], normed(xn_ref[bi], bi))], axis=0)
    proj_all = _dot(hcat, w_ref[:, 0:EV_SHIFTED])
    lane = lax.broadcasted_iota(jnp.int32, (1, LANES), 1)
    is_f = jnp.logical_and(jnp.bitwise_and(jnp.right_shift(lane, 2), 1) == 1, lane < 16)
    w = RW_WIDTH
    for bi in range(n_batch):
        ovo_ref[bi] = vo[bi * tm:(bi + 1) * tm].astype(ovo_ref.dtype)
        proj = proj_all[bi * ext:(bi + 1) * ext]

        def neighbours(cols):
            blk = proj[:, cols]
            dn = pltpu.roll(blk, 1, 0)[HALO:HALO + tm] * keep_dn
            up = pltpu.roll(blk, ext - 1, 0)[HALO:HALO + tm] * keep_up
            return blk[HALO:HALO + tm], dn, up

        qk, dn, up = neighbours(slice(2048, 3072))
        oqk_ref[bi] = (dn * conv_ref[0:1, :] + qk * conv_ref[1:2, :] + up * conv_ref[2:3, :]).astype(oqk_ref.dtype)

        gv = proj[HALO:HALO + tm, RW_FEAT:RW_FEAT + LANES] + gb_ref[...]
        gates = jnp.where(is_f, _log_sigmoid(gv), gv)
        og_ref[bi] = gates
        ogt_ref[bi] = gates.T[0:4 * ML_HEADS, :]

        cur, dn, up = neighbours(slice(0, RW_FEAT))
        rf = cur + (0.5 * (dn + up) - cur) * mu_ref[...]
        r, k, v = rf[:, 0:w], rf[:, w:2 * w], rf[:, 2 * w:3 * w]
        wl = rf[:, 3 * w:3 * w + LANES]
        al = rf[:, 3 * w + LANES:3 * w + 2 * LANES]
        gl = rf[:, 3 * w + 2 * LANES:3 * w + 3 * LANES]
        kk = k * kk_ref[...]
        kk = kk * lax.rsqrt(_group_mean(kk * kk, RW_HEAD_DIM) * RW_HEAD_DIM + 1e-12)
        twl = jnp.tanh(wl)
        kd_sum = jnp.zeros_like(k)
        for d in range(2):
            w_log = -_softplus(-(w0_ref[d:d + 1, :] + _dot_sel(twl, w2_ref[d]))) - 0.5
            a = _sigmoid(a0_ref[d:d + 1, :] + _dot_sel(al, a2_ref[d]))
            kd = k * (1.0 + (a - 1.0) * ka_ref[...])
            kd_sum = kd_sum + kd
            olw_ref[d, bi] = -jnp.exp(w_log)
            okb_ref[d, bi, :, 0:w] = kd.astype(okb_ref.dtype)
            okb_ref[d, bi, :, w:2 * w] = (kk * a).astype(okb_ref.dtype)
        orws_ref[bi, :, 0:w] = r.astype(orws_ref.dtype)
        orws_ref[bi, :, w:2 * w] = v.astype(orws_ref.dtype)
        orws_ref[bi, :, 2 * w:3 * w] = (-kk).astype(orws_ref.dtype)
        bonus = _group_mean(r * kd_sum * rk_ref[...], RW_HEAD_DIM) * RW_HEAD_DIM * v
        obg_ref[bi, :, 0:w] = bonus.astype(obg_ref.dtype)
        obg_ref[bi, :, w:2 * w] = _dot(_bf(_sigmoid(gl)), g2_ref[...]).astype(obg_ref.dtype)


def _even_in(ctx, x, g, mods, w_in, conv, gate_b, mu, w0, w2cat, a0, a2cat, g2, kk, ka, rk):
    b, n, d = x.shape
    tm = ROW_TILE
    assert ctx.shape[1] == tm
    t = n + tm
    nt = t // tm
    rh = tm // HALO
    last_h = n // HALO - 1
    seg = lambda i: (i >= 1).astype(jnp.int32)
    cur = lambda c: (lambda i: (0, i, c))
    lat = lambda i: jnp.maximum(i - 1, 0)
    mod = lambda k: pl.BlockSpec((b, None, None, 1, d), lambda i: (0, seg(i), k, 0, 0))
    w = RW_WIDTH
    return pl.pallas_call(
        functools.partial(_even_in_body, nt),
        grid=(nt,),
        in_specs=[
            pl.BlockSpec((b, tm, d), lambda i: (0, 0, 0)),
            pl.BlockSpec((b, tm, d), lambda i: (0, lat(i), 0)),
            pl.BlockSpec((b, HALO, d), lambda i: (0, jnp.maximum(lat(i) * rh - 1, 0), 0)),
            pl.BlockSpec((b, HALO, d), lambda i: (0, jnp.minimum((lat(i) + 1) * rh, last_h), 0)),
            _const_spec((1, d)), mod(0), mod(1), _const_spec(w_in.shape),
            _const_spec(conv.shape), _const_spec(gate_b.shape), _const_spec(mu.shape),
            _const_spec(w0.shape), _const_spec(w2cat.shape), _const_spec(a0.shape),
            _const_spec(a2cat.shape), _const_spec(g2.shape), _const_spec(kk.shape),
            _const_spec(ka.shape), _const_spec(rk.shape),
        ],
        out_specs=[
            pl.BlockSpec((b, tm, 1024), cur(0)),
            pl.BlockSpec((b, tm, 1024), cur(0)),
            pl.BlockSpec((b, tm, LANES), cur(0)),
            pl.BlockSpec((b, 4 * ML_HEADS, tm), lambda i: (0, 0, i)),
            pl.BlockSpec((b, tm, 3 * w), cur(0)),
            pl.BlockSpec((2, b, tm, w), lambda i: (0, 0, i, 0)),
            pl.BlockSpec((2, b, tm, 2 * w), lambda i: (0, 0, i, 0)),
            pl.BlockSpec((b, tm, 2 * w), cur(0)),
        ],
        out_shape=[
            jax.ShapeDtypeStruct((b, t, 1024), BF16),
            jax.ShapeDtypeStruct((b, t, 1024), BF16),
            jax.ShapeDtypeStruct((b, t, LANES), F32),
            jax.ShapeDtypeStruct((b, 4 * ML_HEADS, t), F32),
            jax.ShapeDtypeStruct((b, t, 3 * w), BF16),
            jax.ShapeDtypeStruct((2, b, t, w), F32),
            jax.ShapeDtypeStruct((2, b, t, 2 * w), BF16),
            jax.ShapeDtypeStruct((b, t, 2 * w), BF16),
        ],
        compiler_params=_params(("parallel",)),
        name="even_in",
    )(ctx, x, x, x, g, mods, mods, w_in, conv, gate_b, mu, w0, w2cat, a0, a2cat, g2, kk, ka, rk)


def _mlstm_body(q_ref, k_ref, v_ref, gc_ref, gr_ref, o_ref, c_scr, n_scr, m_scr):
    d = pl.program_id(0)
    j = pl.program_id(1)
    n_batch = q_ref.shape[0]
    n_rows = ML_CHUNK
    n_sub = q_ref.shape[1] // n_rows

    @pl.when(j == 0)
    def _():
        c_scr[...] = jnp.zeros_like(c_scr)
        n_scr[...] = jnp.zeros_like(n_scr)
        m_scr[...] = jnp.zeros_like(m_scr)

    mask = _order_mask(n_rows, 1 - 2 * d, strict=False)
    mf = mask.astype(BF16)
    row_id = lax.broadcasted_iota(jnp.int32, (n_rows, 1), 0)
    is_last = row_id == (n_rows - 1) * (1 - d)
    pick_row = lax.broadcasted_iota(jnp.int32, (LANES, LANES), 0)
    pick = [(pick_row == 2 * ML_HEADS * d + c).astype(BF16) for c in range(2 * ML_HEADS)]
    ones_cols = jnp.ones((n_rows, LANES), BF16)

    rows = [pl.ds(pl.multiple_of(jnp.where(d == 0, sc, n_sub - 1 - sc) * n_rows, n_rows), n_rows)
            for sc in range(n_sub)]
    heads = [slice(h * ML_DIM, (h + 1) * ML_DIM) for h in range(ML_HEADS)]
    units = [(sc, bi, h) for sc in range(n_sub) for bi in range(n_batch) for h in range(ML_HEADS)]
    idx = range(len(units))
    carried = [bi * ML_HEADS + h for _, bi, h in units]
    gc, b_col, gr, b_row = {}, {}, {}, {}
    for bi in range(n_batch):
        gr_all = gr_ref[bi]
        for sc in range(n_sub):
            gc[sc, bi] = gc_ref[bi, rows[sc], :]
            parts = [gr_all[:, c * n_rows:(c + 1) * n_rows] for c in (sc, n_sub - 1 - sc)]
            gr[sc, bi] = jnp.where(d == 0, parts[0], parts[1])
            b_col[sc, bi] = _sel_dot(mf, gc[sc, bi])
            b_row[sc, bi] = _dot_sel_nt(gr[sc, bi], mf)
    q_f = [q_ref[bi, rows[sc], heads[h]] for sc, bi, h in units]
    q = [_bf(x) for x in q_f]
    kf = [k_ref[bi, rows[sc], heads[h]].astype(F32) * (ML_DIM ** -0.5) for sc, bi, h in units]
    k = [_bf(x) for x in kf]
    vf = [v_ref[bi, rows[sc], heads[h]].astype(F32) for sc, bi, h in units]
    qk = [_dot_nt(q[u], k[u]) for u in idx]
    ic = [_dot_sel(gc[sc, bi], pick[h]) for sc, bi, h in units]
    bc = [_dot_sel(b_col[sc, bi], pick[ML_HEADS + h]) for sc, bi, h in units]
    b_tot = [jnp.sum(jnp.where(is_last, bc[u], 0.0), axis=0, keepdims=True) for u in idx]
    logw = [b_tot[u] - bc[u] + ic[u] for u in idx]
    logw_max = [jnp.max(logw[u], axis=0, keepdims=True) for u in idx]
    logd = [jnp.where(mask, bc[u] - b_row[sc, bi][ML_HEADS + h:ML_HEADS + h + 1, :] + gr[sc, bi][h:h + 1, :], -1e30)
            for u, (sc, bi, h) in enumerate(units)]
    logd_max = [jnp.max(x, axis=-1, keepdims=True) for x in logd]
    m_run = [m_scr[c][0:1, :] for c in range(n_batch * ML_HEADS)]
    wgt, decay, s, m_t, w_inter = [], [], [], [], []
    for u in idx:
        m_prev = m_run[carried[u]]
        m_new = jnp.maximum(b_tot[u] + m_prev, logw_max[u])
        wgt.append(jnp.exp(logw[u] - m_new))
        decay.append(jnp.exp(b_tot[u] + m_prev - m_new))
        m_inter = bc[u] + m_prev
        m_t.append(jnp.maximum(m_inter, logd_max[u]))
        s.append(qk[u] * jnp.exp(logd[u] - m_t[u]))
        w_inter.append(jnp.exp(m_inter - m_t[u]))
        m_run[carried[u]] = m_new
    kv = [_dot_tn(_bf(vf[u] * wgt[u]), k[u]) for u in idx]
    sv = [_dot(_bf(s[u]), jnp.concatenate([_bf(vf[u]), ones_cols], axis=1)) for u in idx]
    c_run = [c_scr[c] for c in range(n_batch * ML_HEADS)]
    n_run = [n_scr[c][0:1, :] for c in range(n_batch * ML_HEADS)]
    per_chunk = n_batch * ML_HEADS
    for sc in range(n_sub):
        ids = range(sc * per_chunk, (sc + 1) * per_chunk)
        qc = [_dot_nt(q[u], jnp.concatenate([_bf(c_run[carried[u]]),
                                             _bf(jnp.broadcast_to(n_run[carried[u]], (LANES, ML_DIM)))], axis=0))
              for u in ids]
        for u, x in zip(ids, qc):
            _, bi, h = units[u]
            num = sv[u][:, :ML_DIM] + w_inter[u] * x[:, :ML_DIM]
            den = sv[u][:, ML_DIM:] + w_inter[u] * x[:, ML_DIM:]
            o_ref[bi, rows[sc], heads[h]] = num / jnp.maximum(jnp.abs(den), jnp.exp(-m_t[u]))
        for u in ids:
            c = carried[u]
            c_run[c] = decay[u] * c_run[c] + kv[u]
            n_run[c] = decay[u] * n_run[c] + jnp.sum(wgt[u] * kf[u], axis=0, keepdims=True)
    for c in range(per_chunk):
        c_scr[c] = c_run[c]
        n_scr[c] = jnp.broadcast_to(n_run[c], (SUBLANES, ML_DIM))
        m_scr[c] = jnp.broadcast_to(m_run[c], (SUBLANES, LANES))


def _mlstm(qk, proj, gates, gates_t, n_ctx):
    b, t, _ = qk.shape
    n_rows = ML_BLOCK
    nc, nctx = t // n_rows, n_ctx // n_rows
    ci = lambda d, j: _chunk_index(d, j, nctx, nc)
    return pl.pallas_call(
        _mlstm_body,
        grid=(2, nc),
        in_specs=[
            pl.BlockSpec((b, n_rows, ML_WIDTH), lambda d, j: (0, ci(d, j), 0)),
            pl.BlockSpec((b, n_rows, ML_WIDTH), lambda d, j: (0, ci(d, j), 1)),
            pl.BlockSpec((b, n_rows, ML_WIDTH), lambda d, j: (0, ci(d, j), 0)),
            pl.BlockSpec((b, n_rows, LANES), lambda d, j: (0, ci(d, j), 0)),
            pl.BlockSpec((b, 2 * ML_HEADS, n_rows), lambda d, j: (0, d, ci(d, j))),
        ],
        out_specs=pl.BlockSpec((None, b, n_rows, ML_WIDTH), lambda d, j: (d, 0, ci(d, j), 0)),
        out_shape=jax.ShapeDtypeStruct((2, b, t, ML_WIDTH), F32),
        scratch_shapes=[
            pltpu.VMEM((b * ML_HEADS, ML_DIM, ML_DIM), F32),
            pltpu.VMEM((b * ML_HEADS, SUBLANES, ML_DIM), F32),
            pltpu.VMEM((b * ML_HEADS, SUBLANES, LANES), F32),
        ],
        compiler_params=_params(("parallel", "arbitrary")),
        name="mlstm",
    )(qk, qk, proj, gates, gates_t)


def _rwkv_body(r_ref, v_ref, a_ref, lw_ref, k_ref, b_ref, o_ref, h_scr):
    d = pl.program_id(0)
    j = pl.program_id(1)
    n_rows = RW_CHUNK
    two = 2 * n_rows
    shift = n_rows.bit_length() - 1

    @pl.when(j == 0)
    def _():
        h_scr[...] = jnp.zeros_like(h_scr)

    sgn = 1 - 2 * d
    cum = _order_mask(n_rows, sgn, strict=False).astype(BF16)
    row = lax.broadcasted_iota(jnp.int32, (two, two), 0)
    col = lax.broadcasted_iota(jnp.int32, (two, two), 1)
    same = jnp.right_shift(row, shift) == jnp.right_shift(col, shift)
    diff = (jnp.bitwise_and(row, n_rows - 1) - jnp.bitwise_and(col, n_rows - 1)) * sgn
    m_strict = jnp.logical_and(same, diff > 0)
    m_incl = jnp.logical_and(same, diff >= 0)
    eye = (row == col).astype(F32)
    head0 = lax.broadcasted_iota(jnp.int32, (1, LANES), 1) < RW_HEAD_DIM

    def stack(x):
        return jnp.concatenate([jnp.where(head0, x, 0.0), jnp.where(head0, 0.0, x)], axis=0)

    def fold(x):
        return x[:n_rows] + x[n_rows:]

    n_batch = r_ref.shape[0]
    n_sub = r_ref.shape[1] // n_rows
    lane_groups = [slice(p * LANES, (p + 1) * LANES) for p in range(RW_WIDTH // LANES)]
    rows = [pl.ds(pl.multiple_of(jnp.where(d == 0, sc, n_sub - 1 - sc) * n_rows, n_rows), n_rows)
            for sc in range(n_sub)]
    groups = [(sc, bi, p) for sc in range(n_sub) for bi in range(n_batch) for p in range(len(lane_groups))]
    x_s, r_s, v_s, bh_s, kh_s, bi_s, ki_s, e_tot = [], [], [], [], [], [], [], []
    for sc in range(n_sub):
        for bi in range(n_batch):
            lw = lw_ref[bi, rows[sc], :]
            g = _sel_dot(cum, lw)
            g_tot = jnp.sum(lw, axis=0, keepdims=True)
            inv = jnp.exp(-g)
            to_end = jnp.exp(g_tot - g)
            at = a_ref[bi, rows[sc], :].astype(F32) * jnp.exp(g - lw)
            rt = r_ref[bi, rows[sc], :].astype(F32) * jnp.exp(g)
            b_in = b_ref[bi, rows[sc], :].astype(F32)
            k_in = k_ref[bi, rows[sc], :].astype(F32)
            vv = v_ref[bi, rows[sc], :].astype(F32)
            for sl in lane_groups:
                x_s.append(stack(at[:, sl]))
                r_s.append(stack(rt[:, sl]))
                v_s.append(stack(vv[:, sl]))
                bh_s.append(stack((b_in * to_end)[:, sl]))
                kh_s.append(stack((k_in * to_end)[:, sl]))
                bi_s.append(stack((b_in * inv)[:, sl]))
                ki_s.append(stack((k_in * inv)[:, sl]))
                e_tot.append(jnp.exp(g_tot)[:, sl])
    big = [_mm_nt(jnp.concatenate([x_s[p], r_s[p]], axis=0), jnp.concatenate([bi_s[p], ki_s[p]], axis=0))
           for p in range(len(groups))]
    a_ab = [jnp.where(m_strict, m[:two, :two], 0.0) for m in big]
    a_rb = [jnp.where(m_incl, m[two:, :two], 0.0) for m in big]
    a_k = [jnp.concatenate([jnp.where(m_strict, m[:two, two:], 0.0), jnp.where(m_incl, m[two:, two:], 0.0)], axis=0)
           for m in big]
    av = [_mm(a_k[p], v_s[p]) for p in range(len(groups))]
    t_inv = [eye + m for m in a_ab]
    pw = [_mm(m, m) for m in a_ab]
    for level in range(1, shift):
        last = level == shift - 1
        res = [_mm(t if last else jnp.concatenate([t, m], axis=0), m) for t, m in zip(t_inv, pw)]
        t_inv = [t + r[:two] for t, r in zip(t_inv, res)]
        if not last:
            pw = [r[two:] for r in res]
    wu = [_mm(t_inv[p], jnp.concatenate([x_s[p], av[p][:two]], axis=1)) for p in range(len(groups))]
    rb = [_mm(a_rb[p], wu[p]) for p in range(len(groups))]
    upd = [_mm_tn(wu[p], bh_s[p]) for p in range(len(groups))]
    gt2 = [_mm_tn(v_s[p], kh_s[p]) for p in range(len(groups))]
    q_eff = [fold(r_s[p] + rb[p][:, :LANES]) for p in range(len(groups))]
    y_in = [fold(rb[p][:, LANES:] + av[p][two:]) for p in range(len(groups))]
    per_chunk = n_batch * len(lane_groups)
    h_t = [h_scr[u] for u in range(per_chunk)]
    for sc in range(n_sub):
        ids = range(sc * per_chunk, (sc + 1) * per_chunk)
        for u, p in enumerate(ids):
            _, bi, lg = groups[p]
            o_ref[bi, rows[sc], lane_groups[lg]] = _mm_nt(q_eff[p], h_t[u]) + y_in[p]
        h_t = [e_tot[p] * h_t[u] + _mm(h_t[u], upd[p][:LANES]) + upd[p][LANES:] + gt2[p] for u, p in enumerate(ids)]
    for u in range(per_chunk):
        h_scr[u] = h_t[u]


def _rwkv(rws, lw, kb, n_ctx):
    b, t, _ = rws.shape
    n_rows = RW_BLOCK
    nc, nctx = t // n_rows, n_ctx // n_rows
    ci = lambda d, j: _chunk_index(d, j, nctx, nc)
    w = RW_WIDTH
    shared = lambda c: pl.BlockSpec((b, n_rows, w), lambda d, j: (0, ci(d, j), c))
    per_dir = lambda c: pl.BlockSpec((None, b, n_rows, w), lambda d, j: (d, 0, ci(d, j), c))
    return pl.pallas_call(
        _rwkv_body,
        grid=(2, nc),
        in_specs=[shared(0), shared(1), shared(2), per_dir(0), per_dir(0), per_dir(1)],
        out_specs=pl.BlockSpec((None, b, n_rows, w), lambda d, j: (d, 0, ci(d, j), 0)),
        out_shape=jax.ShapeDtypeStruct((2, b, t, w), F32),
        scratch_shapes=[pltpu.VMEM((b * w // LANES, LANES, LANES), F32)],
        compiler_params=_params(("parallel", "arbitrary")),
        name="rwkv7",
    )(rws, rws, rws, lw, kb, kb)


def _tail(x, mix, outw_ref, w1_ref, w2_ref, ng_ref, ga_ref, shm_ref, scm_ref, gm_ref):
    x1 = x + ga_ref[...] * _rms(_dot(_bf(mix), outw_ref[...]), ng_ref[1:2, :])
    h = _bf(_rms(x1, ng_ref[2:3, :]) * (1.0 + scm_ref[...]) + shm_ref[...])
    chunks = [slice(c * FF_CHUNK, (c + 1) * FF_CHUNK) for c in range(D_FF // FF_CHUNK)]
    u = [jnp.maximum(_dot(h, w1_ref[:, cols]), 0.0) for cols in chunks]
    dd = _dot(_bf(u[0] * u[0]), w2_ref[chunks[0], :])
    for uc, cols in zip(u[1:], chunks[1:]):
        dd = dd + _dot(_bf(uc * uc), w2_ref[cols, :])
    return x1 + gm_ref[...] * _rms(dd, ng_ref[3:4, :])


def _even_post_body(x_ref, hm_ref, mo_ref, yrw_ref, bg_ref, mln_ref, lnw_ref, lnb_ref,
                    outw_ref, w1_ref, w2_ref, ng_ref, ga_ref, shm_ref, scm_ref, gm_ref, o_ref):
    w = RW_WIDTH
    ym = hm_ref[0, 0] + hm_ref[1, 0]
    ym = (ym * lax.rsqrt(_group_mean(ym * ym, ML_DIM) + EPS) * mln_ref[...]
          * _sigmoid(mo_ref[0].astype(F32)))
    y = yrw_ref[0, 0] + yrw_ref[1, 0]
    yc = y - _group_mean(y, RW_HEAD_DIM)
    y = (yc * lax.rsqrt(_group_mean(yc * yc, RW_HEAD_DIM) + RW_GN_EPS) * lnw_ref[...] + lnb_ref[...]
         + bg_ref[0, :, 0:w].astype(F32))
    mix = jnp.concatenate([ym, y * bg_ref[0, :, w:2 * w].astype(F32)], axis=-1)
    o_ref[...] = _tail(x_ref[...], mix, outw_ref, w1_ref, w2_ref, ng_ref, ga_ref, shm_ref, scm_ref, gm_ref)


def _tail_specs(seg, layer):
    d = D_MODEL
    pick = lambda *shape: pl.BlockSpec((None,) + shape, lambda *_: (layer, 0, 0), pipeline_mode=pl.Buffered(1))
    return [
        pick(d, d), pick(d, D_FF), pick(D_FF, d), pick(4, d),
        _mod_spec(2, seg), _mod_spec(3, seg), _mod_spec(4, seg), _mod_spec(5, seg),
    ]


def _even_post(x_seg, first_row, segment, tm, hm, proj, yrw, bg, mln, lnw, lnb, outw, w1, w2, ng, mods, layer):
    b, rows, d = x_seg.shape
    seg = lambda i: segment
    w = RW_WIDTH
    el = lambda *shape: tuple(pl.Element(s) for s in shape)
    row0 = lambda i: pl.multiple_of(first_row + i * tm, int(np.gcd(first_row, tm)))
    return pl.pallas_call(
        _even_post_body,
        grid=(b, rows // tm),
        in_specs=[
            pl.BlockSpec((None, tm, d), lambda bb, i: (bb, i, 0)),
            pl.BlockSpec(el(2, 1, tm, w), lambda bb, i: (0, bb, row0(i), 0)),
            pl.BlockSpec(el(1, tm, w), lambda bb, i: (bb, row0(i), w)),
            pl.BlockSpec(el(2, 1, tm, w), lambda bb, i: (0, bb, row0(i), 0)),
            pl.BlockSpec(el(1, tm, 2 * w), lambda bb, i: (bb, row0(i), 0)),
            _const_spec((1, w)), _const_spec((1, w)), _const_spec((1, w)),
        ] + _tail_specs(seg, layer),
        out_specs=pl.BlockSpec((None, tm, d), lambda bb, i: (bb, i, 0)),
        out_shape=jax.ShapeDtypeStruct((b, rows, d), F32),
        compiler_params=_params(("parallel", "parallel")),
        name="even_post",
    )(x_seg, hm, proj, yrw, bg, mln, lnw, lnb, outw, w1, w2, ng, mods, mods, mods, mods)


def _hgrn_body(q_ref, f_ref, i_ref, lb_ref, o_ref, s_scr):
    d = pl.program_id(0)
    j = pl.program_id(1)
    n_batch, n_rows = q_ref.shape[0], q_ref.shape[1]
    n_sub = n_rows // HG_CHUNK

    @pl.when(j == 0)
    def _():
        s_scr[...] = jnp.zeros_like(s_scr)

    mask = _order_mask(HG_CHUNK, 1 - 2 * d, strict=False)
    mf = mask.astype(BF16)
    lb = lb_ref[...]
    heads = range(HG_HEADS)
    sls = [slice(h * HG_DIM, (h + 1) * HG_DIM) for h in heads]
    units = [(bi, h) for bi in range(n_batch) for h in heads]
    n_units = len(units)
    s_t = [s_scr[u] for u in range(n_units)]
    rows = [pl.ds(pl.multiple_of(jnp.where(d == 0, sc, n_sub - 1 - sc) * HG_CHUNK, HG_CHUNK), HG_CHUNK)
            for sc in range(n_sub)]
    loaded = [[(q_ref[bi, r, :], f_ref[bi, r, :], i_ref[bi, r, :]) for bi in range(n_batch)] for r in rows]
    q_in, e_tot, q_mids, k_mids, k_outs, vvs = [], [], [], [], [], []

    def prepare(sc):
        for lst in (q_in, e_tot, q_mids, k_mids, k_outs, vvs):
            lst.append([])
        for bi in range(n_batch):
            qa = loaded[sc][bi][0].astype(F32)
            pre = loaded[sc][bi][1].astype(F32)
            vv = loaded[sc][bi][2]
            q = qa * _sigmoid(qa) * (HG_DIM ** -0.5)
            e_abs = jnp.exp(-jnp.abs(pre))
            big = 1.0 / (1.0 + e_abs)
            small = e_abs * big
            log_f = jnp.log(lb + (1.0 - lb) * jnp.where(pre >= 0.0, big, small))
            kf = (1.0 - lb) * jnp.where(pre >= 0.0, small, big)
            bc = _sel_dot(mf, log_f)
            tot = jnp.sum(log_f, axis=0, keepdims=True)
            half = 0.5 * tot
            e_half = jnp.exp(half)
            q_mid_f = q * jnp.exp(bc - half)
            k_mid_f = kf * jnp.exp(half - bc)
            q_mid = _bf(q_mid_f)
            k_mid = _bf(k_mid_f)
            k_out = _bf(k_mid_f * e_half)
            q_all = _bf(q_mid_f * e_half)
            e_all = e_half * e_half
            q_mids[sc] += [q_mid[:, sl] for sl in sls]
            k_mids[sc] += [k_mid[:, sl] for sl in sls]
            k_outs[sc] += [k_out[:, sl] for sl in sls]
            vvs[sc] += [vv[:, sl] for sl in sls]
            q_in[sc] += [q_all[:, sl] for sl in sls]
            e_tot[sc] += [e_all[:, sl] for sl in sls]

    prepare(0)
    for sc in range(n_sub):
        if sc + 1 < n_sub:
            prepare(sc + 1)
        att = [_dot_nt(q_mids[sc][u], k_mids[sc][u]) for u in range(n_units)]
        kv = [_dot_tn(vvs[sc][u], k_outs[sc][u]) for u in range(n_units)]
        intra = [_dot(_bf(jnp.where(mask, att[u], 0.0)), vvs[sc][u]) for u in range(n_units)]
        for u, (bi, h) in enumerate(units):
            o_ref[bi, rows[sc], sls[h]] = intra[u] + _dot_nt(q_in[sc][u], _bf(s_t[u]))
        s_t = [e_tot[sc][u] * s_t[u] + kv[u] for u in range(n_units)]
    for u in range(n_units):
        s_scr[u] = s_t[u]


def _hgrn(proj, lb, n_ctx):
    b, t, _ = proj.shape
    n_rows = HG_BLOCK
    nc, nctx = t // n_rows, n_ctx // n_rows
    ci = lambda d, j: _chunk_index(d, j, nctx, nc)
    w = HG_WIDTH
    return pl.pallas_call(
        _hgrn_body,
        grid=(2, nc),
        in_specs=[
            pl.BlockSpec((b, n_rows, w), lambda d, j: (0, ci(d, j), 0)),
            pl.BlockSpec((b, n_rows, w), lambda d, j: (0, ci(d, j), 1 + d)),
            pl.BlockSpec((b, n_rows, w), lambda d, j: (0, ci(d, j), 3)),
            _const_spec((1, w)),
        ],
        out_specs=pl.BlockSpec((None, b, n_rows, w), lambda d, j: (d, 0, ci(d, j), 0)),
        out_shape=jax.ShapeDtypeStruct((2, b, t, w), F32),
        scratch_shapes=[pltpu.VMEM((b * HG_HEADS, HG_DIM, HG_DIM), F32)],
        compiler_params=_params(("parallel", "arbitrary")),
        name="hgrn2",
    )(proj, proj, proj, lb)


V_ROWS = MLA_V + SUBLANES
OD_MAIN = 5 * HG_WIDTH


def _odd_in_body(c_ref, x_ref, g_ref, sh_ref, sc_ref, w_ref, cos_ref, sin_ref, kvn_ref, qn_ref,
                 wk_ref, wvt_ref, wq1_ref, wq2_ref, proj_ref, k_ref, vt_ref, q_ref):
    n_batch, tm = x_ref.shape[0], x_ref.shape[1]
    is_ctx = pl.program_id(0) == 0
    h = jnp.concatenate([_bf(_rms(jnp.where(is_ctx, c_ref[bi], x_ref[bi]), g_ref[...]) * (1.0 + sc_ref[bi])
                             + sh_ref[bi]) for bi in range(n_batch)], axis=0)
    mla = _dot(h, w_ref[:, OD_MAIN:])
    main = _dot(h, w_ref[:, 0:OD_MAIN])
    cq = mla[:, 0:MLA_Q_RANK]
    base = MLA_Q_RANK
    ckv = mla[:, base:base + MLA_KV_RANK]
    cos = jnp.concatenate([cos_ref[...]] * n_batch, axis=0)
    sin = jnp.concatenate([sin_ref[...]] * n_batch, axis=0)
    k_rope = mla[:, base + LANES:base + 2 * LANES] * cos + mla[:, base + 2 * LANES:base + 3 * LANES] * sin
    kvn = _bf(_rms(ckv, kvn_ref[...]))
    kn = _dot(kvn, wk_ref[...])
    ones = jnp.ones((SUBLANES, tm), BF16)
    cn = _bf(_rms(cq, qn_ref[...]))
    qa = _dot(cn, wq1_ref[...])
    qb = _dot(cn, wq2_ref[...])
    scale = (MLA_NOPE + MLA_ROPE) ** -0.5 * float(np.log2(np.e))
    for hd in range(MLA_HEADS):
        k_h = jnp.concatenate([kn[:, hd * MLA_NOPE:(hd + 1) * MLA_NOPE], k_rope], axis=-1).astype(BF16)
        vt = _dot_nt(wvt_ref[hd], kvn).astype(BF16)
        a = qa[:, hd * 256:(hd + 1) * 256]
        q_rope = a[:, MLA_NOPE:] * cos + qb[:, hd * LANES:(hd + 1) * LANES] * sin
        q_h = (jnp.concatenate([a[:, 0:MLA_NOPE], q_rope], axis=-1) * scale).astype(BF16)
        for bi in range(n_batch):
            rows = slice(bi * tm, (bi + 1) * tm)
            k_ref[bi, hd] = k_h[rows]
            vt_ref[bi, hd] = jnp.concatenate([vt[:, rows], ones], axis=0)
            q_ref[bi, hd] = q_h[rows]
    for bi in range(n_batch):
        proj_ref[bi] = main[bi * tm:(bi + 1) * tm].astype(proj_ref.dtype)


def _odd_in(xc, x, g, mods, w_in, cos, sin, kv_norm, q_norm, wk, wvt, wq1, wq2):
    b, n, d = x.shape
    tm = ROW_TILE
    assert xc.shape[1] == tm
    t = n + tm
    seg = lambda i: (i >= 1).astype(jnp.int32)
    q_blk = lambda i: (0, 0, jnp.maximum(i - 1, 0), 0)
    mod = lambda k: pl.BlockSpec((b, None, None, 1, d), lambda i: (0, seg(i), k, 0, 0))
    return pl.pallas_call(
        _odd_in_body,
        grid=(t // tm,),
        in_specs=[
            pl.BlockSpec((b, tm, d), lambda i: (0, 0, 0)),
            pl.BlockSpec((b, tm, d), lambda i: (0, jnp.maximum(i - 1, 0), 0)),
            _const_spec((1, d)), mod(0), mod(1), _const_spec(w_in.shape),
            pl.BlockSpec((tm, LANES), lambda i: (i, 0)),
            pl.BlockSpec((tm, LANES), lambda i: (i, 0)),
            _const_spec((1, MLA_KV_RANK)), _const_spec((1, MLA_Q_RANK)),
            _const_spec(wk.shape), _const_spec(wvt.shape), _const_spec(wq1.shape), _const_spec(wq2.shape),
        ],
        out_specs=[
            pl.BlockSpec((b, tm, OD_MAIN), lambda i: (0, i, 0)),
            pl.BlockSpec((b, MLA_HEADS, tm, 256), lambda i: (0, 0, i, 0)),
            pl.BlockSpec((b, MLA_HEADS, V_ROWS, tm), lambda i: (0, 0, 0, i)),
            pl.BlockSpec((b, MLA_HEADS, tm, 256), q_blk),
        ],
        out_shape=[
            jax.ShapeDtypeStruct((b, t, OD_MAIN), BF16),
            jax.ShapeDtypeStruct((b, MLA_HEADS, t, 256), BF16),
            jax.ShapeDtypeStruct((b, MLA_HEADS, V_ROWS, t), BF16),
            jax.ShapeDtypeStruct((b, MLA_HEADS, n, 256), BF16),
        ],
        compiler_params=_params(("arbitrary",)),
        name="odd_in",
    )(xc, x, g, mods, mods, w_in, cos, sin, kv_norm, q_norm, wk, wvt, wq1, wq2)


def _attn_body(q_ref, k_ref, vt_ref, o_ref):
    k = k_ref[...]
    vt = vt_ref[...]
    subs = [slice(i * ATTN_SUB, (i + 1) * ATTN_SUB) for i in range(q_ref.shape[0] // ATTN_SUB)]
    st = [_dot_nt(k, q_ref[sl, :]) for sl in subs]
    pt = [_bf(jnp.exp2(s - jnp.max(s, axis=0, keepdims=True))) for s in st]
    ot = [_dot(vt, p) for p in pt]
    for sl, o in zip(subs, ot):
        o_ref[sl, :] = (o[0:MLA_V] / o[MLA_V:MLA_V + 1]).T


def _attention(q, k, vt):
    b, h, n, dq = q.shape
    t = k.shape[2]
    tq = ATTN_TQ
    return pl.pallas_call(
        _attn_body,
        grid=(b, h, n // tq),
        in_specs=[
            pl.BlockSpec((None, None, tq, dq), lambda bb, hh, i: (bb, hh, i, 0)),
            pl.BlockSpec((None, None, t, dq), lambda bb, hh, i: (bb, hh, 0, 0)),
            pl.BlockSpec((None, None, V_ROWS, t), lambda bb, hh, i: (bb, hh, 0, 0)),
        ],
        out_specs=pl.BlockSpec((None, tq, MLA_V), lambda bb, hh, i: (bb, i, hh)),
        out_shape=jax.ShapeDtypeStruct((b, n, h * MLA_V), F32),
        compiler_params=_params(("parallel", "parallel", "parallel")),
        name="mla_attention",
    )(q, k, vt)


def _odd_post_body(x_ref, o_ref_in, hg_ref, att_ref, hgn_ref,
                   outw_ref, w1_ref, w2_ref, ng_ref, ga_ref, shm_ref, scm_ref, gm_ref, o_ref):
    y = o_ref_in[0, 0] + o_ref_in[1, 0]
    g = hg_ref[0].astype(F32)
    y = y * lax.rsqrt(_group_mean(y * y, HG_DIM) + EPS) * hgn_ref[...] * (g * _sigmoid(g))
    mix = jnp.concatenate([y, att_ref[...]], axis=-1)
    o_ref[...] = _tail(x_ref[...], mix, outw_ref, w1_ref, w2_ref, ng_ref, ga_ref, shm_ref, scm_ref, gm_ref)


def _odd_post(x, ohg, proj, att, hgn, outw, w1, w2, ng, mods, n_ctx, layer):
    b, n, d = x.shape
    tm = LAT_TILE
    seg = lambda i: 1
    w = HG_WIDTH
    el = lambda *shape: tuple(pl.Element(s) for s in shape)
    row0 = lambda i: pl.multiple_of(n_ctx + i * tm, int(np.gcd(n_ctx, tm)))
    return pl.pallas_call(
        _odd_post_body,
        grid=(b, n // tm),
        in_specs=[
            pl.BlockSpec((None, tm, d), lambda bb, i: (bb, i, 0)),
            pl.BlockSpec(el(2, 1, tm, w), lambda bb, i: (0, bb, row0(i), 0)),
            pl.BlockSpec(el(1, tm, w), lambda bb, i: (bb, row0(i), 4 * w)),
            pl.BlockSpec((None, tm, w), lambda bb, i: (bb, i, 0)),
            _const_spec((1, w)),
        ] + _tail_specs(seg, layer),
        out_specs=pl.BlockSpec((None, tm, d), lambda bb, i: (bb, i, 0)),
        out_shape=jax.ShapeDtypeStruct((b, n, d), F32),
        compiler_params=_params(("parallel", "parallel")),
        name="odd_post",
    )(x, ohg, proj, att, hgn, outw, w1, w2, ng, mods, mods, mods, mods)


def _rope_tables(n, n_ctx):
    quarter = MLA_ROPE // 4
    inv = ROPE_BASE ** (-jnp.arange(quarter, dtype=F32) / quarter)
    rows = n // GRID_W
    ang_r = jnp.arange(rows, dtype=F32)[:, None] * inv[None, :]
    ang_c = jnp.arange(GRID_W, dtype=F32)[:, None] * inv[None, :]
    by_row = lambda a: jnp.repeat(a, GRID_W, axis=0)
    by_col = lambda a: jnp.tile(a, (rows, 1))
    cos_r, sin_r, cos_c, sin_c = by_row(jnp.cos(ang_r)), by_row(jnp.sin(ang_r)), by_col(jnp.cos(ang_c)), by_col(jnp.sin(ang_c))
    cos = jnp.concatenate([cos_r, cos_r, cos_c, cos_c], axis=-1)
    sin = jnp.concatenate([-sin_r, sin_r, -sin_c, sin_c], axis=-1)
    cos = jnp.concatenate([jnp.ones((n_ctx, MLA_ROPE), F32), cos], axis=0)
    sin = jnp.concatenate([jnp.zeros((n_ctx, MLA_ROPE), F32), sin], axis=0)
    pad = lambda a: jnp.pad(a, ((0, 0), (0, LANES - MLA_ROPE)))
    return pad(cos), pad(sin)


_ROPE_SWAP = np.concatenate([np.arange(16, 32), np.arange(0, 16), np.arange(48, 64), np.arange(32, 48)])


def kernel(x, c, ctx, c_ctx, ada_w, ada_b, norm_g, out_w, mlp_w1, mlp_w2, hg_lb, ev_w_in, ml_conv, ml_gate_b, ml_norm, rw_mu, rw_w0, rw_w2, rw_a0, rw_a2, rw_g2, rw_kk, rw_ka, rw_rk, rw_ln_w, rw_ln_b, od_w_in, hg_norm, mla_q_norm, mla_w_qb, mla_kv_norm, mla_w_kvb):
    b, n, d = x.shape
    n_ctx = ctx.shape[1]
    assert d == D_MODEL and b + 1 <= SUBLANES
    assert n_ctx == ROW_TILE and n % LAT_TILE == 0 and n % ATTN_TQ == 0
    assert all(n_ctx % blk == 0 and n % blk == 0 for blk in (ML_BLOCK, RW_BLOCK, HG_BLOCK))
    row1 = lambda a: a.reshape(1, -1)

    c8 = jnp.concatenate([c, c_ctx[None], jnp.zeros((SUBLANES - b - 1, d), F32)], axis=0)
    mod = _ada_mod(c8, ada_w, ada_b).reshape(-1, SUBLANES, N_MOD, d)

    def layer_mods(l):
        lat = mod[l, :b]
        cm = jnp.broadcast_to(mod[l, b][None], lat.shape)
        return jnp.stack([cm, lat], axis=1)[:, :, :, None, :]

    outw_bf, w1_bf, w2_bf = _bf(out_w), _bf(mlp_w1), _bf(mlp_w2)

    l, e = 0, 0
    mods = layer_mods(l)
    wi = ev_w_in[e]
    w_pad = jnp.concatenate([wi[:, 2064:3984], wi[:, 2048:2064], jnp.zeros((d, LANES - 16), F32), wi[:, 0:2048]], axis=1)
    gate_b = jnp.pad(ml_gate_b[e].reshape(1, 16), ((0, 0), (0, LANES - 16)))
    zeros64 = jnp.zeros((64, RW_WIDTH), F32)
    w2cat = jnp.stack([jnp.concatenate([rw_w2[e, 0], zeros64], 0), jnp.concatenate([zeros64, rw_w2[e, 1]], 0)])
    a2cat = jnp.stack([jnp.concatenate([rw_a2[e, 0], zeros64], 0), jnp.concatenate([zeros64, rw_a2[e, 1]], 0)])
    proj, qk, gates, gates_t, rws, rw_lw, rw_kb, bg = _even_in(
        ctx, x, row1(norm_g[l, 0]), mods, _bf(w_pad),
        ml_conv[e], gate_b, row1(rw_mu[e]), rw_w0[e], _bf(w2cat), rw_a0[e], _bf(a2cat), _bf(rw_g2[e]),
        row1(rw_kk[e]), row1(rw_ka[e]), row1(rw_rk[e]))

    hm = _mlstm(qk, proj, gates, gates_t, n_ctx)
    yrw = _rwkv(rws, rw_lw, rw_kb, n_ctx)
    post_args = (hm, proj, yrw, bg, row1(ml_norm[e]), row1(rw_ln_w[e]), row1(rw_ln_b[e]),
                 outw_bf, w1_bf, w2_bf, norm_g, mods, l)
    xc1 = _even_post(ctx, 0, 0, ROW_TILE, *post_args)
    x1 = _even_post(x, n_ctx, 1, LAT_TILE, *post_args)

    l, o = 1, 0
    mods = layer_mods(l)
    lb_all = jax.nn.softmax(hg_lb.astype(F32), axis=0)
    lb_all = jnp.cumsum(lb_all, axis=0) - lb_all[0]
    wi = od_w_in[o]
    kr = wi[:, 2944:3008]
    z = lambda k: jnp.zeros((d, k), F32)
    w_pad = jnp.concatenate([wi[:, 0:2944], kr, z(64), kr[:, _ROPE_SWAP], z(64)], axis=1)
    cos, sin = _rope_tables(n, n_ctx)
    wkv = mla_w_kvb[o].reshape(MLA_KV_RANK, MLA_HEADS, MLA_NOPE + MLA_V)
    wk = wkv[:, :, :MLA_NOPE].reshape(MLA_KV_RANK, MLA_HEADS * MLA_NOPE)
    wvt = jnp.transpose(wkv[:, :, MLA_NOPE:], (1, 2, 0))
    wq = mla_w_qb[o].reshape(MLA_Q_RANK, MLA_HEADS, MLA_NOPE + MLA_ROPE)
    zq = lambda k: jnp.zeros((MLA_Q_RANK, MLA_HEADS, k), F32)
    wq1 = jnp.concatenate([wq, zq(64)], axis=-1).reshape(MLA_Q_RANK, -1)
    wq2 = jnp.concatenate([wq[:, :, MLA_NOPE:][:, :, _ROPE_SWAP], zq(64)], axis=-1).reshape(MLA_Q_RANK, -1)
    proj, k_all, v_all, q_all = _odd_in(
        xc1, x1, row1(norm_g[l, 0]), mods, _bf(w_pad), cos, sin, row1(mla_kv_norm[o]), row1(mla_q_norm[o]),
        _bf(wk), _bf(wvt), _bf(wq1), _bf(wq2))

    ohg = _hgrn(proj, row1(lb_all[l]), n_ctx)
    att = _attention(q_all, k_all, v_all)

    return _odd_post(x1, ohg, proj, att, row1(hg_norm[o]), outw_bf, w1_bf, w2_bf, norm_g, mods, n_ctx, l)
```

```python
import functools

import numpy as np
import jax
import jax.numpy as jnp
from jax import lax
from jax.experimental import pallas as pl
from jax.experimental.pallas import tpu as pltpu

F32 = jnp.float32
BF16 = jnp.bfloat16

D_MODEL = 1024
GRID_W = 64
D_FF = 4 * D_MODEL
N_MOD = 6
EPS = 1e-6
ML_WIDTH = 512
ML_HEADS = 4
ML_DIM = 128
RW_WIDTH = 512
RW_HEAD_DIM = 64
RW_GN_EPS = 64e-5
RW_FEAT = 1920
HG_WIDTH = 512
HG_HEADS = 4
HG_DIM = 128
MLA_HEADS = 4
MLA_NOPE = 128
MLA_ROPE = 64
MLA_V = 128
MLA_Q_RANK = 256
MLA_KV_RANK = 128
ROPE_BASE = 10000.0

LANES = 128
SUBLANES = 8
HALO = 16
VMEM_LIMIT_BYTES = 56 * 2**20

ROW_TILE = 256
LAT_TILE = 512
FF_CHUNK = 1024
ML_CHUNK = 128
ML_BLOCK = 256
RW_CHUNK = 64
RW_BLOCK = 256
HG_BLOCK = 256
HG_CHUNK = 32
ATTN_TQ = 512
ATTN_SUB = 256


def _dot(a, b, prec=None):
    return jnp.dot(a, b, preferred_element_type=F32, precision=prec)


def _dot_nt(a, b, prec=None):
    return lax.dot_general(a, b, (((1,), (1,)), ((), ())), preferred_element_type=F32, precision=prec)


def _dot_tn(a, b, prec=None):
    return lax.dot_general(a, b, (((0,), (0,)), ((), ())), preferred_element_type=F32, precision=prec)


def _bf(x):
    return x.astype(BF16)


def _mm(a, b):
    return _dot(_bf(a), _bf(b))


def _mm_nt(a, b):
    return _dot_nt(_bf(a), _bf(b))


def _mm_tn(a, b):
    return _dot_tn(_bf(a), _bf(b))


def _split(x):
    hi = _bf(x)
    return hi, _bf(x - hi.astype(F32))


def _sel_dot(sel, x):
    hi, lo = _split(x)
    return _dot(sel, hi) + _dot(sel, lo)


def _dot_sel(x, sel):
    hi, lo = _split(x)
    return _dot(hi, sel) + _dot(lo, sel)


def _dot_sel_nt(x, sel):
    hi, lo = _split(x)
    return _dot_nt(hi, sel) + _dot_nt(lo, sel)


def _dot3(a, b):
    ah, al = _split(a)
    bh, bl = _split(b)
    return _dot(ah, bh) + _dot(ah, bl) + _dot(al, bh)


def _group_mean(x, group):
    low = lax.broadcasted_iota(jnp.int32, (1, LANES), 1) < group
    outs = []
    for c in range(x.shape[1] // LANES):
        blk = x[:, c * LANES:(c + 1) * LANES]
        total = jnp.sum(blk, axis=-1, keepdims=True)
        if group == LANES:
            outs.append(jnp.broadcast_to(total, blk.shape))
        else:
            first = jnp.sum(jnp.where(low, blk, 0.0), axis=-1, keepdims=True)
            outs.append(jnp.where(low, first, total - first))
    return jnp.concatenate(outs, axis=-1) * (1.0 / group)


def _sigmoid(x):
    return 1.0 / (1.0 + jnp.exp(-x))


def _log_sigmoid(x):
    return jnp.minimum(x, 0.0) - jnp.log(1.0 + jnp.exp(-jnp.abs(x)))


def _softplus(x):
    return jnp.maximum(x, 0.0) + jnp.log(1.0 + jnp.exp(-jnp.abs(x)))


def _rms(x, g):
    return x * lax.rsqrt(jnp.mean(x * x, axis=-1, keepdims=True) + EPS) * g


def _order_mask(n, sgn, strict):
    row = lax.broadcasted_iota(jnp.int32, (n, n), 0)
    col = lax.broadcasted_iota(jnp.int32, (n, n), 1)
    diff = (row - col) * sgn
    return diff > 0 if strict else diff >= 0


def _chunk_index(d, j, nctx, nc):
    bwd = jnp.where(j < nctx, nctx - 1 - j, nc - 1 - (j - nctx))
    return jnp.where(d == 0, j, bwd)


def _params(sem):
    return pltpu.CompilerParams(dimension_semantics=sem, vmem_limit_bytes=VMEM_LIMIT_BYTES)


def _const_spec(shape):
    nd = len(shape)
    return pl.BlockSpec(shape, lambda *_: (0,) * nd, pipeline_mode=pl.Buffered(1))


def _mod_spec(k, seg_of):
    return pl.BlockSpec((None, None, None, 1, D_MODEL), lambda b, i: (b, seg_of(i), k, 0, 0))


def _ada_body(c_ref, w_ref, b_ref, o_ref):
    c = c_ref[...]
    o_ref[...] = _dot3(c * _sigmoid(c), w_ref[...]) + b_ref[...]


def _ada_mod(c8, ada_w, ada_b):
    depth, d, n = ada_w.shape
    tn = 1024
    return pl.pallas_call(
        _ada_body,
        grid=(depth, n // tn),
        in_specs=[
            _const_spec((SUBLANES, d)),
            pl.BlockSpec((None, d, tn), lambda l, j: (l, 0, j)),
            pl.BlockSpec((None, 1, tn), lambda l, j: (l, 0, j)),
        ],
        out_specs=pl.BlockSpec((None, SUBLANES, tn), lambda l, j: (l, 0, j)),
        out_shape=jax.ShapeDtypeStruct((depth, SUBLANES, n), F32),
        compiler_params=_params(("parallel", "parallel")),
        name="ada_mod",
    )(c8, ada_w, ada_b.reshape(depth, 1, n))


EV_SHIFTED = 3072


def _even_in_body(nt, c_ref, x_ref, xp_ref, xn_ref, g_ref, sh_ref, sc_ref, w_ref,
                  conv_ref, gb_ref, mu_ref, w0_ref, w2_ref, a0_ref, a2_ref, g2_ref,
                  kk_ref, ka_ref, rk_ref,
                  ovo_ref, oqk_ref, og_ref, ogt_ref, orws_ref, olw_ref, okb_ref, obg_ref):
    i = pl.program_id(0)
    has_prev = (i > 1).astype(F32)
    has_next = jnp.logical_and(i != 0, i != nt - 1).astype(F32)
    n_batch, tm = x_ref.shape[0], x_ref.shape[1]
    ext = tm + 2 * HALO

    def normed(xx, bi, keep=None):
        y = _rms(xx, g_ref[...]) * (1.0 + sc_ref[bi]) + sh_ref[bi]
        return _bf(y if keep is None else y * keep)

    h = [normed(jnp.where(i == 0, c_ref[bi], x_ref[bi]), bi) for bi in range(n_batch)]
    projs = [_dot(jnp.concatenate([normed(xp_ref[bi], bi, has_prev), h[bi], normed(xn_ref[bi], bi, has_next)], axis=0),
                  w_ref[:, 0:EV_SHIFTED]) for bi in range(n_batch)]
    vo = _dot(jnp.concatenate(h, axis=0), w_ref[:, EV_SHIFTED:])
    lane = lax.broadcasted_iota(jnp.int32, (1, LANES), 1)
    is_f = jnp.logical_and(jnp.bitwise_and(jnp.right_shift(lane, 2), 1) == 1, lane < 16)
    w = RW_WIDTH
    for bi in range(n_batch):
        ovo_ref[bi] = vo[bi * tm:(bi + 1) * tm].astype(ovo_ref.dtype)
        proj = projs[bi]

        def neighbours(cols):
            blk = proj[:, cols]
            dn = pltpu.roll(blk, 1, 0)[HALO:HALO + tm]
            up = pltpu.roll(blk, ext - 1, 0)[HALO:HALO + tm]
            return blk[HALO:HALO + tm], dn, up

        qk, dn, up = neighbours(slice(2048, 3072))
        oqk_ref[bi] = (dn * conv_ref[0:1, :] + qk * conv_ref[1:2, :] + up * conv_ref[2:3, :]).astype(oqk_ref.dtype)

        gv = proj[HALO:HALO + tm, RW_FEAT:RW_FEAT + LANES] + gb_ref[...]
        gates = jnp.where(is_f, _log_sigmoid(gv), gv)
        og_ref[bi] = gates
        ogt_ref[bi] = gates.T[0:4 * ML_HEADS, :]

        cur, dn, up = neighbours(slice(0, RW_FEAT))
        rf = cur + (0.5 * (dn + up) - cur) * mu_ref[...]
        r, k, v = rf[:, 0:w], rf[:, w:2 * w], rf[:, 2 * w:3 * w]
        wl = rf[:, 3 * w:3 * w + LANES]
        al = rf[:, 3 * w + LANES:3 * w + 2 * LANES]
        gl = rf[:, 3 * w + 2 * LANES:3 * w + 3 * LANES]
        kk = k * kk_ref[...]
        kk = kk * lax.rsqrt(_group_mean(kk * kk, RW_HEAD_DIM) * RW_HEAD_DIM + 1e-12)
        twl = jnp.tanh(wl)
        kd_sum = jnp.zeros_like(k)
        for d in range(2):
            w_log = -_softplus(-(w0_ref[d:d + 1, :] + _dot_sel(twl, w2_ref[d]))) - 0.5
            a = _sigmoid(a0_ref[d:d + 1, :] + _dot_sel(al, a2_ref[d]))
            kd = k * (1.0 + (a - 1.0) * ka_ref[...])
            kd_sum = kd_sum + kd
            olw_ref[d, bi] = -jnp.exp(w_log)
            okb_ref[d, bi, :, 0:w] = kd.astype(okb_ref.dtype)
            okb_ref[d, bi, :, w:2 * w] = (kk * a).astype(okb_ref.dtype)
        orws_ref[bi, :, 0:w] = r.astype(orws_ref.dtype)
        orws_ref[bi, :, w:2 * w] = v.astype(orws_ref.dtype)
        orws_ref[bi, :, 2 * w:3 * w] = (-kk).astype(orws_ref.dtype)
        bonus = _group_mean(r * kd_sum * rk_ref[...], RW_HEAD_DIM) * RW_HEAD_DIM * v
        obg_ref[bi, :, 0:w] = bonus.astype(obg_ref.dtype)
        obg_ref[bi, :, w:2 * w] = _dot(_bf(_sigmoid(gl)), g2_ref[...]).astype(obg_ref.dtype)


def _even_in(ctx, x, g, mods, w_in, conv, gate_b, mu, w0, w2cat, a0, a2cat, g2, kk, ka, rk):
    b, n, d = x.shape
    tm = ROW_TILE
    assert ctx.shape[1] == tm
    t = n + tm
    nt = t // tm
    rh = tm // HALO
    last_h = n // HALO - 1
    seg = lambda i: (i >= 1).astype(jnp.int32)
    cur = lambda c: (lambda i: (0, i, c))
    lat = lambda i: jnp.maximum(i - 1, 0)
    mod = lambda k: pl.BlockSpec((b, None, None, 1, d), lambda i: (0, seg(i), k, 0, 0))
    w = RW_WIDTH
    return pl.pallas_call(
        functools.partial(_even_in_body, nt),
        grid=(nt,),
        in_specs=[
            pl.BlockSpec((b, tm, d), lambda i: (0, 0, 0)),
            pl.BlockSpec((b, tm, d), lambda i: (0, lat(i), 0)),
            pl.BlockSpec((b, HALO, d), lambda i: (0, jnp.maximum(lat(i) * rh - 1, 0), 0)),
            pl.BlockSpec((b, HALO, d), lambda i: (0, jnp.minimum((lat(i) + 1) * rh, last_h), 0)),
            _const_spec((1, d)), mod(0), mod(1), _const_spec(w_in.shape),
            _const_spec(conv.shape), _const_spec(gate_b.shape), _const_spec(mu.shape),
            _const_spec(w0.shape), _const_spec(w2cat.shape), _const_spec(a0.shape),
            _const_spec(a2cat.shape), _const_spec(g2.shape), _const_spec(kk.shape),
            _const_spec(ka.shape), _const_spec(rk.shape),
        ],
        out_specs=[
            pl.BlockSpec((b, tm, 1024), cur(0)),
            pl.BlockSpec((b, tm, 1024), cur(0)),
            pl.BlockSpec((b, tm, LANES), cur(0)),
            pl.BlockSpec((b, 4 * ML_HEADS, tm), lambda i: (0, 0, i)),
            pl.BlockSpec((b, tm, 3 * w), cur(0)),
            pl.BlockSpec((2, b, tm, w), lambda i: (0, 0, i, 0)),
            pl.BlockSpec((2, b, tm, 2 * w), lambda i: (0, 0, i, 0)),
            pl.BlockSpec((b, tm, 2 * w), cur(0)),
        ],
        out_shape=[
            jax.ShapeDtypeStruct((b, t, 1024), BF16),
            jax.ShapeDtypeStruct((b, t, 1024), BF16),
            jax.ShapeDtypeStruct((b, t, LANES), F32),
            jax.ShapeDtypeStruct((b, 4 * ML_HEADS, t), F32),
            jax.ShapeDtypeStruct((b, t, 3 * w), BF16),
            jax.ShapeDtypeStruct((2, b, t, w), F32),
            jax.ShapeDtypeStruct((2, b, t, 2 * w), BF16),
            jax.ShapeDtypeStruct((b, t, 2 * w), BF16),
        ],
        compiler_params=_params(("parallel",)),
        name="even_in",
    )(ctx, x, x, x, g, mods, mods, w_in, conv, gate_b, mu, w0, w2cat, a0, a2cat, g2, kk, ka, rk)


def _mlstm_body(q_ref, k_ref, v_ref, gc_ref, gr_ref, o_ref, c_scr, n_scr, m_scr):
    d = pl.program_id(0)
    j = pl.program_id(1)
    n_batch = q_ref.shape[0]
    n_rows = ML_CHUNK
    n_sub = q_ref.shape[1] // n_rows

    @pl.when(j == 0)
    def _():
        c_scr[...] = jnp.zeros_like(c_scr)
        n_scr[...] = jnp.zeros_like(n_scr)
        m_scr[...] = jnp.zeros_like(m_scr)

    mask = _order_mask(n_rows, 1 - 2 * d, strict=False)
    mf = mask.astype(BF16)
    row_id = lax.broadcasted_iota(jnp.int32, (n_rows, 1), 0)
    is_last = row_id == (n_rows - 1) * (1 - d)
    pick_row = lax.broadcasted_iota(jnp.int32, (LANES, LANES), 0)
    pick = [(pick_row == 2 * ML_HEADS * d + c).astype(BF16) for c in range(2 * ML_HEADS)]
    ones_cols = jnp.ones((n_rows, LANES), BF16)

    rows = [pl.ds(pl.multiple_of(jnp.where(d == 0, sc, n_sub - 1 - sc) * n_rows, n_rows), n_rows)
            for sc in range(n_sub)]
    heads = [slice(h * ML_DIM, (h + 1) * ML_DIM) for h in range(ML_HEADS)]
    units = [(sc, bi, h) for sc in range(n_sub) for bi in range(n_batch) for h in range(ML_HEADS)]
    idx = range(len(units))
    carried = [bi * ML_HEADS + h for _, bi, h in units]
    gc, b_col, gr, b_row = {}, {}, {}, {}
    for bi in range(n_batch):
        gr_all = gr_ref[bi]
        for sc in range(n_sub):
            gc[sc, bi] = gc_ref[bi, rows[sc], :]
            parts = [gr_all[:, c * n_rows:(c + 1) * n_rows] for c in (sc, n_sub - 1 - sc)]
            gr[sc, bi] = jnp.where(d == 0, parts[0], parts[1])
            b_col[sc, bi] = _sel_dot(mf, gc[sc, bi])
            b_row[sc, bi] = _dot_sel_nt(gr[sc, bi], mf)
    q_f = [q_ref[bi, rows[sc], heads[h]] for sc, bi, h in units]
    q = [_bf(x) for x in q_f]
    kf = [k_ref[bi, rows[sc], heads[h]].astype(F32) * (ML_DIM ** -0.5) for sc, bi, h in units]
    k = [_bf(x) for x in kf]
    vf = [v_ref[bi, rows[sc], heads[h]].astype(F32) for sc, bi, h in units]
    qk = [_dot_nt(q[u], k[u]) for u in idx]
    ic = [_dot_sel(gc[sc, bi], pick[h]) for sc, bi, h in units]
    bc = [_dot_sel(b_col[sc, bi], pick[ML_HEADS + h]) for sc, bi, h in units]
    b_tot = [jnp.sum(jnp.where(is_last, bc[u], 0.0), axis=0, keepdims=True) for u in idx]
    logw = [b_tot[u] - bc[u] + ic[u] for u in idx]
    logw_max = [jnp.max(logw[u], axis=0, keepdims=True) for u in idx]
    logd = [jnp.where(mask, bc[u] - b_row[sc, bi][ML_HEADS + h:ML_HEADS + h + 1, :] + gr[sc, bi][h:h + 1, :], -1e30)
            for u, (sc, bi, h) in enumerate(units)]
    logd_max = [jnp.max(x, axis=-1, keepdims=True) for x in logd]
    m_run = [m_scr[c][0:1, :] for c in range(n_batch * ML_HEADS)]
    wgt, decay, s, m_t, w_inter = [], [], [], [], []
    for u in idx:
        m_prev = m_run[carried[u]]
        m_new = jnp.maximum(b_tot[u] + m_prev, logw_max[u])
        wgt.append(jnp.exp(logw[u] - m_new))
        decay.append(jnp.exp(b_tot[u] + m_prev - m_new))
        m_inter = bc[u] + m_prev
        m_t.append(jnp.maximum(m_inter, logd_max[u]))
        s.append(qk[u] * jnp.exp(logd[u] - m_t[u]))
        w_inter.append(jnp.exp(m_inter - m_t[u]))
        m_run[carried[u]] = m_new
    kv = [_dot_tn(_bf(vf[u] * wgt[u]), k[u]) for u in idx]
    sv = [_dot(_bf(s[u]), jnp.concatenate([_bf(vf[u]), ones_cols], axis=1)) for u in idx]
    c_run = [c_scr[c] for c in range(n_batch * ML_HEADS)]
    n_run = [n_scr[c][0:1, :] for c in range(n_batch * ML_HEADS)]
    per_chunk = n_batch * ML_HEADS
    for sc in range(n_sub):
        ids = range(sc * per_chunk, (sc + 1) * per_chunk)
        qc = [_dot_nt(q[u], jnp.concatenate([_bf(c_run[carried[u]]),
                                             _bf(jnp.broadcast_to(n_run[carried[u]], (LANES, ML_DIM)))], axis=0))
              for u in ids]
        for u, x in zip(ids, qc):
            _, bi, h = units[u]
            num = sv[u][:, :ML_DIM] + w_inter[u] * x[:, :ML_DIM]
            den = sv[u][:, ML_DIM:] + w_inter[u] * x[:, ML_DIM:]
            o_ref[bi, rows[sc], heads[h]] = num / jnp.maximum(jnp.abs(den), jnp.exp(-m_t[u]))
        for u in ids:
            c = carried[u]
            c_run[c] = decay[u] * c_run[c] + kv[u]
            n_run[c] = decay[u] * n_run[c] + jnp.sum(wgt[u] * kf[u], axis=0, keepdims=True)
    for c in range(per_chunk):
        c_scr[c] = c_run[c]
        n_scr[c] = jnp.broadcast_to(n_run[c], (SUBLANES, ML_DIM))
        m_scr[c] = jnp.broadcast_to(m_run[c], (SUBLANES, LANES))


def _mlstm(qk, proj, gates, gates_t, n_ctx):
    b, t, _ = qk.shape
    n_rows = ML_BLOCK
    nc, nctx = t // n_rows, n_ctx // n_rows
    ci = lambda d, j: _chunk_index(d, j, nctx, nc)
    return pl.pallas_call(
        _mlstm_body,
        grid=(2, nc),
        in_specs=[
            pl.BlockSpec((b, n_rows, ML_WIDTH), lambda d, j: (0, ci(d, j), 0)),
            pl.BlockSpec((b, n_rows, ML_WIDTH), lambda d, j: (0, ci(d, j), 1)),
            pl.BlockSpec((b, n_rows, ML_WIDTH), lambda d, j: (0, ci(d, j), 0)),
            pl.BlockSpec((b, n_rows, LANES), lambda d, j: (0, ci(d, j), 0)),
            pl.BlockSpec((b, 2 * ML_HEADS, n_rows), lambda d, j: (0, d, ci(d, j))),
        ],
        out_specs=pl.BlockSpec((None, b, n_rows, ML_WIDTH), lambda d, j: (d, 0, ci(d, j), 0)),
        out_shape=jax.ShapeDtypeStruct((2, b, t, ML_WIDTH), F32),
        scratch_shapes=[
            pltpu.VMEM((b * ML_HEADS, ML_DIM, ML_DIM), F32),
            pltpu.VMEM((b * ML_HEADS, SUBLANES, ML_DIM), F32),
            pltpu.VMEM((b * ML_HEADS, SUBLANES, LANES), F32),
        ],
        compiler_params=_params(("parallel", "arbitrary")),
        name="mlstm",
    )(qk, qk, proj, gates, gates_t)


def _rwkv_body(r_ref, v_ref, a_ref, lw_ref, k_ref, b_ref, o_ref, h_scr):
    d = pl.program_id(0)
    j = pl.program_id(1)
    n_rows = RW_CHUNK
    two = 2 * n_rows
    shift = n_rows.bit_length() - 1

    @pl.when(j == 0)
    def _():
        h_scr[...] = jnp.zeros_like(h_scr)

    sgn = 1 - 2 * d
    cum = _order_mask(n_rows, sgn, strict=False).astype(BF16)
    row = lax.broadcasted_iota(jnp.int32, (two, two), 0)
    col = lax.broadcasted_iota(jnp.int32, (two, two), 1)
    same = jnp.right_shift(row, shift) == jnp.right_shift(col, shift)
    diff = (jnp.bitwise_and(row, n_rows - 1) - jnp.bitwise_and(col, n_rows - 1)) * sgn
    m_strict = jnp.logical_and(same, diff > 0)
    m_incl = jnp.logical_and(same, diff >= 0)
    eye = (row == col).astype(F32)
    head0 = lax.broadcasted_iota(jnp.int32, (1, LANES), 1) < RW_HEAD_DIM

    def stack(x):
        return jnp.concatenate([jnp.where(head0, x, 0.0), jnp.where(head0, 0.0, x)], axis=0)

    def fold(x):
        return x[:n_rows] + x[n_rows:]

    n_batch = r_ref.shape[0]
    n_sub = r_ref.shape[1] // n_rows
    lane_groups = [slice(p * LANES, (p + 1) * LANES) for p in range(RW_WIDTH // LANES)]
    rows = [pl.ds(pl.multiple_of(jnp.where(d == 0, sc, n_sub - 1 - sc) * n_rows, n_rows), n_rows)
            for sc in range(n_sub)]
    groups = [(sc, bi, p) for sc in range(n_sub) for bi in range(n_batch) for p in range(len(lane_groups))]
    x_s, r_s, v_s, bh_s, kh_s, bi_s, ki_s, e_tot = [], [], [], [], [], [], [], []
    for sc in range(n_sub):
        for bi in range(n_batch):
            lw = lw_ref[bi, rows[sc], :]
            g = _sel_dot(cum, lw)
            g_tot = jnp.sum(lw, axis=0, keepdims=True)
            inv = jnp.exp(-g)
            to_end = jnp.exp(g_tot - g)
            at = a_ref[bi, rows[sc], :].astype(F32) * jnp.exp(g - lw)
            rt = r_ref[bi, rows[sc], :].astype(F32) * jnp.exp(g)
            b_in = b_ref[bi, rows[sc], :].astype(F32)
            k_in = k_ref[bi, rows[sc], :].astype(F32)
            vv = v_ref[bi, rows[sc], :].astype(F32)
            for sl in lane_groups:
                x_s.append(stack(at[:, sl]))
                r_s.append(stack(rt[:, sl]))
                v_s.append(stack(vv[:, sl]))
                bh_s.append(stack((b_in * to_end)[:, sl]))
                kh_s.append(stack((k_in * to_end)[:, sl]))
                bi_s.append(stack((b_in * inv)[:, sl]))
                ki_s.append(stack((k_in * inv)[:, sl]))
                e_tot.append(jnp.exp(g_tot)[:, sl])
    big = [_mm_nt(jnp.concatenate([x_s[p], r_s[p]], axis=0), jnp.concatenate([bi_s[p], ki_s[p]], axis=0))
           for p in range(len(groups))]
    a_ab = [jnp.where(m_strict, m[:two, :two], 0.0) for m in big]
    a_rb = [jnp.where(m_incl, m[two:, :two], 0.0) for m in big]
    a_k = [jnp.concatenate([jnp.where(m_strict, m[:two, two:], 0.0), jnp.where(m_incl, m[two:, two:], 0.0)], axis=0)
           for m in big]
    av = [_mm(a_k[p], v_s[p]) for p in range(len(groups))]
    t_inv = [eye + m for m in a_ab]
    pw = [_mm(m, m) for m in a_ab]
    for level in range(1, shift):
        last = level == shift - 1
        res = [_mm(t if last else jnp.concatenate([t, m], axis=0), m) for t, m in zip(t_inv, pw)]
        t_inv = [t + r[:two] for t, r in zip(t_inv, res)]
        if not last:
            pw = [r[two:] for r in res]
    wu = [_mm(t_inv[p], jnp.concatenate([x_s[p], av[p][:two]], axis=1)) for p in range(len(groups))]
    rb = [_mm(a_rb[p], wu[p]) for p in range(len(groups))]
    upd = [_mm_tn(wu[p], bh_s[p]) for p in range(len(groups))]
    gt2 = [_mm_tn(v_s[p], kh_s[p]) for p in range(len(groups))]
    q_eff = [fold(r_s[p] + rb[p][:, :LANES]) for p in range(len(groups))]
    y_in = [fold(rb[p][:, LANES:] + av[p][two:]) for p in range(len(groups))]
    per_chunk = n_batch * len(lane_groups)
    h_t = [h_scr[u] for u in range(per_chunk)]
    for sc in range(n_sub):
        ids = range(sc * per_chunk, (sc + 1) * per_chunk)
        for u, p in enumerate(ids):
            _, bi, lg = groups[p]
            o_ref[bi, rows[sc], lane_groups[lg]] = _mm_nt(q_eff[p], h_t[u]) + y_in[p]
        h_t = [e_tot[p] * h_t[u] + _mm(h_t[u], upd[p][:LANES]) + upd[p][LANES:] + gt2[p] for u, p in enumerate(ids)]
    for u in range(per_chunk):
        h_scr[u] = h_t[u]


def _rwkv(rws, lw, kb, n_ctx):
    b, t, _ = rws.shape
    n_rows = RW_BLOCK
    nc, nctx = t // n_rows, n_ctx // n_rows
    ci = lambda d, j: _chunk_index(d, j, nctx, nc)
    w = RW_WIDTH
    shared = lambda c: pl.BlockSpec((b, n_rows, w), lambda d, j: (0, ci(d, j), c))
    per_dir = lambda c: pl.BlockSpec((None, b, n_rows, w), lambda d, j: (d, 0, ci(d, j), c))
    return pl.pallas_call(
        _rwkv_body,
        grid=(2, nc),
        in_specs=[shared(0), shared(1), shared(2), per_dir(0), per_dir(0), per_dir(1)],
        out_specs=pl.BlockSpec((None, b, n_rows, w), lambda d, j: (d, 0, ci(d, j), 0)),
        out_shape=jax.ShapeDtypeStruct((2, b, t, w), F32),
        scratch_shapes=[pltpu.VMEM((b * w // LANES, LANES, LANES), F32)],
        compiler_params=_params(("parallel", "arbitrary")),
        name="rwkv7",
    )(rws, rws, rws, lw, kb, kb)


def _tail(x, mix, outw_ref, w1_ref, w2_ref, ng_ref, ga_ref, shm_ref, scm_ref, gm_ref):
    x1 = x + ga_ref[...] * _rms(_dot(_bf(mix), outw_ref[...]), ng_ref[1:2, :])
    h = _bf(_rms(x1, ng_ref[2:3, :]) * (1.0 + scm_ref[...]) + shm_ref[...])
    chunks = [slice(c * FF_CHUNK, (c + 1) * FF_CHUNK) for c in range(D_FF // FF_CHUNK)]
    u = [jnp.maximum(_dot(h, w1_ref[:, cols]), 0.0) for cols in chunks]
    dd = _dot(_bf(u[0] * u[0]), w2_ref[chunks[0], :])
    for uc, cols in zip(u[1:], chunks[1:]):
        dd = dd + _dot(_bf(uc * uc), w2_ref[cols, :])
    return x1 + gm_ref[...] * _rms(dd, ng_ref[3:4, :])


def _even_post_body(x_ref, hm_ref, mo_ref, yrw_ref, bg_ref, mln_ref, lnw_ref, lnb_ref,
                    outw_ref, w1_ref, w2_ref, ng_ref, ga_ref, shm_ref, scm_ref, gm_ref, o_ref):
    w = RW_WIDTH
    ym = hm_ref[0, 0] + hm_ref[1, 0]
    ym = (ym * lax.rsqrt(_group_mean(ym * ym, ML_DIM) + EPS) * mln_ref[...]
          * _sigmoid(mo_ref[0].astype(F32)))
    y = yrw_ref[0, 0] + yrw_ref[1, 0]
    yc = y - _group_mean(y, RW_HEAD_DIM)
    y = (yc * lax.rsqrt(_group_mean(yc * yc, RW_HEAD_DIM) + RW_GN_EPS) * lnw_ref[...] + lnb_ref[...]
         + bg_ref[0, :, 0:w].astype(F32))
    mix = jnp.concatenate([ym, y * bg_ref[0, :, w:2 * w].astype(F32)], axis=-1)
    o_ref[...] = _tail(x_ref[...], mix, outw_ref, w1_ref, w2_ref, ng_ref, ga_ref, shm_ref, scm_ref, gm_ref)


def _tail_specs(seg, layer):
    d = D_MODEL
    pick = lambda *shape: pl.BlockSpec((None,) + shape, lambda *_: (layer, 0, 0), pipeline_mode=pl.Buffered(1))
    return [
        pick(d, d), pick(d, D_FF), pick(D_FF, d), pick(4, d),
        _mod_spec(2, seg), _mod_spec(3, seg), _mod_spec(4, seg), _mod_spec(5, seg),
    ]


def _even_post(x_seg, first_row, segment, tm, hm, proj, yrw, bg, mln, lnw, lnb, outw, w1, w2, ng, mods, layer):
    b, rows, d = x_seg.shape
    seg = lambda i: segment
    w = RW_WIDTH
    el = lambda *shape: tuple(pl.Element(s) for s in shape)
    row0 = lambda i: pl.multiple_of(first_row + i * tm, int(np.gcd(first_row, tm)))
    return pl.pallas_call(
        _even_post_body,
        grid=(b, rows // tm),
        in_specs=[
            pl.BlockSpec((None, tm, d), lambda bb, i: (bb, i, 0)),
            pl.BlockSpec(el(2, 1, tm, w), lambda bb, i: (0, bb, row0(i), 0)),
            pl.BlockSpec(el(1, tm, w), lambda bb, i: (bb, row0(i), w)),
            pl.BlockSpec(el(2, 1, tm, w), lambda bb, i: (0, bb, row0(i), 0)),
            pl.BlockSpec(el(1, tm, 2 * w), lambda bb, i: (bb, row0(i), 0)),
            _const_spec((1, w)), _const_spec((1, w)), _const_spec((1, w)),
        ] + _tail_specs(seg, layer),
        out_specs=pl.BlockSpec((None, tm, d), lambda bb, i: (bb, i, 0)),
        out_shape=jax.ShapeDtypeStruct((b, rows, d), F32),
        compiler_params=_params(("parallel", "parallel")),
        name="even_post",
    )(x_seg, hm, proj, yrw, bg, mln, lnw, lnb, outw, w1, w2, ng, mods, mods, mods, mods)


def _hgrn_body(q_ref, f_ref, i_ref, lb_ref, o_ref, s_scr):
    d = pl.program_id(0)
    j = pl.program_id(1)
    n_batch, n_rows = q_ref.shape[0], q_ref.shape[1]
    n_sub = n_rows // HG_CHUNK

    @pl.when(j == 0)
    def _():
        s_scr[...] = jnp.zeros_like(s_scr)

    mask = _order_mask(HG_CHUNK, 1 - 2 * d, strict=False)
    mf = mask.astype(BF16)
    lb = lb_ref[...]
    heads = range(HG_HEADS)
    sls = [slice(h * HG_DIM, (h + 1) * HG_DIM) for h in heads]
    units = [(bi, h) for bi in range(n_batch) for h in heads]
    n_units = len(units)
    s_t = [s_scr[u] for u in range(n_units)]
    rows = [pl.ds(pl.multiple_of(jnp.where(d == 0, sc, n_sub - 1 - sc) * HG_CHUNK, HG_CHUNK), HG_CHUNK)
            for sc in range(n_sub)]
    loaded = [[(q_ref[bi, r, :], f_ref[bi, r, :], i_ref[bi, r, :]) for bi in range(n_batch)] for r in rows]
    q_in, e_tot, q_mids, k_mids, k_outs, vvs = [], [], [], [], [], []

    def prepare(sc):
        for lst in (q_in, e_tot, q_mids, k_mids, k_outs, vvs):
            lst.append([])
        for bi in range(n_batch):
            qa = loaded[sc][bi][0].astype(F32)
            pre = loaded[sc][bi][1].astype(F32)
            vv = loaded[sc][bi][2]
            q = qa * _sigmoid(qa) * (HG_DIM ** -0.5)
            e_abs = jnp.exp(-jnp.abs(pre))
            big = 1.0 / (1.0 + e_abs)
            small = e_abs * big
            log_f = jnp.log(lb + (1.0 - lb) * jnp.where(pre >= 0.0, big, small))
            kf = (1.0 - lb) * jnp.where(pre >= 0.0, small, big)
            bc = _sel_dot(mf, log_f)
            tot = jnp.sum(log_f, axis=0, keepdims=True)
            half = 0.5 * tot
            e_half = jnp.exp(half)
            q_mid_f = q * jnp.exp(bc - half)
            k_mid_f = kf * jnp.exp(half - bc)
            q_mid = _bf(q_mid_f)
            k_mid = _bf(k_mid_f)
            k_out = _bf(k_mid_f * e_half)
            q_all = _bf(q_mid_f * e_half)
            e_all = e_half * e_half
            q_mids[sc] += [q_mid[:, sl] for sl in sls]
            k_mids[sc] += [k_mid[:, sl] for sl in sls]
            k_outs[sc] += [k_out[:, sl] for sl in sls]
            vvs[sc] += [vv[:, sl] for sl in sls]
            q_in[sc] += [q_all[:, sl] for sl in sls]
            e_tot[sc] += [e_all[:, sl] for sl in sls]

    prepare(0)
    for sc in range(n_sub):
        if sc + 1 < n_sub:
            prepare(sc + 1)
        att = [_dot_nt(q_mids[sc][u], k_mids[sc][u]) for u in range(n_units)]
        kv = [_dot_tn(vvs[sc][u], k_outs[sc][u]) for u in range(n_units)]
        intra = [_dot(_bf(jnp.where(mask, att[u], 0.0)), vvs[sc][u]) for u in range(n_units)]
        for u, (bi, h) in enumerate(units):
            o_ref[bi, rows[sc], sls[h]] = intra[u] + _dot_nt(q_in[sc][u], _bf(s_t[u]))
        s_t = [e_tot[sc][u] * s_t[u] + kv[u] for u in range(n_units)]
    for u in range(n_units):
        s_scr[u] = s_t[u]


def _hgrn(proj, lb, n_ctx):
    b, t, _ = proj.shape
    n_rows = HG_BLOCK
    nc, nctx = t // n_rows, n_ctx // n_rows
    ci = lambda d, j: _chunk_index(d, j, nctx, nc)
    w = HG_WIDTH
    return pl.pallas_call(
        _hgrn_body,
        grid=(2, nc),
        in_specs=[
            pl.BlockSpec((b, n_rows, w), lambda d, j: (0, ci(d, j), 0)),
            pl.BlockSpec((b, n_rows, w), lambda d, j: (0, ci(d, j), 1 + d)),
            pl.BlockSpec((b, n_rows, w), lambda d, j: (0, ci(d, j), 3)),
            _const_spec((1, w)),
        ],
        out_specs=pl.BlockSpec((None, b, n_rows, w), lambda d, j: (d, 0, ci(d, j), 0)),
        out_shape=jax.ShapeDtypeStruct((2, b, t, w), F32),
        scratch_shapes=[pltpu.VMEM((b * HG_HEADS, HG_DIM, HG_DIM), F32)],
        compiler_params=_params(("parallel", "arbitrary")),
        name="hgrn2",
    )(proj, proj, proj, lb)


V_ROWS = MLA_V + SUBLANES
OD_MAIN = 5 * HG_WIDTH


def _odd_in_body(c_ref, x_ref, g_ref, sh_ref, sc_ref, w_ref, cos_ref, sin_ref, kvn_ref, qn_ref,
                 wk_ref, wvt_ref, wq1_ref, wq2_ref, proj_ref, k_ref, vt_ref, q_ref):
    n_batch, tm = x_ref.shape[0], x_ref.shape[1]
    is_ctx = pl.program_id(0) == 0
    h = jnp.concatenate([_bf(_rms(jnp.where(is_ctx, c_ref[bi], x_ref[bi]), g_ref[...]) * (1.0 + sc_ref[bi])
                             + sh_ref[bi]) for bi in range(n_batch)], axis=0)
    mla = _dot(h, w_ref[:, OD_MAIN:])
    main = _dot(h, w_ref[:, 0:OD_MAIN])
    cq = mla[:, 0:MLA_Q_RANK]
    base = MLA_Q_RANK
    ckv = mla[:, base:base + MLA_KV_RANK]
    cos = jnp.concatenate([cos_ref[...]] * n_batch, axis=0)
    sin = jnp.concatenate([sin_ref[...]] * n_batch, axis=0)
    k_rope = mla[:, base + LANES:base + 2 * LANES] * cos + mla[:, base + 2 * LANES:base + 3 * LANES] * sin
    kvn = _bf(_rms(ckv, kvn_ref[...]))
    kn = _dot(kvn, wk_ref[...])
    ones = jnp.ones((SUBLANES, tm), BF16)
    cn = _bf(_rms(cq, qn_ref[...]))
    qa = _dot(cn, wq1_ref[...])
    qb = _dot(cn, wq2_ref[...])
    scale = (MLA_NOPE + MLA_ROPE) ** -0.5 * float(np.log2(np.e))
    for hd in range(MLA_HEADS):
        k_h = jnp.concatenate([kn[:, hd * MLA_NOPE:(hd + 1) * MLA_NOPE], k_rope], axis=-1).astype(BF16)
        vt = _dot_nt(wvt_ref[hd], kvn).astype(BF16)
        a = qa[:, hd * 256:(hd + 1) * 256]
        q_rope = a[:, MLA_NOPE:] * cos + qb[:, hd * LANES:(hd + 1) * LANES] * sin
        q_h = (jnp.concatenate([a[:, 0:MLA_NOPE], q_rope], axis=-1) * scale).astype(BF16)
        for bi in range(n_batch):
            rows = slice(bi * tm, (bi + 1) * tm)
            k_ref[bi, hd] = k_h[rows]
            vt_ref[bi, hd] = jnp.concatenate([vt[:, rows], ones], axis=0)
            q_ref[bi, hd] = q_h[rows]
    for bi in range(n_batch):
        proj_ref[bi] = main[bi * tm:(bi + 1) * tm].astype(proj_ref.dtype)


def _odd_in(xc, x, g, mods, w_in, cos, sin, kv_norm, q_norm, wk, wvt, wq1, wq2):
    b, n, d = x.shape
    tm = ROW_TILE
    assert xc.shape[1] == tm
    t = n + tm
    seg = lambda i: (i >= 1).astype(jnp.int32)
    q_blk = lambda i: (0, 0, jnp.maximum(i - 1, 0), 0)
    mod = lambda k: pl.BlockSpec((b, None, None, 1, d), lambda i: (0, seg(i), k, 0, 0))
    return pl.pallas_call(
        _odd_in_body,
        grid=(t // tm,),
        in_specs=[
            pl.BlockSpec((b, tm, d), lambda i: (0, 0, 0)),
            pl.BlockSpec((b, tm, d), lambda i: (0, jnp.maximum(i - 1, 0), 0)),
            _const_spec((1, d)), mod(0), mod(1), _const_spec(w_in.shape),
            pl.BlockSpec((tm, LANES), lambda i: (i, 0)),
            pl.BlockSpec((tm, LANES), lambda i: (i, 0)),
            _const_spec((1, MLA_KV_RANK)), _const_spec((1, MLA_Q_RANK)),
            _const_spec(wk.shape), _const_spec(wvt.shape), _const_spec(wq1.shape), _const_spec(wq2.shape),
        ],
        out_specs=[
            pl.BlockSpec((b, tm, OD_MAIN), lambda i: (0, i, 0)),
            pl.BlockSpec((b, MLA_HEADS, tm, 256), lambda i: (0, 0, i, 0)),
            pl.BlockSpec((b, MLA_HEADS, V_ROWS, tm), lambda i: (0, 0, 0, i)),
            pl.BlockSpec((b, MLA_HEADS, tm, 256), q_blk),
        ],
        out_shape=[
            jax.ShapeDtypeStruct((b, t, OD_MAIN), BF16),
            jax.ShapeDtypeStruct((b, MLA_HEADS, t, 256), BF16),
            jax.ShapeDtypeStruct((b, MLA_HEADS, V_ROWS, t), BF16),
            jax.ShapeDtypeStruct((b, MLA_HEADS, n, 256), BF16),
        ],
        compiler_params=_params(("arbitrary",)),
        name="odd_in",
    )(xc, x, g, mods, mods, w_in, cos, sin, kv_norm, q_norm, wk, wvt, wq1, wq2)


def _attn_body(q_ref, k_ref, vt_ref, o_ref):
    k = k_ref[...]
    vt = vt_ref[...]
    subs = [slice(i * ATTN_SUB, (i + 1) * ATTN_SUB) for i in range(q_ref.shape[0] // ATTN_SUB)]
    st = [_dot_nt(k, q_ref[sl, :]) for sl in subs]
    pt = [_bf(jnp.exp2(s - jnp.max(s, axis=0, keepdims=True))) for s in st]
    ot = [_dot(vt, p) for p in pt]
    for sl, o in zip(subs, ot):
        o_ref[sl, :] = (o[0:MLA_V] / o[MLA_V:MLA_V + 1]).T


def _attention(q, k, vt):
    b, h, n, dq = q.shape
    t = k.shape[2]
    tq = ATTN_TQ
    return pl.pallas_call(
        _attn_body,
        grid=(b, h, n // tq),
        in_specs=[
            pl.BlockSpec((None, None, tq, dq), lambda bb, hh, i: (bb, hh, i, 0)),
            pl.BlockSpec((None, None, t, dq), lambda bb, hh, i: (bb, hh, 0, 0)),
            pl.BlockSpec((None, None, V_ROWS, t), lambda bb, hh, i: (bb, hh, 0, 0)),
        ],
        out_specs=pl.BlockSpec((None, tq, MLA_V), lambda bb, hh, i: (bb, i, hh)),
        out_shape=jax.ShapeDtypeStruct((b, n, h * MLA_V), F32),
        compiler_params=_params(("parallel", "parallel", "parallel")),
        name="mla_attention",
    )(q, k, vt)


def _odd_post_body(x_ref, o_ref_in, hg_ref, att_ref, hgn_ref,
                   outw_ref, w1_ref, w2_ref, ng_ref, ga_ref, shm_ref, scm_ref, gm_ref, o_ref):
    y = o_ref_in[0, 0] + o_ref_in[1, 0]
    g = hg_ref[0].astype(F32)
    y = y * lax.rsqrt(_group_mean(y * y, HG_DIM) + EPS) * hgn_ref[...] * (g * _sigmoid(g))
    mix = jnp.concatenate([y, att_ref[...]], axis=-1)
    o_ref[...] = _tail(x_ref[...], mix, outw_ref, w1_ref, w2_ref, ng_ref, ga_ref, shm_ref, scm_ref, gm_ref)


def _odd_post(x, ohg, proj, att, hgn, outw, w1, w2, ng, mods, n_ctx, layer):
    b, n, d = x.shape
    tm = LAT_TILE
    seg = lambda i: 1
    w = HG_WIDTH
    el = lambda *shape: tuple(pl.Element(s) for s in shape)
    row0 = lambda i: pl.multiple_of(n_ctx + i * tm, int(np.gcd(n_ctx, tm)))
    return pl.pallas_call(
        _odd_post_body,
        grid=(b, n // tm),
        in_specs=[
            pl.BlockSpec((None, tm, d), lambda bb, i: (bb, i, 0)),
            pl.BlockSpec(el(2, 1, tm, w), lambda bb, i: (0, bb, row0(i), 0)),
            pl.BlockSpec(el(1, tm, w), lambda bb, i: (bb, row0(i), 4 * w)),
            pl.BlockSpec((None, tm, w), lambda bb, i: (bb, i, 0)),
            _const_spec((1, w)),
        ] + _tail_specs(seg, layer),
        out_specs=pl.BlockSpec((None, tm, d), lambda bb, i: (bb, i, 0)),
        out_shape=jax.ShapeDtypeStruct((b, n, d), F32),
        compiler_params=_params(("parallel", "parallel")),
        name="odd_post",
    )(x, ohg, proj, att, hgn, outw, w1, w2, ng, mods, mods, mods, mods)


def _rope_tables(n, n_ctx):
    quarter = MLA_ROPE // 4
    inv = ROPE_BASE ** (-jnp.arange(quarter, dtype=F32) / quarter)
    rows = n // GRID_W
    ang_r = jnp.arange(rows, dtype=F32)[:, None] * inv[None, :]
    ang_c = jnp.arange(GRID_W, dtype=F32)[:, None] * inv[None, :]
    by_row = lambda a: jnp.repeat(a, GRID_W, axis=0)
    by_col = lambda a: jnp.tile(a, (rows, 1))
    cos_r, sin_r, cos_c, sin_c = by_row(jnp.cos(ang_r)), by_row(jnp.sin(ang_r)), by_col(jnp.cos(ang_c)), by_col(jnp.sin(ang_c))
    cos = jnp.concatenate([cos_r, cos_r, cos_c, cos_c], axis=-1)
    sin = jnp.concatenate([-sin_r, sin_r, -sin_c, sin_c], axis=-1)
    cos = jnp.concatenate([jnp.ones((n_ctx, MLA_ROPE), F32), cos], axis=0)
    sin = jnp.concatenate([jnp.zeros((n_ctx, MLA_ROPE), F32), sin], axis=0)
    pad = lambda a: jnp.pad(a, ((0, 0), (0, LANES - MLA_ROPE)))
    return pad(cos), pad(sin)


_ROPE_SWAP = np.concatenate([np.arange(16, 32), np.arange(0, 16), np.arange(48, 64), np.arange(32, 48)])


def kernel(x, c, ctx, c_ctx, ada_w, ada_b, norm_g, out_w, mlp_w1, mlp_w2, hg_lb, ev_w_in, ml_conv, ml_gate_b, ml_norm, rw_mu, rw_w0, rw_w2, rw_a0, rw_a2, rw_g2, rw_kk, rw_ka, rw_rk, rw_ln_w, rw_ln_b, od_w_in, hg_norm, mla_q_norm, mla_w_qb, mla_kv_norm, mla_w_kvb):
    b, n, d = x.shape
    n_ctx = ctx.shape[1]
    assert d == D_MODEL and b + 1 <= SUBLANES
    assert n_ctx == ROW_TILE and n % LAT_TILE == 0 and n % ATTN_TQ == 0
    assert all(n_ctx % blk == 0 and n % blk == 0 for blk in (ML_BLOCK, RW_BLOCK, HG_BLOCK))
    row1 = lambda a: a.reshape(1, -1)

    c8 = jnp.concatenate([c, c_ctx[None], jnp.zeros((SUBLANES - b - 1, d), F32)], axis=0)
    mod = _ada_mod(c8, ada_w, ada_b).reshape(-1, SUBLANES, N_MOD, d)

    def layer_mods(l):
        lat = mod[l, :b]
        cm = jnp.broadcast_to(mod[l, b][None], lat.shape)
        return jnp.stack([cm, lat], axis=1)[:, :, :, None, :]

    outw_bf, w1_bf, w2_bf = _bf(out_w), _bf(mlp_w1), _bf(mlp_w2)

    l, e = 0, 0
    mods = layer_mods(l)
    wi = ev_w_in[e]
    w_pad = jnp.concatenate([wi[:, 2064:3984], wi[:, 2048:2064], jnp.zeros((d, LANES - 16), F32), wi[:, 0:2048]], axis=1)
    gate_b = jnp.pad(ml_gate_b[e].reshape(1, 16), ((0, 0), (0, LANES - 16)))
    zeros64 = jnp.zeros((64, RW_WIDTH), F32)
    w2cat = jnp.stack([jnp.concatenate([rw_w2[e, 0], zeros64], 0), jnp.concatenate([zeros64, rw_w2[e, 1]], 0)])
    a2cat = jnp.stack([jnp.concatenate([rw_a2[e, 0], zeros64], 0), jnp.concatenate([zeros64, rw_a2[e, 1]], 0)])
    proj, qk, gates, gates_t, rws, rw_lw, rw_kb, bg = _even_in(
        ctx, x, row1(norm_g[l, 0]), mods, _bf(w_pad),
        ml_conv[e], gate_b, row1(rw_mu[e]), rw_w0[e], _bf(w2cat), rw_a0[e], _bf(a2cat), _bf(rw_g2[e]),
        row1(rw_kk[e]), row1(rw_ka[e]), row1(rw_rk[e]))

    hm = _mlstm(qk, proj, gates, gates_t, n_ctx)
    yrw = _rwkv(rws, rw_lw, rw_kb, n_ctx)
    post_args = (hm, proj, yrw, bg, row1(ml_norm[e]), row1(rw_ln_w[e]), row1(rw_ln_b[e]),
                 outw_bf, w1_bf, w2_bf, norm_g, mods, l)
    xc1 = _even_post(ctx, 0, 0, ROW_TILE, *post_args)
    x1 = _even_post(x, n_ctx, 1, LAT_TILE, *post_args)

    l, o = 1, 0
    mods = layer_mods(l)
    lb_all = jax.nn.softmax(hg_lb.astype(F32), axis=0)
    lb_all = jnp.cumsum(lb_all, axis=0) - lb_all[0]
    wi = od_w_in[o]
    kr = wi[:, 2944:3008]
    z = lambda k: jnp.zeros((d, k), F32)
    w_pad = jnp.concatenate([wi[:, 0:2944], kr, z(64), kr[:, _ROPE_SWAP], z(64)], axis=1)
    cos, sin = _rope_tables(n, n_ctx)
    wkv = mla_w_kvb[o].reshape(MLA_KV_RANK, MLA_HEADS, MLA_NOPE + MLA_V)
    wk = wkv[:, :, :MLA_NOPE].reshape(MLA_KV_RANK, MLA_HEADS * MLA_NOPE)
    wvt = jnp.transpose(wkv[:, :, MLA_NOPE:], (1, 2, 0))
    wq = mla_w_qb[o].reshape(MLA_Q_RANK, MLA_HEADS, MLA_NOPE + MLA_ROPE)
    zq = lambda k: jnp.zeros((MLA_Q_RANK, MLA_HEADS, k), F32)
    wq1 = jnp.concatenate([wq, zq(64)], axis=-1).reshape(MLA_Q_RANK, -1)
    wq2 = jnp.concatenate([wq[:, :, MLA_NOPE:][:, :, _ROPE_SWAP], zq(64)], axis=-1).reshape(MLA_Q_RANK, -1)
    proj, k_all, v_all, q_all = _odd_in(
        xc1, x1, row1(norm_g[l, 0]), mods, _bf(w_pad), cos, sin, row1(mla_kv_norm[o]), row1(mla_q_norm[o]),
        _bf(wk), _bf(wvt), _bf(wq1), _bf(wq2))

    ohg = _hgrn(proj, row1(lb_all[l]), n_ctx)
    att = _attention(q_all, k_all, v_all)

    return _odd_post(x1, ohg, proj, att, row1(hg_norm[o]), outw_bf, w1_bf, w2_bf, norm_g, mods, n_ctx, l)
```

```python
import functools

import numpy as np
import jax
import jax.numpy as jnp
from jax import lax
from jax.experimental import pallas as pl
from jax.experimental.pallas import tpu as pltpu

F32 = jnp.float32
BF16 = jnp.bfloat16

D_MODEL = 1024
GRID_W = 64
D_FF = 4 * D_MODEL
N_MOD = 6
EPS = 1e-6
ML_WIDTH = 512
ML_HEADS = 4
ML_DIM = 128
RW_WIDTH = 512
RW_HEAD_DIM = 64
RW_GN_EPS = 64e-5
RW_FEAT = 1920
HG_WIDTH = 512
HG_HEADS = 4
HG_DIM = 128
MLA_HEADS = 4
MLA_NOPE = 128
MLA_ROPE = 64
MLA_V = 128
MLA_Q_RANK = 256
MLA_KV_RANK = 128
ROPE_BASE = 10000.0

LANES = 128
SUBLANES = 8
HALO = 16
VMEM_LIMIT_BYTES = 56 * 2**20

ROW_TILE = 256
LAT_TILE = 512
FF_CHUNK = 1024
ML_CHUNK = 128
ML_BLOCK = 256
RW_CHUNK = 64
RW_BLOCK = 256
HG_BLOCK = 256
HG_CHUNK = 32
ATTN_TQ = 1024
ATTN_SUB = 256


def _dot(a, b, prec=None):
    return jnp.dot(a, b, preferred_element_type=F32, precision=prec)


def _dot_nt(a, b, prec=None):
    return lax.dot_general(a, b, (((1,), (1,)), ((), ())), preferred_element_type=F32, precision=prec)


def _dot_tn(a, b, prec=None):
    return lax.dot_general(a, b, (((0,), (0,)), ((), ())), preferred_element_type=F32, precision=prec)


def _bf(x):
    return x.astype(BF16)


def _mm(a, b):
    return _dot(_bf(a), _bf(b))


def _mm_nt(a, b):
    return _dot_nt(_bf(a), _bf(b))


def _mm_tn(a, b):
    return _dot_tn(_bf(a), _bf(b))


def _split(x):
    hi = _bf(x)
    return hi, _bf(x - hi.astype(F32))


def _sel_dot(sel, x):
    hi, lo = _split(x)
    return _dot(sel, hi) + _dot(sel, lo)


def _dot_sel(x, sel):
    hi, lo = _split(x)
    return _dot(hi, sel) + _dot(lo, sel)


def _dot_sel_nt(x, sel):
    hi, lo = _split(x)
    return _dot_nt(hi, sel) + _dot_nt(lo, sel)


def _dot3(a, b):
    ah, al = _split(a)
    bh, bl = _split(b)
    return _dot(ah, bh) + _dot(ah, bl) + _dot(al, bh)


def _group_mean(x, group):
    low = lax.broadcasted_iota(jnp.int32, (1, LANES), 1) < group
    outs = []
    for c in range(x.shape[1] // LANES):
        blk = x[:, c * LANES:(c + 1) * LANES]
        total = jnp.sum(blk, axis=-1, keepdims=True)
        if group == LANES:
            outs.append(jnp.broadcast_to(total, blk.shape))
        else:
            first = jnp.sum(jnp.where(low, blk, 0.0), axis=-1, keepdims=True)
            outs.append(jnp.where(low, first, total - first))
    return jnp.concatenate(outs, axis=-1) * (1.0 / group)


def _sigmoid(x):
    return 1.0 / (1.0 + jnp.exp(-x))


def _log_sigmoid(x):
    return jnp.minimum(x, 0.0) - jnp.log(1.0 + jnp.exp(-jnp.abs(x)))


def _softplus(x):
    return jnp.maximum(x, 0.0) + jnp.log(1.0 + jnp.exp(-jnp.abs(x)))


def _rms(x, g):
    return x * lax.rsqrt(jnp.mean(x * x, axis=-1, keepdims=True) + EPS) * g


def _order_mask(n, sgn, strict):
    row = lax.broadcasted_iota(jnp.int32, (n, n), 0)
    col = lax.broadcasted_iota(jnp.int32, (n, n), 1)
    diff = (row - col) * sgn
    return diff > 0 if strict else diff >= 0


def _chunk_index(d, j, nctx, nc):
    bwd = jnp.where(j < nctx, nctx - 1 - j, nc - 1 - (j - nctx))
    return jnp.where(d == 0, j, bwd)


def _params(sem):
    return pltpu.CompilerParams(dimension_semantics=sem, vmem_limit_bytes=VMEM_LIMIT_BYTES)


def _const_spec(shape):
    nd = len(shape)
    return pl.BlockSpec(shape, lambda *_: (0,) * nd, pipeline_mode=pl.Buffered(1))


def _mod_spec(k, seg_of):
    return pl.BlockSpec((None, None, None, 1, D_MODEL), lambda b, i: (b, seg_of(i), k, 0, 0))


def _ada_body(c_ref, w_ref, b_ref, o_ref):
    c = c_ref[...]
    o_ref[...] = _dot3(c * _sigmoid(c), w_ref[...]) + b_ref[...]


def _ada_mod(c8, ada_w, ada_b):
    depth, d, n = ada_w.shape
    tn = 1024
    return pl.pallas_call(
        _ada_body,
        grid=(depth, n // tn),
        in_specs=[
            _const_spec((SUBLANES, d)),
            pl.BlockSpec((None, d, tn), lambda l, j: (l, 0, j)),
            pl.BlockSpec((None, 1, tn), lambda l, j: (l, 0, j)),
        ],
        out_specs=pl.BlockSpec((None, SUBLANES, tn), lambda l, j: (l, 0, j)),
        out_shape=jax.ShapeDtypeStruct((depth, SUBLANES, n), F32),
        compiler_params=_params(("parallel", "parallel")),
        name="ada_mod",
    )(c8, ada_w, ada_b.reshape(depth, 1, n))


EV_SHIFTED = 3072


def _even_in_body(nt, c_ref, x_ref, xp_ref, xn_ref, g_ref, sh_ref, sc_ref, w_ref,
                  conv_ref, gb_ref, mu_ref, w0_ref, w2_ref, a0_ref, a2_ref, g2_ref,
                  kk_ref, ka_ref, rk_ref,
                  ovo_ref, oqk_ref, og_ref, ogt_ref, orws_ref, olw_ref, okb_ref, obg_ref):
    i = pl.program_id(0)
    has_prev = (i > 1).astype(F32)
    has_next = jnp.logical_and(i != 0, i != nt - 1).astype(F32)
    n_batch, tm = x_ref.shape[0], x_ref.shape[1]
    ext = tm + 2 * HALO

    def normed(xx, bi, keep=None):
        y = _rms(xx, g_ref[...]) * (1.0 + sc_ref[bi]) + sh_ref[bi]
        return _bf(y if keep is None else y * keep)

    h = [normed(jnp.where(i == 0, c_ref[bi], x_ref[bi]), bi) for bi in range(n_batch)]
    projs = [_dot(jnp.concatenate([normed(xp_ref[bi], bi, has_prev), h[bi], normed(xn_ref[bi], bi, has_next)], axis=0),
                  w_ref[:, 0:EV_SHIFTED]) for bi in range(n_batch)]
    vo = _dot(jnp.concatenate(h, axis=0), w_ref[:, EV_SHIFTED:])
    lane = lax.broadcasted_iota(jnp.int32, (1, LANES), 1)
    is_f = jnp.logical_and(jnp.bitwise_and(jnp.right_shift(lane, 2), 1) == 1, lane < 16)
    w = RW_WIDTH
    for bi in range(n_batch):
        ovo_ref[bi] = vo[bi * tm:(bi + 1) * tm].astype(ovo_ref.dtype)
        proj = projs[bi]

        def neighbours(cols):
            blk = proj[:, cols]
            dn = pltpu.roll(blk, 1, 0)[HALO:HALO + tm]
            up = pltpu.roll(blk, ext - 1, 0)[HALO:HALO + tm]
            return blk[HALO:HALO + tm], dn, up

        qk, dn, up = neighbours(slice(2048, 3072))
        oqk_ref[bi] = (dn * conv_ref[0:1, :] + qk * conv_ref[1:2, :] + up * conv_ref[2:3, :]).astype(oqk_ref.dtype)

        gv = proj[HALO:HALO + tm, RW_FEAT:RW_FEAT + LANES] + gb_ref[...]
        gates = jnp.where(is_f, _log_sigmoid(gv), gv)
        og_ref[bi] = gates
        ogt_ref[bi] = gates.T[0:4 * ML_HEADS, :]

        cur, dn, up = neighbours(slice(0, RW_FEAT))
        rf = cur + (0.5 * (dn + up) - cur) * mu_ref[...]
        r, k, v = rf[:, 0:w], rf[:, w:2 * w], rf[:, 2 * w:3 * w]
        wl = rf[:, 3 * w:3 * w + LANES]
        al = rf[:, 3 * w + LANES:3 * w + 2 * LANES]
        gl = rf[:, 3 * w + 2 * LANES:3 * w + 3 * LANES]
        kk = k * kk_ref[...]
        kk = kk * lax.rsqrt(_group_mean(kk * kk, RW_HEAD_DIM) * RW_HEAD_DIM + 1e-12)
        twl = jnp.tanh(wl)
        kd_sum = jnp.zeros_like(k)
        for d in range(2):
            w_log = -_softplus(-(w0_ref[d:d + 1, :] + _dot_sel(twl, w2_ref[d]))) - 0.5
            a = _sigmoid(a0_ref[d:d + 1, :] + _dot_sel(al, a2_ref[d]))
            kd = k * (1.0 + (a - 1.0) * ka_ref[...])
            kd_sum = kd_sum + kd
            olw_ref[d, bi] = -jnp.exp(w_log)
            okb_ref[d, bi, :, 0:w] = kd.astype(okb_ref.dtype)
            okb_ref[d, bi, :, w:2 * w] = (kk * a).astype(okb_ref.dtype)
        orws_ref[bi, :, 0:w] = r.astype(orws_ref.dtype)
        orws_ref[bi, :, w:2 * w] = v.astype(orws_ref.dtype)
        orws_ref[bi, :, 2 * w:3 * w] = (-kk).astype(orws_ref.dtype)
        bonus = _group_mean(r * kd_sum * rk_ref[...], RW_HEAD_DIM) * RW_HEAD_DIM * v
        obg_ref[bi, :, 0:w] = bonus.astype(obg_ref.dtype)
        obg_ref[bi, :, w:2 * w] = _dot(_bf(_sigmoid(gl)), g2_ref[...]).astype(obg_ref.dtype)


def _even_in(ctx, x, g, mods, w_in, conv, gate_b, mu, w0, w2cat, a0, a2cat, g2, kk, ka, rk):
    b, n, d = x.shape
    tm = ROW_TILE
    assert ctx.shape[1] == tm
    t = n + tm
    nt = t // tm
    rh = tm // HALO
    last_h = n // HALO - 1
    seg = lambda i: (i >= 1).astype(jnp.int32)
    cur = lambda c: (lambda i: (0, i, c))
    lat = lambda i: jnp.maximum(i - 1, 0)
    mod = lambda k: pl.BlockSpec((b, None, None, 1, d), lambda i: (0, seg(i), k, 0, 0))
    w = RW_WIDTH
    return pl.pallas_call(
        functools.partial(_even_in_body, nt),
        grid=(nt,),
        in_specs=[
            pl.BlockSpec((b, tm, d), lambda i: (0, 0, 0)),
            pl.BlockSpec((b, tm, d), lambda i: (0, lat(i), 0)),
            pl.BlockSpec((b, HALO, d), lambda i: (0, jnp.maximum(lat(i) * rh - 1, 0), 0)),
            pl.BlockSpec((b, HALO, d), lambda i: (0, jnp.minimum((lat(i) + 1) * rh, last_h), 0)),
            _const_spec((1, d)), mod(0), mod(1), _const_spec(w_in.shape),
            _const_spec(conv.shape), _const_spec(gate_b.shape), _const_spec(mu.shape),
            _const_spec(w0.shape), _const_spec(w2cat.shape), _const_spec(a0.shape),
            _const_spec(a2cat.shape), _const_spec(g2.shape), _const_spec(kk.shape),
            _const_spec(ka.shape), _const_spec(rk.shape),
        ],
        out_specs=[
            pl.BlockSpec((b, tm, 1024), cur(0)),
            pl.BlockSpec((b, tm, 1024), cur(0)),
            pl.BlockSpec((b, tm, LANES), cur(0)),
            pl.BlockSpec((b, 4 * ML_HEADS, tm), lambda i: (0, 0, i)),
            pl.BlockSpec((b, tm, 3 * w), cur(0)),
            pl.BlockSpec((2, b, tm, w), lambda i: (0, 0, i, 0)),
            pl.BlockSpec((2, b, tm, 2 * w), lambda i: (0, 0, i, 0)),
            pl.BlockSpec((b, tm, 2 * w), cur(0)),
        ],
        out_shape=[
            jax.ShapeDtypeStruct((b, t, 1024), BF16),
            jax.ShapeDtypeStruct((b, t, 1024), BF16),
            jax.ShapeDtypeStruct((b, t, LANES), F32),
            jax.ShapeDtypeStruct((b, 4 * ML_HEADS, t), F32),
            jax.ShapeDtypeStruct((b, t, 3 * w), BF16),
            jax.ShapeDtypeStruct((2, b, t, w), F32),
            jax.ShapeDtypeStruct((2, b, t, 2 * w), BF16),
            jax.ShapeDtypeStruct((b, t, 2 * w), BF16),
        ],
        compiler_params=_params(("parallel",)),
        name="even_in",
    )(ctx, x, x, x, g, mods, mods, w_in, conv, gate_b, mu, w0, w2cat, a0, a2cat, g2, kk, ka, rk)


def _mlstm_body(q_ref, k_ref, v_ref, gc_ref, gr_ref, o_ref, c_scr, n_scr, m_scr):
    d = pl.program_id(0)
    j = pl.program_id(1)
    n_batch = q_ref.shape[0]
    n_rows = ML_CHUNK
    n_sub = q_ref.shape[1] // n_rows

    @pl.when(j == 0)
    def _():
        c_scr[...] = jnp.zeros_like(c_scr)
        n_scr[...] = jnp.zeros_like(n_scr)
        m_scr[...] = jnp.zeros_like(m_scr)

    mask = _order_mask(n_rows, 1 - 2 * d, strict=False)
    mf = mask.astype(BF16)
    row_id = lax.broadcasted_iota(jnp.int32, (n_rows, 1), 0)
    is_last = row_id == (n_rows - 1) * (1 - d)
    pick_row = lax.broadcasted_iota(jnp.int32, (LANES, LANES), 0)
    pick = [(pick_row == 2 * ML_HEADS * d + c).astype(BF16) for c in range(2 * ML_HEADS)]
    ones_cols = jnp.ones((n_rows, LANES), BF16)

    rows = [pl.ds(pl.multiple_of(jnp.where(d == 0, sc, n_sub - 1 - sc) * n_rows, n_rows), n_rows)
            for sc in range(n_sub)]
    heads = [slice(h * ML_DIM, (h + 1) * ML_DIM) for h in range(ML_HEADS)]
    units = [(sc, bi, h) for sc in range(n_sub) for bi in range(n_batch) for h in range(ML_HEADS)]
    idx = range(len(units))
    carried = [bi * ML_HEADS + h for _, bi, h in units]
    gc, b_col, gr, b_row = {}, {}, {}, {}
    for bi in range(n_batch):
        gr_all = gr_ref[bi]
        for sc in range(n_sub):
            gc[sc, bi] = gc_ref[bi, rows[sc], :]
            parts = [gr_all[:, c * n_rows:(c + 1) * n_rows] for c in (sc, n_sub - 1 - sc)]
            gr[sc, bi] = jnp.where(d == 0, parts[0], parts[1])
            b_col[sc, bi] = _sel_dot(mf, gc[sc, bi])
            b_row[sc, bi] = _dot_sel_nt(gr[sc, bi], mf)
    q_f = [q_ref[bi, rows[sc], heads[h]] for sc, bi, h in units]
    q = [_bf(x) for x in q_f]
    kf = [k_ref[bi, rows[sc], heads[h]].astype(F32) * (ML_DIM ** -0.5) for sc, bi, h in units]
    k = [_bf(x) for x in kf]
    vf = [v_ref[bi, rows[sc], heads[h]].astype(F32) for sc, bi, h in units]
    qk = [_dot_nt(q[u], k[u]) for u in idx]
    ic = [_dot_sel(gc[sc, bi], pick[h]) for sc, bi, h in units]
    bc = [_dot_sel(b_col[sc, bi], pick[ML_HEADS + h]) for sc, bi, h in units]
    b_tot = [jnp.sum(jnp.where(is_last, bc[u], 0.0), axis=0, keepdims=True) for u in idx]
    logw = [b_tot[u] - bc[u] + ic[u] for u in idx]
    logw_max = [jnp.max(logw[u], axis=0, keepdims=True) for u in idx]
    logd = [jnp.where(mask, bc[u] - b_row[sc, bi][ML_HEADS + h:ML_HEADS + h + 1, :] + gr[sc, bi][h:h + 1, :], -1e30)
            for u, (sc, bi, h) in enumerate(units)]
    logd_max = [jnp.max(x, axis=-1, keepdims=True) for x in logd]
    m_run = [m_scr[c][0:1, :] for c in range(n_batch * ML_HEADS)]
    wgt, decay, s, m_t, w_inter = [], [], [], [], []
    for u in idx:
        m_prev = m_run[carried[u]]
        m_new = jnp.maximum(b_tot[u] + m_prev, logw_max[u])
        wgt.append(jnp.exp(logw[u] - m_new))
        decay.append(jnp.exp(b_tot[u] + m_prev - m_new))
        m_inter = bc[u] + m_prev
        m_t.append(jnp.maximum(m_inter, logd_max[u]))
        s.append(qk[u] * jnp.exp(logd[u] - m_t[u]))
        w_inter.append(jnp.exp(m_inter - m_t[u]))
        m_run[carried[u]] = m_new
    kv = [_dot_tn(_bf(vf[u] * wgt[u]), k[u]) for u in idx]
    sv = [_dot(_bf(s[u]), jnp.concatenate([_bf(vf[u]), ones_cols], axis=1)) for u in idx]
    c_run = [c_scr[c] for c in range(n_batch * ML_HEADS)]
    n_run = [n_scr[c][0:1, :] for c in range(n_batch * ML_HEADS)]
    per_chunk = n_batch * ML_HEADS
    for sc in range(n_sub):
        ids = range(sc * per_chunk, (sc + 1) * per_chunk)
        qc = [_dot_nt(q[u], jnp.concatenate([_bf(c_run[carried[u]]),
                                             _bf(jnp.broadcast_to(n_run[carried[u]], (LANES, ML_DIM)))], axis=0))
              for u in ids]
        for u, x in zip(ids, qc):
            _, bi, h = units[u]
            num = sv[u][:, :ML_DIM] + w_inter[u] * x[:, :ML_DIM]
            den = sv[u][:, ML_DIM:] + w_inter[u] * x[:, ML_DIM:]
            o_ref[bi, rows[sc], heads[h]] = num / jnp.maximum(jnp.abs(den), jnp.exp(-m_t[u]))
        for u in ids:
            c = carried[u]
            c_run[c] = decay[u] * c_run[c] + kv[u]
            n_run[c] = decay[u] * n_run[c] + jnp.sum(wgt[u] * kf[u], axis=0, keepdims=True)
    for c in range(per_chunk):
        c_scr[c] = c_run[c]
        n_scr[c] = jnp.broadcast_to(n_run[c], (SUBLANES, ML_DIM))
        m_scr[c] = jnp.broadcast_to(m_run[c], (SUBLANES, LANES))


def _mlstm(qk, proj, gates, gates_t, n_ctx):
    b, t, _ = qk.shape
    n_rows = ML_BLOCK
    nc, nctx = t // n_rows, n_ctx // n_rows
    ci = lambda d, j: _chunk_index(d, j, nctx, nc)
    return pl.pallas_call(
        _mlstm_body,
        grid=(2, nc),
        in_specs=[
            pl.BlockSpec((b, n_rows, ML_WIDTH), lambda d, j: (0, ci(d, j), 0)),
            pl.BlockSpec((b, n_rows, ML_WIDTH), lambda d, j: (0, ci(d, j), 1)),
            pl.BlockSpec((b, n_rows, ML_WIDTH), lambda d, j: (0, ci(d, j), 0)),
            pl.BlockSpec((b, n_rows, LANES), lambda d, j: (0, ci(d, j), 0)),
            pl.BlockSpec((b, 2 * ML_HEADS, n_rows), lambda d, j: (0, d, ci(d, j))),
        ],
        out_specs=pl.BlockSpec((None, b, n_rows, ML_WIDTH), lambda d, j: (d, 0, ci(d, j), 0)),
        out_shape=jax.ShapeDtypeStruct((2, b, t, ML_WIDTH), F32),
        scratch_shapes=[
            pltpu.VMEM((b * ML_HEADS, ML_DIM, ML_DIM), F32),
            pltpu.VMEM((b * ML_HEADS, SUBLANES, ML_DIM), F32),
            pltpu.VMEM((b * ML_HEADS, SUBLANES, LANES), F32),
        ],
        compiler_params=_params(("parallel", "arbitrary")),
        name="mlstm",
    )(qk, qk, proj, gates, gates_t)


def _rwkv_body(r_ref, v_ref, a_ref, lw_ref, k_ref, b_ref, o_ref, h_scr):
    d = pl.program_id(0)
    j = pl.program_id(1)
    n_rows = RW_CHUNK
    two = 2 * n_rows
    shift = n_rows.bit_length() - 1

    @pl.when(j == 0)
    def _():
        h_scr[...] = jnp.zeros_like(h_scr)

    sgn = 1 - 2 * d
    cum = _order_mask(n_rows, sgn, strict=False).astype(BF16)
    row = lax.broadcasted_iota(jnp.int32, (two, two), 0)
    col = lax.broadcasted_iota(jnp.int32, (two, two), 1)
    same = jnp.right_shift(row, shift) == jnp.right_shift(col, shift)
    diff = (jnp.bitwise_and(row, n_rows - 1) - jnp.bitwise_and(col, n_rows - 1)) * sgn
    m_strict = jnp.logical_and(same, diff > 0)
    m_incl = jnp.logical_and(same, diff >= 0)
    eye = (row == col).astype(F32)
    head0 = lax.broadcasted_iota(jnp.int32, (1, LANES), 1) < RW_HEAD_DIM

    def stack(x):
        return jnp.concatenate([jnp.where(head0, x, 0.0), jnp.where(head0, 0.0, x)], axis=0)

    def fold(x):
        return x[:n_rows] + x[n_rows:]

    n_batch = r_ref.shape[0]
    n_sub = r_ref.shape[1] // n_rows
    lane_groups = [slice(p * LANES, (p + 1) * LANES) for p in range(RW_WIDTH // LANES)]
    rows = [pl.ds(pl.multiple_of(jnp.where(d == 0, sc, n_sub - 1 - sc) * n_rows, n_rows), n_rows)
            for sc in range(n_sub)]
    groups = [(sc, bi, p) for sc in range(n_sub) for bi in range(n_batch) for p in range(len(lane_groups))]
    x_s, r_s, v_s, bh_s, kh_s, bi_s, ki_s, e_tot = [], [], [], [], [], [], [], []
    for sc in range(n_sub):
        for bi in range(n_batch):
            lw = lw_ref[bi, rows[sc], :]
            g = _sel_dot(cum, lw)
            g_tot = jnp.sum(lw, axis=0, keepdims=True)
            inv = jnp.exp(-g)
            to_end = jnp.exp(g_tot - g)
            at = a_ref[bi, rows[sc], :].astype(F32) * jnp.exp(g - lw)
            rt = r_ref[bi, rows[sc], :].astype(F32) * jnp.exp(g)
            b_in = b_ref[bi, rows[sc], :].astype(F32)
            k_in = k_ref[bi, rows[sc], :].astype(F32)
            vv = v_ref[bi, rows[sc], :].astype(F32)
            for sl in lane_groups:
                x_s.append(stack(at[:, sl]))
                r_s.append(stack(rt[:, sl]))
                v_s.append(stack(vv[:, sl]))
                bh_s.append(stack((b_in * to_end)[:, sl]))
                kh_s.append(stack((k_in * to_end)[:, sl]))
                bi_s.append(stack((b_in * inv)[:, sl]))
                ki_s.append(stack((k_in * inv)[:, sl]))
                e_tot.append(jnp.exp(g_tot)[:, sl])
    big = [_mm_nt(jnp.concatenate([x_s[p], r_s[p]], axis=0), jnp.concatenate([bi_s[p], ki_s[p]], axis=0))
           for p in range(len(groups))]
    a_ab = [jnp.where(m_strict, m[:two, :two], 0.0) for m in big]
    a_rb = [jnp.where(m_incl, m[two:, :two], 0.0) for m in big]
    a_k = [jnp.concatenate([jnp.where(m_strict, m[:two, two:], 0.0), jnp.where(m_incl, m[two:, two:], 0.0)], axis=0)
           for m in big]
    av = [_mm(a_k[p], v_s[p]) for p in range(len(groups))]
    t_inv = [eye + m for m in a_ab]
    pw = [_mm(m, m) for m in a_ab]
    for level in range(1, shift):
        last = level == shift - 1
        res = [_mm(t if last else jnp.concatenate([t, m], axis=0), m) for t, m in zip(t_inv, pw)]
        t_inv = [t + r[:two] for t, r in zip(t_inv, res)]
        if not last:
            pw = [r[two:] for r in res]
    wu = [_mm(t_inv[p], jnp.concatenate([x_s[p], av[p][:two]], axis=1)) for p in range(len(groups))]
    rb = [_mm(a_rb[p], wu[p]) for p in range(len(groups))]
    upd = [_mm_tn(wu[p], bh_s[p]) for p in range(len(groups))]
    gt2 = [_mm_tn(v_s[p], kh_s[p]) for p in range(len(groups))]
    q_eff = [fold(r_s[p] + rb[p][:, :LANES]) for p in range(len(groups))]
    y_in = [fold(rb[p][:, LANES:] + av[p][two:]) for p in range(len(groups))]
    per_chunk = n_batch * len(lane_groups)
    h_t = [h_scr[u] for u in range(per_chunk)]
    for sc in range(n_sub):
        ids = range(sc * per_chunk, (sc + 1) * per_chunk)
        for u, p in enumerate(ids):
            _, bi, lg = groups[p]
            o_ref[bi, rows[sc], lane_groups[lg]] = _mm_nt(q_eff[p], h_t[u]) + y_in[p]
        h_t = [e_tot[p] * h_t[u] + _mm(h_t[u], upd[p][:LANES]) + upd[p][LANES:] + gt2[p] for u, p in enumerate(ids)]
    for u in range(per_chunk):
        h_scr[u] = h_t[u]


def _rwkv(rws, lw, kb, n_ctx):
    b, t, _ = rws.shape
    n_rows = RW_BLOCK
    nc, nctx = t // n_rows, n_ctx // n_rows
    ci = lambda d, j: _chunk_index(d, j, nctx, nc)
    w = RW_WIDTH
    shared = lambda c: pl.BlockSpec((b, n_rows, w), lambda d, j: (0, ci(d, j), c))
    per_dir = lambda c: pl.BlockSpec((None, b, n_rows, w), lambda d, j: (d, 0, ci(d, j), c))
    return pl.pallas_call(
        _rwkv_body,
        grid=(2, nc),
        in_specs=[shared(0), shared(1), shared(2), per_dir(0), per_dir(0), per_dir(1)],
        out_specs=pl.BlockSpec((None, b, n_rows, w), lambda d, j: (d, 0, ci(d, j), 0)),
        out_shape=jax.ShapeDtypeStruct((2, b, t, w), F32),
        scratch_shapes=[pltpu.VMEM((b * w // LANES, LANES, LANES), F32)],
        compiler_params=_params(("parallel", "arbitrary")),
        name="rwkv7",
    )(rws, rws, rws, lw, kb, kb)


def _tail(x, mix, outw_ref, w1_ref, w2_ref, ng_ref, ga_ref, shm_ref, scm_ref, gm_ref):
    x1 = x + ga_ref[...] * _rms(_dot(_bf(mix), outw_ref[...]), ng_ref[1:2, :])
    h = _bf(_rms(x1, ng_ref[2:3, :]) * (1.0 + scm_ref[...]) + shm_ref[...])
    chunks = [slice(c * FF_CHUNK, (c + 1) * FF_CHUNK) for c in range(D_FF // FF_CHUNK)]
    u = [jnp.maximum(_dot(h, w1_ref[:, cols]), 0.0) for cols in chunks]
    dd = _dot(_bf(u[0] * u[0]), w2_ref[chunks[0], :])
    for uc, cols in zip(u[1:], chunks[1:]):
        dd = dd + _dot(_bf(uc * uc), w2_ref[cols, :])
    return x1 + gm_ref[...] * _rms(dd, ng_ref[3:4, :])


def _even_post_body(x_ref, hm_ref, mo_ref, yrw_ref, bg_ref, mln_ref, lnw_ref, lnb_ref,
                    outw_ref, w1_ref, w2_ref, ng_ref, ga_ref, shm_ref, scm_ref, gm_ref, o_ref):
    w = RW_WIDTH
    ym = hm_ref[0, 0] + hm_ref[1, 0]
    ym = (ym * lax.rsqrt(_group_mean(ym * ym, ML_DIM) + EPS) * mln_ref[...]
          * _sigmoid(mo_ref[0].astype(F32)))
    y = yrw_ref[0, 0] + yrw_ref[1, 0]
    yc = y - _group_mean(y, RW_HEAD_DIM)
    y = (yc * lax.rsqrt(_group_mean(yc * yc, RW_HEAD_DIM) + RW_GN_EPS) * lnw_ref[...] + lnb_ref[...]
         + bg_ref[0, :, 0:w].astype(F32))
    mix = jnp.concatenate([ym, y * bg_ref[0, :, w:2 * w].astype(F32)], axis=-1)
    o_ref[...] = _tail(x_ref[...], mix, outw_ref, w1_ref, w2_ref, ng_ref, ga_ref, shm_ref, scm_ref, gm_ref)


def _tail_specs(seg, layer):
    d = D_MODEL
    pick = lambda *shape: pl.BlockSpec((None,) + shape, lambda *_: (layer, 0, 0), pipeline_mode=pl.Buffered(1))
    return [
        pick(d, d), pick(d, D_FF), pick(D_FF, d), pick(4, d),
        _mod_spec(2, seg), _mod_spec(3, seg), _mod_spec(4, seg), _mod_spec(5, seg),
    ]


def _even_post(x_seg, first_row, segment, tm, hm, proj, yrw, bg, mln, lnw, lnb, outw, w1, w2, ng, mods, layer):
    b, rows, d = x_seg.shape
    seg = lambda i: segment
    w = RW_WIDTH
    el = lambda *shape: tuple(pl.Element(s) for s in shape)
    row0 = lambda i: pl.multiple_of(first_row + i * tm, int(np.gcd(first_row, tm)))
    return pl.pallas_call(
        _even_post_body,
        grid=(b, rows // tm),
        in_specs=[
            pl.BlockSpec((None, tm, d), lambda bb, i: (bb, i, 0)),
            pl.BlockSpec(el(2, 1, tm, w), lambda bb, i: (0, bb, row0(i), 0)),
            pl.BlockSpec(el(1, tm, w), lambda bb, i: (bb, row0(i), w)),
            pl.BlockSpec(el(2, 1, tm, w), lambda bb, i: (0, bb, row0(i), 0)),
            pl.BlockSpec(el(1, tm, 2 * w), lambda bb, i: (bb, row0(i), 0)),
            _const_spec((1, w)), _const_spec((1, w)), _const_spec((1, w)),
        ] + _tail_specs(seg, layer),
        out_specs=pl.BlockSpec((None, tm, d), lambda bb, i: (bb, i, 0)),
        out_shape=jax.ShapeDtypeStruct((b, rows, d), F32),
        compiler_params=_params(("parallel", "parallel")),
        name="even_post",
    )(x_seg, hm, proj, yrw, bg, mln, lnw, lnb, outw, w1, w2, ng, mods, mods, mods, mods)


def _hgrn_body(q_ref, f_ref, i_ref, lb_ref, o_ref, s_scr):
    d = pl.program_id(0)
    j = pl.program_id(1)
    n_batch, n_rows = q_ref.shape[0], q_ref.shape[1]
    n_sub = n_rows // HG_CHUNK

    @pl.when(j == 0)
    def _():
        s_scr[...] = jnp.zeros_like(s_scr)

    mask = _order_mask(HG_CHUNK, 1 - 2 * d, strict=False)
    mf = mask.astype(BF16)
    lb = lb_ref[...]
    heads = range(HG_HEADS)
    sls = [slice(h * HG_DIM, (h + 1) * HG_DIM) for h in heads]
    units = [(bi, h) for bi in range(n_batch) for h in heads]
    n_units = len(units)
    s_t = [s_scr[u] for u in range(n_units)]
    rows = [pl.ds(pl.multiple_of(jnp.where(d == 0, sc, n_sub - 1 - sc) * HG_CHUNK, HG_CHUNK), HG_CHUNK)
            for sc in range(n_sub)]
    loaded = [[(q_ref[bi, r, :], f_ref[bi, r, :], i_ref[bi, r, :]) for bi in range(n_batch)] for r in rows]
    q_in, e_tot, q_mids, k_mids, k_outs, vvs = [], [], [], [], [], []

    def prepare(sc):
        for lst in (q_in, e_tot, q_mids, k_mids, k_outs, vvs):
            lst.append([])
        for bi in range(n_batch):
            qa = loaded[sc][bi][0].astype(F32)
            pre = loaded[sc][bi][1].astype(F32)
            vv = loaded[sc][bi][2]
            q = qa * _sigmoid(qa) * (HG_DIM ** -0.5)
            e_abs = jnp.exp(-jnp.abs(pre))
            big = 1.0 / (1.0 + e_abs)
            small = e_abs * big
            log_f = jnp.log(lb + (1.0 - lb) * jnp.where(pre >= 0.0, big, small))
            kf = (1.0 - lb) * jnp.where(pre >= 0.0, small, big)
            bc = _sel_dot(mf, log_f)
            tot = jnp.sum(log_f, axis=0, keepdims=True)
            half = 0.5 * tot
            e_half = jnp.exp(half)
            q_mid_f = q * jnp.exp(bc - half)
            k_mid_f = kf * jnp.exp(half - bc)
            q_mid = _bf(q_mid_f)
            k_mid = _bf(k_mid_f)
            k_out = _bf(k_mid_f * e_half)
            q_all = _bf(q_mid_f * e_half)
            e_all = e_half * e_half
            q_mids[sc] += [q_mid[:, sl] for sl in sls]
            k_mids[sc] += [k_mid[:, sl] for sl in sls]
            k_outs[sc] += [k_out[:, sl] for sl in sls]
            vvs[sc] += [vv[:, sl] for sl in sls]
            q_in[sc] += [q_all[:, sl] for sl in sls]
            e_tot[sc] += [e_all[:, sl] for sl in sls]

    prepare(0)
    for sc in range(n_sub):
        if sc + 1 < n_sub:
            prepare(sc + 1)
        att = [_dot_nt(q_mids[sc][u], k_mids[sc][u]) for u in range(n_units)]
        kv = [_dot_tn(vvs[sc][u], k_outs[sc][u]) for u in range(n_units)]
        intra = [_dot(_bf(jnp.where(mask, att[u], 0.0)), vvs[sc][u]) for u in range(n_units)]
        for u, (bi, h) in enumerate(units):
            o_ref[bi, rows[sc], sls[h]] = intra[u] + _dot_nt(q_in[sc][u], _bf(s_t[u]))
        s_t = [e_tot[sc][u] * s_t[u] + kv[u] for u in range(n_units)]
    for u in range(n_units):
        s_scr[u] = s_t[u]


def _hgrn(proj, lb, n_ctx):
    b, t, _ = proj.shape
    n_rows = HG_BLOCK
    nc, nctx = t // n_rows, n_ctx // n_rows
    ci = lambda d, j: _chunk_index(d, j, nctx, nc)
    w = HG_WIDTH
    return pl.pallas_call(
        _hgrn_body,
        grid=(2, nc),
        in_specs=[
            pl.BlockSpec((b, n_rows, w), lambda d, j: (0, ci(d, j), 0)),
            pl.BlockSpec((b, n_rows, w), lambda d, j: (0, ci(d, j), 1 + d)),
            pl.BlockSpec((b, n_rows, w), lambda d, j: (0, ci(d, j), 3)),
            _const_spec((1, w)),
        ],
        out_specs=pl.BlockSpec((None, b, n_rows, w), lambda d, j: (d, 0, ci(d, j), 0)),
        out_shape=jax.ShapeDtypeStruct((2, b, t, w), F32),
        scratch_shapes=[pltpu.VMEM((b * HG_HEADS, HG_DIM, HG_DIM), F32)],
        compiler_params=_params(("parallel", "arbitrary")),
        name="hgrn2",
    )(proj, proj, proj, lb)


V_ROWS = MLA_V + SUBLANES
OD_MAIN = 5 * HG_WIDTH


def _odd_in_body(c_ref, x_ref, g_ref, sh_ref, sc_ref, w_ref, cos_ref, sin_ref, kvn_ref, qn_ref,
                 wk_ref, wvt_ref, wq1_ref, wq2_ref, proj_ref, k_ref, vt_ref, q_ref):
    n_batch, tm = x_ref.shape[0], x_ref.shape[1]
    is_ctx = pl.program_id(0) == 0
    h = jnp.concatenate([_bf(_rms(jnp.where(is_ctx, c_ref[bi], x_ref[bi]), g_ref[...]) * (1.0 + sc_ref[bi])
                             + sh_ref[bi]) for bi in range(n_batch)], axis=0)
    mla = _dot(h, w_ref[:, OD_MAIN:])
    main = _dot(h, w_ref[:, 0:OD_MAIN])
    cq = mla[:, 0:MLA_Q_RANK]
    base = MLA_Q_RANK
    ckv = mla[:, base:base + MLA_KV_RANK]
    cos = jnp.concatenate([cos_ref[...]] * n_batch, axis=0)
    sin = jnp.concatenate([sin_ref[...]] * n_batch, axis=0)
    k_rope = mla[:, base + LANES:base + 2 * LANES] * cos + mla[:, base + 2 * LANES:base + 3 * LANES] * sin
    kvn = _bf(_rms(ckv, kvn_ref[...]))
    kn = _dot(kvn, wk_ref[...])
    ones = jnp.ones((SUBLANES, tm), BF16)
    cn = _bf(_rms(cq, qn_ref[...]))
    qa = _dot(cn, wq1_ref[...])
    qb = _dot(cn, wq2_ref[...])
    scale = (MLA_NOPE + MLA_ROPE) ** -0.5 * float(np.log2(np.e))
    for hd in range(MLA_HEADS):
        k_h = jnp.concatenate([kn[:, hd * MLA_NOPE:(hd + 1) * MLA_NOPE], k_rope], axis=-1).astype(BF16)
        vt = _dot_nt(wvt_ref[hd], kvn).astype(BF16)
        a = qa[:, hd * 256:(hd + 1) * 256]
        q_rope = a[:, MLA_NOPE:] * cos + qb[:, hd * LANES:(hd + 1) * LANES] * sin
        q_h = (jnp.concatenate([a[:, 0:MLA_NOPE], q_rope], axis=-1) * scale).astype(BF16)
        for bi in range(n_batch):
            rows = slice(bi * tm, (bi + 1) * tm)
            k_ref[bi, hd] = k_h[rows]
            vt_ref[bi, hd] = jnp.concatenate([vt[:, rows], ones], axis=0)
            q_ref[bi, hd] = q_h[rows]
    for bi in range(n_batch):
        proj_ref[bi] = main[bi * tm:(bi + 1) * tm].astype(proj_ref.dtype)


def _odd_in(xc, x, g, mods, w_in, cos, sin, kv_norm, q_norm, wk, wvt, wq1, wq2):
    b, n, d = x.shape
    tm = ROW_TILE
    assert xc.shape[1] == tm
    t = n + tm
    seg = lambda i: (i >= 1).astype(jnp.int32)
    q_blk = lambda i: (0, 0, jnp.maximum(i - 1, 0), 0)
    mod = lambda k: pl.BlockSpec((b, None, None, 1, d), lambda i: (0, seg(i), k, 0, 0))
    return pl.pallas_call(
        _odd_in_body,
        grid=(t // tm,),
        in_specs=[
            pl.BlockSpec((b, tm, d), lambda i: (0, 0, 0)),
            pl.BlockSpec((b, tm, d), lambda i: (0, jnp.maximum(i - 1, 0), 0)),
            _const_spec((1, d)), mod(0), mod(1), _const_spec(w_in.shape),
            pl.BlockSpec((tm, LANES), lambda i: (i, 0)),
            pl.BlockSpec((tm, LANES), lambda i: (i, 0)),
            _const_spec((1, MLA_KV_RANK)), _const_spec((1, MLA_Q_RANK)),
            _const_spec(wk.shape), _const_spec(wvt.shape), _const_spec(wq1.shape), _const_spec(wq2.shape),
        ],
        out_specs=[
            pl.BlockSpec((b, tm, OD_MAIN), lambda i: (0, i, 0)),
            pl.BlockSpec((b, MLA_HEADS, tm, 256), lambda i: (0, 0, i, 0)),
            pl.BlockSpec((b, MLA_HEADS, V_ROWS, tm), lambda i: (0, 0, 0, i)),
            pl.BlockSpec((b, MLA_HEADS, tm, 256), q_blk),
        ],
        out_shape=[
            jax.ShapeDtypeStruct((b, t, OD_MAIN), BF16),
            jax.ShapeDtypeStruct((b, MLA_HEADS, t, 256), BF16),
            jax.ShapeDtypeStruct((b, MLA_HEADS, V_ROWS, t), BF16),
            jax.ShapeDtypeStruct((b, MLA_HEADS, n, 256), BF16),
        ],
        compiler_params=_params(("arbitrary",)),
        name="odd_in",
    )(xc, x, g, mods, mods, w_in, cos, sin, kv_norm, q_norm, wk, wvt, wq1, wq2)


def _attn_body(q_ref, k_ref, vt_ref, o_ref):
    k = k_ref[...]
    vt = vt_ref[...]
    subs = [slice(i * ATTN_SUB, (i + 1) * ATTN_SUB) for i in range(q_ref.shape[0] // ATTN_SUB)]
    st = [_dot_nt(k, q_ref[sl, :]) for sl in subs]
    pt = [_bf(jnp.exp2(s - jnp.max(s, axis=0, keepdims=True))) for s in st]
    ot = [_dot(vt, p) for p in pt]
    for sl, o in zip(subs, ot):
        o_ref[sl, :] = (o[0:MLA_V] / o[MLA_V:MLA_V + 1]).T


def _attention(q, k, vt):
    b, h, n, dq = q.shape
    t = k.shape[2]
    tq = ATTN_TQ
    return pl.pallas_call(
        _attn_body,
        grid=(b, h, n // tq),
        in_specs=[
            pl.BlockSpec((None, None, tq, dq), lambda bb, hh, i: (bb, hh, i, 0)),
            pl.BlockSpec((None, None, t, dq), lambda bb, hh, i: (bb, hh, 0, 0), pipeline_mode=pl.Buffered(1)),
            pl.BlockSpec((None, None, V_ROWS, t), lambda bb, hh, i: (bb, hh, 0, 0), pipeline_mode=pl.Buffered(1)),
        ],
        out_specs=pl.BlockSpec((None, tq, MLA_V), lambda bb, hh, i: (bb, i, hh)),
        out_shape=jax.ShapeDtypeStruct((b, n, h * MLA_V), F32),
        compiler_params=_params(("parallel", "parallel", "parallel")),
        name="mla_attention",
    )(q, k, vt)


def _odd_post_body(x_ref, o_ref_in, hg_ref, att_ref, hgn_ref,
                   outw_ref, w1_ref, w2_ref, ng_ref, ga_ref, shm_ref, scm_ref, gm_ref, o_ref):
    y = o_ref_in[0, 0] + o_ref_in[1, 0]
    g = hg_ref[0].astype(F32)
    y = y * lax.rsqrt(_group_mean(y * y, HG_DIM) + EPS) * hgn_ref[...] * (g * _sigmoid(g))
    mix = jnp.concatenate([y, att_ref[...]], axis=-1)
    o_ref[...] = _tail(x_ref[...], mix, outw_ref, w1_ref, w2_ref, ng_ref, ga_ref, shm_ref, scm_ref, gm_ref)


def _odd_post(x, ohg, proj, att, hgn, outw, w1, w2, ng, mods, n_ctx, layer):
    b, n, d = x.shape
    tm = LAT_TILE
    seg = lambda i: 1
    w = HG_WIDTH
    el = lambda *shape: tuple(pl.Element(s) for s in shape)
    row0 = lambda i: pl.multiple_of(n_ctx + i * tm, int(np.gcd(n_ctx, tm)))
    return pl.pallas_call(
        _odd_post_body,
        grid=(b, n // tm),
        in_specs=[
            pl.BlockSpec((None, tm, d), lambda bb, i: (bb, i, 0)),
            pl.BlockSpec(el(2, 1, tm, w), lambda bb, i: (0, bb, row0(i), 0)),
            pl.BlockSpec(el(1, tm, w), lambda bb, i: (bb, row0(i), 4 * w)),
            pl.BlockSpec((None, tm, w), lambda bb, i: (bb, i, 0)),
            _const_spec((1, w)),
        ] + _tail_specs(seg, layer),
        out_specs=pl.BlockSpec((None, tm, d), lambda bb, i: (bb, i, 0)),
        out_shape=jax.ShapeDtypeStruct((b, n, d), F32),
        compiler_params=_params(("parallel", "parallel")),
        name="odd_post",
    )(x, ohg, proj, att, hgn, outw, w1, w2, ng, mods, mods, mods, mods)


def _rope_tables(n, n_ctx):
    quarter = MLA_ROPE // 4
    inv = ROPE_BASE ** (-jnp.arange(quarter, dtype=F32) / quarter)
    rows = n // GRID_W
    ang_r = jnp.arange(rows, dtype=F32)[:, None] * inv[None, :]
    ang_c = jnp.arange(GRID_W, dtype=F32)[:, None] * inv[None, :]
    by_row = lambda a: jnp.repeat(a, GRID_W, axis=0)
    by_col = lambda a: jnp.tile(a, (rows, 1))
    cos_r, sin_r, cos_c, sin_c = by_row(jnp.cos(ang_r)), by_row(jnp.sin(ang_r)), by_col(jnp.cos(ang_c)), by_col(jnp.sin(ang_c))
    cos = jnp.concatenate([cos_r, cos_r, cos_c, cos_c], axis=-1)
    sin = jnp.concatenate([-sin_r, sin_r, -sin_c, sin_c], axis=-1)
    cos = jnp.concatenate([jnp.ones((n_ctx, MLA_ROPE), F32), cos], axis=0)
    sin = jnp.concatenate([jnp.zeros((n_ctx, MLA_ROPE), F32), sin], axis=0)
    pad = lambda a: jnp.pad(a, ((0, 0), (0, LANES - MLA_ROPE)))
    return pad(cos), pad(sin)


_ROPE_SWAP = np.concatenate([np.arange(16, 32), np.arange(0, 16), np.arange(48, 64), np.arange(32, 48)])


def kernel(x, c, ctx, c_ctx, ada_w, ada_b, norm_g, out_w, mlp_w1, mlp_w2, hg_lb, ev_w_in, ml_conv, ml_gate_b, ml_norm, rw_mu, rw_w0, rw_w2, rw_a0, rw_a2, rw_g2, rw_kk, rw_ka, rw_rk, rw_ln_w, rw_ln_b, od_w_in, hg_norm, mla_q_norm, mla_w_qb, mla_kv_norm, mla_w_kvb):
    b, n, d = x.shape
    n_ctx = ctx.shape[1]
    assert d == D_MODEL and b + 1 <= SUBLANES
    assert n_ctx == ROW_TILE and n % LAT_TILE == 0 and n % ATTN_TQ == 0
    assert all(n_ctx % blk == 0 and n % blk == 0 for blk in (ML_BLOCK, RW_BLOCK, HG_BLOCK))
    row1 = lambda a: a.reshape(1, -1)

    c8 = jnp.concatenate([c, c_ctx[None], jnp.zeros((SUBLANES - b - 1, d), F32)], axis=0)
    mod = _ada_mod(c8, ada_w, ada_b).reshape(-1, SUBLANES, N_MOD, d)

    def layer_mods(l):
        lat = mod[l, :b]
        cm = jnp.broadcast_to(mod[l, b][None], lat.shape)
        return jnp.stack([cm, lat], axis=1)[:, :, :, None, :]

    outw_bf, w1_bf, w2_bf = _bf(out_w), _bf(mlp_w1), _bf(mlp_w2)

    l, e = 0, 0
    mods = layer_mods(l)
    wi = ev_w_in[e]
    w_pad = jnp.concatenate([wi[:, 2064:3984], wi[:, 2048:2064], jnp.zeros((d, LANES - 16), F32), wi[:, 0:2048]], axis=1)
    gate_b = jnp.pad(ml_gate_b[e].reshape(1, 16), ((0, 0), (0, LANES - 16)))
    zeros64 = jnp.zeros((64, RW_WIDTH), F32)
    w2cat = jnp.stack([jnp.concatenate([rw_w2[e, 0], zeros64], 0), jnp.concatenate([zeros64, rw_w2[e, 1]], 0)])
    a2cat = jnp.stack([jnp.concatenate([rw_a2[e, 0], zeros64], 0), jnp.concatenate([zeros64, rw_a2[e, 1]], 0)])
    proj, qk, gates, gates_t, rws, rw_lw, rw_kb, bg = _even_in(
        ctx, x, row1(norm_g[l, 0]), mods, _bf(w_pad),
        ml_conv[e], gate_b, row1(rw_mu[e]), rw_w0[e], _bf(w2cat), rw_a0[e], _bf(a2cat), _bf(rw_g2[e]),
        row1(rw_kk[e]), row1(rw_ka[e]), row1(rw_rk[e]))

    hm = _mlstm(qk, proj, gates, gates_t, n_ctx)
    yrw = _rwkv(rws, rw_lw, rw_kb, n_ctx)
    post_args = (hm, proj, yrw, bg, row1(ml_norm[e]), row1(rw_ln_w[e]), row1(rw_ln_b[e]),
                 outw_bf, w1_bf, w2_bf, norm_g, mods, l)
    xc1 = _even_post(ctx, 0, 0, ROW_TILE, *post_args)
    x1 = _even_post(x, n_ctx, 1, LAT_TILE, *post_args)

    l, o = 1, 0
    mods = layer_mods(l)
    lb_all = jax.nn.softmax(hg_lb.astype(F32), axis=0)
    lb_all = jnp.cumsum(lb_all, axis=0) - lb_all[0]
    wi = od_w_in[o]
    kr = wi[:, 2944:3008]
    z = lambda k: jnp.zeros((d, k), F32)
    w_pad = jnp.concatenate([wi[:, 0:2944], kr, z(64), kr[:, _ROPE_SWAP], z(64)], axis=1)
    cos, sin = _rope_tables(n, n_ctx)
    wkv = mla_w_kvb[o].reshape(MLA_KV_RANK, MLA_HEADS, MLA_NOPE + MLA_V)
    wk = wkv[:, :, :MLA_NOPE].reshape(MLA_KV_RANK, MLA_HEADS * MLA_NOPE)
    wvt = jnp.transpose(wkv[:, :, MLA_NOPE:], (1, 2, 0))
    wq = mla_w_qb[o].reshape(MLA_Q_RANK, MLA_HEADS, MLA_NOPE + MLA_ROPE)
    zq = lambda k: jnp.zeros((MLA_Q_RANK, MLA_HEADS, k), F32)
    wq1 = jnp.concatenate([wq, zq(64)], axis=-1).reshape(MLA_Q_RANK, -1)
    wq2 = jnp.concatenate([wq[:, :, MLA_NOPE:][:, :, _ROPE_SWAP], zq(64)], axis=-1).reshape(MLA_Q_RANK, -1)
    proj, k_all, v_all, q_all = _odd_in(
        xc1, x1, row1(norm_g[l, 0]), mods, _bf(w_pad), cos, sin, row1(mla_kv_norm[o]), row1(mla_q_norm[o]),
        _bf(wk), _bf(wvt), _bf(wq1), _bf(wq2))

    ohg = _hgrn(proj, row1(lb_all[l]), n_ctx)
    att = _attention(q_all, k_all, v_all)

    return _odd_post(x1, ohg, proj, att, row1(hg_norm[o]), outw_bf, w1_bf, w2_bf, norm_g, mods, n_ctx, l)
```
